```python
import math
import jax
import jax.numpy as jnp
from jax import lax
import numpy as np

D_MODEL = 1024
BATCH = 8
SEQ = 4096
DEPTH = 2

GRID_W = 64
CTX_LEN = 256
W_HALF = D_MODEL // 2
MIX_W = 2 * W_HALF
EPS = 1e-6
LRU_HEADS = 8
LRU_HEAD_DIM = W_HALF // LRU_HEADS
LRU_CONV = 4
LRU_C = 8.0
HY_ORDER = 2
HY_SHORT = 3
HY_BANDS = 16
HY_EMB = 2 * HY_BANDS + 1
HY_FFN = 64
HY_MAX_DECAY = math.log(1e-2) / 0.3
HY_MIN_DECAY = math.log(1e-2) / 1.5
HG_HEADS = 4
HG_DK = W_HALF // HG_HEADS
HG_DV = W_HALF // HG_HEADS
HG_CHUNK = 64
S5_H = 16
S5_GROUPS = W_HALF // S5_H
S5_P = 64
S5_MAX_RE = -1e-4
N_EXPERTS = 16
EC_CAPACITY = 2
D_FF_EXPERT = 3 * D_MODEL // 2
N_EVEN = (DEPTH + 1) // 2
N_ODD = DEPTH // 2

kernel_name = "hybrid_lru_hyena_hgrn2_s5_ec_diffusion"


def _rms_norm(x, g):
    xf = x.astype(jnp.float32)
    y = xf * lax.rsqrt(jnp.mean(xf * xf, axis=-1, keepdims=True) + EPS)
    return (y * g.astype(jnp.float32)).astype(x.dtype)


def _modulate(h, shift, scale):
    return h * (1.0 + scale) + shift


def _dwconv(x, w, b):
    k = w.shape[0]
    left = (k - 1) // 2
    y = lax.conv_general_dilated(x, w[:, None, :].astype(x.dtype), window_strides=(1,),
                                 padding=[(left, k - 1 - left)],
                                 dimension_numbers=('NWC', 'WIO', 'NWC'),
                                 feature_group_count=x.shape[-1])
    return y + b.astype(x.dtype)


def _linear_scan(a, b, h0, reverse):
    if reverse:
        a = jnp.flip(a, 1)
        b = jnp.flip(b, 1)
    b = b.at[:, 0].add(a[:, 0] * h0)

    def comb(l, r):
        return l[0] * r[0], r[0] * l[1] + r[1]

    _, h = lax.associative_scan(comb, (a, b), axis=1)
    last = h[:, -1]
    if reverse:
        h = jnp.flip(h, 1)
    return h, last


def _to_col_major(t, rows):
    bsz, l, ch = t.shape
    return t.reshape(bsz, rows, GRID_W, ch).transpose(0, 2, 1, 3).reshape(bsz, l, ch)


def _to_row_major(t, rows):
    bsz, l, ch = t.shape
    return t.reshape(bsz, GRID_W, rows, ch).transpose(0, 2, 1, 3).reshape(bsz, l, ch)


def _rglru_coeffs(xh, wa, ba, wx, bx, lam):
    r = jax.nn.sigmoid(jnp.einsum('blhi,hij->blhj', xh, wa) + ba)
    i = jax.nn.sigmoid(jnp.einsum('blhi,hij->blhj', xh, wx) + bx)
    log_a = -LRU_C * r * jax.nn.softplus(-lam.astype(jnp.float32))
    a = jnp.exp(log_a)
    return a, jnp.sqrt(-jnp.expm1(2.0 * log_a)) * i * xh


def _rglru_mixer(xr_c, xr_l, xg_c, xg_l, conv_w, conv_b, wa, ba, wx, bx, lam):
    need_ctx = xg_c is not None

    def prep(xr):
        bsz, l, _ = xr.shape
        return _dwconv(xr, conv_w, conv_b).astype(jnp.float32).reshape(bsz, l, LRU_HEADS, LRU_HEAD_DIM)

    xc, xl = prep(xr_c), prep(xr_l)
    bsz, l = xl.shape[:2]
    y_c = 0.0
    y_l = 0.0
    for d in range(2):
        rev = d == 1
        a, bb = _rglru_coeffs(xc, wa[d], ba[d], wx[d], bx[d], lam[d])
        h_c, last = _linear_scan(a, bb, jnp.zeros((bsz, LRU_HEADS, LRU_HEAD_DIM), jnp.float32), rev)
        a, bb = _rglru_coeffs(xl, wa[d], ba[d], wx[d], bx[d], lam[d])
        h_l, _ = _linear_scan(a, bb, last, rev)
        y_l = y_l + h_l
        if need_ctx:
            y_c = y_c + h_c
    out_l = y_l.reshape(bsz, l, W_HALF).astype(xg_l.dtype) * jax.nn.gelu(xg_l)
    if not need_ctx:
        return None, out_l
    out_c = y_c.reshape(bsz, xc.shape[1], W_HALF).astype(xg_c.dtype) * jax.nn.gelu(xg_c)
    return out_c, out_l


def _hyena_filters(l, w1, b1, w2, b2, w3, freq):
    t = jnp.arange(l, dtype=jnp.float32)
    t_unit = t / max(l - 1, 1)
    bands = jnp.linspace(1e-4, HY_BANDS - 1, HY_BANDS, dtype=jnp.float32)
    ang = (2.0 * math.pi / l) * t[:, None] * bands[None, :]
    z = jnp.concatenate([t_unit[:, None], jnp.cos(ang), -jnp.sin(ang)], axis=-1)
    fr = freq.astype(jnp.float32)
    h = jnp.sin(fr * (z @ w1.astype(jnp.float32) + b1.astype(jnp.float32)))
    h = jnp.sin(fr * (h @ w2.astype(jnp.float32) + b2.astype(jnp.float32)))
    h = h @ w3.astype(jnp.float32)
    dist = jnp.abs(t - (l // 2)) / (l / 2.0)
    deltas = jnp.abs(jnp.linspace(HY_MIN_DECAY, HY_MAX_DECAY, HY_ORDER * W_HALF, dtype=jnp.float32))
    h = h * jnp.exp(-dist[:, None] * deltas[None, :])
    h = h * lax.rsqrt(jnp.sum(h * h, axis=0, keepdims=True) + EPS)
    return h.reshape(l, HY_ORDER, W_HALF).transpose(1, 0, 2)


def _fft_conv(u, filt, skip):
    l = u.shape[1]
    n = 2 * l
    uf = u.astype(jnp.float32)
    spec = jnp.fft.rfft(uf, n=n, axis=1) * jnp.fft.rfft(filt, n=n, axis=0)[None]
    y = jnp.fft.irfft(spec, n=n, axis=1)[:, l // 2: l // 2 + l]
    return (y + uf * skip.astype(jnp.float32)).astype(u.dtype)


def _hyena(u, conv_w, conv_b, w1, b1, w2, b2, w3, freq, skip):
    l = u.shape[1]
    u = _dwconv(u, conv_w, conv_b)
    v, x1, x2 = jnp.split(u, 3, axis=-1)
    filt = _hyena_filters(l, w1, b1, w2, b2, w3, freq)
    z = x1 * _fft_conv(v, filt[0], skip[0])
    z = x2 * _fft_conv(z, filt[1], skip[1])
    return z


def _even_mixer(h_ctx, h_lat, need_ctx, in_w, out_w, lru_conv_w, lru_conv_b, lru_wa, lru_ba, lru_wx, lru_bx,
                lru_lam, hy_conv_w, hy_conv_b, hy_w1, hy_b1, hy_w2, hy_b2, hy_w3, hy_freq, hy_skip):
    w = W_HALF
    u_l = h_lat @ in_w
    u_c = h_ctx @ (in_w if need_ctx else in_w[:, :w])
    ya_c, ya_l = _rglru_mixer(u_c[..., :w], u_l[..., :w], u_c[..., w:2 * w] if need_ctx else None,
                              u_l[..., w:2 * w], lru_conv_w, lru_conv_b, lru_wa, lru_ba, lru_wx, lru_bx, lru_lam)
    yb_l = _hyena(u_l[..., 2 * w:], hy_conv_w, hy_conv_b, hy_w1, hy_b1, hy_w2, hy_b2, hy_w3, hy_freq, hy_skip)
    y_l = jnp.concatenate([ya_l, yb_l], axis=-1) @ out_w
    if not need_ctx:
        return None, y_l
    yb_c = _hyena(u_c[..., 2 * w:], hy_conv_w, hy_conv_b, hy_w1, hy_b1, hy_w2, hy_b2, hy_w3, hy_freq, hy_skip)
    y_c = jnp.concatenate([ya_c, yb_c], axis=-1) @ out_w
    return y_c, y_l


def _hgrn_gates(fp, lb):
    lbh = lb.reshape(HG_HEADS, HG_DK)
    logf = jnp.logaddexp(jnp.log(lbh), jnp.log1p(-lbh) + jax.nn.log_sigmoid(fp))
    k = (1.0 - lbh) * jax.nn.sigmoid(-fp)
    return k, logf


def _hgrn_chunk_scan(k, v, logf, q, s0):
    bsz, l, h, _ = k.shape
    n = l // HG_CHUNK

    def chunks(t):
        return t.reshape(bsz, n, HG_CHUNK, h, t.shape[-1]).transpose(1, 0, 3, 2, 4)

    with_out = q is not None
    xs = (chunks(k), chunks(v), chunks(logf)) + ((chunks(q),) if with_out else ())
    lower = jnp.tril(jnp.ones((HG_CHUNK, HG_CHUNK), dtype=bool))

    def step(s, inp):
        kc, vc, gc = inp[0], inp[1], inp[2]
        bcum = jnp.cumsum(gc, axis=2)
        b_last = bcum[:, :, -1:, :]
        s_new = (jnp.exp(b_last[:, :, 0, :])[..., None] * s
                 + jnp.einsum('bhcd,bhce->bhde', kc * jnp.exp(b_last - bcum), vc))
        if not with_out:
            return s_new, None
        qc = inp[3]
        o = jnp.einsum('bhcd,bhde->bhce', qc * jnp.exp(bcum), s)
        diff = jnp.where(lower[:, :, None], bcum[:, :, :, None, :] - bcum[:, :, None, :, :], -jnp.inf)
        att = jnp.einsum('bhtd,bhsd,bhtsd->bhts', qc, kc, jnp.exp(diff))
        o = o + jnp.einsum('bhts,bhse->bhte', att, vc)
        return s_new, o

    s_fin, o = lax.scan(step, s0, xs)
    if with_out:
        o = o.transpose(1, 0, 3, 2, 4).reshape(bsz, l, h, -1)
    return o, s_fin


def _hgrn_bidir(kf, lf, kb, lbk, v, q, s0f, s0b):
    def fl(t):
        return None if t is None else jnp.flip(t, 1)

    o_f, s_f = _hgrn_chunk_scan(kf, v, lf, q, s0f)
    o_b, s_b = _hgrn_chunk_scan(fl(kb), fl(v), fl(lbk), fl(q), s0b)
    o = None if q is None else o_f + jnp.flip(o_b, 1)
    return o, s_f, s_b


def _hgrn_out(o, g, norm_g):
    bsz, l = o.shape[:2]
    o = o * lax.rsqrt(jnp.mean(o * o, axis=-1, keepdims=True) + EPS) * norm_g.astype(jnp.float32)
    return o.reshape(bsz, l, W_HALF).astype(g.dtype) * jax.nn.silu(g)


def _s5_discrete(lam_re, lam_im, log_dt):
    lam = lax.complex(jnp.minimum(lam_re.astype(jnp.float32), S5_MAX_RE), lam_im.astype(jnp.float32))
    dt = jnp.exp(log_dt.astype(jnp.float32))[:, None]
    lam_bar = jnp.exp(lam * dt)
    return lam_bar, (lam_bar - 1.0) / lam


def _s5_bidir(u, lam_re, lam_im, log_dt, b_c, c_c, h0f, h0b, with_out):
    bu = jnp.einsum('blgh,gph->blgp', u.astype(jnp.complex64), b_c)
    y = 0.0
    h0s = (h0f, h0b)
    lasts = []
    for d in range(2):
        lam_bar, coef = _s5_discrete(lam_re[d], lam_im[d], log_dt[d])
        h, last = _linear_scan(jnp.broadcast_to(lam_bar, bu.shape), coef * bu, h0s[d], d == 1)
        lasts.append(last)
        if with_out:
            y = y + jnp.real(jnp.einsum('blgp,ghp->blgh', h, c_c))
    return (y if with_out else None), lasts[0], lasts[1]


def _s5_out(y, u, d_skip, glu_w, glu_b, dtype):
    bsz, l = u.shape[:2]
    y = jax.nn.gelu((y + u * d_skip.astype(jnp.float32)).reshape(bsz, l, W_HALF))
    return (y * jax.nn.sigmoid(y @ glu_w.astype(jnp.float32) + glu_b.astype(jnp.float32))).astype(dtype)


def _odd_mixer(h_ctx, h_lat, rows, need_ctx, in_w, out_w, lb, hg_norm_g, lam_re, lam_im, log_dt,
               b_re, b_im, c_re, c_im, d_skip, glu_w, glu_b):
    w = W_HALF
    u_l = _to_col_major(h_lat, rows) @ in_w
    u_c = h_ctx @ (in_w if need_ctx else in_w[:, :4 * w])

    def hg_parts(u, with_q):
        bsz, l, _ = u.shape

        def hd(t):
            return t.astype(jnp.float32).reshape(bsz, l, HG_HEADS, -1)

        kf, lf = _hgrn_gates(hd(u[..., :w]), lb)
        kb, lbk = _hgrn_gates(hd(u[..., w:2 * w]), lb)
        v = hd(u[..., 2 * w:3 * w])
        q = jax.nn.silu(hd(u[..., 4 * w:5 * w])) if with_q else None
        return kf, lf, kb, lbk, v, q

    bc = h_ctx.shape[0]
    s0 = jnp.zeros((bc, HG_HEADS, HG_DK, HG_DV), jnp.float32)
    o_c, s_f, s_b = _hgrn_bidir(*hg_parts(u_c, need_ctx), s0, s0)
    o_l, _, _ = _hgrn_bidir(*hg_parts(u_l, True), s_f, s_b)
    hg_l = _hgrn_out(o_l, u_l[..., 5 * w:], hg_norm_g)

    b_c = lax.complex(b_re.astype(jnp.float32), b_im.astype(jnp.float32))
    c_c = lax.complex(c_re.astype(jnp.float32), c_im.astype(jnp.float32))

    def s5_in(u):
        bsz, l, _ = u.shape
        return u[..., 3 * w:4 * w].astype(jnp.float32).reshape(bsz, l, S5_GROUPS, S5_H)

    sc, sl = s5_in(u_c), s5_in(u_l)
    z0 = jnp.zeros((bc, S5_GROUPS, S5_P), jnp.complex64)
    yc, h_f, h_b = _s5_bidir(sc, lam_re, lam_im, log_dt, b_c, c_c, z0, z0, need_ctx)
    yl, _, _ = _s5_bidir(sl, lam_re, lam_im, log_dt, b_c, c_c, h_f, h_b, True)
    s5_l = _s5_out(yl, sl, d_skip, glu_w, glu_b, h_lat.dtype)
    y_l = _to_row_major(jnp.concatenate([hg_l, s5_l], axis=-1) @ out_w, rows)
    if not need_ctx:
        return None, y_l
    hg_c = _hgrn_out(o_c, u_c[..., 5 * w:], hg_norm_g)
    s5_c = _s5_out(yc, sc, d_skip, glu_w, glu_b, h_ctx.dtype)
    y_c = jnp.concatenate([hg_c, s5_c], axis=-1) @ out_w
    return y_c, y_l


def _expert_choice_ffn(h, w_router, w_gate, w_up, w_down):
    bsz, n, d = h.shape
    cap = max(1, (EC_CAPACITY * n) // N_EXPERTS)
    aff = jax.nn.softmax(jnp.einsum('bnd,de->bne', h, w_router).astype(jnp.float32), axis=-1)
    gate, idx = lax.top_k(jnp.swapaxes(aff, 1, 2), cap)
    xin = jax.vmap(lambda hb, ib: hb[ib])(h, idx)
    a = jnp.einsum('becd,edf->becf', xin, w_gate)
    u = jnp.einsum('becd,edf->becf', xin, w_up)
    y = jnp.einsum('becf,efd->becd', jax.nn.silu(a) * u, w_down) * gate[..., None].astype(h.dtype)
    return jax.vmap(lambda yb, ib: jnp.zeros((n, d), yb.dtype).at[ib.reshape(-1)].add(yb.reshape(-1, d)))(y, idx)


def setup_inputs(seed: int = 0) -> dict:
    key = jax.random.key(seed)
    ks = iter(jax.random.split(key, 64))
    f32 = jnp.float32
    d, w = D_MODEL, W_HALF

    def nrm(shape, scale):
        return jax.random.normal(next(ks), shape, f32) * scale

    a8 = jax.random.uniform(next(ks), (N_EVEN, 2, LRU_HEADS, LRU_HEAD_DIM), f32, 0.9, 0.999)
    a_base = a8 ** (1.0 / LRU_C)
    lru_lam = jnp.log(a_base) - jnp.log1p(-a_base)
    s5_im = jnp.pi * jnp.arange(S5_P, dtype=f32)
    return {
        "x": nrm((BATCH, SEQ, d), 1.0),
        "c": nrm((BATCH, d), 1.0),
        "ctx": nrm((BATCH, CTX_LEN, d), 1.0),
        "c_ctx": nrm((d,), 1.0),
        "mod_w": nrm((DEPTH, d, 6 * d), 0.3 * d ** -0.5),
        "mod_b": nrm((DEPTH, 6 * d), 0.02),
        "norm1_g": 1.0 + nrm((DEPTH, d), 0.02),
        "norm2_g": 1.0 + nrm((DEPTH, d), 0.02),
        "out_w": nrm((DEPTH, MIX_W, d), MIX_W ** -0.5),
        "even_in_w": nrm((N_EVEN, d, 5 * w), d ** -0.5),
        "lru_conv_w": nrm((N_EVEN, LRU_CONV, w), LRU_CONV ** -0.5),
        "lru_conv_b": nrm((N_EVEN, w), 0.01),
        "lru_wa": nrm((N_EVEN, 2, LRU_HEADS, LRU_HEAD_DIM, LRU_HEAD_DIM), LRU_HEAD_DIM ** -0.5),
        "lru_ba": nrm((N_EVEN, 2, LRU_HEADS, LRU_HEAD_DIM), 0.01),
        "lru_wx": nrm((N_EVEN, 2, LRU_HEADS, LRU_HEAD_DIM, LRU_HEAD_DIM), LRU_HEAD_DIM ** -0.5),
        "lru_bx": nrm((N_EVEN, 2, LRU_HEADS, LRU_HEAD_DIM), 0.01),
        "lru_lam": lru_lam,
        "hy_conv_w": nrm((N_EVEN, HY_SHORT, 3 * w), HY_SHORT ** -0.5),
        "hy_conv_b": nrm((N_EVEN, 3 * w), 0.01),
        "hy_w1": nrm((N_EVEN, HY_EMB, HY_FFN), HY_EMB ** -0.5),
        "hy_b1": nrm((N_EVEN, HY_FFN), 0.01),
        "hy_w2": nrm((N_EVEN, HY_FFN, HY_FFN), HY_FFN ** -0.5),
        "hy_b2": nrm((N_EVEN, HY_FFN), 0.01),
        "hy_w3": nrm((N_EVEN, HY_FFN, HY_ORDER * w), HY_FFN ** -0.5),
        "hy_freq": 1.0 + nrm((N_EVEN, HY_FFN), 0.02),
        "hy_skip": nrm((N_EVEN, HY_ORDER, w), 0.5),
        "odd_in_w": nrm((N_ODD, d, 6 * w), d ** -0.5),
        "hg_lb": nrm((DEPTH, w), 0.1),
        "hg_norm_g": 1.0 + nrm((N_ODD, HG_DV), 0.02),
        "s5_lam_re": -0.5 + nrm((N_ODD, 2, S5_GROUPS, S5_P), 0.01),
        "s5_lam_im": s5_im + nrm((N_ODD, 2, S5_GROUPS, S5_P), 0.01),
        "s5_log_dt": jax.random.uniform(next(ks), (N_ODD, 2, S5_GROUPS), f32, math.log(1e-3), math.log(1e-1)),
        "s5_b_re": nrm((N_ODD, S5_GROUPS, S5_P, S5_H), (2 * S5_H) ** -0.5),
        "s5_b_im": nrm((N_ODD, S5_GROUPS, S5_P, S5_H), (2 * S5_H) ** -0.5),
        "s5_c_re": nrm((N_ODD, S5_GROUPS, S5_H, S5_P), S5_P ** -0.5),
        "s5_c_im": nrm((N_ODD, S5_GROUPS, S5_H, S5_P), S5_P ** -0.5),
        "s5_d": nrm((N_ODD, S5_GROUPS, S5_H), 1.0),
        "s5_glu_w": nrm((N_ODD, w, w), w ** -0.5),
        "s5_glu_b": nrm((N_ODD, w), 0.01),
        "router_w": nrm((DEPTH, d, N_EXPERTS), d ** -0.5),
        "ex_w_gate": nrm((DEPTH, N_EXPERTS, d, D_FF_EXPERT), d ** -0.5),
        "ex_w_up": nrm((DEPTH, N_EXPERTS, d, D_FF_EXPERT), d ** -0.5),
        "ex_w_down": nrm((DEPTH, N_EXPERTS, D_FF_EXPERT, d), D_FF_EXPERT ** -0.5),
        "final_g": 1.0 + nrm((d,), 0.02),
    }


def reference(x, c, ctx, c_ctx, mod_w, mod_b, norm1_g, norm2_g, out_w, even_in_w, lru_conv_w, lru_conv_b,
              lru_wa, lru_ba, lru_wx, lru_bx, lru_lam, hy_conv_w, hy_conv_b, hy_w1, hy_b1, hy_w2, hy_b2, hy_w3,
              hy_freq, hy_skip, odd_in_w, hg_lb, hg_norm_g, s5_lam_re, s5_lam_im, s5_log_dt, s5_b_re, s5_b_im,
              s5_c_re, s5_c_im, s5_d, s5_glu_w, s5_glu_b, router_w, ex_w_gate, ex_w_up, ex_w_down, final_g):
    rows = x.shape[1] // GRID_W
    p = jax.nn.softmax(hg_lb.astype(jnp.float32), axis=0)
    lb_all = jnp.clip(jnp.cumsum(p, axis=0) - p[0], 0.0, 1.0 - 1e-4)
    sc_lat = jax.nn.silu(c)
    sc_ctx = jax.nn.silu(c_ctx)[None]
    for i in range(DEPTH):
        need_ctx = i < DEPTH - 1
        j = i // 2
        ml = [t[:, None, :] for t in jnp.split(sc_lat @ mod_w[i] + mod_b[i], 6, axis=-1)]
        mc = [t[:, None, :] for t in jnp.split(sc_ctx @ mod_w[i] + mod_b[i], 6, axis=-1)]
        h_lat = _modulate(_rms_norm(x, norm1_g[i]), ml[0], ml[1])
        h_ctx = _modulate(_rms_norm(ctx, norm1_g[i]), mc[0], mc[1])
        if i % 2 == 0:
            y_c, y_l = _even_mixer(h_ctx, h_lat, need_ctx, even_in_w[j], out_w[i], lru_conv_w[j], lru_conv_b[j],
                                   lru_wa[j], lru_ba[j], lru_wx[j], lru_bx[j], lru_lam[j], hy_conv_w[j],
                                   hy_conv_b[j], hy_w1[j], hy_b1[j], hy_w2[j], hy_b2[j], hy_w3[j], hy_freq[j],
                                   hy_skip[j])
        else:
            y_c, y_l = _odd_mixer(h_ctx, h_lat, rows, need_ctx, odd_in_w[j], out_w[i], lb_all[i], hg_norm_g[j],
                                  s5_lam_re[j], s5_lam_im[j], s5_log_dt[j], s5_b_re[j], s5_b_im[j], s5_c_re[j],
                                  s5_c_im[j], s5_d[j], s5_glu_w[j], s5_glu_b[j])
        x = x + ml[2] * y_l
        x = x + ml[5] * _expert_choice_ffn(_modulate(_rms_norm(x, norm2_g[i]), ml[3], ml[4]),
                                           router_w[i], ex_w_gate[i], ex_w_up[i], ex_w_down[i])
        if need_ctx:
            ctx = ctx + mc[2] * y_c
            ctx = ctx + mc[5] * _expert_choice_ffn(_modulate(_rms_norm(ctx, norm2_g[i]), mc[3], mc[4]),
                                                   router_w[i], ex_w_gate[i], ex_w_up[i], ex_w_down[i])
    return _rms_norm(x, final_g)
```

```python
import math
from functools import partial

import jax
import jax.numpy as jnp
from jax import lax
from jax.experimental import pallas as pl
from jax.experimental.pallas import tpu as pltpu

D_MODEL = 1024
DEPTH = 2
GRID_W = 64
W_HALF = D_MODEL // 2
EPS = 1e-6
LRU_HEADS = 8
LRU_HEAD_DIM = W_HALF // LRU_HEADS
LRU_C = 8.0
HY_ORDER = 2
HY_BANDS = 16
HY_MAX_DECAY = math.log(1e-2) / 0.3
HY_MIN_DECAY = math.log(1e-2) / 1.5
HG_HEADS = 4
HG_DK = W_HALF // HG_HEADS
HG_CHUNK = 64
S5_H = 16
S5_GROUPS = W_HALF // S5_H
S5_P = 64
S5_MAX_RE = -1e-4
N_EXPERTS = 16
EC_CAPACITY = 2

_VMEM_LIMIT = 48 * 1024 * 1024


def _mm_body(a_ref, b_ref, o_ref):
    o_ref[...] = jnp.dot(a_ref[...].astype(jnp.bfloat16), b_ref[...],
                         preferred_element_type=jnp.float32)


def _pick(n, pref):
    for t in pref:
        if n % t == 0:
            return t
    return n


def _mm(a, b):
    m, k = a.shape
    n = b.shape[1]
    tm = _pick(m, (512, 256, 128, 64, 32, 16, 8))
    tn = _pick(n, (512, 256, 128))
    return pl.pallas_call(
        _mm_body,
        grid=(m // tm, n // tn),
        in_specs=[pl.BlockSpec((tm, k), lambda i, j: (i, 0)),
                  pl.BlockSpec((k, tn), lambda i, j: (0, j))],
        out_specs=pl.BlockSpec((tm, tn), lambda i, j: (i, j)),
        out_shape=jax.ShapeDtypeStruct((m, n), jnp.float32),
        compiler_params=pltpu.CompilerParams(
            dimension_semantics=("parallel", "parallel"), vmem_limit_bytes=_VMEM_LIMIT),
        name="mm",
    )(a, b.astype(jnp.bfloat16))


def _bmm_body(a_ref, b_ref, o_ref):
    o_ref[0] = jnp.dot(a_ref[0].astype(jnp.bfloat16), b_ref[0],
                       preferred_element_type=jnp.float32)


def _bmm(a, b):
    e, m, k = a.shape
    n = b.shape[2]
    tm = _pick(m, (512, 256, 128, 64, 32, 16, 8))
    tn = _pick(n, (512, 256, 128))
    return pl.pallas_call(
        _bmm_body,
        grid=(e, m // tm, n // tn),
        in_specs=[pl.BlockSpec((1, tm, k), lambda g, i, j: (g, i, 0)),
                  pl.BlockSpec((1, k, tn), lambda g, i, j: (g, 0, j))],
        out_specs=pl.BlockSpec((1, tm, tn), lambda g, i, j: (g, i, j)),
        out_shape=jax.ShapeDtypeStruct((e, m, n), jnp.float32),
        compiler_params=pltpu.CompilerParams(
            dimension_semantics=("parallel", "parallel", "parallel"), vmem_limit_bytes=_VMEM_LIMIT),
        name="bmm",
    )(a, b.astype(jnp.bfloat16))


def _mm3(a, b):
    bsz, l, k = a.shape
    return _mm(a.reshape(bsz * l, k), b).reshape(bsz, l, b.shape[1])


def _rms_norm(x, g):
    y = x * lax.rsqrt(jnp.mean(x * x, axis=-1, keepdims=True) + EPS)
    return y * g


def _modulate(h, shift, scale):
    return h * (1.0 + scale) + shift


def _dwconv(x, w, b):
    k = w.shape[0]
    left = (k - 1) // 2
    y = lax.conv_general_dilated(x, w[:, None, :], window_strides=(1,),
                                 padding=[(left, k - 1 - left)],
                                 dimension_numbers=('NWC', 'WIO', 'NWC'),
                                 feature_group_count=x.shape[-1])
    return y + b


def _linear_scan(a, b, h0, reverse):
    if reverse:
        a = jnp.flip(a, 1)
        b = jnp.flip(b, 1)
    b = b.at[:, 0].add(a[:, 0] * h0)

    def comb(l, r):
        return l[0] * r[0], r[0] * l[1] + r[1]

    _, h = lax.associative_scan(comb, (a, b), axis=1)
    last = h[:, -1]
    if reverse:
        h = jnp.flip(h, 1)
    return h, last


def _to_col_major(t, rows):
    bsz, l, ch = t.shape
    return t.reshape(bsz, rows, GRID_W, ch).transpose(0, 2, 1, 3).reshape(bsz, l, ch)


def _to_row_major(t, rows):
    bsz, l, ch = t.shape
    return t.reshape(bsz, GRID_W, rows, ch).transpose(0, 2, 1, 3).reshape(bsz, l, ch)


def _rglru_coeffs(xh, wa, ba, wx, bx, lam):
    r = jax.nn.sigmoid(jnp.einsum('blhi,hij->blhj', xh, wa) + ba)
    i = jax.nn.sigmoid(jnp.einsum('blhi,hij->blhj', xh, wx) + bx)
    log_a = -LRU_C * r * jax.nn.softplus(-lam)
    a = jnp.exp(log_a)
    return a, jnp.sqrt(-jnp.expm1(2.0 * log_a)) * i * xh


def _rglru_mixer(xr_c, xr_l, xg_c, xg_l, conv_w, conv_b, wa, ba, wx, bx, lam):
    need_ctx = xg_c is not None

    def prep(xr):
        bsz, l, _ = xr.shape
        return _dwconv(xr, conv_w, conv_b).reshape(bsz, l, LRU_HEADS, LRU_HEAD_DIM)

    xc, xl = prep(xr_c), prep(xr_l)
    bsz, l = xl.shape[:2]
    y_c = 0.0
    y_l = 0.0
    for d in range(2):
        rev = d == 1
        a, bb = _rglru_coeffs(xc, wa[d], ba[d], wx[d], bx[d], lam[d])
        h_c, last = _linear_scan(a, bb, jnp.zeros((bsz, LRU_HEADS, LRU_HEAD_DIM), jnp.float32), rev)
        a, bb = _rglru_coeffs(xl, wa[d], ba[d], wx[d], bx[d], lam[d])
        h_l, _ = _linear_scan(a, bb, last, rev)
        y_l = y_l + h_l
        if need_ctx:
            y_c = y_c + h_c
    out_l = y_l.reshape(bsz, l, W_HALF) * jax.nn.gelu(xg_l)
    if not need_ctx:
        return None, out_l
    out_c = y_c.reshape(bsz, xc.shape[1], W_HALF) * jax.nn.gelu(xg_c)
    return out_c, out_l


def _hyena_filters(l, w1, b1, w2, b2, w3, freq):
    t = jnp.arange(l, dtype=jnp.float32)
    t_unit = t / max(l - 1, 1)
    bands = jnp.linspace(1e-4, HY_BANDS - 1, HY_BANDS, dtype=jnp.float32)
    ang = (2.0 * math.pi / l) * t[:, None] * bands[None, :]
    z = jnp.concatenate([t_unit[:, None], jnp.cos(ang), -jnp.sin(ang)], axis=-1)
    h = jnp.sin(freq * (z @ w1 + b1))
    h = jnp.sin(freq * (h @ w2 + b2))
    h = h @ w3
    dist = jnp.abs(t - (l // 2)) / (l / 2.0)
    deltas = jnp.abs(jnp.linspace(HY_MIN_DECAY, HY_MAX_DECAY, HY_ORDER * W_HALF, dtype=jnp.float32))
    h = h * jnp.exp(-dist[:, None] * deltas[None, :])
    h = h * lax.rsqrt(jnp.sum(h * h, axis=0, keepdims=True) + EPS)
    return h.reshape(l, HY_ORDER, W_HALF).transpose(1, 0, 2)


def _fft_conv(u, filt, skip):
    l = u.shape[1]
    n = 2 * l
    spec = jnp.fft.rfft(u, n=n, axis=1) * jnp.fft.rfft(filt, n=n, axis=0)[None]
    y = jnp.fft.irfft(spec, n=n, axis=1)[:, l // 2: l // 2 + l]
    return y + u * skip


def _hyena(u, conv_w, conv_b, w1, b1, w2, b2, w3, freq, skip):
    l = u.shape[1]
    u = _dwconv(u, conv_w, conv_b)
    v, x1, x2 = jnp.split(u, 3, axis=-1)
    filt = _hyena_filters(l, w1, b1, w2, b2, w3, freq)
    z = x1 * _fft_conv(v, filt[0], skip[0])
    z = x2 * _fft_conv(z, filt[1], skip[1])
    return z


def _even_mixer(h_ctx, h_lat, need_ctx, in_w, out_w, lru_conv_w, lru_conv_b, lru_wa, lru_ba, lru_wx, lru_bx,
                lru_lam, hy_conv_w, hy_conv_b, hy_w1, hy_b1, hy_w2, hy_b2, hy_w3, hy_freq, hy_skip):
    w = W_HALF
    u_l = _mm3(h_lat, in_w)
    u_c = _mm3(h_ctx, in_w if need_ctx else in_w[:, :w])
    ya_c, ya_l = _rglru_mixer(u_c[..., :w], u_l[..., :w], u_c[..., w:2 * w] if need_ctx else None,
                              u_l[..., w:2 * w], lru_conv_w, lru_conv_b, lru_wa, lru_ba, lru_wx, lru_bx, lru_lam)
    yb_l = _hyena(u_l[..., 2 * w:], hy_conv_w, hy_conv_b, hy_w1, hy_b1, hy_w2, hy_b2, hy_w3, hy_freq, hy_skip)
    y_l = _mm3(jnp.concatenate([ya_l, yb_l], axis=-1), out_w)
    if not need_ctx:
        return None, y_l
    yb_c = _hyena(u_c[..., 2 * w:], hy_conv_w, hy_conv_b, hy_w1, hy_b1, hy_w2, hy_b2, hy_w3, hy_freq, hy_skip)
    y_c = _mm3(jnp.concatenate([ya_c, yb_c], axis=-1), out_w)
    return y_c, y_l


def _hgrn_gates(fp, lb):
    lbh = lb.reshape(HG_HEADS, HG_DK)
    logf = jnp.logaddexp(jnp.log(lbh), jnp.log1p(-lbh) + jax.nn.log_sigmoid(fp))
    k = (1.0 - lbh) * jax.nn.sigmoid(-fp)
    return k, logf


def _hgrn_chunk_scan(k, v, logf, q, s0):
    bsz, l, h, _ = k.shape
    n = l // HG_CHUNK

    def chunks(t):
        return t.reshape(bsz, n, HG_CHUNK, h, t.shape[-1]).transpose(1, 0, 3, 2, 4)

    with_out = q is not None
    xs = (chunks(k), chunks(v), chunks(logf)) + ((chunks(q),) if with_out else ())
    lower = jnp.tril(jnp.ones((HG_CHUNK, HG_CHUNK), dtype=bool))

    def step(s, inp):
        kc, vc, gc = inp[0], inp[1], inp[2]
        bcum = jnp.cumsum(gc, axis=2)
        b_last = bcum[:, :, -1:, :]
        s_new = (jnp.exp(b_last[:, :, 0, :])[..., None] * s
                 + jnp.einsum('bhcd,bhce->bhde', kc * jnp.exp(b_last - bcum), vc))
        if not with_out:
            return s_new, None
        qc = inp[3]
        o = jnp.einsum('bhcd,bhde->bhce', qc * jnp.exp(bcum), s)
        diff = jnp.where(lower[:, :, None], bcum[:, :, :, None, :] - bcum[:, :, None, :, :], -jnp.inf)
        att = jnp.einsum('bhtd,bhsd,bhtsd->bhts', qc, kc, jnp.exp(diff))
        o = o + jnp.einsum('bhts,bhse->bhte', att, vc)
        return s_new, o

    s_fin, o = lax.scan(step, s0, xs)
    if with_out:
        o = o.transpose(1, 0, 3, 2, 4).reshape(bsz, l, h, -1)
    return o, s_fin


def _hgrn_bidir(kf, lf, kb, lbk, v, q, s0f, s0b):
    def fl(t):
        return None if t is None else jnp.flip(t, 1)

    o_f, s_f = _hgrn_chunk_scan(kf, v, lf, q, s0f)
    o_b, s_b = _hgrn_chunk_scan(fl(kb), fl(v), fl(lbk), fl(q), s0b)
    o = None if q is None else o_f + jnp.flip(o_b, 1)
    return o, s_f, s_b


def _hgrn_out(o, g, norm_g):
    bsz, l = o.shape[:2]
    o = o * lax.rsqrt(jnp.mean(o * o, axis=-1, keepdims=True) + EPS) * norm_g
    return o.reshape(bsz, l, W_HALF) * jax.nn.silu(g)


def _s5_discrete(lam_re, lam_im, log_dt):
    lam = lax.complex(jnp.minimum(lam_re, S5_MAX_RE), lam_im)
    dt = jnp.exp(log_dt)[:, None]
    lam_bar = jnp.exp(lam * dt)
    return lam_bar, (lam_bar - 1.0) / lam


def _s5_bidir(u, lam_re, lam_im, log_dt, b_c, c_c, h0f, h0b, with_out):
    bu = jnp.einsum('blgh,gph->blgp', u.astype(jnp.complex64), b_c)
    y = 0.0
    h0s = (h0f, h0b)
    lasts = []
    for d in range(2):
        lam_bar, coef = _s5_discrete(lam_re[d], lam_im[d], log_dt[d])
        h, last = _linear_scan(jnp.broadcast_to(lam_bar, bu.shape), coef * bu, h0s[d], d == 1)
        lasts.append(last)
        if with_out:
            y = y + jnp.real(jnp.einsum('blgp,ghp->blgh', h, c_c))
    return (y if with_out else None), lasts[0], lasts[1]


def _s5_out(y, u, d_skip, glu_w, glu_b):
    bsz, l = u.shape[:2]
    y = jax.nn.gelu((y + u * d_skip).reshape(bsz, l, W_HALF))
    return y * jax.nn.sigmoid(_mm3(y, glu_w) + glu_b)


def _odd_mixer(h_ctx, h_lat, rows, need_ctx, in_w, out_w, lb, hg_norm_g, lam_re, lam_im, log_dt,
               b_re, b_im, c_re, c_im, d_skip, glu_w, glu_b):
    w = W_HALF
    u_l = _mm3(_to_col_major(h_lat, rows), in_w)
    u_c = _mm3(h_ctx, in_w if need_ctx else in_w[:, :4 * w])

    def hg_parts(u, with_q):
        bsz, l, _ = u.shape

        def hd(t):
            return t.reshape(bsz, l, HG_HEADS, -1)

        kf, lf = _hgrn_gates(hd(u[..., :w]), lb)
        kb, lbk = _hgrn_gates(hd(u[..., w:2 * w]), lb)
        v = hd(u[..., 2 * w:3 * w])
        q = jax.nn.silu(hd(u[..., 4 * w:5 * w])) if with_q else None
        return kf, lf, kb, lbk, v, q

    bc = h_ctx.shape[0]
    s0 = jnp.zeros((bc, HG_HEADS, HG_DK, HG_DK), jnp.float32)
    o_c, s_f, s_b = _hgrn_bidir(*hg_parts(u_c, need_ctx), s0, s0)
    o_l, _, _ = _hgrn_bidir(*hg_parts(u_l, True), s_f, s_b)
    hg_l = _hgrn_out(o_l, u_l[..., 5 * w:], hg_norm_g)

    b_c = lax.complex(b_re, b_im)
    c_c = lax.complex(c_re, c_im)

    def s5_in(u):
        bsz, l, _ = u.shape
        return u[..., 3 * w:4 * w].reshape(bsz, l, S5_GROUPS, S5_H)

    sc, sl = s5_in(u_c), s5_in(u_l)
    z0 = jnp.zeros((bc, S5_GROUPS, S5_P), jnp.complex64)
    yc, h_f, h_b = _s5_bidir(sc, lam_re, lam_im, log_dt, b_c, c_c, z0, z0, need_ctx)
    yl, _, _ = _s5_bidir(sl, lam_re, lam_im, log_dt, b_c, c_c, h_f, h_b, True)
    s5_l = _s5_out(yl, sl, d_skip, glu_w, glu_b)
    y_l = _to_row_major(_mm3(jnp.concatenate([hg_l, s5_l], axis=-1), out_w), rows)
    if not need_ctx:
        return None, y_l
    hg_c = _hgrn_out(o_c, u_c[..., 5 * w:], hg_norm_g)
    s5_c = _s5_out(yc, sc, d_skip, glu_w, glu_b)
    y_c = _mm3(jnp.concatenate([hg_c, s5_c], axis=-1), out_w)
    return y_c, y_l


def _expert_choice_ffn(h, w_router, w_gate, w_up, w_down):
    bsz, n, d = h.shape
    cap = max(1, (EC_CAPACITY * n) // N_EXPERTS)
    aff = jax.nn.softmax(jnp.einsum('bnd,de->bne', h, w_router), axis=-1)
    gate, idx = lax.top_k(jnp.swapaxes(aff, 1, 2), cap)
    xin = jax.vmap(lambda hb, ib: hb[ib])(h, idx)
    xe = xin.transpose(1, 0, 2, 3).reshape(N_EXPERTS, bsz * cap, d)
    a = _bmm(xe, w_gate)
    u = _bmm(xe, w_up)
    y = _bmm(jax.nn.silu(a) * u, w_down)
    y = y.reshape(N_EXPERTS, bsz, cap, d).transpose(1, 0, 2, 3) * gate[..., None]
    return jax.vmap(lambda yb, ib: jnp.zeros((n, d), yb.dtype).at[ib.reshape(-1)].add(yb.reshape(-1, d)))(y, idx)


def kernel(x, c, ctx, c_ctx, mod_w, mod_b, norm1_g, norm2_g, out_w, even_in_w, lru_conv_w, lru_conv_b, lru_wa, lru_ba, lru_wx, lru_bx, lru_lam, hy_conv_w, hy_conv_b, hy_w1, hy_b1, hy_w2, hy_b2, hy_w3, hy_freq, hy_skip, odd_in_w, hg_lb, hg_norm_g, s5_lam_re, s5_lam_im, s5_log_dt, s5_b_re, s5_b_im, s5_c_re, s5_c_im, s5_d, s5_glu_w, s5_glu_b, router_w, ex_w_gate, ex_w_up, ex_w_down, final_g):
    rows = x.shape[1] // GRID_W
    p = jax.nn.softmax(hg_lb, axis=0)
    lb_all = jnp.clip(jnp.cumsum(p, axis=0) - p[0], 0.0, 1.0 - 1e-4)
    sc_lat = jax.nn.silu(c)
    sc_ctx = jax.nn.silu(c_ctx)[None]
    for i in range(DEPTH):
        need_ctx = i < DEPTH - 1
        j = i // 2
        ml = [t[:, None, :] for t in jnp.split(sc_lat @ mod_w[i] + mod_b[i], 6, axis=-1)]
        mc = [t[:, None, :] for t in jnp.split(sc_ctx @ mod_w[i] + mod_b[i], 6, axis=-1)]
        h_lat = _modulate(_rms_norm(x, norm1_g[i]), ml[0], ml[1])
        h_ctx = _modulate(_rms_norm(ctx, norm1_g[i]), mc[0], mc[1])
        if i % 2 == 0:
            y_c, y_l = _even_mixer(h_ctx, h_lat, need_ctx, even_in_w[j], out_w[i], lru_conv_w[j], lru_conv_b[j],
                                   lru_wa[j], lru_ba[j], lru_wx[j], lru_bx[j], lru_lam[j], hy_conv_w[j],
                                   hy_conv_b[j], hy_w1[j], hy_b1[j], hy_w2[j], hy_b2[j], hy_w3[j], hy_freq[j],
                                   hy_skip[j])
        else:
            y_c, y_l = _odd_mixer(h_ctx, h_lat, rows, need_ctx, odd_in_w[j], out_w[i], lb_all[i], hg_norm_g[j],
                                  s5_lam_re[j], s5_lam_im[j], s5_log_dt[j], s5_b_re[j], s5_b_im[j], s5_c_re[j],
                                  s5_c_im[j], s5_d[j], s5_glu_w[j], s5_glu_b[j])
        x = x + ml[2] * y_l
        x = x + ml[5] * _expert_choice_ffn(_modulate(_rms_norm(x, norm2_g[i]), ml[3], ml[4]),
                                           router_w[i], ex_w_gate[i], ex_w_up[i], ex_w_down[i])
        if need_ctx:
            ctx = ctx + mc[2] * y_c
            ctx = ctx + mc[5] * _expert_choice_ffn(_modulate(_rms_norm(ctx, norm2_g[i]), mc[3], mc[4]),
                                                   router_w[i], ex_w_gate[i], ex_w_up[i], ex_w_down[i])
    return _rms_norm(x, final_g)
```

```python
import math
from functools import partial

import jax
import jax.numpy as jnp
from jax import lax
from jax.experimental import pallas as pl
from jax.experimental.pallas import tpu as pltpu

D_MODEL = 1024
DEPTH = 2
GRID_W = 64
W_HALF = D_MODEL // 2
EPS = 1e-6
LRU_HEADS = 8
LRU_HEAD_DIM = W_HALF // LRU_HEADS
LRU_C = 8.0
HY_ORDER = 2
HY_BANDS = 16
HY_MAX_DECAY = math.log(1e-2) / 0.3
HY_MIN_DECAY = math.log(1e-2) / 1.5
HG_HEADS = 4
HG_DK = W_HALF // HG_HEADS
HG_C = 16
HG_TB = 256
S5_H = 16
S5_P = 64
S5_MAX_RE = -1e-4
S5_T = 32
N_EXPERTS = 16
EC_CAPACITY = 2

_SUBLANES = 8
_VMEM_LIMIT = 48 * 1024 * 1024
_HI = lax.Precision.HIGHEST
_NEG = -1e30


def _mm_body(a_ref, b_ref, o_ref):
    o_ref[...] = jnp.dot(a_ref[...].astype(jnp.bfloat16), b_ref[...],
                         preferred_element_type=jnp.float32)


def _pick(n, pref):
    for t in pref:
        if n % t == 0:
            return t
    return n


def _mm(a, b):
    m, k = a.shape
    n = b.shape[1]
    tm = _pick(m, (512, 256, 128, 64, 32, 16, 8))
    tn = _pick(n, (512, 256, 128))
    return pl.pallas_call(
        _mm_body,
        grid=(m // tm, n // tn),
        in_specs=[pl.BlockSpec((tm, k), lambda i, j: (i, 0)),
                  pl.BlockSpec((k, tn), lambda i, j: (0, j))],
        out_specs=pl.BlockSpec((tm, tn), lambda i, j: (i, j)),
        out_shape=jax.ShapeDtypeStruct((m, n), jnp.float32),
        compiler_params=pltpu.CompilerParams(
            dimension_semantics=("parallel", "parallel"), vmem_limit_bytes=_VMEM_LIMIT),
        name="mm",
    )(a, b.astype(jnp.bfloat16))


def _bmm_body(a_ref, b_ref, o_ref):
    o_ref[0] = jnp.dot(a_ref[0].astype(jnp.bfloat16), b_ref[0],
                       preferred_element_type=jnp.float32)


def _bmm(a, b):
    e, m, k = a.shape
    n = b.shape[2]
    tm = _pick(m, (512, 256, 128, 64, 32, 16, 8))
    tn = _pick(n, (512, 256, 128))
    return pl.pallas_call(
        _bmm_body,
        grid=(e, m // tm, n // tn),
        in_specs=[pl.BlockSpec((1, tm, k), lambda g, i, j: (g, i, 0)),
                  pl.BlockSpec((1, k, tn), lambda g, i, j: (g, 0, j))],
        out_specs=pl.BlockSpec((1, tm, tn), lambda g, i, j: (g, i, j)),
        out_shape=jax.ShapeDtypeStruct((e, m, n), jnp.float32),
        compiler_params=pltpu.CompilerParams(
            dimension_semantics=("parallel", "parallel", "parallel"), vmem_limit_bytes=_VMEM_LIMIT),
        name="bmm",
    )(a, b.astype(jnp.bfloat16))


def _mm3(a, b):
    bsz, l, k = a.shape
    return _mm(a.reshape(bsz * l, k), b).reshape(bsz, l, b.shape[1])


def _rms_norm(x, g):
    y = x * lax.rsqrt(jnp.mean(x * x, axis=-1, keepdims=True) + EPS)
    return y * g


def _modulate(h, shift, scale):
    return h * (1.0 + scale) + shift


def _dwconv(x, w, b):
    k = w.shape[0]
    left = (k - 1) // 2
    l = x.shape[1]
    xp = jnp.pad(x, ((0, 0), (left, k - 1 - left), (0, 0)))
    y = b
    for i in range(k):
        y = y + xp[:, i:i + l, :] * w[i]
    return y


def _to_col_major(t, rows):
    bsz, l, ch = t.shape
    return t.reshape(bsz, rows, GRID_W, ch).transpose(0, 2, 1, 3).reshape(bsz, l, ch)


def _to_row_major(t, rows):
    bsz, l, ch = t.shape
    return t.reshape(bsz, GRID_W, rows, ch).transpose(0, 2, 1, 3).reshape(bsz, l, ch)


def _lru_scan_body(af_ref, bf_ref, ab_ref, bb_ref, hf_ref, hb_ref, sf_scr, sb_scr, *, tb):
    @pl.when(pl.program_id(1) == 0)
    def _():
        sf_scr[...] = jnp.zeros_like(sf_scr)
        sb_scr[...] = jnp.zeros_like(sb_scr)

    def tile(i, carry):
        hf, hb = carry
        base_f = pl.multiple_of(i * _SUBLANES, _SUBLANES)
        base_b = pl.multiple_of(tb - _SUBLANES - i * _SUBLANES, _SUBLANES)
        for r in range(_SUBLANES):
            rf = pl.ds(base_f + r, 1)
            hf = af_ref[0, rf, :] * hf + bf_ref[0, rf, :]
            hf_ref[0, rf, :] = hf
            rb = pl.ds(base_b + (_SUBLANES - 1 - r), 1)
            hb = ab_ref[0, rb, :] * hb + bb_ref[0, rb, :]
            hb_ref[0, rb, :] = hb
        return hf, hb

    hf, hb = lax.fori_loop(0, tb // _SUBLANES, tile, (sf_scr[...], sb_scr[...]))
    sf_scr[...] = hf
    sb_scr[...] = hb


def _lru_scan(a_f, b_f, a_b, b_b, l_ctx):
    bsz, l_tot, ch = a_f.shape
    tb = l_ctx
    nb = l_tot // tb
    fwd = pl.BlockSpec((1, tb, ch), lambda b, j: (b, j, 0))
    bwd = pl.BlockSpec((1, tb, ch), lambda b, j: (b, jnp.where(j == 0, 0, nb - j), 0))
    return pl.pallas_call(
        partial(_lru_scan_body, tb=tb),
        grid=(bsz, nb),
        in_specs=[fwd, fwd, bwd, bwd],
        out_specs=[fwd, bwd],
        out_shape=[jax.ShapeDtypeStruct(a_f.shape, jnp.float32)] * 2,
        scratch_shapes=[pltpu.VMEM((1, ch), jnp.float32), pltpu.VMEM((1, ch), jnp.float32)],
        compiler_params=pltpu.CompilerParams(dimension_semantics=("parallel", "arbitrary")),
        name="lru_scan",
    )(a_f, b_f, a_b, b_b)


def _rglru_coeffs(xh, wa, ba, wx, bx, lam):
    r = jax.nn.sigmoid(jnp.einsum('blhi,hij->blhj', xh, wa) + ba)
    i = jax.nn.sigmoid(jnp.einsum('blhi,hij->blhj', xh, wx) + bx)
    log_a = -LRU_C * r * jax.nn.softplus(-lam)
    a = jnp.exp(log_a)
    return a, jnp.sqrt(-jnp.expm1(2.0 * log_a)) * i * xh


def _rglru_mixer(xr, xg, l_ctx, conv_w, conv_b, wa, ba, wx, bx, lam):
    bsz, l_tot, _ = xr.shape
    xc = jnp.concatenate([_dwconv(xr[:, :l_ctx], conv_w, conv_b), _dwconv(xr[:, l_ctx:], conv_w, conv_b)], axis=1)
    xh = xc.reshape(bsz, l_tot, LRU_HEADS, LRU_HEAD_DIM)
    ab = []
    for d in range(2):
        a, bb = _rglru_coeffs(xh, wa[d], ba[d], wx[d], bx[d], lam[d])
        ab += [a.reshape(bsz, l_tot, W_HALF), bb.reshape(bsz, l_tot, W_HALF)]
    h_f, h_b = _lru_scan(ab[0], ab[1], ab[2], ab[3], l_ctx)
    return (h_f + h_b) * jax.nn.gelu(xg)


def _hyena_filters(l, w1, b1, w2, b2, w3, freq):
    t = jnp.arange(l, dtype=jnp.float32)
    t_unit = t / max(l - 1, 1)
    bands = jnp.linspace(1e-4, HY_BANDS - 1, HY_BANDS, dtype=jnp.float32)
    ang = (2.0 * math.pi / l) * t[:, None] * bands[None, :]
    z = jnp.concatenate([t_unit[:, None], jnp.cos(ang), -jnp.sin(ang)], axis=-1)
    h = jnp.sin(freq * (z @ w1 + b1))
    h = jnp.sin(freq * (h @ w2 + b2))
    h = h @ w3
    dist = jnp.abs(t - (l // 2)) / (l / 2.0)
    deltas = jnp.abs(jnp.linspace(HY_MIN_DECAY, HY_MAX_DECAY, HY_ORDER * W_HALF, dtype=jnp.float32))
    h = h * jnp.exp(-dist[:, None] * deltas[None, :])
    h = h * lax.rsqrt(jnp.sum(h * h, axis=0, keepdims=True) + EPS)
    return h.reshape(l, HY_ORDER, W_HALF).transpose(1, 0, 2)


def _fft_conv(u, filt, skip):
    l = u.shape[1]
    n = 2 * l
    spec = jnp.fft.rfft(u, n=n, axis=1) * jnp.fft.rfft(filt, n=n, axis=0)[None]
    y = jnp.fft.irfft(spec, n=n, axis=1)[:, l // 2: l // 2 + l]
    return y + u * skip


def _hyena(u, conv_w, conv_b, w1, b1, w2, b2, w3, freq, skip):
    l = u.shape[1]
    u = _dwconv(u, conv_w, conv_b)
    v, x1, x2 = jnp.split(u, 3, axis=-1)
    filt = _hyena_filters(l, w1, b1, w2, b2, w3, freq)
    z = x1 * _fft_conv(v, filt[0], skip[0])
    z = x2 * _fft_conv(z, filt[1], skip[1])
    return z


def _even_mixer(h_all, l_ctx, in_w, out_w, lru_conv_w, lru_conv_b, lru_wa, lru_ba, lru_wx, lru_bx,
                lru_lam, hy_conv_w, hy_conv_b, hy_w1, hy_b1, hy_w2, hy_b2, hy_w3, hy_freq, hy_skip):
    w = W_HALF
    u = _mm3(h_all, in_w)
    ya = _rglru_mixer(u[..., :w], u[..., w:2 * w], l_ctx, lru_conv_w, lru_conv_b, lru_wa, lru_ba, lru_wx,
                      lru_bx, lru_lam)
    hy = (hy_conv_w, hy_conv_b, hy_w1, hy_b1, hy_w2, hy_b2, hy_w3, hy_freq, hy_skip)
    yb = jnp.concatenate([_hyena(u[:, :l_ctx, 2 * w:], *hy), _hyena(u[:, l_ctx:, 2 * w:], *hy)], axis=1)
    return _mm3(jnp.concatenate([ya, yb], axis=-1), out_w)


def _split3_dot(m, x):
    hi = x.astype(jnp.bfloat16)
    r1 = x - hi.astype(jnp.float32)
    mid = r1.astype(jnp.bfloat16)
    lo = (r1 - mid.astype(jnp.float32)).astype(jnp.bfloat16)
    d = partial(jnp.dot, preferred_element_type=jnp.float32)
    return d(m, hi) + d(m, mid) + d(m, lo)


def _log_sigmoid(x):
    return jnp.minimum(x, 0.0) - jnp.log1p(jnp.exp(-jnp.abs(x)))


def _hgrn_gate(fp, loglb, log1mlb, onemlb):
    a, b = loglb, log1mlb + _log_sigmoid(fp)
    mx = jnp.maximum(a, b)
    logf = mx + jnp.log1p(jnp.exp(-jnp.abs(a - b)))
    k = onemlb * (1.0 / (1.0 + jnp.exp(fp)))
    return k, logf


def _hgrn_body(ff_ref, fb_ref, v_ref, q_ref, g_ref, lbp_ref, ng_ref, o_ref, acc_scr, st_scr, *, n_blk, l_ctx):
    tb, c = HG_TB, HG_C
    n_c = tb // c
    acc_scr[...] = jnp.zeros_like(acc_scr)
    st_scr[...] = jnp.zeros_like(st_scr)
    loglb, log1mlb, onemlb = lbp_ref[0, 0:1, :], lbp_ref[0, 1:2, :], lbp_ref[0, 2:3, :]
    ri = lax.broadcasted_iota(jnp.int32, (tb, tb), 0)
    ci = lax.broadcasted_iota(jnp.int32, (tb, tb), 1)
    same = (ri // c) == (ci // c)
    ones_bd = same.astype(jnp.bfloat16)
    tril_bd = (same & (ci <= ri)).astype(jnp.bfloat16)
    triu_bd = (same & (ci >= ri)).astype(jnp.bfloat16)
    pos = lax.broadcasted_iota(jnp.int32, (tb, HG_DK), 0) % c
    nt = (((1,), (1,)), ((), ()))
    tn = (((0,), (0,)), ((), ()))

    def direction(base, f_ref, d, cum_m):
        rows = pl.ds(base, tb)
        k, logf = _hgrn_gate(f_ref[0, rows, :], loglb, log1mlb, onemlb)
        v = v_ref[0, rows, :]
        qr = q_ref[0, rows, :]
        q = qr * (1.0 / (1.0 + jnp.exp(-qr)))
        cum = _split3_dot(cum_m, logf)
        tot = _split3_dot(ones_bd, logf)
        qe = (q * jnp.exp(cum)).astype(jnp.bfloat16)
        kt = (k * jnp.exp(tot - cum)).astype(jnp.bfloat16)
        dec = jnp.exp(tot)
        vb = v.astype(jnp.bfloat16)
        o = jnp.sum(q * k, axis=-1, keepdims=True) * v
        for j in range(1, c):
            sh = j if d == 0 else tb - j
            valid = (pos >= j) if d == 0 else (pos < c - j)
            kj = pltpu.roll(k, sh, 0)
            cj = pltpu.roll(cum, sh, 0)
            vj = pltpu.roll(v, sh, 0)
            e = jnp.exp(jnp.where(valid, cum - cj, _NEG))
            o = o + jnp.sum(q * kj * e, axis=-1, keepdims=True) * vj
        acc_scr[rows, :] += o
        for i in range(n_c):
            cc = i if d == 0 else n_c - 1 - i
            r0 = cc * c
            st = st_scr[d]
            oi = lax.dot_general(qe[r0:r0 + c], st.astype(jnp.bfloat16), nt,
                                 preferred_element_type=jnp.float32)
            acc_scr[pl.ds(base + r0, c), :] += oi
            kv = lax.dot_general(vb[r0:r0 + c], kt[r0:r0 + c], tn, preferred_element_type=jnp.float32)
            st_scr[d] = st * dec[r0:r0 + 1, :] + kv

    def step(j, carry):
        base_f = pl.multiple_of(j * tb, tb)
        jb = jnp.where(j == 0, 0, n_blk - j)
        base_b = pl.multiple_of(jb * tb, tb)
        direction(base_f, ff_ref, 0, tril_bd)
        direction(base_b, fb_ref, 1, triu_bd)
        return carry

    lax.fori_loop(0, n_blk, step, 0)
    o = acc_scr[l_ctx:, :]
    g = g_ref[0, l_ctx:, :]
    o = o * lax.rsqrt(jnp.mean(o * o, axis=-1, keepdims=True) + EPS) * ng_ref[...]
    o_ref[0] = o * (g * (1.0 / (1.0 + jnp.exp(-g))))


def _hgrn_mix(u_all, lb, norm_g, l_ctx):
    bsz, l_tot, _ = u_all.shape
    nh = HG_HEADS
    n_blk = l_tot // HG_TB
    lbh = lb.reshape(nh, 1, HG_DK)
    lbp = jnp.concatenate([jnp.log(lbh), jnp.log1p(-lbh), 1.0 - lbh,
                           jnp.zeros((nh, _SUBLANES - 3, HG_DK), jnp.float32)], axis=1)

    def slab(k):
        return pl.BlockSpec((1, l_tot, HG_DK), lambda b, h, k=k: (b, 0, k * nh + h))

    return pl.pallas_call(
        partial(_hgrn_body, n_blk=n_blk, l_ctx=l_ctx),
        grid=(bsz, nh),
        in_specs=[slab(0), slab(1), slab(2), slab(4), slab(5),
                  pl.BlockSpec((1, _SUBLANES, HG_DK), lambda b, h: (h, 0, 0)),
                  pl.BlockSpec((1, HG_DK), lambda b, h: (0, 0))],
        out_specs=pl.BlockSpec((1, l_tot - l_ctx, HG_DK), lambda b, h: (b, 0, h)),
        out_shape=jax.ShapeDtypeStruct((bsz, l_tot - l_ctx, W_HALF), jnp.float32),
        scratch_shapes=[pltpu.VMEM((l_tot, HG_DK), jnp.float32), pltpu.VMEM((2, HG_DK, HG_DK), jnp.float32)],
        compiler_params=pltpu.CompilerParams(dimension_semantics=("parallel", "parallel"),
                                             vmem_limit_bytes=_VMEM_LIMIT),
        name="hgrn_mix",
    )(u_all, u_all, u_all, u_all, u_all, lbp, norm_g.reshape(1, HG_DK))


def _s5_weights(lam_re, lam_im, log_dt, b_re, b_im, c_re, c_im):
    t_len, f32 = S5_T, jnp.float32
    lre = jnp.minimum(lam_re, S5_MAX_RE)
    dt = jnp.exp(log_dt)[..., None]
    ar, ai = lre * dt, lam_im * dt
    k = jnp.arange(t_len + 1, dtype=f32)[:, None, None, None]
    mag = jnp.exp(k * ar)
    pre, pim = mag * jnp.cos(k * ai), mag * jnp.sin(k * ai)
    nr, ni, den = pre[1] - 1.0, pim[1], lre * lre + lam_im * lam_im
    cr, ci = (nr * lre + ni * lam_im) / den, (ni * lre - nr * lam_im) / den
    cbr = cr[..., None] * b_re - ci[..., None] * b_im
    cbi = cr[..., None] * b_im + ci[..., None] * b_re
    c_re_t, c_im_t = jnp.swapaxes(c_re, 1, 2), jnp.swapaxes(c_im, 1, 2)
    clre = c_re_t * pre[..., None] - c_im_t * pim[..., None]
    clim = c_re_t * pim[..., None] + c_im_t * pre[..., None]
    g = lam_re.shape[1]

    def w_dir(d, e):
        pr, pi = pre[e, d], pim[e, d]
        wr = pr[..., None] * cbr[d] - pi[..., None] * cbi[d]
        wi = pr[..., None] * cbi[d] + pi[..., None] * cbr[d]
        return [jnp.transpose(z, (1, 0, 3, 2)).reshape(g, t_len * S5_H, S5_P) for z in (wr, wi)]

    w = jnp.concatenate(w_dir(0, jnp.arange(t_len - 1, -1, -1)) + w_dir(1, jnp.arange(t_len)), axis=-1)
    dec = jnp.concatenate([pre[t_len, 0], pre[t_len, 0], -pim[t_len, 0], pim[t_len, 0],
                           pre[t_len, 1], pre[t_len, 1], -pim[t_len, 1], pim[t_len, 1]], axis=-1)[:, None, :]

    def m_dir(d, e):
        mr, mi = clre[e, d], -clim[e, d]
        return [jnp.transpose(z, (1, 2, 0, 3)).reshape(g, S5_P, t_len * S5_H) for z in (mr, mi)]

    m = jnp.concatenate(m_dir(0, jnp.arange(1, t_len + 1)) + m_dir(1, jnp.arange(t_len, 0, -1)), axis=1)
    kk = (jnp.einsum('kdgpo,dgpi->kdgoi', clre[:t_len], cbr, precision=_HI)
          - jnp.einsum('kdgpo,dgpi->kdgoi', clim[:t_len], cbi, precision=_HI))
    delta = jnp.arange(t_len)[None, :] - jnp.arange(t_len)[:, None]
    kf = kk[jnp.clip(delta, 0, t_len - 1), 0]
    kb = kk[jnp.clip(-delta, 0, t_len - 1), 1]
    dl = delta[:, :, None, None, None]
    toep = jnp.where(dl > 0, kf, 0.0) + jnp.where(dl < 0, kb, 0.0) + jnp.where(dl == 0, kf + kb, 0.0)
    toep = jnp.transpose(toep, (2, 0, 4, 1, 3)).reshape(g, t_len * S5_H, t_len * S5_H)
    return w, dec, m, toep


def _s5_body(x_ref, w_ref, dec_ref, m_ref, toep_ref, y_ref, z_scr, s_scr, *, bsz, n_ctx, n_lat):
    p2 = 2 * S5_P
    n_all = n_ctx + n_lat
    z_scr[...] = jnp.dot(x_ref[0], w_ref[0], preferred_element_type=jnp.float32)
    dec = dec_ref[0]
    a1f, a2f = dec[:, 0:p2], dec[:, p2:2 * p2]
    a1b, a2b = dec[:, 2 * p2:3 * p2], dec[:, 3 * p2:4 * p2]

    def fwd(c, s):
        r = pl.multiple_of(c * bsz, bsz)
        s_scr[pl.ds(r, bsz), 0:p2] = s
        return s * a1f + pltpu.roll(s, S5_P, 1) * a2f + z_scr[pl.ds(r, bsz), 0:p2]

    lax.fori_loop(0, n_all, fwd, jnp.zeros((bsz, p2), jnp.float32))

    def bwd_at(c, s):
        r = pl.multiple_of(c * bsz, bsz)
        s_scr[pl.ds(r, bsz), p2:2 * p2] = s
        return s * a1b + pltpu.roll(s, S5_P, 1) * a2b + z_scr[pl.ds(r, bsz), p2:2 * p2]

    s = lax.fori_loop(0, n_ctx, lambda i, s: bwd_at(n_ctx - 1 - i, s), jnp.zeros((bsz, p2), jnp.float32))
    lax.fori_loop(0, n_lat, lambda i, s: bwd_at(n_all - 1 - i, s), s)
    r0 = n_ctx * bsz
    y_ref[0] = (jnp.dot(x_ref[0, r0:, :], toep_ref[0], preferred_element_type=jnp.float32)
                + jnp.dot(s_scr[r0:, :].astype(jnp.bfloat16), m_ref[0], preferred_element_type=jnp.float32))


def _s5_mix(u_c, u_l, lam_re, lam_im, log_dt, b_re, b_im, c_re, c_im):
    bsz, l_ctx, gh = u_c.shape
    l_lat = u_l.shape[1]
    g, t_len = gh // S5_H, S5_T
    n_ctx, n_lat = l_ctx // t_len, l_lat // t_len
    w, dec, m, toep = _s5_weights(lam_re, lam_im, log_dt, b_re, b_im, c_re, c_im)

    def chunked(u, n):
        return jnp.transpose(u.reshape(bsz, n, t_len, g, S5_H), (3, 1, 0, 2, 4)).reshape(g, n * bsz, t_len * S5_H)

    x = jnp.concatenate([chunked(u_c, n_ctx), chunked(u_l, n_lat)], axis=1).astype(jnp.bfloat16)
    rows, k = (n_ctx + n_lat) * bsz, t_len * S5_H
    y = pl.pallas_call(
        partial(_s5_body, bsz=bsz, n_ctx=n_ctx, n_lat=n_lat),
        grid=(g,),
        in_specs=[pl.BlockSpec((1, rows, k), lambda i: (i, 0, 0)),
                  pl.BlockSpec((1, k, 4 * S5_P), lambda i: (i, 0, 0)),
                  pl.BlockSpec((1, 1, 8 * S5_P), lambda i: (i, 0, 0)),
                  pl.BlockSpec((1, 4 * S5_P, k), lambda i: (i, 0, 0)),
                  pl.BlockSpec((1, k, k), lambda i: (i, 0, 0))],
        out_specs=pl.BlockSpec((1, n_lat * bsz, k), lambda i: (i, 0, 0)),
        out_shape=jax.ShapeDtypeStruct((g, n_lat * bsz, k), jnp.float32),
        scratch_shapes=[pltpu.VMEM((rows, 4 * S5_P), jnp.float32), pltpu.VMEM((rows, 4 * S5_P), jnp.float32)],
        compiler_params=pltpu.CompilerParams(dimension_semantics=("parallel",), vmem_limit_bytes=_VMEM_LIMIT),
        name="s5_mix",
    )(x, w.astype(jnp.bfloat16), dec, m.astype(jnp.bfloat16), toep.astype(jnp.bfloat16))
    return jnp.transpose(y.reshape(g, n_lat, bsz, t_len, S5_H), (2, 1, 3, 0, 4)).reshape(bsz, l_lat, gh)


def _s5_out(y, u, d_skip, glu_w, glu_b):
    y = jax.nn.gelu(y + u * d_skip.reshape(-1))
    return y * jax.nn.sigmoid(_mm3(y, glu_w) + glu_b)


def _odd_mixer(h_all, l_ctx, rows, in_w, out_w, lb, hg_norm_g, lam_re, lam_im, log_dt,
               b_re, b_im, c_re, c_im, d_skip, glu_w, glu_b):
    w = W_HALF
    u = _mm3(h_all, in_w)
    hg_l = _hgrn_mix(u, lb, hg_norm_g, l_ctx)
    s_c, s_l = u[:, :l_ctx, 3 * w:4 * w], u[:, l_ctx:, 3 * w:4 * w]
    y_l = _s5_mix(s_c, s_l, lam_re, lam_im, log_dt, b_re, b_im, c_re, c_im)
    s5_l = _s5_out(y_l, s_l, d_skip, glu_w, glu_b)
    return _to_row_major(_mm3(jnp.concatenate([hg_l, s5_l], axis=-1), out_w), rows)


def _expert_choice_ffn(h, w_router, w_gate, w_up, w_down):
    bsz, n, d = h.shape
    cap = max(1, (EC_CAPACITY * n) // N_EXPERTS)
    aff = jax.nn.softmax(jnp.einsum('bnd,de->bne', h, w_router), axis=-1)
    gate, idx = lax.top_k(jnp.swapaxes(aff, 1, 2), cap)
    xin = jax.vmap(lambda hb, ib: hb[ib])(h, idx)
    xe = xin.transpose(1, 0, 2, 3).reshape(N_EXPERTS, bsz * cap, d)
    a = _bmm(xe, w_gate)
    u = _bmm(xe, w_up)
    y = _bmm(jax.nn.silu(a) * u, w_down)
    y = y.reshape(N_EXPERTS, bsz, cap, d).transpose(1, 0, 2, 3) * gate[..., None]
    return jax.vmap(lambda yb, ib: jnp.zeros((n, d), yb.dtype).at[ib.reshape(-1)].add(yb.reshape(-1, d)))(y, idx)


def kernel(x, c, ctx, c_ctx, mod_w, mod_b, norm1_g, norm2_g, out_w, even_in_w, lru_conv_w, lru_conv_b, lru_wa, lru_ba, lru_wx, lru_bx, lru_lam, hy_conv_w, hy_conv_b, hy_w1, hy_b1, hy_w2, hy_b2, hy_w3, hy_freq, hy_skip, odd_in_w, hg_lb, hg_norm_g, s5_lam_re, s5_lam_im, s5_log_dt, s5_b_re, s5_b_im, s5_c_re, s5_c_im, s5_d, s5_glu_w, s5_glu_b, router_w, ex_w_gate, ex_w_up, ex_w_down, final_g):
    rows = x.shape[1] // GRID_W
    l_ctx = ctx.shape[1]
    p = jax.nn.softmax(hg_lb, axis=0)
    lb_all = jnp.clip(jnp.cumsum(p, axis=0) - p[0], 0.0, 1.0 - 1e-4)
    sc_lat = jax.nn.silu(c)
    sc_ctx = jax.nn.silu(c_ctx)[None]
    for i in range(DEPTH):
        need_ctx = i < DEPTH - 1
        j = i // 2
        ml = [t[:, None, :] for t in jnp.split(sc_lat @ mod_w[i] + mod_b[i], 6, axis=-1)]
        mc = [t[:, None, :] for t in jnp.split(sc_ctx @ mod_w[i] + mod_b[i], 6, axis=-1)]
        h_lat = _modulate(_rms_norm(x, norm1_g[i]), ml[0], ml[1])
        h_ctx = _modulate(_rms_norm(ctx, norm1_g[i]), mc[0], mc[1])
        if i % 2 == 0:
            y = _even_mixer(jnp.concatenate([h_ctx, h_lat], axis=1), l_ctx, even_in_w[j], out_w[i],
                            lru_conv_w[j], lru_conv_b[j], lru_wa[j], lru_ba[j], lru_wx[j], lru_bx[j], lru_lam[j],
                            hy_conv_w[j], hy_conv_b[j], hy_w1[j], hy_b1[j], hy_w2[j], hy_b2[j], hy_w3[j],
                            hy_freq[j], hy_skip[j])
            y_c, y_l = y[:, :l_ctx], y[:, l_ctx:]
        else:
            assert not need_ctx
            y_l = _odd_mixer(jnp.concatenate([h_ctx, _to_col_major(h_lat, rows)], axis=1), l_ctx, rows,
                             odd_in_w[j], out_w[i], lb_all[i], hg_norm_g[j], s5_lam_re[j], s5_lam_im[j],
                             s5_log_dt[j], s5_b_re[j], s5_b_im[j], s5_c_re[j], s5_c_im[j], s5_d[j],
                             s5_glu_w[j], s5_glu_b[j])
        x = x + ml[2] * y_l
        x = x + ml[5] * _expert_choice_ffn(_modulate(_rms_norm(x, norm2_g[i]), ml[3], ml[4]),
                                           router_w[i], ex_w_gate[i], ex_w_up[i], ex_w_down[i])
        if need_ctx:
            ctx = ctx + mc[2] * y_c
            ctx = ctx + mc[5] * _expert_choice_ffn(_modulate(_rms_norm(ctx, norm2_g[i]), mc[3], mc[4]),
                                                   router_w[i], ex_w_gate[i], ex_w_up[i], ex_w_down[i])
    return _rms_norm(x, final_g)
```

```python
import math
from functools import partial

import jax
import jax.numpy as jnp
from jax import lax
from jax.experimental import pallas as pl
from jax.experimental.pallas import tpu as pltpu

D_MODEL = 1024
DEPTH = 2
GRID_W = 64
W_HALF = D_MODEL // 2
EPS = 1e-6
LRU_HEADS = 8
LRU_HEAD_DIM = W_HALF // LRU_HEADS
LRU_C = 8.0
HY_ORDER = 2
HY_BANDS = 16
HY_MAX_DECAY = math.log(1e-2) / 0.3
HY_MIN_DECAY = math.log(1e-2) / 1.5
HG_HEADS = 4
HG_DK = W_HALF // HG_HEADS
HG_C = 16
HG_TB = 256
S5_H = 16
S5_P = 64
S5_MAX_RE = -1e-4
S5_T = 32
N_EXPERTS = 16
EC_CAPACITY = 2

_SUBLANES = 8
_VMEM_LIMIT = 48 * 1024 * 1024
_HI = lax.Precision.HIGHEST
_NEG = -1e30


def _mm_body(a_ref, b_ref, o_ref):
    o_ref[...] = jnp.dot(a_ref[...].astype(jnp.bfloat16), b_ref[...],
                         preferred_element_type=jnp.float32)


def _pick(n, pref):
    for t in pref:
        if n % t == 0:
            return t
    return n


def _mm(a, b):
    m, k = a.shape
    n = b.shape[1]
    tm = _pick(m, (512, 256, 128, 64, 32, 16, 8))
    tn = _pick(n, (512, 256, 128))
    return pl.pallas_call(
        _mm_body,
        grid=(m // tm, n // tn),
        in_specs=[pl.BlockSpec((tm, k), lambda i, j: (i, 0)),
                  pl.BlockSpec((k, tn), lambda i, j: (0, j))],
        out_specs=pl.BlockSpec((tm, tn), lambda i, j: (i, j)),
        out_shape=jax.ShapeDtypeStruct((m, n), jnp.float32),
        compiler_params=pltpu.CompilerParams(
            dimension_semantics=("parallel", "parallel"), vmem_limit_bytes=_VMEM_LIMIT),
        name="mm",
    )(a, b.astype(jnp.bfloat16))


def _mm3(a, b):
    bsz, l, k = a.shape
    return _mm(a.reshape(bsz * l, k), b).reshape(bsz, l, b.shape[1])


def _two_part(t_ctx, t_lat):
    bsz, _, d = t_lat.shape
    return jnp.stack([jnp.broadcast_to(t_ctx, (bsz, 1, d)), t_lat], axis=1).reshape(2 * bsz, 1, d)


def _norm_mm_body(x_ref, g_ref, sh_ref, sc_ref, w_ref, o_ref):
    x = x_ref[0]
    h = x * lax.rsqrt(jnp.mean(x * x, axis=-1, keepdims=True) + EPS) * g_ref[...]
    h = h * (1.0 + sc_ref[0]) + sh_ref[0]
    o_ref[0] = jnp.dot(h.astype(jnp.bfloat16), w_ref[...], preferred_element_type=jnp.float32)


def _norm_mm(x_all, norm_g, shift, scale, w, l_ctx):
    bsz, l_tot, d = x_all.shape
    n = w.shape[1]
    tm = l_ctx
    vec = pl.BlockSpec((1, 1, d), lambda b, j: (2 * b + jnp.minimum(j, 1), 0, 0))
    return pl.pallas_call(
        _norm_mm_body,
        grid=(bsz, l_tot // tm),
        in_specs=[pl.BlockSpec((1, tm, d), lambda b, j: (b, j, 0)),
                  pl.BlockSpec((1, d), lambda b, j: (0, 0)), vec, vec,
                  pl.BlockSpec((d, n), lambda b, j: (0, 0))],
        out_specs=pl.BlockSpec((1, tm, n), lambda b, j: (b, j, 0)),
        out_shape=jax.ShapeDtypeStruct((bsz, l_tot, n), jnp.float32),
        compiler_params=pltpu.CompilerParams(dimension_semantics=("parallel", "parallel"),
                                             vmem_limit_bytes=_VMEM_LIMIT),
        name="norm_mm",
    )(x_all, norm_g.reshape(1, d), shift, scale, w.astype(jnp.bfloat16))


def _proj_res_body(ya_ref, yb_ref, wa_ref, wb_ref, x_ref, gt_ref, o_ref):
    y = (jnp.dot(ya_ref[0].astype(jnp.bfloat16), wa_ref[...], preferred_element_type=jnp.float32)
         + jnp.dot(yb_ref[0].astype(jnp.bfloat16), wb_ref[...], preferred_element_type=jnp.float32))
    o_ref[0] = x_ref[0] + gt_ref[0] * y


def _proj_res(ya, yb, w, x_all, gate, l_ctx, row0):
    bsz, l_y, ka = ya.shape
    d = w.shape[1]
    tm = l_ctx
    j0 = row0 // tm
    yblk = lambda k: pl.BlockSpec((1, tm, k), lambda b, j: (b, j, 0))
    return pl.pallas_call(
        _proj_res_body,
        grid=(bsz, l_y // tm),
        in_specs=[yblk(ka), yblk(yb.shape[2]),
                  pl.BlockSpec((ka, d), lambda b, j: (0, 0)), pl.BlockSpec((yb.shape[2], d), lambda b, j: (0, 0)),
                  pl.BlockSpec((1, tm, d), lambda b, j: (b, j + j0, 0)),
                  pl.BlockSpec((1, 1, d), lambda b, j: (2 * b + jnp.minimum(j + j0, 1), 0, 0))],
        out_specs=pl.BlockSpec((1, tm, d), lambda b, j: (b, j, 0)),
        out_shape=jax.ShapeDtypeStruct((bsz, l_y, d), jnp.float32),
        compiler_params=pltpu.CompilerParams(dimension_semantics=("parallel", "parallel"),
                                             vmem_limit_bytes=_VMEM_LIMIT),
        name="proj_res",
    )(ya, yb, w[:ka].astype(jnp.bfloat16), w[ka:].astype(jnp.bfloat16), x_all, gate)


def _rms_norm(x, g):
    y = x * lax.rsqrt(jnp.mean(x * x, axis=-1, keepdims=True) + EPS)
    return y * g


def _modulate(h, shift, scale):
    return h * (1.0 + scale) + shift


def _dwconv(x, w, b):
    k = w.shape[0]
    left = (k - 1) // 2
    l = x.shape[1]
    xp = jnp.pad(x, ((0, 0), (left, k - 1 - left), (0, 0)))
    y = b
    for i in range(k):
        y = y + xp[:, i:i + l, :] * w[i]
    return y


def _to_col_major(t, rows):
    bsz, l, ch = t.shape
    return t.reshape(bsz, rows, GRID_W, ch).transpose(0, 2, 1, 3).reshape(bsz, l, ch)


def _to_row_major(t, rows):
    bsz, l, ch = t.shape
    return t.reshape(bsz, GRID_W, rows, ch).transpose(0, 2, 1, 3).reshape(bsz, l, ch)


def _lru_scan_body(af_ref, bf_ref, ab_ref, bb_ref, hf_ref, hb_ref, sf_scr, sb_scr, *, tb):
    @pl.when(pl.program_id(1) == 0)
    def _():
        sf_scr[...] = jnp.zeros_like(sf_scr)
        sb_scr[...] = jnp.zeros_like(sb_scr)

    def tile(i, carry):
        hf, hb = carry
        base_f = pl.multiple_of(i * _SUBLANES, _SUBLANES)
        base_b = pl.multiple_of(tb - _SUBLANES - i * _SUBLANES, _SUBLANES)
        for r in range(_SUBLANES):
            rf = pl.ds(base_f + r, 1)
            hf = af_ref[0, rf, :] * hf + bf_ref[0, rf, :]
            hf_ref[0, rf, :] = hf
            rb = pl.ds(base_b + (_SUBLANES - 1 - r), 1)
            hb = ab_ref[0, rb, :] * hb + bb_ref[0, rb, :]
            hb_ref[0, rb, :] = hb
        return hf, hb

    hf, hb = lax.fori_loop(0, tb // _SUBLANES, tile, (sf_scr[...], sb_scr[...]))
    sf_scr[...] = hf
    sb_scr[...] = hb


def _lru_scan(a_f, b_f, a_b, b_b, l_ctx):
    bsz, l_tot, ch = a_f.shape
    tb = l_ctx
    nb = l_tot // tb
    fwd = pl.BlockSpec((1, tb, ch), lambda b, j: (b, j, 0))
    bwd = pl.BlockSpec((1, tb, ch), lambda b, j: (b, jnp.where(j == 0, 0, nb - j), 0))
    return pl.pallas_call(
        partial(_lru_scan_body, tb=tb),
        grid=(bsz, nb),
        in_specs=[fwd, fwd, bwd, bwd],
        out_specs=[fwd, bwd],
        out_shape=[jax.ShapeDtypeStruct(a_f.shape, jnp.float32)] * 2,
        scratch_shapes=[pltpu.VMEM((1, ch), jnp.float32), pltpu.VMEM((1, ch), jnp.float32)],
        compiler_params=pltpu.CompilerParams(dimension_semantics=("parallel", "arbitrary")),
        name="lru_scan",
    )(a_f, b_f, a_b, b_b)


def _rglru_coeffs(xh, wa, ba, wx, bx, lam):
    r = jax.nn.sigmoid(jnp.einsum('blhi,hij->blhj', xh, wa) + ba)
    i = jax.nn.sigmoid(jnp.einsum('blhi,hij->blhj', xh, wx) + bx)
    log_a = -LRU_C * r * jax.nn.softplus(-lam)
    a = jnp.exp(log_a)
    return a, jnp.sqrt(-jnp.expm1(2.0 * log_a)) * i * xh


def _rglru_mixer(xr, xg, l_ctx, conv_w, conv_b, wa, ba, wx, bx, lam):
    bsz, l_tot, _ = xr.shape
    xc = jnp.concatenate([_dwconv(xr[:, :l_ctx], conv_w, conv_b), _dwconv(xr[:, l_ctx:], conv_w, conv_b)], axis=1)
    xh = xc.reshape(bsz, l_tot, LRU_HEADS, LRU_HEAD_DIM)
    ab = []
    for d in range(2):
        a, bb = _rglru_coeffs(xh, wa[d], ba[d], wx[d], bx[d], lam[d])
        ab += [a.reshape(bsz, l_tot, W_HALF), bb.reshape(bsz, l_tot, W_HALF)]
    h_f, h_b = _lru_scan(ab[0], ab[1], ab[2], ab[3], l_ctx)
    return (h_f + h_b) * jax.nn.gelu(xg)


def _hyena_filters(l, w1, b1, w2, b2, w3, freq):
    t = jnp.arange(l, dtype=jnp.float32)
    t_unit = t / max(l - 1, 1)
    bands = jnp.linspace(1e-4, HY_BANDS - 1, HY_BANDS, dtype=jnp.float32)
    ang = (2.0 * math.pi / l) * t[:, None] * bands[None, :]
    z = jnp.concatenate([t_unit[:, None], jnp.cos(ang), -jnp.sin(ang)], axis=-1)
    h = jnp.sin(freq * (z @ w1 + b1))
    h = jnp.sin(freq * (h @ w2 + b2))
    h = h @ w3
    dist = jnp.abs(t - (l // 2)) / (l / 2.0)
    deltas = jnp.abs(jnp.linspace(HY_MIN_DECAY, HY_MAX_DECAY, HY_ORDER * W_HALF, dtype=jnp.float32))
    h = h * jnp.exp(-dist[:, None] * deltas[None, :])
    h = h * lax.rsqrt(jnp.sum(h * h, axis=0, keepdims=True) + EPS)
    return h


def _dft_mats(l, tf):
    n = 2 * l
    k = jnp.arange(l, dtype=jnp.int32)
    t = jnp.arange(l, dtype=jnp.int32)
    w = 2.0 * math.pi / n

    def tile_rows(re, im):
        x = re.shape[1]
        return jnp.concatenate([re.reshape(l // tf, tf, x), im.reshape(l // tf, tf, x)], axis=1).reshape(n, x)

    ang = ((k[:, None] * t[None, :]) % n).astype(jnp.float32) * w
    f_re = jnp.cos(ang)
    f_im = -jnp.sin(ang)
    f_im = f_im.at[0].set(jnp.where(t % 2 == 0, 1.0, -1.0))
    f = tile_rows(f_re, f_im)
    tp = t + l // 2
    ang = ((k[:, None] * tp[None, :]) % n).astype(jnp.float32) * w
    g_re = (2.0 / n) * jnp.cos(ang)
    g_im = (-2.0 / n) * jnp.sin(ang)
    g_re = g_re.at[0].set(1.0 / n)
    g_im = g_im.at[0].set(jnp.where(tp % 2 == 0, 1.0, -1.0) / n)
    g = tile_rows(g_re, g_im).T
    return f.astype(jnp.bfloat16), g.astype(jnp.bfloat16)


def _spec_body(f_ref, u_ref, a_ref, b_ref, d_ref, y_ref, *, tf):
    acc = jnp.dot(f_ref[...], u_ref[0], preferred_element_type=jnp.float32)
    xr, xi = acc[:tf], acc[tf:]
    y_ref[0, :tf, :] = (xr * a_ref[...] - xi * b_ref[...]).astype(jnp.bfloat16)
    y_ref[0, tf:, :] = (xr * b_ref[...] + xi * d_ref[...]).astype(jnp.bfloat16)


def _spec_mul(f, u, a, b, d, tf):
    bsz, l, c = u.shape
    n = 2 * l
    coef = pl.BlockSpec((tf, c), lambda i, bb: (i, 0))
    return pl.pallas_call(
        partial(_spec_body, tf=tf),
        grid=(l // tf, bsz),
        in_specs=[pl.BlockSpec((2 * tf, l), lambda i, bb: (i, 0)),
                  pl.BlockSpec((1, l, c), lambda i, bb: (bb, 0, 0)), coef, coef, coef],
        out_specs=pl.BlockSpec((1, 2 * tf, c), lambda i, bb: (bb, i, 0)),
        out_shape=jax.ShapeDtypeStruct((bsz, n, c), jnp.bfloat16),
        compiler_params=pltpu.CompilerParams(dimension_semantics=("parallel", "parallel"),
                                             vmem_limit_bytes=_VMEM_LIMIT),
        name="hy_spec",
    )(f, u, a, b, d)


def _inv_body(g_ref, y_ref, u_ref, x_ref, skip_ref, z_ref):
    acc = jnp.dot(g_ref[...], y_ref[0], preferred_element_type=jnp.float32)
    z_ref[0] = x_ref[0] * (acc + u_ref[0] * skip_ref[...])


def _inv_gate(g, y, u, xg, skip, tm):
    bsz, l, c = u.shape
    blk = pl.BlockSpec((1, tm, c), lambda i, bb: (bb, i, 0))
    return pl.pallas_call(
        _inv_body,
        grid=(l // tm, bsz),
        in_specs=[pl.BlockSpec((tm, 2 * l), lambda i, bb: (i, 0)),
                  pl.BlockSpec((1, 2 * l, c), lambda i, bb: (bb, 0, 0)), blk, blk,
                  pl.BlockSpec((1, c), lambda i, bb: (0, 0))],
        out_specs=blk,
        out_shape=jax.ShapeDtypeStruct((bsz, l, c), jnp.float32),
        compiler_params=pltpu.CompilerParams(dimension_semantics=("parallel", "parallel"),
                                             vmem_limit_bytes=_VMEM_LIMIT),
        name="hy_inv",
    )(g, y, u, xg, skip.reshape(1, c))


def _hyena_conv2(v, x1, x2, filt, skip):
    bsz, l, c = v.shape
    tf = min(512, l // 2)
    f, g = _dft_mats(l, tf)
    hs = _mm(f, filt)
    hs = hs.reshape(l // tf, 2, tf, 2 * c)
    h_re, h_im = hs[:, 0].reshape(l, 2 * c), hs[:, 1].reshape(l, 2 * c)
    slot0 = (jnp.arange(l) == 0)[:, None]
    a = h_re
    b = jnp.where(slot0, 0.0, h_im)
    d = jnp.where(slot0, h_im, h_re)
    z = v
    for o, xg in enumerate((x1, x2)):
        cs = slice(o * c, (o + 1) * c)
        y = _spec_mul(f, z.astype(jnp.bfloat16), a[:, cs], b[:, cs], d[:, cs], tf)
        z = _inv_gate(g, y, z, xg, skip[o], min(512, l))
    return z


def _hyena(u, conv_w, conv_b, w1, b1, w2, b2, w3, freq, skip):
    l = u.shape[1]
    u = _dwconv(u, conv_w, conv_b)
    v, x1, x2 = jnp.split(u, 3, axis=-1)
    filt = _hyena_filters(l, w1, b1, w2, b2, w3, freq)
    return _hyena_conv2(v, x1, x2, filt, skip)


def _even_mixer(u, l_ctx, lru_conv_w, lru_conv_b, lru_wa, lru_ba, lru_wx, lru_bx,
                lru_lam, hy_conv_w, hy_conv_b, hy_w1, hy_b1, hy_w2, hy_b2, hy_w3, hy_freq, hy_skip):
    w = W_HALF
    ya = _rglru_mixer(u[..., :w], u[..., w:2 * w], l_ctx, lru_conv_w, lru_conv_b, lru_wa, lru_ba, lru_wx,
                      lru_bx, lru_lam)
    hy = (hy_conv_w, hy_conv_b, hy_w1, hy_b1, hy_w2, hy_b2, hy_w3, hy_freq, hy_skip)
    yb = jnp.concatenate([_hyena(u[:, :l_ctx, 2 * w:], *hy), _hyena(u[:, l_ctx:, 2 * w:], *hy)], axis=1)
    return ya, yb


def _split3_dot(m, x):
    hi = x.astype(jnp.bfloat16)
    r1 = x - hi.astype(jnp.float32)
    mid = r1.astype(jnp.bfloat16)
    lo = (r1 - mid.astype(jnp.float32)).astype(jnp.bfloat16)
    d = partial(jnp.dot, preferred_element_type=jnp.float32)
    return d(m, hi) + d(m, mid) + d(m, lo)


def _log_sigmoid(x):
    return jnp.minimum(x, 0.0) - jnp.log1p(jnp.exp(-jnp.abs(x)))


def _hgrn_gate(fp, loglb, log1mlb, onemlb):
    a, b = loglb, log1mlb + _log_sigmoid(fp)
    mx = jnp.maximum(a, b)
    logf = mx + jnp.log1p(jnp.exp(-jnp.abs(a - b)))
    k = onemlb * (1.0 / (1.0 + jnp.exp(fp)))
    return k, logf


def _hgrn_body(ff_ref, fb_ref, v_ref, q_ref, g_ref, lbp_ref, ng_ref, o_ref, acc_scr, st_scr, *, n_blk, l_ctx):
    tb, c = HG_TB, HG_C
    n_c = tb // c
    acc_scr[...] = jnp.zeros_like(acc_scr)
    st_scr[...] = jnp.zeros_like(st_scr)
    loglb, log1mlb, onemlb = lbp_ref[0, 0:1, :], lbp_ref[0, 1:2, :], lbp_ref[0, 2:3, :]
    ri = lax.broadcasted_iota(jnp.int32, (tb, tb), 0)
    ci = lax.broadcasted_iota(jnp.int32, (tb, tb), 1)
    same = (ri // c) == (ci // c)
    ones_bd = same.astype(jnp.bfloat16)
    tril_bd = (same & (ci <= ri)).astype(jnp.bfloat16)
    triu_bd = (same & (ci >= ri)).astype(jnp.bfloat16)
    pos = lax.broadcasted_iota(jnp.int32, (tb, HG_DK), 0) % c
    nt = (((1,), (1,)), ((), ()))
    tn = (((0,), (0,)), ((), ()))

    def direction(base, f_ref, d, cum_m):
        rows = pl.ds(base, tb)
        k, logf = _hgrn_gate(f_ref[0, rows, :], loglb, log1mlb, onemlb)
        v = v_ref[0, rows, :]
        qr = q_ref[0, rows, :]
        q = qr * (1.0 / (1.0 + jnp.exp(-qr)))
        cum = _split3_dot(cum_m, logf)
        tot = _split3_dot(ones_bd, logf)
        qe = (q * jnp.exp(cum)).astype(jnp.bfloat16)
        kt = (k * jnp.exp(tot - cum)).astype(jnp.bfloat16)
        dec = jnp.exp(tot)
        vb = v.astype(jnp.bfloat16)
        o = jnp.sum(q * k, axis=-1, keepdims=True) * v
        for j in range(1, c):
            sh = j if d == 0 else tb - j
            valid = (pos >= j) if d == 0 else (pos < c - j)
            kj = pltpu.roll(k, sh, 0)
            cj = pltpu.roll(cum, sh, 0)
            vj = pltpu.roll(v, sh, 0)
            e = jnp.exp(jnp.where(valid, cum - cj, _NEG))
            o = o + jnp.sum(q * kj * e, axis=-1, keepdims=True) * vj
        acc_scr[rows, :] += o
        for i in range(n_c):
            cc = i if d == 0 else n_c - 1 - i
            r0 = cc * c
            st = st_scr[d]
            oi = lax.dot_general(qe[r0:r0 + c], st.astype(jnp.bfloat16), nt,
                                 preferred_element_type=jnp.float32)
            acc_scr[pl.ds(base + r0, c), :] += oi
            kv = lax.dot_general(vb[r0:r0 + c], kt[r0:r0 + c], tn, preferred_element_type=jnp.float32)
            st_scr[d] = st * dec[r0:r0 + 1, :] + kv

    def step(j, carry):
        base_f = pl.multiple_of(j * tb, tb)
        jb = jnp.where(j == 0, 0, n_blk - j)
        base_b = pl.multiple_of(jb * tb, tb)
        direction(base_f, ff_ref, 0, tril_bd)
        direction(base_b, fb_ref, 1, triu_bd)
        return carry

    lax.fori_loop(0, n_blk, step, 0)
    o = acc_scr[l_ctx:, :]
    g = g_ref[0, l_ctx:, :]
    o = o * lax.rsqrt(jnp.mean(o * o, axis=-1, keepdims=True) + EPS) * ng_ref[...]
    o_ref[0] = o * (g * (1.0 / (1.0 + jnp.exp(-g))))


def _hgrn_mix(u_all, lb, norm_g, l_ctx):
    bsz, l_tot, _ = u_all.shape
    nh = HG_HEADS
    n_blk = l_tot // HG_TB
    lbh = lb.reshape(nh, 1, HG_DK)
    lbp = jnp.concatenate([jnp.log(lbh), jnp.log1p(-lbh), 1.0 - lbh,
                           jnp.zeros((nh, _SUBLANES - 3, HG_DK), jnp.float32)], axis=1)

    def slab(k):
        return pl.BlockSpec((1, l_tot, HG_DK), lambda b, h, k=k: (b, 0, k * nh + h))

    return pl.pallas_call(
        partial(_hgrn_body, n_blk=n_blk, l_ctx=l_ctx),
        grid=(bsz, nh),
        in_specs=[slab(0), slab(1), slab(2), slab(4), slab(5),
                  pl.BlockSpec((1, _SUBLANES, HG_DK), lambda b, h: (h, 0, 0)),
                  pl.BlockSpec((1, HG_DK), lambda b, h: (0, 0))],
        out_specs=pl.BlockSpec((1, l_tot - l_ctx, HG_DK), lambda b, h: (b, 0, h)),
        out_shape=jax.ShapeDtypeStruct((bsz, l_tot - l_ctx, W_HALF), jnp.float32),
        scratch_shapes=[pltpu.VMEM((l_tot, HG_DK), jnp.float32), pltpu.VMEM((2, HG_DK, HG_DK), jnp.float32)],
        compiler_params=pltpu.CompilerParams(dimension_semantics=("parallel", "parallel"),
                                             vmem_limit_bytes=_VMEM_LIMIT),
        name="hgrn_mix",
    )(u_all, u_all, u_all, u_all, u_all, lbp, norm_g.reshape(1, HG_DK))


def _s5_weights(lam_re, lam_im, log_dt, b_re, b_im, c_re, c_im):
    t_len, f32 = S5_T, jnp.float32
    lre = jnp.minimum(lam_re, S5_MAX_RE)
    dt = jnp.exp(log_dt)[..., None]
    ar, ai = lre * dt, lam_im * dt
    k = jnp.arange(t_len + 1, dtype=f32)[:, None, None, None]
    mag = jnp.exp(k * ar)
    pre, pim = mag * jnp.cos(k * ai), mag * jnp.sin(k * ai)
    nr, ni, den = pre[1] - 1.0, pim[1], lre * lre + lam_im * lam_im
    cr, ci = (nr * lre + ni * lam_im) / den, (ni * lre - nr * lam_im) / den
    cbr = cr[..., None] * b_re - ci[..., None] * b_im
    cbi = cr[..., None] * b_im + ci[..., None] * b_re
    c_re_t, c_im_t = jnp.swapaxes(c_re, 1, 2), jnp.swapaxes(c_im, 1, 2)
    clre = c_re_t * pre[..., None] - c_im_t * pim[..., None]
    clim = c_re_t * pim[..., None] + c_im_t * pre[..., None]
    g = lam_re.shape[1]

    def w_dir(d, e):
        pr, pi = pre[e, d], pim[e, d]
        wr = pr[..., None] * cbr[d] - pi[..., None] * cbi[d]
        wi = pr[..., None] * cbi[d] + pi[..., None] * cbr[d]
        return [jnp.transpose(z, (1, 0, 3, 2)).reshape(g, t_len * S5_H, S5_P) for z in (wr, wi)]

    w = jnp.concatenate(w_dir(0, jnp.arange(t_len - 1, -1, -1)) + w_dir(1, jnp.arange(t_len)), axis=-1)
    dec = jnp.concatenate([pre[t_len, 0], pre[t_len, 0], -pim[t_len, 0], pim[t_len, 0],
                           pre[t_len, 1], pre[t_len, 1], -pim[t_len, 1], pim[t_len, 1]], axis=-1)[:, None, :]

    def m_dir(d, e):
        mr, mi = clre[e, d], -clim[e, d]
        return [jnp.transpose(z, (1, 2, 0, 3)).reshape(g, S5_P, t_len * S5_H) for z in (mr, mi)]

    m = jnp.concatenate(m_dir(0, jnp.arange(1, t_len + 1)) + m_dir(1, jnp.arange(t_len, 0, -1)), axis=1)
    kk = (jnp.einsum('kdgpo,dgpi->kdgoi', clre[:t_len], cbr, precision=_HI)
          - jnp.einsum('kdgpo,dgpi->kdgoi', clim[:t_len], cbi, precision=_HI))
    delta = jnp.arange(t_len)[None, :] - jnp.arange(t_len)[:, None]
    kf = kk[jnp.clip(delta, 0, t_len - 1), 0]
    kb = kk[jnp.clip(-delta, 0, t_len - 1), 1]
    dl = delta[:, :, None, None, None]
    toep = jnp.where(dl > 0, kf, 0.0) + jnp.where(dl < 0, kb, 0.0) + jnp.where(dl == 0, kf + kb, 0.0)
    toep = jnp.transpose(toep, (2, 0, 4, 1, 3)).reshape(g, t_len * S5_H, t_len * S5_H)
    return w, dec, m, toep


def _s5_body(x_ref, w_ref, dec_ref, m_ref, toep_ref, y_ref, z_scr, s_scr, *, bsz, n_ctx, n_lat):
    p2 = 2 * S5_P
    n_all = n_ctx + n_lat
    z_scr[...] = jnp.dot(x_ref[0], w_ref[0], preferred_element_type=jnp.float32)
    dec = dec_ref[0]
    a1f, a2f = dec[:, 0:p2], dec[:, p2:2 * p2]
    a1b, a2b = dec[:, 2 * p2:3 * p2], dec[:, 3 * p2:4 * p2]

    def fwd(c, s):
        r = pl.multiple_of(c * bsz, bsz)
        s_scr[pl.ds(r, bsz), 0:p2] = s
        return s * a1f + pltpu.roll(s, S5_P, 1) * a2f + z_scr[pl.ds(r, bsz), 0:p2]

    lax.fori_loop(0, n_all, fwd, jnp.zeros((bsz, p2), jnp.float32))

    def bwd_at(c, s):
        r = pl.multiple_of(c * bsz, bsz)
        s_scr[pl.ds(r, bsz), p2:2 * p2] = s
        return s * a1b + pltpu.roll(s, S5_P, 1) * a2b + z_scr[pl.ds(r, bsz), p2:2 * p2]

    s = lax.fori_loop(0, n_ctx, lambda i, s: bwd_at(n_ctx - 1 - i, s), jnp.zeros((bsz, p2), jnp.float32))
    lax.fori_loop(0, n_lat, lambda i, s: bwd_at(n_all - 1 - i, s), s)
    r0 = n_ctx * bsz
    y_ref[0] = (jnp.dot(x_ref[0, r0:, :], toep_ref[0], preferred_element_type=jnp.float32)
                + jnp.dot(s_scr[r0:, :].astype(jnp.bfloat16), m_ref[0], preferred_element_type=jnp.float32))


def _s5_mix(u_c, u_l, lam_re, lam_im, log_dt, b_re, b_im, c_re, c_im):
    bsz, l_ctx, gh = u_c.shape
    l_lat = u_l.shape[1]
    g, t_len = gh // S5_H, S5_T
    n_ctx, n_lat = l_ctx // t_len, l_lat // t_len
    w, dec, m, toep = _s5_weights(lam_re, lam_im, log_dt, b_re, b_im, c_re, c_im)

    def chunked(u, n):
        return jnp.transpose(u.reshape(bsz, n, t_len, g, S5_H), (3, 1, 0, 2, 4)).reshape(g, n * bsz, t_len * S5_H)

    x = jnp.concatenate([chunked(u_c, n_ctx), chunked(u_l, n_lat)], axis=1).astype(jnp.bfloat16)
    rows, k = (n_ctx + n_lat) * bsz, t_len * S5_H
    y = pl.pallas_call(
        partial(_s5_body, bsz=bsz, n_ctx=n_ctx, n_lat=n_lat),
        grid=(g,),
        in_specs=[pl.BlockSpec((1, rows, k), lambda i: (i, 0, 0)),
                  pl.BlockSpec((1, k, 4 * S5_P), lambda i: (i, 0, 0)),
                  pl.BlockSpec((1, 1, 8 * S5_P), lambda i: (i, 0, 0)),
                  pl.BlockSpec((1, 4 * S5_P, k), lambda i: (i, 0, 0)),
                  pl.BlockSpec((1, k, k), lambda i: (i, 0, 0))],
        out_specs=pl.BlockSpec((1, n_lat * bsz, k), lambda i: (i, 0, 0)),
        out_shape=jax.ShapeDtypeStruct((g, n_lat * bsz, k), jnp.float32),
        scratch_shapes=[pltpu.VMEM((rows, 4 * S5_P), jnp.float32), pltpu.VMEM((rows, 4 * S5_P), jnp.float32)],
        compiler_params=pltpu.CompilerParams(dimension_semantics=("parallel",), vmem_limit_bytes=_VMEM_LIMIT),
        name="s5_mix",
    )(x, w.astype(jnp.bfloat16), dec, m.astype(jnp.bfloat16), toep.astype(jnp.bfloat16))
    return jnp.transpose(y.reshape(g, n_lat, bsz, t_len, S5_H), (2, 1, 3, 0, 4)).reshape(bsz, l_lat, gh)


def _s5_out(y, u, d_skip, glu_w, glu_b):
    y = jax.nn.gelu(y + u * d_skip.reshape(-1))
    return y * jax.nn.sigmoid(_mm3(y, glu_w) + glu_b)


def _odd_mixer(u, l_ctx, lb, hg_norm_g, lam_re, lam_im, log_dt, b_re, b_im, c_re, c_im, d_skip, glu_w, glu_b):
    w = W_HALF
    hg_l = _hgrn_mix(u, lb, hg_norm_g, l_ctx)
    s_c, s_l = u[:, :l_ctx, 3 * w:4 * w], u[:, l_ctx:, 3 * w:4 * w]
    y_l = _s5_mix(s_c, s_l, lam_re, lam_im, log_dt, b_re, b_im, c_re, c_im)
    return hg_l, _s5_out(y_l, s_l, d_skip, glu_w, glu_b)


def _router_body(x_ref, g_ref, sh_ref, sc_ref, wr_ref, h_ref, lg_ref):
    x = x_ref[0]
    h = x * lax.rsqrt(jnp.mean(x * x, axis=-1, keepdims=True) + EPS) * g_ref[...]
    h = h * (1.0 + sc_ref[0]) + sh_ref[0]
    hb = h.astype(jnp.bfloat16)
    lg_ref[0] = jnp.dot(hb, wr_ref[...], preferred_element_type=jnp.float32)
    bits = pltpu.bitcast(hb.astype(jnp.float32), jnp.uint32)
    half = bits.shape[1] // 2
    h_ref[0] = (bits[:, :half] >> 16) | (bits[:, half:] & jnp.uint32(0xFFFF0000))


def _router(x, norm_g, shift, scale, w_router):
    bsz, n, d = x.shape
    tm = min(n, 512)
    vec = pl.BlockSpec((1, 1, d), lambda b, i: (b, 0, 0))
    return pl.pallas_call(
        _router_body,
        grid=(bsz, n // tm),
        in_specs=[pl.BlockSpec((1, tm, d), lambda b, i: (b, i, 0)),
                  pl.BlockSpec((1, d), lambda b, i: (0, 0)), vec, vec,
                  pl.BlockSpec((d, N_EXPERTS), lambda b, i: (0, 0))],
        out_specs=[pl.BlockSpec((1, tm, d // 2), lambda b, i: (b, i, 0)),
                   pl.BlockSpec((1, tm, N_EXPERTS), lambda b, i: (b, i, 0))],
        out_shape=[jax.ShapeDtypeStruct((bsz, n, d // 2), jnp.uint32),
                   jax.ShapeDtypeStruct((bsz, n, N_EXPERTS), jnp.float32)],
        compiler_params=pltpu.CompilerParams(dimension_semantics=("parallel", "parallel"),
                                             vmem_limit_bytes=_VMEM_LIMIT),
        name="moe_router",
    )(x, norm_g.reshape(1, d), jnp.broadcast_to(shift, (bsz, 1, d)), jnp.broadcast_to(scale, (bsz, 1, d)),
      w_router.astype(jnp.bfloat16))


def _expert_body(idx_ref, h_ref, gate_ref, mod_ref, wg_ref, wu_ref, wd_ref, y_ref, x_scr, acc_scr, *, cap):
    f = pl.program_id(2)

    @pl.when(f == 0)
    def _():
        def grab(g, carry):
            base = pl.multiple_of(g * _SUBLANES, _SUBLANES)
            for r in range(_SUBLANES):
                t = idx_ref[0, 0, base + r]
                w = h_ref[0, pl.ds(t, 1), :]
                lo = pltpu.bitcast(w << 16, jnp.float32)
                hi = pltpu.bitcast(w & jnp.uint32(0xFFFF0000), jnp.float32)
                x_scr[pl.ds(base + r, 1), :] = jnp.concatenate([lo, hi], axis=1)
            return carry
        lax.fori_loop(0, cap // _SUBLANES, grab, 0)

    xb = x_scr[...].astype(jnp.bfloat16)
    a = jnp.dot(xb, wg_ref[0], preferred_element_type=jnp.float32)
    u = jnp.dot(xb, wu_ref[0], preferred_element_type=jnp.float32)
    hh = (a * (1.0 / (1.0 + jnp.exp(-a))) * u).astype(jnp.bfloat16)
    part = jnp.dot(hh, wd_ref[0], preferred_element_type=jnp.float32)

    @pl.when(f == 0)
    def _():
        acc_scr[...] = part

    @pl.when(f > 0)
    def _():
        acc_scr[...] += part

    @pl.when(f == pl.num_programs(2) - 1)
    def _():
        y_ref[0] = acc_scr[...] * gate_ref[0] * mod_ref[0]


def _expert_ffn(h32, idx, gate, mod, w_gate, w_up, w_down):
    bsz, n, dh = h32.shape
    d = 2 * dh
    cap = idx.shape[-1]
    e, _, ff = w_gate.shape
    nf = 3
    tf = ff // nf
    be = lambda b, k, f: (b * e + k, 0, 0)
    return pl.pallas_call(
        partial(_expert_body, cap=cap),
        grid=(bsz, e, nf),
        in_specs=[pl.BlockSpec((1, 1, cap), be, memory_space=pltpu.SMEM),
                  pl.BlockSpec((1, n, dh), lambda b, k, f: (b, 0, 0)),
                  pl.BlockSpec((1, cap, 1), be),
                  pl.BlockSpec((1, 1, d), lambda b, k, f: (b, 0, 0)),
                  pl.BlockSpec((1, d, tf), lambda b, k, f: (k, 0, f)),
                  pl.BlockSpec((1, d, tf), lambda b, k, f: (k, 0, f)),
                  pl.BlockSpec((1, tf, d), lambda b, k, f: (k, f, 0))],
        out_specs=pl.BlockSpec((1, cap, d), be),
        out_shape=jax.ShapeDtypeStruct((bsz * e, cap, d), jnp.float32),
        scratch_shapes=[pltpu.VMEM((cap, d), jnp.float32), pltpu.VMEM((cap, d), jnp.float32)],
        compiler_params=pltpu.CompilerParams(dimension_semantics=("parallel", "parallel", "arbitrary"),
                                             vmem_limit_bytes=_VMEM_LIMIT),
        name="moe_expert",
    )(idx.reshape(bsz * e, 1, cap), h32, gate.reshape(bsz * e, cap, 1), jnp.broadcast_to(mod, (bsz, 1, d)),
      w_gate.astype(jnp.bfloat16), w_up.astype(jnp.bfloat16), w_down.astype(jnp.bfloat16))


def _combine_body(idx_ref, y_ref, o_ref, *, cap):
    @pl.when(pl.program_id(1) == 0)
    def _():
        o_ref[...] = jnp.zeros_like(o_ref)

    def group(g, carry):
        base = pl.multiple_of(g * _SUBLANES, _SUBLANES)
        ys = y_ref[0, pl.ds(base, _SUBLANES), :]
        ts = [idx_ref[0, 0, base + r] for r in range(_SUBLANES)]
        cur = [o_ref[0, pl.ds(t, 1), :] for t in ts]
        for r in range(_SUBLANES):
            o_ref[0, pl.ds(ts[r], 1), :] = cur[r] + ys[r:r + 1, :]
        return carry

    lax.fori_loop(0, cap // _SUBLANES, group, 0)


def _combine(y, idx, n):
    be, cap, d = y.shape
    bsz = be // N_EXPERTS
    return pl.pallas_call(
        partial(_combine_body, cap=cap),
        grid=(bsz, N_EXPERTS),
        in_specs=[pl.BlockSpec((1, 1, cap), lambda b, k: (b * N_EXPERTS + k, 0, 0), memory_space=pltpu.SMEM),
                  pl.BlockSpec((1, cap, d), lambda b, k: (b * N_EXPERTS + k, 0, 0))],
        out_specs=pl.BlockSpec((1, n, d), lambda b, k: (b, 0, 0)),
        out_shape=jax.ShapeDtypeStruct((bsz, n, d), jnp.float32),
        compiler_params=pltpu.CompilerParams(dimension_semantics=("parallel", "arbitrary"),
                                             vmem_limit_bytes=_VMEM_LIMIT),
        name="moe_combine",
    )(idx.reshape(be, 1, cap), y)


def _moe(x, norm_g, shift, scale, mod, w_router, w_gate, w_up, w_down):
    bsz, n, d = x.shape
    cap = max(1, (EC_CAPACITY * n) // N_EXPERTS)
    h32, logits = _router(x, norm_g, shift, scale, w_router)
    aff = jax.nn.softmax(logits, axis=-1)
    gate, idx = lax.top_k(jnp.swapaxes(aff, 1, 2), cap)
    y = _expert_ffn(h32, idx, gate, mod, w_gate, w_up, w_down)
    return _combine(y, idx, n)


def kernel(x, c, ctx, c_ctx, mod_w, mod_b, norm1_g, norm2_g, out_w, even_in_w, lru_conv_w, lru_conv_b, lru_wa, lru_ba, lru_wx, lru_bx, lru_lam, hy_conv_w, hy_conv_b, hy_w1, hy_b1, hy_w2, hy_b2, hy_w3, hy_freq, hy_skip, odd_in_w, hg_lb, hg_norm_g, s5_lam_re, s5_lam_im, s5_log_dt, s5_b_re, s5_b_im, s5_c_re, s5_c_im, s5_d, s5_glu_w, s5_glu_b, router_w, ex_w_gate, ex_w_up, ex_w_down, final_g):
    rows = x.shape[1] // GRID_W
    l_ctx = ctx.shape[1]
    p = jax.nn.softmax(hg_lb, axis=0)
    lb_all = jnp.clip(jnp.cumsum(p, axis=0) - p[0], 0.0, 1.0 - 1e-4)
    sc_lat = jax.nn.silu(c)
    sc_ctx = jax.nn.silu(c_ctx)[None]
    for i in range(DEPTH):
        need_ctx = i < DEPTH - 1
        j = i // 2
        ml = [t[:, None, :] for t in jnp.split(sc_lat @ mod_w[i] + mod_b[i], 6, axis=-1)]
        mc = [t[:, None, :] for t in jnp.split(sc_ctx @ mod_w[i] + mod_b[i], 6, axis=-1)]
        shift, scale, gate = (_two_part(mc[k], ml[k]) for k in range(3))
        if i % 2 == 0:
            x_all = jnp.concatenate([ctx, x], axis=1)
            u = _norm_mm(x_all, norm1_g[i], shift, scale, even_in_w[j], l_ctx)
            ya, yb = _even_mixer(u, l_ctx, lru_conv_w[j], lru_conv_b[j], lru_wa[j], lru_ba[j], lru_wx[j],
                                 lru_bx[j], lru_lam[j], hy_conv_w[j], hy_conv_b[j], hy_w1[j], hy_b1[j],
                                 hy_w2[j], hy_b2[j], hy_w3[j], hy_freq[j], hy_skip[j])
            x_all = _proj_res(ya, yb, out_w[i], x_all, gate, l_ctx, 0)
            ctx, x = x_all[:, :l_ctx], x_all[:, l_ctx:]
        else:
            assert not need_ctx
            x_all = jnp.concatenate([ctx, _to_col_major(x, rows)], axis=1)
            u = _norm_mm(x_all, norm1_g[i], shift, scale, odd_in_w[j], l_ctx)
            hg_l, s5_l = _odd_mixer(u, l_ctx, lb_all[i], hg_norm_g[j], s5_lam_re[j], s5_lam_im[j],
                                    s5_log_dt[j], s5_b_re[j], s5_b_im[j], s5_c_re[j], s5_c_im[j], s5_d[j],
                                    s5_glu_w[j], s5_glu_b[j])
            x = _to_row_major(_proj_res(hg_l, s5_l, out_w[i], x_all, gate, l_ctx, l_ctx), rows)
        ex = (router_w[i], ex_w_gate[i], ex_w_up[i], ex_w_down[i])
        x = x + _moe(x, norm2_g[i], ml[3], ml[4], ml[5], *ex)
        if need_ctx:
            ctx = ctx + _moe(ctx, norm2_g[i], mc[3], mc[4], mc[5], *ex)
    return _rms_norm(x, final_g)
```

```python
import math
from functools import partial

import numpy as np
import jax
import jax.numpy as jnp
from jax import lax
from jax.experimental import pallas as pl
from jax.experimental.pallas import tpu as pltpu

D_MODEL = 1024
DEPTH = 2
GRID_W = 64
W_HALF = D_MODEL // 2
EPS = 1e-6
LRU_HEADS = 8
LRU_HEAD_DIM = W_HALF // LRU_HEADS
LRU_C = 8.0
HY_ORDER = 2
HY_BANDS = 16
HY_MAX_DECAY = math.log(1e-2) / 0.3
HY_MIN_DECAY = math.log(1e-2) / 1.5
HG_HEADS = 4
HG_DK = W_HALF // HG_HEADS
HG_C = 16
HG_SUB = 4
HG_TB = 256
S5_H = 16
S5_P = 64
S5_MAX_RE = -1e-4
S5_T = 32
_DFT_LOW = 64
N_EXPERTS = 16
EC_CAPACITY = 2

_SUBLANES = 8
_VMEM_LIMIT = 48 * 1024 * 1024
_VMEM_LIMIT_BIG = 56 * 1024 * 1024
_HI = lax.Precision.HIGHEST
_NEG = -1e30


def _mm_body(a_ref, b_ref, o_ref):
    o_ref[...] = jnp.dot(a_ref[...].astype(jnp.bfloat16), b_ref[...],
                         preferred_element_type=jnp.float32)


def _pick(n, pref):
    for t in pref:
        if n % t == 0:
            return t
    return n


def _mm(a, b):
    m, k = a.shape
    n = b.shape[1]
    tm = _pick(m, (512, 256, 128, 64, 32, 16, 8))
    tn = _pick(n, (512, 256, 128))
    return pl.pallas_call(
        _mm_body,
        grid=(m // tm, n // tn),
        in_specs=[pl.BlockSpec((tm, k), lambda i, j: (i, 0)),
                  pl.BlockSpec((k, tn), lambda i, j: (0, j))],
        out_specs=pl.BlockSpec((tm, tn), lambda i, j: (i, j)),
        out_shape=jax.ShapeDtypeStruct((m, n), jnp.float32),
        compiler_params=pltpu.CompilerParams(
            dimension_semantics=("parallel", "parallel"), vmem_limit_bytes=_VMEM_LIMIT),
        name="mm",
    )(a, b.astype(jnp.bfloat16))


def _mm3(a, b):
    bsz, l, k = a.shape
    return _mm(a.reshape(bsz * l, k), b).reshape(bsz, l, b.shape[1])


def _add_norm_body(x_ref, m_ref, g_ref, o_ref):
    x = x_ref[...] + m_ref[...]
    o_ref[...] = x * lax.rsqrt(jnp.mean(x * x, axis=-1, keepdims=True) + EPS) * g_ref[...]


def _add_norm(x, m, g):
    bsz, n, d = x.shape
    tm = min(n, 512)
    blk = pl.BlockSpec((1, tm, d), lambda b, i: (b, i, 0))
    return pl.pallas_call(
        _add_norm_body,
        grid=(bsz, n // tm),
        in_specs=[blk, blk, pl.BlockSpec((1, 1, d), lambda b, i: (0, 0, 0))],
        out_specs=blk,
        out_shape=jax.ShapeDtypeStruct(x.shape, jnp.float32),
        compiler_params=pltpu.CompilerParams(dimension_semantics=("parallel", "parallel")),
        name="add_norm",
    )(x, m, g.reshape(1, 1, d))


def _two_part(t_ctx, t_lat):
    bsz, _, d = t_lat.shape
    return jnp.stack([jnp.broadcast_to(t_ctx, (bsz, 1, d)), t_lat], axis=1).reshape(2 * bsz, 1, d)


def _norm_mm_body(x_ref, g_ref, sh_ref, sc_ref, w_ref, o_ref):
    x = x_ref[0]
    h = x * lax.rsqrt(jnp.mean(x * x, axis=-1, keepdims=True) + EPS) * g_ref[...]
    h = h * (1.0 + sc_ref[0]) + sh_ref[0]
    o_ref[0] = jnp.dot(h.astype(jnp.bfloat16), w_ref[...], preferred_element_type=jnp.float32)


def _norm_mm(x_all, norm_g, shift, scale, w, l_ctx):
    bsz, l_tot, d = x_all.shape
    n = w.shape[1]
    tm = l_ctx
    vec = pl.BlockSpec((1, 1, d), lambda b, j: (2 * b + jnp.minimum(j, 1), 0, 0))
    return pl.pallas_call(
        _norm_mm_body,
        grid=(bsz, l_tot // tm),
        in_specs=[pl.BlockSpec((1, tm, d), lambda b, j: (b, j, 0)),
                  pl.BlockSpec((1, d), lambda b, j: (0, 0)), vec, vec,
                  pl.BlockSpec((d, n), lambda b, j: (0, 0))],
        out_specs=pl.BlockSpec((1, tm, n), lambda b, j: (b, j, 0)),
        out_shape=jax.ShapeDtypeStruct((bsz, l_tot, n), jnp.float32),
        compiler_params=pltpu.CompilerParams(dimension_semantics=("parallel", "parallel"),
                                             vmem_limit_bytes=_VMEM_LIMIT),
        name="norm_mm",
    )(x_all, norm_g.reshape(1, d), shift, scale, w.astype(jnp.bfloat16))


def _proj_res_body(ya_ref, yb_ref, wa_ref, wb_ref, x_ref, gt_ref, o_ref):
    y = (jnp.dot(ya_ref[0].astype(jnp.bfloat16), wa_ref[...], preferred_element_type=jnp.float32)
         + jnp.dot(yb_ref[0].astype(jnp.bfloat16), wb_ref[...], preferred_element_type=jnp.float32))
    o_ref[0] = x_ref[0] + gt_ref[0] * y


def _proj_res_lru_body(hf_ref, hb_ref, u_ref, yb_ref, wa_ref, wb_ref, x_ref, gt_ref, o_ref):
    ya = (hf_ref[0] + hb_ref[0]) * _gelu_tanh(u_ref[0])
    y = (jnp.dot(ya.astype(jnp.bfloat16), wa_ref[...], preferred_element_type=jnp.float32)
         + jnp.dot(yb_ref[0].astype(jnp.bfloat16), wb_ref[...], preferred_element_type=jnp.float32))
    o_ref[0] = x_ref[0] + gt_ref[0] * y


def _proj_res_lru(h_f, h_b, u_all, yb, w, x_all, gate, l_ctx):
    bsz, l_tot, ka = h_f.shape
    d = w.shape[1]
    tm = l_ctx
    blk = lambda k, c=0: pl.BlockSpec((1, tm, k), lambda b, j: (b, j, c))
    return pl.pallas_call(
        _proj_res_lru_body,
        grid=(bsz, l_tot // tm),
        in_specs=[blk(ka), blk(ka), blk(ka, 1), blk(yb.shape[2]),
                  pl.BlockSpec((ka, d), lambda b, j: (0, 0)), pl.BlockSpec((yb.shape[2], d), lambda b, j: (0, 0)),
                  blk(d), pl.BlockSpec((1, 1, d), lambda b, j: (2 * b + jnp.minimum(j, 1), 0, 0))],
        out_specs=blk(d),
        out_shape=jax.ShapeDtypeStruct((bsz, l_tot, d), jnp.float32),
        compiler_params=pltpu.CompilerParams(dimension_semantics=("parallel", "parallel"),
                                             vmem_limit_bytes=_VMEM_LIMIT),
        name="proj_res_lru",
    )(h_f, h_b, u_all, yb, w[:ka].astype(jnp.bfloat16), w[ka:].astype(jnp.bfloat16), x_all, gate)


def _proj_res(ya, yb, w, x_all, gate, l_ctx, row0):
    bsz, l_y, ka = ya.shape
    d = w.shape[1]
    tm = l_ctx
    j0 = row0 // tm
    yblk = lambda k: pl.BlockSpec((1, tm, k), lambda b, j: (b, j, 0))
    return pl.pallas_call(
        _proj_res_body,
        grid=(bsz, l_y // tm),
        in_specs=[yblk(ka), yblk(yb.shape[2]),
                  pl.BlockSpec((ka, d), lambda b, j: (0, 0)), pl.BlockSpec((yb.shape[2], d), lambda b, j: (0, 0)),
                  pl.BlockSpec((1, tm, d), lambda b, j: (b, j + j0, 0)),
                  pl.BlockSpec((1, 1, d), lambda b, j: (2 * b + jnp.minimum(j + j0, 1), 0, 0))],
        out_specs=pl.BlockSpec((1, tm, d), lambda b, j: (b, j, 0)),
        out_shape=jax.ShapeDtypeStruct((bsz, l_y, d), jnp.float32),
        compiler_params=pltpu.CompilerParams(dimension_semantics=("parallel", "parallel"),
                                             vmem_limit_bytes=_VMEM_LIMIT),
        name="proj_res",
    )(ya, yb, w[:ka].astype(jnp.bfloat16), w[ka:].astype(jnp.bfloat16), x_all, gate)


def _gelu_tanh(x):
    return 0.5 * x * (1.0 + jnp.tanh(math.sqrt(2.0 / math.pi) * (x + 0.044715 * (x * x * x))))


def _dwconv(x, w, b):
    k = w.shape[0]
    left = (k - 1) // 2
    l = x.shape[1]
    xp = jnp.pad(x, ((0, 0), (left, k - 1 - left), (0, 0)))
    y = b
    for i in range(k):
        y = y + xp[:, i:i + l, :] * w[i]
    return y


def _to_col_major(t, rows):
    bsz, l, ch = t.shape
    return t.reshape(bsz, rows, GRID_W, ch).transpose(0, 2, 1, 3).reshape(bsz, l, ch)


def _to_row_major(t, rows):
    bsz, l, ch = t.shape
    return t.reshape(bsz, GRID_W, rows, ch).transpose(0, 2, 1, 3).reshape(bsz, l, ch)


def _lru_body(xf_ref, xb_ref, wa_ref, wx_ref, par_ref, hf_ref, hb_ref,
              af_scr, bf_scr, ab_scr, bb_scr, sf_scr, sb_scr, *, tb):
    @pl.when(pl.program_id(1) == 0)
    def _():
        sf_scr[...] = jnp.zeros_like(sf_scr)
        sb_scr[...] = jnp.zeros_like(sb_scr)

    def coeffs(x, d, a_scr, b_scr):
        xb = x.astype(jnp.bfloat16)
        r = jnp.dot(xb, wa_ref[d], preferred_element_type=jnp.float32) + par_ref[d, 0:1, :]
        i = jnp.dot(xb, wx_ref[d], preferred_element_type=jnp.float32) + par_ref[d, 1:2, :]
        r = 1.0 / (1.0 + jnp.exp(-r))
        i = 1.0 / (1.0 + jnp.exp(-i))
        log_a = -LRU_C * r * par_ref[d, 2:3, :]
        a = jnp.exp(log_a)
        a_scr[...] = a
        b_scr[...] = jnp.sqrt(-jnp.tanh(log_a) * (a * a + 1.0)) * i * x

    coeffs(xf_ref[0], 0, af_scr, bf_scr)
    coeffs(xb_ref[0], 1, ab_scr, bb_scr)

    def tile(i, carry):
        hf, hb = carry
        base_f = pl.multiple_of(i * _SUBLANES, _SUBLANES)
        base_b = pl.multiple_of(tb - _SUBLANES - i * _SUBLANES, _SUBLANES)
        for r in range(_SUBLANES):
            rf = pl.ds(base_f + r, 1)
            hf = af_scr[rf, :] * hf + bf_scr[rf, :]
            hf_ref[0, rf, :] = hf
            rb = pl.ds(base_b + (_SUBLANES - 1 - r), 1)
            hb = ab_scr[rb, :] * hb + bb_scr[rb, :]
            hb_ref[0, rb, :] = hb
        return hf, hb

    hf, hb = lax.fori_loop(0, tb // _SUBLANES, tile, (sf_scr[...], sb_scr[...]))
    sf_scr[...] = hf
    sb_scr[...] = hb


def _block_diag(w):
    two, h, n, _ = w.shape
    eye = jnp.eye(h, dtype=w.dtype)
    return (w[:, :, :, None, :] * eye[None, :, None, :, None]).reshape(two, h * n, h * n)


def _lru_scan(xc, l_ctx, wa, ba, wx, bx, lam):
    bsz, l_tot, ch = xc.shape
    tb = l_ctx
    nb = l_tot // tb
    par = jnp.stack([ba.reshape(2, ch), bx.reshape(2, ch), jax.nn.softplus(-lam).reshape(2, ch)], axis=1)
    par = jnp.concatenate([par, jnp.zeros((2, _SUBLANES - 3, ch), jnp.float32)], axis=1)
    fwd = pl.BlockSpec((1, tb, ch), lambda b, j: (b, j, 0))
    bwd = pl.BlockSpec((1, tb, ch), lambda b, j: (b, jnp.where(j == 0, 0, nb - j), 0))
    wspec = pl.BlockSpec((2, ch, ch), lambda b, j: (0, 0, 0))
    return pl.pallas_call(
        partial(_lru_body, tb=tb),
        grid=(bsz, nb),
        in_specs=[fwd, bwd, wspec, wspec, pl.BlockSpec((2, _SUBLANES, ch), lambda b, j: (0, 0, 0))],
        out_specs=[fwd, bwd],
        out_shape=[jax.ShapeDtypeStruct(xc.shape, jnp.float32)] * 2,
        scratch_shapes=[pltpu.VMEM((tb, ch), jnp.float32)] * 4 + [pltpu.VMEM((1, ch), jnp.float32)] * 2,
        compiler_params=pltpu.CompilerParams(dimension_semantics=("parallel", "arbitrary")),
        name="lru_scan",
    )(xc, xc, _block_diag(wa).astype(jnp.bfloat16), _block_diag(wx).astype(jnp.bfloat16), par)


def _rglru_mixer(xr, l_ctx, conv_w, conv_b, wa, ba, wx, bx, lam):
    xc = jnp.concatenate([_dwconv(xr[:, :l_ctx], conv_w, conv_b), _dwconv(xr[:, l_ctx:], conv_w, conv_b)], axis=1)
    return _lru_scan(xc, l_ctx, wa, ba, wx, bx, lam)


def _hyena_filters(l, w1, b1, w2, b2, w3, freq):
    t = jnp.arange(l, dtype=jnp.float32)
    t_unit = t / max(l - 1, 1)
    bands = jnp.linspace(1e-4, HY_BANDS - 1, HY_BANDS, dtype=jnp.float32)
    ang = (2.0 * math.pi / l) * t[:, None] * bands[None, :]
    z = jnp.concatenate([t_unit[:, None], jnp.cos(ang), -jnp.sin(ang)], axis=-1)
    h = jnp.sin(freq * (z @ w1 + b1))
    h = jnp.sin(freq * (h @ w2 + b2))
    h = h @ w3
    dist = jnp.abs(t - (l // 2)) / (l / 2.0)
    deltas = jnp.abs(jnp.linspace(HY_MIN_DECAY, HY_MAX_DECAY, HY_ORDER * W_HALF, dtype=jnp.float32))
    h = h * jnp.exp(-dist[:, None] * deltas[None, :])
    h = h * lax.rsqrt(jnp.sum(h * h, axis=0, keepdims=True) + EPS)
    return h


def _dft_mats(l, tf):
    n = 2 * l
    nl = _DFT_LOW
    nh = l // nl
    t = jnp.arange(l, dtype=jnp.int32)
    w = 2.0 * math.pi / n

    def tables(tv):
        ah = (((nl * jnp.arange(nh, dtype=jnp.int32))[:, None] * tv[None, :]) % n).astype(jnp.float32) * w
        al = ((jnp.arange(nl, dtype=jnp.int32)[:, None] * tv[None, :]) % n).astype(jnp.float32) * w
        return jnp.cos(ah), jnp.sin(ah), jnp.cos(al), jnp.sin(al)

    ch, sh, cl, sl = tables(t)
    f_re = (ch[:, None, :] * cl[None] - sh[:, None, :] * sl[None]).reshape(l, l)
    f_im = -(sh[:, None, :] * cl[None] + ch[:, None, :] * sl[None]).reshape(l, l)
    f_im = f_im.at[0].set(jnp.where(t % 2 == 0, 1.0, -1.0))
    f = jnp.concatenate([f_re.reshape(l // tf, tf, l), f_im.reshape(l // tf, tf, l)], axis=1).reshape(n, l)
    tp = t + l // 2
    ch, sh, cl, sl = (z.T for z in tables(tp))
    g_re = (2.0 / n) * (ch[:, :, None] * cl[:, None, :] - sh[:, :, None] * sl[:, None, :]).reshape(l, l)
    g_im = (-2.0 / n) * (sh[:, :, None] * cl[:, None, :] + ch[:, :, None] * sl[:, None, :]).reshape(l, l)
    g_re = g_re.at[:, 0].set(1.0 / n)
    g_im = g_im.at[:, 0].set(jnp.where(tp % 2 == 0, 1.0, -1.0) / n)
    g = jnp.stack([g_re.reshape(l, l // tf, tf), g_im.reshape(l, l // tf, tf)], axis=2).reshape(l, n)
    return f.astype(jnp.bfloat16), g.astype(jnp.bfloat16)


def _spec_body(f_ref, u_ref, a_ref, b_ref, d_ref, y_ref, *, tf):
    acc = jnp.dot(f_ref[...], u_ref[0], preferred_element_type=jnp.float32)
    xr, xi = acc[:tf], acc[tf:]
    y_ref[0, :tf, :] = (xr * a_ref[...] - xi * b_ref[...]).astype(jnp.bfloat16)
    y_ref[0, tf:, :] = (xr * b_ref[...] + xi * d_ref[...]).astype(jnp.bfloat16)


def _spec_mul(f, u, a, b, d, tf):
    bsz, l, c = u.shape
    n = 2 * l
    coef = pl.BlockSpec((tf, c), lambda i, bb: (i, 0))
    return pl.pallas_call(
        partial(_spec_body, tf=tf),
        grid=(l // tf, bsz),
        in_specs=[pl.BlockSpec((2 * tf, l), lambda i, bb: (i, 0)),
                  pl.BlockSpec((1, l, c), lambda i, bb: (bb, 0, 0)), coef, coef, coef],
        out_specs=pl.BlockSpec((1, 2 * tf, c), lambda i, bb: (bb, i, 0)),
        out_shape=jax.ShapeDtypeStruct((bsz, n, c), jnp.bfloat16),
        compiler_params=pltpu.CompilerParams(dimension_semantics=("parallel", "parallel"),
                                             vmem_limit_bytes=_VMEM_LIMIT),
        name="hy_spec",
    )(f, u, a, b, d)


def _inv_body(g_ref, y_ref, u_ref, x_ref, skip_ref, z_ref):
    acc = jnp.dot(g_ref[...], y_ref[0], preferred_element_type=jnp.float32)
    z_ref[0] = x_ref[0] * (acc + u_ref[0] * skip_ref[...])


def _inv_gate(g, y, u, xg, skip, tm):
    bsz, l, c = u.shape
    blk = pl.BlockSpec((1, tm, c), lambda i, bb: (bb, i, 0))
    return pl.pallas_call(
        _inv_body,
        grid=(l // tm, bsz),
        in_specs=[pl.BlockSpec((tm, 2 * l), lambda i, bb: (i, 0)),
                  pl.BlockSpec((1, 2 * l, c), lambda i, bb: (bb, 0, 0)), blk, blk,
                  pl.BlockSpec((1, c), lambda i, bb: (0, 0))],
        out_specs=blk,
        out_shape=jax.ShapeDtypeStruct((bsz, l, c), jnp.float32),
        compiler_params=pltpu.CompilerParams(dimension_semantics=("parallel", "parallel"),
                                             vmem_limit_bytes=_VMEM_LIMIT),
        name="hy_inv",
    )(g, y, u, xg, skip.reshape(1, c))


def _hyena_conv2(v, x1, x2, filt, skip):
    bsz, l, c = v.shape
    tf = min(512, l // 2)
    f, g = _dft_mats(l, tf)
    hs = _mm(f, filt)
    hs = hs.reshape(l // tf, 2, tf, 2 * c)
    h_re, h_im = hs[:, 0].reshape(l, 2 * c), hs[:, 1].reshape(l, 2 * c)
    slot0 = (jnp.arange(l) == 0)[:, None]
    a = h_re
    b = jnp.where(slot0, 0.0, h_im)
    d = jnp.where(slot0, h_im, h_re)
    z = v
    for o, xg in enumerate((x1, x2)):
        cs = slice(o * c, (o + 1) * c)
        y = _spec_mul(f, z.astype(jnp.bfloat16), a[:, cs], b[:, cs], d[:, cs], tf)
        z = _inv_gate(g, y, z, xg, skip[o], min(512, l))
    return z


def _hyena(u, conv_w, conv_b, w1, b1, w2, b2, w3, freq, skip):
    l = u.shape[1]
    u = _dwconv(u, conv_w, conv_b)
    v, x1, x2 = jnp.split(u, 3, axis=-1)
    filt = _hyena_filters(l, w1, b1, w2, b2, w3, freq)
    return _hyena_conv2(v, x1, x2, filt, skip)


def _even_mixer(u, l_ctx, lru_conv_w, lru_conv_b, lru_wa, lru_ba, lru_wx, lru_bx,
                lru_lam, hy_conv_w, hy_conv_b, hy_w1, hy_b1, hy_w2, hy_b2, hy_w3, hy_freq, hy_skip):
    w = W_HALF
    h_f, h_b = _rglru_mixer(u[..., :w], l_ctx, lru_conv_w, lru_conv_b, lru_wa, lru_ba, lru_wx, lru_bx, lru_lam)
    hy = (hy_conv_w, hy_conv_b, hy_w1, hy_b1, hy_w2, hy_b2, hy_w3, hy_freq, hy_skip)
    yb = jnp.concatenate([_hyena(u[:, :l_ctx, 2 * w:], *hy), _hyena(u[:, l_ctx:, 2 * w:], *hy)], axis=1)
    return h_f, h_b, yb


def _split3(x):
    hi = x.astype(jnp.bfloat16)
    r1 = x - hi.astype(jnp.float32)
    mid = r1.astype(jnp.bfloat16)
    lo = (r1 - mid.astype(jnp.float32)).astype(jnp.bfloat16)
    return jnp.concatenate([hi, mid, lo], axis=1)


def _sum3(r, c):
    return r[:, 0:c] + r[:, c:2 * c] + r[:, 2 * c:3 * c]


def _log_sigmoid(x):
    return jnp.minimum(x, 0.0) - jnp.log1p(jnp.exp(-jnp.abs(x)))


def _hgrn_consts():
    tb, c, sb = HG_TB, HG_C, HG_SUB
    r = np.arange(tb)
    same = (r[:, None] // c) == (r[None, :] // c)
    same4 = (r[:, None] // sb) == (r[None, :] // sb)
    le = r[None, :] <= r[:, None]
    sub = (r % c) // sb
    cum, sel, msk = [], [], []
    for d in range(2):
        tri = le if d == 0 else le.T
        cum.append(np.concatenate([same & tri, same, same4 & tri], axis=0))
        ss, mm = [], []
        for i in range(c // sb - 1):
            if d == 0:
                ref_row = (r // c) * c + sb * (i + 1) - 1
                mm.append(same & (sub[:, None] == i + 1) & (sub[None, :] <= i))
            else:
                ref_row = (r // c) * c + sb * (i + 1)
                mm.append(same & (sub[:, None] == i) & (sub[None, :] > i))
            ss.append(r[None, :] == ref_row[:, None])
        sel.append(np.concatenate(ss, axis=0))
        msk.append(np.stack(mm))
    as_bf16 = lambda z: jnp.asarray(np.stack(z).astype(np.float32), dtype=jnp.bfloat16)
    return as_bf16(cum), as_bf16(sel), jnp.asarray(np.stack(msk).astype(np.float32))


def _hgrn_body(ff_ref, fb_ref, v_ref, q_ref, g_ref, lbp_ref, ng_ref, cum_ref, sel_ref, msk_ref, o_ref,
               acc_scr, st_scr, *, n_blk, l_ctx):
    tb, c, sb, dk = HG_TB, HG_C, HG_SUB, HG_DK
    n_c = tb // c
    n_sub = c // sb
    acc_scr[...] = jnp.zeros_like(acc_scr)
    st_scr[...] = jnp.zeros_like(st_scr)
    loglb, log1mlb = lbp_ref[0, 0:1, :], lbp_ref[0, 1:2, :]
    pos = lax.broadcasted_iota(jnp.int32, (tb, dk), 0) % sb
    nt = (((1,), (1,)), ((), ()))
    tn = (((0,), (0,)), ((), ()))
    dot = partial(jnp.dot, preferred_element_type=jnp.float32)

    def direction(base, f_ref, d):
        rows = pl.ds(base, tb)
        fp = f_ref[0, rows, :]
        ls = _log_sigmoid(fp)
        b2 = log1mlb + ls
        logf = jnp.maximum(loglb, b2) + jnp.log1p(jnp.exp(-jnp.abs(loglb - b2)))
        lk = log1mlb + (ls - fp)
        v = v_ref[0, rows, :]
        qr = q_ref[0, rows, :]
        q = qr * (1.0 / (1.0 + jnp.exp(-qr)))
        r = dot(cum_ref[d], _split3(logf))
        cum, tot, pre = _sum3(r[0:tb], dk), _sum3(r[tb:2 * tb], dk), _sum3(r[2 * tb:3 * tb], dk)
        g = lk - cum
        rr = dot(sel_ref[d], _split3(cum))
        qe = (q * jnp.exp(cum)).astype(jnp.bfloat16)
        kt = jnp.exp(g + tot).astype(jnp.bfloat16)
        dec = jnp.exp(tot)
        vb = v.astype(jnp.bfloat16)
        o = jnp.sum(q * jnp.exp(lk), axis=-1, keepdims=True) * v
        for j in range(1, sb):
            sh = j if d == 0 else tb - j
            valid = (pos >= j) if d == 0 else (pos < sb - j)
            gj = pltpu.roll(g, sh, 0)
            vj = pltpu.roll(v, sh, 0)
            e = jnp.exp(jnp.where(valid, cum + gj, _NEG))
            o = o + jnp.sum(q * e, axis=-1, keepdims=True) * vj
        q4 = (q * jnp.exp(pre)).astype(jnp.bfloat16)
        att = None
        for i in range(n_sub - 1):
            r_i = _sum3(rr[i * tb:(i + 1) * tb], dk)
            kh = jnp.exp(lk + jnp.minimum(r_i - cum, 0.0)).astype(jnp.bfloat16)
            a_i = lax.dot_general(q4, kh, nt, preferred_element_type=jnp.float32) * msk_ref[d, i]
            att = a_i if att is None else att + a_i
        o = o + dot(att.astype(jnp.bfloat16), vb)
        order = [i if d == 0 else n_c - 1 - i for i in range(n_c)]
        kvs = [lax.dot_general(vb[cc * c:(cc + 1) * c], kt[cc * c:(cc + 1) * c], tn,
                               preferred_element_type=jnp.float32) for cc in order]
        st = st_scr[d]
        sts = []
        for cc, kv in zip(order, kvs):
            sts.append(st.astype(jnp.bfloat16))
            st = st * dec[cc * c:cc * c + 1, :] + kv
        st_scr[d] = st
        o_st = [None] * n_c
        for cc, sb16 in zip(order, sts):
            o_st[cc] = lax.dot_general(qe[cc * c:(cc + 1) * c], sb16, nt, preferred_element_type=jnp.float32)
        acc_scr[rows, :] += o + jnp.concatenate(o_st, axis=0)

    def step(j, carry):
        base_f = pl.multiple_of(j * tb, tb)
        jb = jnp.where(j == 0, 0, n_blk - j)
        base_b = pl.multiple_of(jb * tb, tb)
        direction(base_f, ff_ref, 0)
        direction(base_b, fb_ref, 1)
        return carry

    lax.fori_loop(0, n_blk, step, 0)
    o = acc_scr[l_ctx:, :]
    g = g_ref[0, l_ctx:, :]
    o = o * lax.rsqrt(jnp.mean(o * o, axis=-1, keepdims=True) + EPS) * ng_ref[...]
    o_ref[0] = o * (g * (1.0 / (1.0 + jnp.exp(-g))))


def _hgrn_mix(u_all, lb, norm_g, l_ctx):
    bsz, l_tot, _ = u_all.shape
    nh = HG_HEADS
    n_blk = l_tot // HG_TB
    lbh = lb.reshape(nh, 1, HG_DK)
    lbp = jnp.concatenate([jnp.log(lbh), jnp.log1p(-lbh),
                           jnp.zeros((nh, _SUBLANES - 2, HG_DK), jnp.float32)], axis=1)
    cum_m, sel_m, msk_m = _hgrn_consts()

    def slab(k):
        return pl.BlockSpec((1, l_tot, HG_DK), lambda b, h, k=k: (b, 0, k * nh + h))

    def const(a):
        return pl.BlockSpec(a.shape, lambda b, h: (0,) * a.ndim)

    return pl.pallas_call(
        partial(_hgrn_body, n_blk=n_blk, l_ctx=l_ctx),
        grid=(bsz, nh),
        in_specs=[slab(0), slab(1), slab(2), slab(4), slab(5),
                  pl.BlockSpec((1, _SUBLANES, HG_DK), lambda b, h: (h, 0, 0)),
                  pl.BlockSpec((1, HG_DK), lambda b, h: (0, 0)),
                  const(cum_m), const(sel_m), const(msk_m)],
        out_specs=pl.BlockSpec((1, l_tot - l_ctx, HG_DK), lambda b, h: (b, 0, h)),
        out_shape=jax.ShapeDtypeStruct((bsz, l_tot - l_ctx, W_HALF), jnp.float32),
        scratch_shapes=[pltpu.VMEM((l_tot, HG_DK), jnp.float32), pltpu.VMEM((2, HG_DK, HG_DK), jnp.float32)],
        compiler_params=pltpu.CompilerParams(dimension_semantics=("parallel", "parallel"),
                                             vmem_limit_bytes=_VMEM_LIMIT),
        name="hgrn_mix",
    )(u_all, u_all, u_all, u_all, u_all, lbp, norm_g.reshape(1, HG_DK), cum_m, sel_m, msk_m)


def _s5_weights(lam_re, lam_im, log_dt, b_re, b_im, c_re, c_im):
    t_len, f32 = S5_T, jnp.float32
    lre = jnp.minimum(lam_re, S5_MAX_RE)
    dt = jnp.exp(log_dt)[..., None]
    ar, ai = lre * dt, lam_im * dt
    k = jnp.arange(t_len + 1, dtype=f32)[:, None, None, None]
    mag = jnp.exp(k * ar)
    pre, pim = mag * jnp.cos(k * ai), mag * jnp.sin(k * ai)
    nr, ni, den = pre[1] - 1.0, pim[1], lre * lre + lam_im * lam_im
    cr, ci = (nr * lre + ni * lam_im) / den, (ni * lre - nr * lam_im) / den
    cbr = cr[..., None] * b_re - ci[..., None] * b_im
    cbi = cr[..., None] * b_im + ci[..., None] * b_re
    c_re_t, c_im_t = jnp.swapaxes(c_re, 1, 2), jnp.swapaxes(c_im, 1, 2)
    clre = c_re_t * pre[..., None] - c_im_t * pim[..., None]
    clim = c_re_t * pim[..., None] + c_im_t * pre[..., None]
    g = lam_re.shape[1]

    def w_dir(d, e):
        pr, pi = pre[e, d], pim[e, d]
        wr = pr[..., None] * cbr[d] - pi[..., None] * cbi[d]
        wi = pr[..., None] * cbi[d] + pi[..., None] * cbr[d]
        return [jnp.transpose(z, (1, 0, 3, 2)).reshape(g, t_len * S5_H, S5_P) for z in (wr, wi)]

    wf, wb = w_dir(0, jnp.arange(t_len - 1, -1, -1)), w_dir(1, jnp.arange(t_len))
    w = jnp.concatenate(wf + wf[::-1] + wb + wb[::-1], axis=-1)
    dec = jnp.concatenate([pre[t_len, 0], pre[t_len, 0], -pim[t_len, 0], pim[t_len, 0],
                           pre[t_len, 1], pre[t_len, 1], -pim[t_len, 1], pim[t_len, 1]], axis=-1)[:, None, :]

    def m_dir(d, e):
        mr, mi = clre[e, d], -clim[e, d]
        return [jnp.transpose(z, (1, 2, 0, 3)).reshape(g, S5_P, t_len * S5_H) for z in (mr, mi)]

    m = jnp.concatenate(m_dir(0, jnp.arange(1, t_len + 1)) + m_dir(1, jnp.arange(t_len, 0, -1)), axis=1)
    kk = (jnp.einsum('kdgpo,dgpi->kdgoi', clre[:t_len], cbr, precision=_HI)
          - jnp.einsum('kdgpo,dgpi->kdgoi', clim[:t_len], cbi, precision=_HI))
    delta = jnp.arange(t_len)[None, :] - jnp.arange(t_len)[:, None]
    kf = kk[jnp.clip(delta, 0, t_len - 1), 0]
    kb = kk[jnp.clip(-delta, 0, t_len - 1), 1]
    dl = delta[:, :, None, None, None]
    toep = jnp.where(dl > 0, kf, 0.0) + jnp.where(dl < 0, kb, 0.0) + jnp.where(dl == 0, kf + kb, 0.0)
    toep = jnp.transpose(toep, (2, 0, 4, 1, 3)).reshape(g, t_len * S5_H, t_len * S5_H)
    return w, dec, m, toep


def _s5_body(x_ref, w_ref, dec_ref, m_ref, toep_ref, y_ref, z_scr, s_scr, *, bsz, n_ctx, n_lat):
    p2 = 2 * S5_P
    n_all = n_ctx + n_lat
    z_scr[...] = jnp.dot(x_ref[0], w_ref[0], preferred_element_type=jnp.float32)
    dec = jnp.broadcast_to(dec_ref[0], (bsz, 4 * p2))
    a1f, a2f = dec[:, 0:p2], dec[:, p2:2 * p2]
    a1b, a2b = dec[:, 2 * p2:3 * p2], dec[:, 3 * p2:4 * p2]

    def step(i, carry):
        sf, tf, sb, tb = carry
        rf = pl.multiple_of(i * bsz, bsz)
        cb = jnp.where(i < n_ctx, n_ctx - 1 - i, n_all + n_ctx - 1 - i)
        rb = pl.multiple_of(cb * bsz, bsz)
        s_scr[pl.ds(rf, bsz), 0:p2] = sf
        s_scr[pl.ds(rb, bsz), p2:2 * p2] = sb
        zf = z_scr[pl.ds(rf, bsz), 0:2 * p2]
        zb = z_scr[pl.ds(rb, bsz), 2 * p2:4 * p2]
        return (sf * a1f + tf * a2f + zf[:, :p2], tf * a1f - sf * a2f + zf[:, p2:],
                sb * a1b + tb * a2b + zb[:, :p2], tb * a1b - sb * a2b + zb[:, p2:])

    zero = jnp.zeros((bsz, p2), jnp.float32)
    lax.fori_loop(0, n_all, step, (zero, zero, zero, zero), unroll=4)
    r0 = n_ctx * bsz
    y_ref[0] = (jnp.dot(x_ref[0, r0:, :], toep_ref[0], preferred_element_type=jnp.float32)
                + jnp.dot(s_scr[r0:, :].astype(jnp.bfloat16), m_ref[0], preferred_element_type=jnp.float32))


def _s5_mix(u_c, u_l, lam_re, lam_im, log_dt, b_re, b_im, c_re, c_im):
    bsz, l_ctx, gh = u_c.shape
    l_lat = u_l.shape[1]
    g, t_len = gh // S5_H, S5_T
    n_ctx, n_lat = l_ctx // t_len, l_lat // t_len
    w, dec, m, toep = _s5_weights(lam_re, lam_im, log_dt, b_re, b_im, c_re, c_im)

    def chunked(u, n):
        return jnp.transpose(u.reshape(bsz, n, t_len, g, S5_H), (3, 1, 0, 2, 4)).reshape(g, n * bsz, t_len * S5_H)

    x = jnp.concatenate([chunked(u_c, n_ctx), chunked(u_l, n_lat)], axis=1).astype(jnp.bfloat16)
    rows, k = (n_ctx + n_lat) * bsz, t_len * S5_H
    y = pl.pallas_call(
        partial(_s5_body, bsz=bsz, n_ctx=n_ctx, n_lat=n_lat),
        grid=(g,),
        in_specs=[pl.BlockSpec((1, rows, k), lambda i: (i, 0, 0)),
                  pl.BlockSpec((1, k, 8 * S5_P), lambda i: (i, 0, 0)),
                  pl.BlockSpec((1, 1, 8 * S5_P), lambda i: (i, 0, 0)),
                  pl.BlockSpec((1, 4 * S5_P, k), lambda i: (i, 0, 0)),
                  pl.BlockSpec((1, k, k), lambda i: (i, 0, 0))],
        out_specs=pl.BlockSpec((1, n_lat * bsz, k), lambda i: (i, 0, 0)),
        out_shape=jax.ShapeDtypeStruct((g, n_lat * bsz, k), jnp.float32),
        scratch_shapes=[pltpu.VMEM((rows, 8 * S5_P), jnp.float32), pltpu.VMEM((rows, 4 * S5_P), jnp.float32)],
        compiler_params=pltpu.CompilerParams(dimension_semantics=("parallel",), vmem_limit_bytes=_VMEM_LIMIT),
        name="s5_mix",
    )(x, w.astype(jnp.bfloat16), dec, m.astype(jnp.bfloat16), toep.astype(jnp.bfloat16))
    return jnp.transpose(y.reshape(g, n_lat, bsz, t_len, S5_H), (2, 1, 3, 0, 4)).reshape(bsz, l_lat, gh)


def _glu_body(y_ref, u_ref, d_ref, w_ref, b_ref, o_ref):
    y = _gelu_tanh(y_ref[0] + u_ref[0] * d_ref[...])
    z = jnp.dot(y.astype(jnp.bfloat16), w_ref[...], preferred_element_type=jnp.float32) + b_ref[...]
    o_ref[0] = y * (1.0 / (1.0 + jnp.exp(-z)))


def _s5_out(y, u_all, l_ctx, col0, d_skip, glu_w, glu_b):
    bsz, l, c = y.shape
    tm = l_ctx
    vec = pl.BlockSpec((1, c), lambda b, i: (0, 0))
    return pl.pallas_call(
        _glu_body,
        grid=(bsz, l // tm),
        in_specs=[pl.BlockSpec((1, tm, c), lambda b, i: (b, i, 0)),
                  pl.BlockSpec((1, tm, c), lambda b, i: (b, i + 1, col0 // c)), vec,
                  pl.BlockSpec((c, c), lambda b, i: (0, 0)), vec],
        out_specs=pl.BlockSpec((1, tm, c), lambda b, i: (b, i, 0)),
        out_shape=jax.ShapeDtypeStruct(y.shape, jnp.float32),
        compiler_params=pltpu.CompilerParams(dimension_semantics=("parallel", "parallel")),
        name="s5_glu",
    )(y, u_all, d_skip.reshape(1, c), glu_w.astype(jnp.bfloat16), glu_b.reshape(1, c))


def _odd_mixer(u, l_ctx, lb, hg_norm_g, lam_re, lam_im, log_dt, b_re, b_im, c_re, c_im, d_skip, glu_w, glu_b):
    w = W_HALF
    hg_l = _hgrn_mix(u, lb, hg_norm_g, l_ctx)
    s_c, s_l = u[:, :l_ctx, 3 * w:4 * w], u[:, l_ctx:, 3 * w:4 * w]
    y_l = _s5_mix(s_c, s_l, lam_re, lam_im, log_dt, b_re, b_im, c_re, c_im)
    return hg_l, _s5_out(y_l, u, l_ctx, 3 * w, d_skip, glu_w, glu_b)


def _router_body(x_ref, g_ref, sh_ref, sc_ref, wr_ref, h_ref, lg_ref):
    x = x_ref[0]
    h = x * lax.rsqrt(jnp.mean(x * x, axis=-1, keepdims=True) + EPS) * g_ref[...]
    h = h * (1.0 + sc_ref[0]) + sh_ref[0]
    hb = h.astype(jnp.bfloat16)
    lg_ref[0] = jnp.dot(hb, wr_ref[...], preferred_element_type=jnp.float32)
    bits = pltpu.bitcast(hb.astype(jnp.float32), jnp.uint32)
    half = bits.shape[1] // 2
    h_ref[0] = (bits[:, :half] >> 16) | (bits[:, half:] & jnp.uint32(0xFFFF0000))


def _router(x, norm_g, shift, scale, w_router):
    bsz, n, d = x.shape
    tm = min(n, 512)
    vec = pl.BlockSpec((1, 1, d), lambda b, i: (b, 0, 0))
    return pl.pallas_call(
        _router_body,
        grid=(bsz, n // tm),
        in_specs=[pl.BlockSpec((1, tm, d), lambda b, i: (b, i, 0)),
                  pl.BlockSpec((1, d), lambda b, i: (0, 0)), vec, vec,
                  pl.BlockSpec((d, N_EXPERTS), lambda b, i: (0, 0))],
        out_specs=[pl.BlockSpec((1, tm, d // 2), lambda b, i: (b, i, 0)),
                   pl.BlockSpec((1, tm, N_EXPERTS), lambda b, i: (b, i, 0))],
        out_shape=[jax.ShapeDtypeStruct((bsz, n, d // 2), jnp.uint32),
                   jax.ShapeDtypeStruct((bsz, n, N_EXPERTS), jnp.float32)],
        compiler_params=pltpu.CompilerParams(dimension_semantics=("parallel", "parallel"),
                                             vmem_limit_bytes=_VMEM_LIMIT),
        name="moe_router",
    )(x, norm_g.reshape(1, d), jnp.broadcast_to(shift, (bsz, 1, d)), jnp.broadcast_to(scale, (bsz, 1, d)),
      w_router.astype(jnp.bfloat16))


def _expert_body(idx_ref, h_ref, gate_ref, mod_ref, wg_ref, wu_ref, wd_ref, y_ref, x_scr, *, cap, n_f):
    def grab(g, carry):
        base = pl.multiple_of(g * _SUBLANES, _SUBLANES)
        for r in range(_SUBLANES):
            x_scr[pl.ds(base + r, 1), :] = h_ref[0, pl.ds(idx_ref[0, 0, base + r], 1), :]
        return carry

    lax.fori_loop(0, cap // _SUBLANES, grab, 0)
    w32 = x_scr[...]
    xb = jnp.concatenate([pltpu.bitcast(w32 << 16, jnp.float32),
                          pltpu.bitcast(w32 & jnp.uint32(0xFFFF0000), jnp.float32)], axis=1).astype(jnp.bfloat16)
    tf = wg_ref.shape[2] // n_f
    acc = None
    for f in range(n_f):
        cs = slice(f * tf, (f + 1) * tf)
        a = jnp.dot(xb, wg_ref[0, :, cs], preferred_element_type=jnp.float32)
        u = jnp.dot(xb, wu_ref[0, :, cs], preferred_element_type=jnp.float32)
        hh = (a * (1.0 / (1.0 + jnp.exp(-a))) * u).astype(jnp.bfloat16)
        part = jnp.dot(hh, wd_ref[0, cs, :], preferred_element_type=jnp.float32)
        acc = part if acc is None else acc + part
    y_ref[0] = acc * gate_ref[0] * mod_ref[0]


def _expert_ffn(h32, idx, gate, mod, w_gate, w_up, w_down):
    bsz, n, dh = h32.shape
    d = 2 * dh
    cap = idx.shape[-1]
    e, _, ff = w_gate.shape
    be = lambda k, b: (b * e + k, 0, 0)
    wspec = lambda s: pl.BlockSpec((1,) + s, lambda k, b: (k, 0, 0))
    return pl.pallas_call(
        partial(_expert_body, cap=cap, n_f=3),
        grid=(e, bsz),
        in_specs=[pl.BlockSpec((1, 1, cap), be, memory_space=pltpu.SMEM),
                  pl.BlockSpec((1, n, dh), lambda k, b: (b, 0, 0)),
                  pl.BlockSpec((1, cap, 1), be),
                  pl.BlockSpec((1, 1, d), lambda k, b: (b, 0, 0)),
                  wspec((d, ff)), wspec((d, ff)), wspec((ff, d))],
        out_specs=pl.BlockSpec((1, cap, d), be),
        out_shape=jax.ShapeDtypeStruct((bsz * e, cap, d), jnp.float32),
        scratch_shapes=[pltpu.VMEM((cap, dh), jnp.uint32)],
        compiler_params=pltpu.CompilerParams(dimension_semantics=("parallel", "parallel"),
                                             vmem_limit_bytes=_VMEM_LIMIT_BIG),
        name="moe_expert",
    )(idx.reshape(bsz * e, 1, cap), h32, gate.reshape(bsz * e, cap, 1), jnp.broadcast_to(mod, (bsz, 1, d)),
      w_gate.astype(jnp.bfloat16), w_up.astype(jnp.bfloat16), w_down.astype(jnp.bfloat16))


def _combine_body(idx_ref, y_ref, o_ref, *, cap):
    @pl.when(pl.program_id(1) == 0)
    def _():
        o_ref[...] = jnp.zeros_like(o_ref)

    def group(g, carry):
        base = pl.multiple_of(g * _SUBLANES, _SUBLANES)
        ys = y_ref[0, pl.ds(base, _SUBLANES), :]
        ts = [idx_ref[0, 0, base + r] for r in range(_SUBLANES)]
        cur = [o_ref[0, pl.ds(t, 1), :] for t in ts]
        for r in range(_SUBLANES):
            o_ref[0, pl.ds(ts[r], 1), :] = cur[r] + ys[r:r + 1, :]
        return carry

    lax.fori_loop(0, cap // _SUBLANES, group, 0)


def _combine(y, idx, n):
    be, cap, d = y.shape
    bsz = be // N_EXPERTS
    return pl.pallas_call(
        partial(_combine_body, cap=cap),
        grid=(bsz, N_EXPERTS),
        in_specs=[pl.BlockSpec((1, 1, cap), lambda b, k: (b * N_EXPERTS + k, 0, 0), memory_space=pltpu.SMEM),
                  pl.BlockSpec((1, cap, d), lambda b, k: (b * N_EXPERTS + k, 0, 0))],
        out_specs=pl.BlockSpec((1, n, d), lambda b, k: (b, 0, 0)),
        out_shape=jax.ShapeDtypeStruct((bsz, n, d), jnp.float32),
        compiler_params=pltpu.CompilerParams(dimension_semantics=("parallel", "arbitrary"),
                                             vmem_limit_bytes=_VMEM_LIMIT),
        name="moe_combine",
    )(idx.reshape(be, 1, cap), y)


def _moe(x, norm_g, shift, scale, mod, w_router, w_gate, w_up, w_down):
    bsz, n, d = x.shape
    cap = max(1, (EC_CAPACITY * n) // N_EXPERTS)
    h32, logits = _router(x, norm_g, shift, scale, w_router)
    aff = jax.nn.softmax(logits, axis=-1)
    gate, idx = lax.top_k(jnp.swapaxes(aff, 1, 2), cap)
    y = _expert_ffn(h32, idx, gate, mod, w_gate, w_up, w_down)
    return _combine(y, idx, n)


def kernel(x, c, ctx, c_ctx, mod_w, mod_b, norm1_g, norm2_g, out_w, even_in_w, lru_conv_w, lru_conv_b, lru_wa, lru_ba, lru_wx, lru_bx, lru_lam, hy_conv_w, hy_conv_b, hy_w1, hy_b1, hy_w2, hy_b2, hy_w3, hy_freq, hy_skip, odd_in_w, hg_lb, hg_norm_g, s5_lam_re, s5_lam_im, s5_log_dt, s5_b_re, s5_b_im, s5_c_re, s5_c_im, s5_d, s5_glu_w, s5_glu_b, router_w, ex_w_gate, ex_w_up, ex_w_down, final_g):
    rows = x.shape[1] // GRID_W
    l_ctx = ctx.shape[1]
    p = jax.nn.softmax(hg_lb, axis=0)
    lb_all = jnp.clip(jnp.cumsum(p, axis=0) - p[0], 0.0, 1.0 - 1e-4)
    sc_lat = jax.nn.silu(c)
    sc_ctx = jax.nn.silu(c_ctx)[None]
    for i in range(DEPTH):
        need_ctx = i < DEPTH - 1
        j = i // 2
        ml = [t[:, None, :] for t in jnp.split(sc_lat @ mod_w[i] + mod_b[i], 6, axis=-1)]
        mc = [t[:, None, :] for t in jnp.split(sc_ctx @ mod_w[i] + mod_b[i], 6, axis=-1)]
        shift, scale, gate = (_two_part(mc[k], ml[k]) for k in range(3))
        if i % 2 == 0:
            x_all = jnp.concatenate([ctx, x], axis=1)
            u = _norm_mm(x_all, norm1_g[i], shift, scale, even_in_w[j], l_ctx)
            h_f, h_b, yb = _even_mixer(u, l_ctx, lru_conv_w[j], lru_conv_b[j], lru_wa[j], lru_ba[j], lru_wx[j],
                                       lru_bx[j], lru_lam[j], hy_conv_w[j], hy_conv_b[j], hy_w1[j], hy_b1[j],
                                       hy_w2[j], hy_b2[j], hy_w3[j], hy_freq[j], hy_skip[j])
            x_all = _proj_res_lru(h_f, h_b, u, yb, out_w[i], x_all, gate, l_ctx)
            ctx, x = x_all[:, :l_ctx], x_all[:, l_ctx:]
        else:
            assert not need_ctx
            x_all = jnp.concatenate([ctx, _to_col_major(x, rows)], axis=1)
            u = _norm_mm(x_all, norm1_g[i], shift, scale, odd_in_w[j], l_ctx)
            hg_l, s5_l = _odd_mixer(u, l_ctx, lb_all[i], hg_norm_g[j], s5_lam_re[j], s5_lam_im[j],
                                    s5_log_dt[j], s5_b_re[j], s5_b_im[j], s5_c_re[j], s5_c_im[j], s5_d[j],
                                    s5_glu_w[j], s5_glu_b[j])
            x = _to_row_major(_proj_res(hg_l, s5_l, out_w[i], x_all, gate, l_ctx, l_ctx), rows)
        ex = (router_w[i], ex_w_gate[i], ex_w_up[i], ex_w_down[i])
        moe_l = _moe(x, norm2_g[i], ml[3], ml[4], ml[5], *ex)
        if need_ctx:
            x = x + moe_l
            ctx = ctx + _moe(ctx, norm2_g[i], mc[3], mc[4], mc[5], *ex)
    return _add_norm(x, moe_l, final_g)
```

```python
import math
from functools import partial

import numpy as np
import jax
import jax.numpy as jnp
from jax import lax
from jax.experimental import pallas as pl
from jax.experimental.pallas import tpu as pltpu

D_MODEL = 1024
DEPTH = 2
GRID_W = 64
W_HALF = D_MODEL // 2
EPS = 1e-6
LRU_HEADS = 8
LRU_HEAD_DIM = W_HALF // LRU_HEADS
LRU_C = 8.0
HY_ORDER = 2
HY_BANDS = 16
HY_MAX_DECAY = math.log(1e-2) / 0.3
HY_MIN_DECAY = math.log(1e-2) / 1.5
HG_HEADS = 4
HG_DK = W_HALF // HG_HEADS
HG_C = 16
HG_SUB = 4
HG_TB = 256
S5_H = 16
S5_P = 64
S5_MAX_RE = -1e-4
S5_T = 16
_DFT_LOW = 64
N_EXPERTS = 16
EC_CAPACITY = 2

_SUBLANES = 8
_DW_ROWS = 256
_VMEM_LIMIT = 48 * 1024 * 1024
_VMEM_LIMIT_BIG = 56 * 1024 * 1024
_HI = lax.Precision.HIGHEST
_NEG = -1e30


def _mm_body(a_ref, b_ref, o_ref):
    o_ref[...] = jnp.dot(a_ref[...].astype(jnp.bfloat16), b_ref[...],
                         preferred_element_type=jnp.float32)


def _pick(n, pref):
    for t in pref:
        if n % t == 0:
            return t
    return n


def _mm(a, b):
    m, k = a.shape
    n = b.shape[1]
    tm = _pick(m, (512, 256, 128, 64, 32, 16, 8))
    tn = _pick(n, (512, 256, 128))
    return pl.pallas_call(
        _mm_body,
        grid=(m // tm, n // tn),
        in_specs=[pl.BlockSpec((tm, k), lambda i, j: (i, 0)),
                  pl.BlockSpec((k, tn), lambda i, j: (0, j))],
        out_specs=pl.BlockSpec((tm, tn), lambda i, j: (i, j)),
        out_shape=jax.ShapeDtypeStruct((m, n), jnp.float32),
        compiler_params=pltpu.CompilerParams(
            dimension_semantics=("parallel", "parallel"), vmem_limit_bytes=_VMEM_LIMIT),
        name="mm",
    )(a, b.astype(jnp.bfloat16))


def _mm3(a, b):
    bsz, l, k = a.shape
    return _mm(a.reshape(bsz * l, k), b).reshape(bsz, l, b.shape[1])


def _add_norm_body(x_ref, m_ref, g_ref, o_ref):
    x = x_ref[...] + m_ref[...]
    o_ref[...] = x * lax.rsqrt(jnp.mean(x * x, axis=-1, keepdims=True) + EPS) * g_ref[...]


def _add_norm(x, m, g):
    bsz, n, d = x.shape
    tm = min(n, 512)
    blk = pl.BlockSpec((1, tm, d), lambda b, i: (b, i, 0))
    return pl.pallas_call(
        _add_norm_body,
        grid=(bsz, n // tm),
        in_specs=[blk, blk, pl.BlockSpec((1, 1, d), lambda b, i: (0, 0, 0))],
        out_specs=blk,
        out_shape=jax.ShapeDtypeStruct(x.shape, jnp.float32),
        compiler_params=pltpu.CompilerParams(dimension_semantics=("parallel", "parallel")),
        name="add_norm",
    )(x, m, g.reshape(1, 1, d))


def _two_part(t_ctx, t_lat):
    bsz, _, d = t_lat.shape
    return jnp.stack([jnp.broadcast_to(t_ctx, (bsz, 1, d)), t_lat], axis=1).reshape(2 * bsz, 1, d)


def _norm_mm_body(x_ref, g_ref, sh_ref, sc_ref, w_ref, o_ref):
    x = x_ref[0]
    h = x * lax.rsqrt(jnp.mean(x * x, axis=-1, keepdims=True) + EPS) * g_ref[...]
    h = h * (1.0 + sc_ref[0]) + sh_ref[0]
    o_ref[0] = jnp.dot(h.astype(jnp.bfloat16), w_ref[...], preferred_element_type=jnp.float32)


def _norm_mm(x_all, norm_g, shift, scale, w, l_ctx):
    bsz, l_tot, d = x_all.shape
    n = w.shape[1]
    tm = l_ctx
    vec = pl.BlockSpec((1, 1, d), lambda b, j: (2 * b + jnp.minimum(j, 1), 0, 0))
    return pl.pallas_call(
        _norm_mm_body,
        grid=(bsz, l_tot // tm),
        in_specs=[pl.BlockSpec((1, tm, d), lambda b, j: (b, j, 0)),
                  pl.BlockSpec((1, d), lambda b, j: (0, 0)), vec, vec,
                  pl.BlockSpec((d, n), lambda b, j: (0, 0))],
        out_specs=pl.BlockSpec((1, tm, n), lambda b, j: (b, j, 0)),
        out_shape=jax.ShapeDtypeStruct((bsz, l_tot, n), jnp.float32),
        compiler_params=pltpu.CompilerParams(dimension_semantics=("parallel", "parallel"),
                                             vmem_limit_bytes=_VMEM_LIMIT),
        name="norm_mm",
    )(x_all, norm_g.reshape(1, d), shift, scale, w.astype(jnp.bfloat16))


def _proj_res_body(ya_ref, yb_ref, wa_ref, wb_ref, x_ref, gt_ref, o_ref):
    y = (jnp.dot(ya_ref[0].astype(jnp.bfloat16), wa_ref[...], preferred_element_type=jnp.float32)
         + jnp.dot(yb_ref[0].astype(jnp.bfloat16), wb_ref[...], preferred_element_type=jnp.float32))
    o_ref[0] = x_ref[0] + gt_ref[0] * y


def _proj_res_lru_body(hf_ref, hb_ref, u_ref, yb_ref, wa_ref, wb_ref, x_ref, gt_ref, o_ref):
    ya = (hf_ref[0] + hb_ref[0]) * _gelu_tanh(u_ref[0])
    y = (jnp.dot(ya.astype(jnp.bfloat16), wa_ref[...], preferred_element_type=jnp.float32)
         + jnp.dot(yb_ref[0].astype(jnp.bfloat16), wb_ref[...], preferred_element_type=jnp.float32))
    o_ref[0] = x_ref[0] + gt_ref[0] * y


def _proj_res_lru(h_f, h_b, u_all, yb, w, x_all, gate, l_ctx):
    bsz, l_tot, ka = h_f.shape
    d = w.shape[1]
    tm = l_ctx
    blk = lambda k, c=0: pl.BlockSpec((1, tm, k), lambda b, j: (b, j, c))
    return pl.pallas_call(
        _proj_res_lru_body,
        grid=(bsz, l_tot // tm),
        in_specs=[blk(ka), blk(ka), blk(ka, 1), blk(yb.shape[2]),
                  pl.BlockSpec((ka, d), lambda b, j: (0, 0)), pl.BlockSpec((yb.shape[2], d), lambda b, j: (0, 0)),
                  blk(d), pl.BlockSpec((1, 1, d), lambda b, j: (2 * b + jnp.minimum(j, 1), 0, 0))],
        out_specs=blk(d),
        out_shape=jax.ShapeDtypeStruct((bsz, l_tot, d), jnp.float32),
        compiler_params=pltpu.CompilerParams(dimension_semantics=("parallel", "parallel"),
                                             vmem_limit_bytes=_VMEM_LIMIT),
        name="proj_res_lru",
    )(h_f, h_b, u_all, yb, w[:ka].astype(jnp.bfloat16), w[ka:].astype(jnp.bfloat16), x_all, gate)


def _proj_res(ya, yb, w, x_all, gate, l_ctx, row0):
    bsz, l_y, ka = ya.shape
    d = w.shape[1]
    tm = l_ctx
    j0 = row0 // tm
    yblk = lambda k: pl.BlockSpec((1, tm, k), lambda b, j: (b, j, 0))
    return pl.pallas_call(
        _proj_res_body,
        grid=(bsz, l_y // tm),
        in_specs=[yblk(ka), yblk(yb.shape[2]),
                  pl.BlockSpec((ka, d), lambda b, j: (0, 0)), pl.BlockSpec((yb.shape[2], d), lambda b, j: (0, 0)),
                  pl.BlockSpec((1, tm, d), lambda b, j: (b, j + j0, 0)),
                  pl.BlockSpec((1, 1, d), lambda b, j: (2 * b + jnp.minimum(j + j0, 1), 0, 0))],
        out_specs=pl.BlockSpec((1, tm, d), lambda b, j: (b, j, 0)),
        out_shape=jax.ShapeDtypeStruct((bsz, l_y, d), jnp.float32),
        compiler_params=pltpu.CompilerParams(dimension_semantics=("parallel", "parallel"),
                                             vmem_limit_bytes=_VMEM_LIMIT),
        name="proj_res",
    )(ya, yb, w[:ka].astype(jnp.bfloat16), w[ka:].astype(jnp.bfloat16), x_all, gate)


def _gelu_tanh(x):
    return 0.5 * x * (1.0 + jnp.tanh(math.sqrt(2.0 / math.pi) * (x + 0.044715 * (x * x * x))))


def _dwconv_body(u_ref, w_ref, o_ref, *rest, l_ctx, taps, left, emit_bf16):
    l_tot = u_ref.shape[1]
    tb, h = _DW_ROWS, _SUBLANES
    row = lax.broadcasted_iota(jnp.int32, (tb, 1), 0)

    def block(j, carry):
        r0 = pl.multiple_of(j * tb, tb)
        lo = jnp.where(r0 < l_ctx, 0, l_ctx)
        hi = jnp.where(r0 < l_ctx, l_ctx, l_tot)
        prev = u_ref[0, pl.ds(pl.multiple_of(jnp.maximum(r0 - h, 0), h), h), :]
        nxt = u_ref[0, pl.ds(pl.multiple_of(jnp.minimum(r0 + tb, l_tot - h), h), h), :]
        ext = jnp.concatenate([prev, u_ref[0, pl.ds(r0, tb), :], nxt], axis=0)
        y = jnp.broadcast_to(w_ref[0, taps:taps + 1, :], (tb, ext.shape[1]))
        for t in range(taps):
            off = t - left
            src = r0 + row + off
            y = y + jnp.where((src >= lo) & (src < hi), ext[h + off:h + off + tb, :], 0.0) * w_ref[0, t:t + 1, :]
        o_ref[0, pl.ds(r0, tb), :] = y
        if emit_bf16:
            rest[0][0, pl.ds(r0, tb), :] = y.astype(jnp.bfloat16)
        return carry

    lax.fori_loop(0, l_tot // tb, block, 0)


def _dwconv_slabs(u_all, col0, w, b, l_ctx, emit_bf16):
    bsz, l_tot, _ = u_all.shape
    taps, nc = w.shape
    c = W_HALF
    n = nc // c
    wb = jnp.concatenate([w, b[None], jnp.zeros((_SUBLANES - taps - 1, nc), jnp.float32)], axis=0)
    wb = wb.reshape(_SUBLANES, n, c).transpose(1, 0, 2)
    blk = pl.BlockSpec((1, l_tot, c), lambda bb, s: (bb, 0, s))
    shapes = [jax.ShapeDtypeStruct((bsz, l_tot, nc), jnp.float32)]
    if emit_bf16:
        shapes.append(jax.ShapeDtypeStruct((bsz, l_tot, nc), jnp.bfloat16))
    out = pl.pallas_call(
        partial(_dwconv_body, l_ctx=l_ctx, taps=taps, left=(taps - 1) // 2, emit_bf16=emit_bf16),
        grid=(bsz, n),
        in_specs=[pl.BlockSpec((1, l_tot, c), lambda bb, s: (bb, 0, s + col0 // c)),
                  pl.BlockSpec((1, _SUBLANES, c), lambda bb, s: (s, 0, 0))],
        out_specs=[blk] * len(shapes),
        out_shape=shapes,
        compiler_params=pltpu.CompilerParams(dimension_semantics=("parallel", "parallel"),
                                             vmem_limit_bytes=_VMEM_LIMIT),
        name="dwconv",
    )(u_all, wb)
    return out if emit_bf16 else out[0]


def _to_col_major(t, rows):
    bsz, l, ch = t.shape
    return t.reshape(bsz, rows, GRID_W, ch).transpose(0, 2, 1, 3).reshape(bsz, l, ch)


def _to_row_major(t, rows):
    bsz, l, ch = t.shape
    return t.reshape(bsz, GRID_W, rows, ch).transpose(0, 2, 1, 3).reshape(bsz, l, ch)


def _lru_body(xf_ref, xb_ref, wa_ref, wx_ref, par_ref, hf_ref, hb_ref,
              af_scr, bf_scr, ab_scr, bb_scr, sf_scr, sb_scr, *, tb):
    @pl.when(pl.program_id(1) == 0)
    def _():
        sf_scr[...] = jnp.zeros_like(sf_scr)
        sb_scr[...] = jnp.zeros_like(sb_scr)

    def coeffs(x, d, a_scr, b_scr):
        xb = x.astype(jnp.bfloat16)
        r = jnp.dot(xb, wa_ref[d], preferred_element_type=jnp.float32) + par_ref[d, 0:1, :]
        i = jnp.dot(xb, wx_ref[d], preferred_element_type=jnp.float32) + par_ref[d, 1:2, :]
        r = 1.0 / (1.0 + jnp.exp(-r))
        i = 1.0 / (1.0 + jnp.exp(-i))
        log_a = -LRU_C * r * par_ref[d, 2:3, :]
        a = jnp.exp(log_a)
        a_scr[...] = a
        b_scr[...] = jnp.sqrt(-jnp.tanh(log_a) * (a * a + 1.0)) * i * x

    coeffs(xf_ref[0], 0, af_scr, bf_scr)
    coeffs(xb_ref[0], 1, ab_scr, bb_scr)

    def tile(i, carry):
        hf, hb = carry
        base_f = pl.multiple_of(i * _SUBLANES, _SUBLANES)
        base_b = pl.multiple_of(tb - _SUBLANES - i * _SUBLANES, _SUBLANES)
        for r in range(_SUBLANES):
            rf = pl.ds(base_f + r, 1)
            hf = af_scr[rf, :] * hf + bf_scr[rf, :]
            hf_ref[0, rf, :] = hf
            rb = pl.ds(base_b + (_SUBLANES - 1 - r), 1)
            hb = ab_scr[rb, :] * hb + bb_scr[rb, :]
            hb_ref[0, rb, :] = hb
        return hf, hb

    hf, hb = lax.fori_loop(0, tb // _SUBLANES, tile, (sf_scr[...], sb_scr[...]))
    sf_scr[...] = hf
    sb_scr[...] = hb


def _block_diag(w):
    two, h, n, _ = w.shape
    eye = jnp.eye(h, dtype=w.dtype)
    return (w[:, :, :, None, :] * eye[None, :, None, :, None]).reshape(two, h * n, h * n)


def _lru_scan(xc, l_ctx, wa, ba, wx, bx, lam):
    bsz, l_tot, ch = xc.shape
    tb = l_ctx
    nb = l_tot // tb
    par = jnp.stack([ba.reshape(2, ch), bx.reshape(2, ch), jax.nn.softplus(-lam).reshape(2, ch)], axis=1)
    par = jnp.concatenate([par, jnp.zeros((2, _SUBLANES - 3, ch), jnp.float32)], axis=1)
    fwd = pl.BlockSpec((1, tb, ch), lambda b, j: (b, j, 0))
    bwd = pl.BlockSpec((1, tb, ch), lambda b, j: (b, jnp.where(j == 0, 0, nb - j), 0))
    wspec = pl.BlockSpec((2, ch, ch), lambda b, j: (0, 0, 0))
    return pl.pallas_call(
        partial(_lru_body, tb=tb),
        grid=(bsz, nb),
        in_specs=[fwd, bwd, wspec, wspec, pl.BlockSpec((2, _SUBLANES, ch), lambda b, j: (0, 0, 0))],
        out_specs=[fwd, bwd],
        out_shape=[jax.ShapeDtypeStruct(xc.shape, jnp.float32)] * 2,
        scratch_shapes=[pltpu.VMEM((tb, ch), jnp.float32)] * 4 + [pltpu.VMEM((1, ch), jnp.float32)] * 2,
        compiler_params=pltpu.CompilerParams(dimension_semantics=("parallel", "arbitrary")),
        name="lru_scan",
    )(xc, xc, _block_diag(wa).astype(jnp.bfloat16), _block_diag(wx).astype(jnp.bfloat16), par)


def _hyena_filters(l, w1, b1, w2, b2, w3, freq):
    t = jnp.arange(l, dtype=jnp.float32)
    t_unit = t / max(l - 1, 1)
    bands = jnp.linspace(1e-4, HY_BANDS - 1, HY_BANDS, dtype=jnp.float32)
    ang = (2.0 * math.pi / l) * t[:, None] * bands[None, :]
    z = jnp.concatenate([t_unit[:, None], jnp.cos(ang), -jnp.sin(ang)], axis=-1)
    h = jnp.sin(freq * (z @ w1 + b1))
    h = jnp.sin(freq * (h @ w2 + b2))
    h = h @ w3
    dist = jnp.abs(t - (l // 2)) / (l / 2.0)
    deltas = jnp.abs(jnp.linspace(HY_MIN_DECAY, HY_MAX_DECAY, HY_ORDER * W_HALF, dtype=jnp.float32))
    h = h * jnp.exp(-dist[:, None] * deltas[None, :])
    h = h * lax.rsqrt(jnp.sum(h * h, axis=0, keepdims=True) + EPS)
    return h


def _dft_mats(l, tf):
    n = 2 * l
    nl = _DFT_LOW
    nh = l // nl
    t = jnp.arange(l, dtype=jnp.int32)
    w = 2.0 * math.pi / n

    def tables(tv):
        ah = (((nl * jnp.arange(nh, dtype=jnp.int32))[:, None] * tv[None, :]) % n).astype(jnp.float32) * w
        al = ((jnp.arange(nl, dtype=jnp.int32)[:, None] * tv[None, :]) % n).astype(jnp.float32) * w
        return jnp.cos(ah), jnp.sin(ah), jnp.cos(al), jnp.sin(al)

    ch, sh, cl, sl = tables(t)
    f_re = (ch[:, None, :] * cl[None] - sh[:, None, :] * sl[None]).reshape(l, l)
    f_im = -(sh[:, None, :] * cl[None] + ch[:, None, :] * sl[None]).reshape(l, l)
    f_im = f_im.at[0].set(jnp.where(t % 2 == 0, 1.0, -1.0))
    f = jnp.concatenate([f_re.reshape(l // tf, tf, l), f_im.reshape(l // tf, tf, l)], axis=1).reshape(n, l)
    tp = t + l // 2
    ch, sh, cl, sl = (z.T for z in tables(tp))
    g_re = (2.0 / n) * (ch[:, :, None] * cl[:, None, :] - sh[:, :, None] * sl[:, None, :]).reshape(l, l)
    g_im = (-2.0 / n) * (sh[:, :, None] * cl[:, None, :] + ch[:, :, None] * sl[:, None, :]).reshape(l, l)
    g_re = g_re.at[:, 0].set(1.0 / n)
    g_im = g_im.at[:, 0].set(jnp.where(tp % 2 == 0, 1.0, -1.0) / n)
    g = jnp.stack([g_re.reshape(l, l // tf, tf), g_im.reshape(l, l // tf, tf)], axis=2).reshape(l, n)
    return f.astype(jnp.bfloat16), g.astype(jnp.bfloat16)


def _spec_body(f_ref, u_ref, a_ref, b_ref, d_ref, y_ref, *, tf, r0, l):
    acc = jnp.dot(f_ref[...], u_ref[0, r0:r0 + l, :], preferred_element_type=jnp.float32)
    xr, xi = acc[:tf], acc[tf:]
    y_ref[0, :tf, :] = (xr * a_ref[...] - xi * b_ref[...]).astype(jnp.bfloat16)
    y_ref[0, tf:, :] = (xr * b_ref[...] + xi * d_ref[...]).astype(jnp.bfloat16)


def _spec_mul(f, u, col, r0, l, a, b, d, tf):
    bsz, l_u, _ = u.shape
    c = a.shape[1]
    coef = pl.BlockSpec((tf, c), lambda i, bb: (i, 0))
    return pl.pallas_call(
        partial(_spec_body, tf=tf, r0=r0, l=l),
        grid=(l // tf, bsz),
        in_specs=[pl.BlockSpec((2 * tf, l), lambda i, bb: (i, 0)),
                  pl.BlockSpec((1, l_u, c), lambda i, bb: (bb, 0, col)), coef, coef, coef],
        out_specs=pl.BlockSpec((1, 2 * tf, c), lambda i, bb: (bb, i, 0)),
        out_shape=jax.ShapeDtypeStruct((bsz, 2 * l, c), jnp.bfloat16),
        compiler_params=pltpu.CompilerParams(dimension_semantics=("parallel", "parallel"),
                                             vmem_limit_bytes=_VMEM_LIMIT),
        name="hy_spec",
    )(f, u, a, b, d)


def _inv_body(g_ref, y_ref, u_ref, x_ref, skip_ref, z_ref, zb_ref):
    acc = jnp.dot(g_ref[...], y_ref[0], preferred_element_type=jnp.float32)
    z = x_ref[0] * (acc + u_ref[0] * skip_ref[...])
    z_ref[0] = z
    zb_ref[0] = z.astype(jnp.bfloat16)


def _inv_gate(g, y, u, ucol, urb, xg, xcol, xrb, skip, tm):
    bsz, n, c = y.shape
    l = n // 2
    blk = pl.BlockSpec((1, tm, c), lambda i, bb: (bb, i, 0))
    return pl.pallas_call(
        _inv_body,
        grid=(l // tm, bsz),
        in_specs=[pl.BlockSpec((tm, n), lambda i, bb: (i, 0)),
                  pl.BlockSpec((1, n, c), lambda i, bb: (bb, 0, 0)),
                  pl.BlockSpec((1, tm, c), lambda i, bb: (bb, i + urb, ucol)),
                  pl.BlockSpec((1, tm, c), lambda i, bb: (bb, i + xrb, xcol)),
                  pl.BlockSpec((1, c), lambda i, bb: (0, 0))],
        out_specs=[blk, blk],
        out_shape=[jax.ShapeDtypeStruct((bsz, l, c), jnp.float32), jax.ShapeDtypeStruct((bsz, l, c), jnp.bfloat16)],
        compiler_params=pltpu.CompilerParams(dimension_semantics=("parallel", "parallel"),
                                             vmem_limit_bytes=_VMEM_LIMIT),
        name="hy_inv",
    )(g, y, u, xg, skip.reshape(1, c))


def _hyena_conv2(hc, hb, r0, l, filt, skip):
    c = hc.shape[2] // 3
    tf = min(512, l // 2)
    tm = min(_DW_ROWS, l)
    f, g = _dft_mats(l, tf)
    hs = _mm(f, filt)
    hs = hs.reshape(l // tf, 2, tf, 2 * c)
    h_re, h_im = hs[:, 0].reshape(l, 2 * c), hs[:, 1].reshape(l, 2 * c)
    slot0 = (jnp.arange(l) == 0)[:, None]
    a = h_re
    b = jnp.where(slot0, 0.0, h_im)
    d = jnp.where(slot0, h_im, h_re)
    y = _spec_mul(f, hb, 0, r0, l, a[:, :c], b[:, :c], d[:, :c], tf)
    z, zb = _inv_gate(g, y, hc, 0, r0 // tm, hc, 1, r0 // tm, skip[0], tm)
    y = _spec_mul(f, zb, 0, 0, l, a[:, c:], b[:, c:], d[:, c:], tf)
    z, _ = _inv_gate(g, y, z, 0, 0, hc, 2, r0 // tm, skip[1], tm)
    return z


def _hyena(u, l_ctx, conv_w, conv_b, w1, b1, w2, b2, w3, freq, skip):
    hc, hb = _dwconv_slabs(u, 2 * W_HALF, conv_w, conv_b, l_ctx, True)
    zs = []
    for r0, l in ((0, l_ctx), (l_ctx, u.shape[1] - l_ctx)):
        filt = _hyena_filters(l, w1, b1, w2, b2, w3, freq)
        zs.append(_hyena_conv2(hc, hb, r0, l, filt, skip))
    return jnp.concatenate(zs, axis=1)


def _even_mixer(u, l_ctx, lru_conv_w, lru_conv_b, lru_wa, lru_ba, lru_wx, lru_bx,
                lru_lam, hy_conv_w, hy_conv_b, hy_w1, hy_b1, hy_w2, hy_b2, hy_w3, hy_freq, hy_skip):
    w = W_HALF
    xc = _dwconv_slabs(u, 0, lru_conv_w, lru_conv_b, l_ctx, False)
    h_f, h_b = _lru_scan(xc, l_ctx, lru_wa, lru_ba, lru_wx, lru_bx, lru_lam)
    yb = _hyena(u, l_ctx, hy_conv_w, hy_conv_b, hy_w1, hy_b1, hy_w2, hy_b2, hy_w3, hy_freq, hy_skip)
    return h_f, h_b, yb


def _split3(x):
    hi = x.astype(jnp.bfloat16)
    r1 = x - hi.astype(jnp.float32)
    mid = r1.astype(jnp.bfloat16)
    lo = (r1 - mid.astype(jnp.float32)).astype(jnp.bfloat16)
    return jnp.concatenate([hi, mid, lo], axis=1)


def _sum3(r, c):
    return r[:, 0:c] + r[:, c:2 * c] + r[:, 2 * c:3 * c]


def _log_sigmoid(x):
    return jnp.minimum(x, 0.0) - jnp.log1p(jnp.exp(-jnp.abs(x)))


def _hgrn_consts():
    tb, c, sb = HG_TB, HG_C, HG_SUB
    r = np.arange(tb)
    same = (r[:, None] // c) == (r[None, :] // c)
    same4 = (r[:, None] // sb) == (r[None, :] // sb)
    le = r[None, :] <= r[:, None]
    sub = (r % c) // sb
    cum, sel, msk = [], [], []
    for d in range(2):
        tri = le if d == 0 else le.T
        cum.append(np.concatenate([same & tri, same, same4 & tri], axis=0))
        ss, mm = [], []
        for i in range(c // sb - 1):
            if d == 0:
                ref_row = (r // c) * c + sb * (i + 1) - 1
                mm.append(same & (sub[:, None] == i + 1) & (sub[None, :] <= i))
            else:
                ref_row = (r // c) * c + sb * (i + 1)
                mm.append(same & (sub[:, None] == i) & (sub[None, :] > i))
            ss.append(r[None, :] == ref_row[:, None])
        sel.append(np.concatenate(ss, axis=0))
        msk.append(np.stack(mm))
    as_bf16 = lambda z: jnp.asarray(np.stack(z).astype(np.float32), dtype=jnp.bfloat16)
    return as_bf16(cum), as_bf16(sel), jnp.asarray(np.stack(msk).astype(np.float32))


def _hgrn_body(ff_ref, fb_ref, v_ref, q_ref, g_ref, lbp_ref, ng_ref, cum_ref, sel_ref, msk_ref, o_ref,
               acc_scr, st_scr, *, n_blk, l_ctx):
    tb, c, sb, dk = HG_TB, HG_C, HG_SUB, HG_DK
    n_c = tb // c
    n_sub = c // sb
    acc_scr[...] = jnp.zeros_like(acc_scr)
    st_scr[...] = jnp.zeros_like(st_scr)
    loglb, log1mlb = lbp_ref[0, 0:1, :], lbp_ref[0, 1:2, :]
    pos = lax.broadcasted_iota(jnp.int32, (tb, dk), 0) % sb
    nt = (((1,), (1,)), ((), ()))
    tn = (((0,), (0,)), ((), ()))
    dot = partial(jnp.dot, preferred_element_type=jnp.float32)

    def direction(base, f_ref, d):
        rows = pl.ds(base, tb)
        fp = f_ref[0, rows, :]
        ls = _log_sigmoid(fp)
        b2 = log1mlb + ls
        logf = jnp.maximum(loglb, b2) + jnp.log1p(jnp.exp(-jnp.abs(loglb - b2)))
        lk = log1mlb + (ls - fp)
        v = v_ref[0, rows, :]
        qr = q_ref[0, rows, :]
        q = qr * (1.0 / (1.0 + jnp.exp(-qr)))
        r = dot(cum_ref[d], _split3(logf))
        cum, tot, pre = _sum3(r[0:tb], dk), _sum3(r[tb:2 * tb], dk), _sum3(r[2 * tb:3 * tb], dk)
        g = lk - cum
        rr = dot(sel_ref[d], _split3(cum))
        qe = (q * jnp.exp(cum)).astype(jnp.bfloat16)
        kt = jnp.exp(g + tot).astype(jnp.bfloat16)
        dec = jnp.exp(tot)
        vb = v.astype(jnp.bfloat16)
        yield
        o = jnp.sum(q * jnp.exp(lk), axis=-1, keepdims=True) * v
        for j in range(1, sb):
            sh = j if d == 0 else tb - j
            valid = (pos >= j) if d == 0 else (pos < sb - j)
            gj = pltpu.roll(g, sh, 0)
            vj = pltpu.roll(v, sh, 0)
            e = jnp.exp(jnp.where(valid, cum + gj, _NEG))
            o = o + jnp.sum(q * e, axis=-1, keepdims=True) * vj
        yield
        q4 = (q * jnp.exp(pre)).astype(jnp.bfloat16)
        att = None
        for i in range(n_sub - 1):
            r_i = _sum3(rr[i * tb:(i + 1) * tb], dk)
            kh = jnp.exp(lk + jnp.minimum(r_i - cum, 0.0)).astype(jnp.bfloat16)
            a_i = lax.dot_general(q4, kh, nt, preferred_element_type=jnp.float32) * msk_ref[d, i]
            att = a_i if att is None else att + a_i
        o = o + dot(att.astype(jnp.bfloat16), vb)
        yield
        order = [i if d == 0 else n_c - 1 - i for i in range(n_c)]
        kvs = [lax.dot_general(vb[cc * c:(cc + 1) * c], kt[cc * c:(cc + 1) * c], tn,
                               preferred_element_type=jnp.float32) for cc in order]
        yield
        st = st_scr[d]
        sts = []
        for cc, kv in zip(order, kvs):
            sts.append(st.astype(jnp.bfloat16))
            st = st * dec[cc * c:cc * c + 1, :] + kv
        st_scr[d] = st
        o_st = [None] * n_c
        for cc, sb16 in zip(order, sts):
            o_st[cc] = lax.dot_general(qe[cc * c:(cc + 1) * c], sb16, nt, preferred_element_type=jnp.float32)
        acc_scr[rows, :] += o + jnp.concatenate(o_st, axis=0)

    def step(j, carry):
        base_f = pl.multiple_of(j * tb, tb)
        jb = jnp.where(j == 0, 0, n_blk - j)
        base_b = pl.multiple_of(jb * tb, tb)
        todo = [direction(base_f, ff_ref, 0), direction(base_b, fb_ref, 1)]
        while todo:
            todo = [gen for gen in todo if next(gen, True) is None]
        return carry

    lax.fori_loop(0, n_blk, step, 0)
    o = acc_scr[l_ctx:, :]
    g = g_ref[0, l_ctx:, :]
    o = o * lax.rsqrt(jnp.mean(o * o, axis=-1, keepdims=True) + EPS) * ng_ref[...]
    o_ref[0] = o * (g * (1.0 / (1.0 + jnp.exp(-g))))


def _hgrn_mix(u_all, lb, norm_g, l_ctx):
    bsz, l_tot, _ = u_all.shape
    nh = HG_HEADS
    n_blk = l_tot // HG_TB
    lbh = lb.reshape(nh, 1, HG_DK)
    lbp = jnp.concatenate([jnp.log(lbh), jnp.log1p(-lbh),
                           jnp.zeros((nh, _SUBLANES - 2, HG_DK), jnp.float32)], axis=1)
    cum_m, sel_m, msk_m = _hgrn_consts()

    def slab(k):
        return pl.BlockSpec((1, l_tot, HG_DK), lambda b, h, k=k: (b, 0, k * nh + h))

    def const(a):
        return pl.BlockSpec(a.shape, lambda b, h: (0,) * a.ndim)

    return pl.pallas_call(
        partial(_hgrn_body, n_blk=n_blk, l_ctx=l_ctx),
        grid=(bsz, nh),
        in_specs=[slab(0), slab(1), slab(2), slab(4), slab(5),
                  pl.BlockSpec((1, _SUBLANES, HG_DK), lambda b, h: (h, 0, 0)),
                  pl.BlockSpec((1, HG_DK), lambda b, h: (0, 0)),
                  const(cum_m), const(sel_m), const(msk_m)],
        out_specs=pl.BlockSpec((1, l_tot - l_ctx, HG_DK), lambda b, h: (b, 0, h)),
        out_shape=jax.ShapeDtypeStruct((bsz, l_tot - l_ctx, W_HALF), jnp.float32),
        scratch_shapes=[pltpu.VMEM((l_tot, HG_DK), jnp.float32), pltpu.VMEM((2, HG_DK, HG_DK), jnp.float32)],
        compiler_params=pltpu.CompilerParams(dimension_semantics=("parallel", "parallel"),
                                             vmem_limit_bytes=_VMEM_LIMIT),
        name="hgrn_mix",
    )(u_all, u_all, u_all, u_all, u_all, lbp, norm_g.reshape(1, HG_DK), cum_m, sel_m, msk_m)


def _s5_weights(lam_re, lam_im, log_dt, b_re, b_im, c_re, c_im):
    t_len, f32 = S5_T, jnp.float32
    lre = jnp.minimum(lam_re, S5_MAX_RE)
    dt = jnp.exp(log_dt)[..., None]
    ar, ai = lre * dt, lam_im * dt
    k = jnp.arange(t_len + 1, dtype=f32)[:, None, None, None]
    mag = jnp.exp(k * ar)
    pre, pim = mag * jnp.cos(k * ai), mag * jnp.sin(k * ai)
    nr, ni, den = pre[1] - 1.0, pim[1], lre * lre + lam_im * lam_im
    cr, ci = (nr * lre + ni * lam_im) / den, (ni * lre - nr * lam_im) / den
    cbr = cr[..., None] * b_re - ci[..., None] * b_im
    cbi = cr[..., None] * b_im + ci[..., None] * b_re
    c_re_t, c_im_t = jnp.swapaxes(c_re, 1, 2), jnp.swapaxes(c_im, 1, 2)
    clre = c_re_t * pre[..., None] - c_im_t * pim[..., None]
    clim = c_re_t * pim[..., None] + c_im_t * pre[..., None]
    g = lam_re.shape[1]

    def w_dir(d, e):
        pr, pi = pre[e, d], pim[e, d]
        wr = pr[..., None] * cbr[d] - pi[..., None] * cbi[d]
        wi = pr[..., None] * cbi[d] + pi[..., None] * cbr[d]
        return [jnp.transpose(z, (1, 0, 3, 2)).reshape(g, t_len * S5_H, S5_P) for z in (wr, wi)]

    wf, wb = w_dir(0, jnp.arange(t_len - 1, -1, -1)), w_dir(1, jnp.arange(t_len))
    w = jnp.concatenate(wf + wf[::-1] + wb + wb[::-1], axis=-1)
    dec = jnp.concatenate([pre[t_len, 0], pre[t_len, 0], -pim[t_len, 0], pim[t_len, 0],
                           pre[t_len, 1], pre[t_len, 1], -pim[t_len, 1], pim[t_len, 1]], axis=-1)[:, None, :]

    def m_dir(d, e):
        mr, mi = clre[e, d], -clim[e, d]
        return [jnp.transpose(z, (1, 2, 0, 3)).reshape(g, S5_P, t_len * S5_H) for z in (mr, mi)]

    m = jnp.concatenate(m_dir(0, jnp.arange(1, t_len + 1)) + m_dir(1, jnp.arange(t_len, 0, -1)), axis=1)
    kk = (jnp.einsum('kdgpo,dgpi->kdgoi', clre[:t_len], cbr, precision=_HI)
          - jnp.einsum('kdgpo,dgpi->kdgoi', clim[:t_len], cbi, precision=_HI))
    delta = jnp.arange(t_len)[None, :] - jnp.arange(t_len)[:, None]
    kf = kk[jnp.clip(delta, 0, t_len - 1), 0]
    kb = kk[jnp.clip(-delta, 0, t_len - 1), 1]
    dl = delta[:, :, None, None, None]
    toep = jnp.where(dl > 0, kf, 0.0) + jnp.where(dl < 0, kb, 0.0) + jnp.where(dl == 0, kf + kb, 0.0)
    toep = jnp.transpose(toep, (2, 0, 4, 1, 3)).reshape(g, t_len * S5_H, t_len * S5_H)
    return w, dec, m, toep


def _s5_body(x_ref, w_ref, dec_ref, m_ref, toep_ref, y_ref, z_scr, s_scr, *, bsz, n_ctx, n_lat):
    p2 = 2 * S5_P
    n_all = n_ctx + n_lat
    xb = x_ref[0].astype(jnp.bfloat16)
    z_scr[...] = jnp.dot(xb, w_ref[0], preferred_element_type=jnp.float32)
    dec = jnp.broadcast_to(dec_ref[0], (bsz, 4 * p2))
    a1f, a2f = dec[:, 0:p2], dec[:, p2:2 * p2]
    a1b, a2b = dec[:, 2 * p2:3 * p2], dec[:, 3 * p2:4 * p2]

    def step(i, carry):
        sf, tf, sb, tb = carry
        rf = pl.multiple_of(i * bsz, bsz)
        cb = jnp.where(i < n_ctx, n_ctx - 1 - i, n_all + n_ctx - 1 - i)
        rb = pl.multiple_of(cb * bsz, bsz)
        s_scr[pl.ds(rf, bsz), 0:p2] = sf
        s_scr[pl.ds(rb, bsz), p2:2 * p2] = sb
        zf = z_scr[pl.ds(rf, bsz), 0:2 * p2]
        zb = z_scr[pl.ds(rb, bsz), 2 * p2:4 * p2]
        return (sf * a1f + tf * a2f + zf[:, :p2], tf * a1f - sf * a2f + zf[:, p2:],
                sb * a1b + tb * a2b + zb[:, :p2], tb * a1b - sb * a2b + zb[:, p2:])

    zero = jnp.zeros((bsz, p2), jnp.float32)
    lax.fori_loop(0, n_all, step, (zero, zero, zero, zero), unroll=4)
    r0 = n_ctx * bsz
    y_ref[0] = (jnp.dot(xb[r0:], toep_ref[0], preferred_element_type=jnp.float32)
                + jnp.dot(s_scr[r0:, :].astype(jnp.bfloat16), m_ref[0], preferred_element_type=jnp.float32))


def _s5_mix(u_c, u_l, lam_re, lam_im, log_dt, b_re, b_im, c_re, c_im):
    bsz, l_ctx, gh = u_c.shape
    l_lat = u_l.shape[1]
    g, t_len = gh // S5_H, S5_T
    n_ctx, n_lat = l_ctx // t_len, l_lat // t_len
    w, dec, m, toep = _s5_weights(lam_re, lam_im, log_dt, b_re, b_im, c_re, c_im)

    def chunked(u, n):
        return jnp.transpose(u.reshape(bsz, n, t_len, g, S5_H), (3, 1, 0, 2, 4)).reshape(g, n * bsz, t_len * S5_H)

    x = jnp.concatenate([chunked(u_c, n_ctx), chunked(u_l, n_lat)], axis=1)
    rows, k = (n_ctx + n_lat) * bsz, t_len * S5_H
    y = pl.pallas_call(
        partial(_s5_body, bsz=bsz, n_ctx=n_ctx, n_lat=n_lat),
        grid=(g,),
        in_specs=[pl.BlockSpec((1, rows, k), lambda i: (i, 0, 0)),
                  pl.BlockSpec((1, k, 8 * S5_P), lambda i: (i, 0, 0)),
                  pl.BlockSpec((1, 1, 8 * S5_P), lambda i: (i, 0, 0)),
                  pl.BlockSpec((1, 4 * S5_P, k), lambda i: (i, 0, 0)),
                  pl.BlockSpec((1, k, k), lambda i: (i, 0, 0))],
        out_specs=pl.BlockSpec((1, n_lat * bsz, k), lambda i: (i, 0, 0)),
        out_shape=jax.ShapeDtypeStruct((g, n_lat * bsz, k), jnp.float32),
        scratch_shapes=[pltpu.VMEM((rows, 8 * S5_P), jnp.float32), pltpu.VMEM((rows, 4 * S5_P), jnp.float32)],
        compiler_params=pltpu.CompilerParams(dimension_semantics=("parallel",), vmem_limit_bytes=_VMEM_LIMIT),
        name="s5_mix",
    )(x, w.astype(jnp.bfloat16), dec, m.astype(jnp.bfloat16), toep.astype(jnp.bfloat16))
    return jnp.transpose(y.reshape(g, n_lat, bsz, t_len, S5_H), (2, 1, 3, 0, 4)).reshape(bsz, l_lat, gh)


def _glu_body(y_ref, u_ref, d_ref, w_ref, b_ref, o_ref):
    y = _gelu_tanh(y_ref[0] + u_ref[0] * d_ref[...])
    z = jnp.dot(y.astype(jnp.bfloat16), w_ref[...], preferred_element_type=jnp.float32) + b_ref[...]
    o_ref[0] = y * (1.0 / (1.0 + jnp.exp(-z)))


def _s5_out(y, u_all, l_ctx, col0, d_skip, glu_w, glu_b):
    bsz, l, c = y.shape
    tm = l_ctx
    vec = pl.BlockSpec((1, c), lambda b, i: (0, 0))
    return pl.pallas_call(
        _glu_body,
        grid=(bsz, l // tm),
        in_specs=[pl.BlockSpec((1, tm, c), lambda b, i: (b, i, 0)),
                  pl.BlockSpec((1, tm, c), lambda b, i: (b, i + 1, col0 // c)), vec,
                  pl.BlockSpec((c, c), lambda b, i: (0, 0)), vec],
        out_specs=pl.BlockSpec((1, tm, c), lambda b, i: (b, i, 0)),
        out_shape=jax.ShapeDtypeStruct(y.shape, jnp.float32),
        compiler_params=pltpu.CompilerParams(dimension_semantics=("parallel", "parallel")),
        name="s5_glu",
    )(y, u_all, d_skip.reshape(1, c), glu_w.astype(jnp.bfloat16), glu_b.reshape(1, c))


def _odd_mixer(u, l_ctx, lb, hg_norm_g, lam_re, lam_im, log_dt, b_re, b_im, c_re, c_im, d_skip, glu_w, glu_b):
    w = W_HALF
    hg_l = _hgrn_mix(u, lb, hg_norm_g, l_ctx)
    s_c, s_l = u[:, :l_ctx, 3 * w:4 * w], u[:, l_ctx:, 3 * w:4 * w]
    y_l = _s5_mix(s_c, s_l, lam_re, lam_im, log_dt, b_re, b_im, c_re, c_im)
    return hg_l, _s5_out(y_l, u, l_ctx, 3 * w, d_skip, glu_w, glu_b)


def _router_body(x_ref, g_ref, sh_ref, sc_ref, wr_ref, h_ref, lg_ref):
    x = x_ref[0]
    h = x * lax.rsqrt(jnp.mean(x * x, axis=-1, keepdims=True) + EPS) * g_ref[...]
    h = h * (1.0 + sc_ref[0]) + sh_ref[0]
    hb = h.astype(jnp.bfloat16)
    lg_ref[0] = jnp.dot(hb, wr_ref[...], preferred_element_type=jnp.float32)
    bits = pltpu.bitcast(hb.astype(jnp.float32), jnp.uint32)
    half = bits.shape[1] // 2
    h_ref[0] = (bits[:, :half] >> 16) | (bits[:, half:] & jnp.uint32(0xFFFF0000))


def _router(x, norm_g, shift, scale, w_router):
    bsz, n, d = x.shape
    tm = min(n, 512)
    vec = pl.BlockSpec((1, 1, d), lambda b, i: (b, 0, 0))
    return pl.pallas_call(
        _router_body,
        grid=(bsz, n // tm),
        in_specs=[pl.BlockSpec((1, tm, d), lambda b, i: (b, i, 0)),
                  pl.BlockSpec((1, d), lambda b, i: (0, 0)), vec, vec,
                  pl.BlockSpec((d, N_EXPERTS), lambda b, i: (0, 0))],
        out_specs=[pl.BlockSpec((1, tm, d // 2), lambda b, i: (b, i, 0)),
                   pl.BlockSpec((1, tm, N_EXPERTS), lambda b, i: (b, i, 0))],
        out_shape=[jax.ShapeDtypeStruct((bsz, n, d // 2), jnp.uint32),
                   jax.ShapeDtypeStruct((bsz, n, N_EXPERTS), jnp.float32)],
        compiler_params=pltpu.CompilerParams(dimension_semantics=("parallel", "parallel"),
                                             vmem_limit_bytes=_VMEM_LIMIT),
        name="moe_router",
    )(x, norm_g.reshape(1, d), jnp.broadcast_to(shift, (bsz, 1, d)), jnp.broadcast_to(scale, (bsz, 1, d)),
      w_router.astype(jnp.bfloat16))


def _expert_body(idx_ref, h_ref, gate_ref, mod_ref, wg_ref, wu_ref, wd_ref, y_ref, x_scr, *, cap, n_f):
    def grab(g, carry):
        base = pl.multiple_of(g * _SUBLANES, _SUBLANES)
        for r in range(_SUBLANES):
            x_scr[pl.ds(base + r, 1), :] = h_ref[0, pl.ds(idx_ref[0, 0, base + r], 1), :]
        return carry

    lax.fori_loop(0, cap // _SUBLANES, grab, 0)
    w32 = x_scr[...]
    xb = jnp.concatenate([pltpu.bitcast(w32 << 16, jnp.float32),
                          pltpu.bitcast(w32 & jnp.uint32(0xFFFF0000), jnp.float32)], axis=1).astype(jnp.bfloat16)
    tf = wg_ref.shape[3] // n_f
    acc = None
    for f in range(n_f):
        cs = slice(f * tf, (f + 1) * tf)
        a = jnp.dot(xb, wg_ref[0, 0, :, cs], preferred_element_type=jnp.float32)
        u = jnp.dot(xb, wu_ref[0, 0, :, cs], preferred_element_type=jnp.float32)
        hh = (a * (1.0 / (1.0 + jnp.exp(-a))) * u).astype(jnp.bfloat16)
        part = jnp.dot(hh, wd_ref[0, 0, cs, :], preferred_element_type=jnp.float32)
        acc = part if acc is None else acc + part
    y_ref[0] = acc * gate_ref[0] * mod_ref[0]


def _expert_ffn(h32, idx, gate, mod, layer, w_gate, w_up, w_down):
    bsz, n, dh = h32.shape
    d = 2 * dh
    cap = idx.shape[-1]
    _, e, _, ff = w_gate.shape
    be = lambda k, b: (b * e + k, 0, 0)
    wspec = lambda s: pl.BlockSpec((1, 1) + s, lambda k, b: (layer, k, 0, 0))
    return pl.pallas_call(
        partial(_expert_body, cap=cap, n_f=3),
        grid=(e, bsz),
        in_specs=[pl.BlockSpec((1, 1, cap), be, memory_space=pltpu.SMEM),
                  pl.BlockSpec((1, n, dh), lambda k, b: (b, 0, 0)),
                  pl.BlockSpec((1, cap, 1), be),
                  pl.BlockSpec((1, 1, d), lambda k, b: (b, 0, 0)),
                  wspec((d, ff)), wspec((d, ff)), wspec((ff, d))],
        out_specs=pl.BlockSpec((1, cap, d), be),
        out_shape=jax.ShapeDtypeStruct((bsz * e, cap, d), jnp.float32),
        scratch_shapes=[pltpu.VMEM((cap, dh), jnp.uint32)],
        compiler_params=pltpu.CompilerParams(dimension_semantics=("parallel", "parallel"),
                                             vmem_limit_bytes=_VMEM_LIMIT_BIG),
        name="moe_expert",
    )(idx.reshape(bsz * e, 1, cap), h32, gate.reshape(bsz * e, cap, 1), jnp.broadcast_to(mod, (bsz, 1, d)),
      w_gate, w_up, w_down)


def _combine_body(idx_ref, y_ref, o_ref, *, cap):
    @pl.when(pl.program_id(1) == 0)
    def _():
        o_ref[...] = jnp.zeros_like(o_ref)

    def group(g, carry):
        base = pl.multiple_of(g * _SUBLANES, _SUBLANES)
        ys = y_ref[0, pl.ds(base, _SUBLANES), :]
        ts = [idx_ref[0, 0, base + r] for r in range(_SUBLANES)]
        cur = [o_ref[0, pl.ds(t, 1), :] for t in ts]
        for r in range(_SUBLANES):
            o_ref[0, pl.ds(ts[r], 1), :] = cur[r] + ys[r:r + 1, :]
        return carry

    lax.fori_loop(0, cap // _SUBLANES, group, 0)


def _combine(y, idx, n):
    be, cap, d = y.shape
    bsz = be // N_EXPERTS
    return pl.pallas_call(
        partial(_combine_body, cap=cap),
        grid=(bsz, N_EXPERTS),
        in_specs=[pl.BlockSpec((1, 1, cap), lambda b, k: (b * N_EXPERTS + k, 0, 0), memory_space=pltpu.SMEM),
                  pl.BlockSpec((1, cap, d), lambda b, k: (b * N_EXPERTS + k, 0, 0))],
        out_specs=pl.BlockSpec((1, n, d), lambda b, k: (b, 0, 0)),
        out_shape=jax.ShapeDtypeStruct((bsz, n, d), jnp.float32),
        compiler_params=pltpu.CompilerParams(dimension_semantics=("parallel", "arbitrary"),
                                             vmem_limit_bytes=_VMEM_LIMIT),
        name="moe_combine",
    )(idx.reshape(be, 1, cap), y)


def _moe(x, norm_g, shift, scale, mod, w_router, layer, w_gate, w_up, w_down):
    bsz, n, d = x.shape
    cap = max(1, (EC_CAPACITY * n) // N_EXPERTS)
    h32, logits = _router(x, norm_g, shift, scale, w_router)
    aff = jax.nn.softmax(logits, axis=-1)
    gate, idx = lax.top_k(jnp.swapaxes(aff, 1, 2), cap)
    y = _expert_ffn(h32, idx, gate, mod, layer, w_gate, w_up, w_down)
    return _combine(y, idx, n)


def kernel(x, c, ctx, c_ctx, mod_w, mod_b, norm1_g, norm2_g, out_w, even_in_w, lru_conv_w, lru_conv_b, lru_wa, lru_ba, lru_wx, lru_bx, lru_lam, hy_conv_w, hy_conv_b, hy_w1, hy_b1, hy_w2, hy_b2, hy_w3, hy_freq, hy_skip, odd_in_w, hg_lb, hg_norm_g, s5_lam_re, s5_lam_im, s5_log_dt, s5_b_re, s5_b_im, s5_c_re, s5_c_im, s5_d, s5_glu_w, s5_glu_b, router_w, ex_w_gate, ex_w_up, ex_w_down, final_g):
    rows = x.shape[1] // GRID_W
    l_ctx = ctx.shape[1]
    p = jax.nn.softmax(hg_lb, axis=0)
    lb_all = jnp.clip(jnp.cumsum(p, axis=0) - p[0], 0.0, 1.0 - 1e-4)
    ex_wg, ex_wu, ex_wd = (t.astype(jnp.bfloat16) for t in (ex_w_gate, ex_w_up, ex_w_down))
    sc_lat = jax.nn.silu(c)
    sc_ctx = jax.nn.silu(c_ctx)[None]
    for i in range(DEPTH):
        need_ctx = i < DEPTH - 1
        j = i // 2
        ml = [t[:, None, :] for t in jnp.split(sc_lat @ mod_w[i] + mod_b[i], 6, axis=-1)]
        mc = [t[:, None, :] for t in jnp.split(sc_ctx @ mod_w[i] + mod_b[i], 6, axis=-1)]
        shift, scale, gate = (_two_part(mc[k], ml[k]) for k in range(3))
        if i % 2 == 0:
            x_all = jnp.concatenate([ctx, x], axis=1)
            u = _norm_mm(x_all, norm1_g[i], shift, scale, even_in_w[j], l_ctx)
            h_f, h_b, yb = _even_mixer(u, l_ctx, lru_conv_w[j], lru_conv_b[j], lru_wa[j], lru_ba[j], lru_wx[j],
                                       lru_bx[j], lru_lam[j], hy_conv_w[j], hy_conv_b[j], hy_w1[j], hy_b1[j],
                                       hy_w2[j], hy_b2[j], hy_w3[j], hy_freq[j], hy_skip[j])
            x_all = _proj_res_lru(h_f, h_b, u, yb, out_w[i], x_all, gate, l_ctx)
            ctx, x = x_all[:, :l_ctx], x_all[:, l_ctx:]
        else:
            assert not need_ctx
            x_all = jnp.concatenate([ctx, _to_col_major(x, rows)], axis=1)
            u = _norm_mm(x_all, norm1_g[i], shift, scale, odd_in_w[j], l_ctx)
            hg_l, s5_l = _odd_mixer(u, l_ctx, lb_all[i], hg_norm_g[j], s5_lam_re[j], s5_lam_im[j],
                                    s5_log_dt[j], s5_b_re[j], s5_b_im[j], s5_c_re[j], s5_c_im[j], s5_d[j],
                                    s5_glu_w[j], s5_glu_b[j])
            x = _to_row_major(_proj_res(hg_l, s5_l, out_w[i], x_all, gate, l_ctx, l_ctx), rows)
        ex = (router_w[i], i, ex_wg, ex_wu, ex_wd)
        moe_l = _moe(x, norm2_g[i], ml[3], ml[4], ml[5], *ex)
        if need_ctx:
            x = x + moe_l
            ctx = ctx + _moe(ctx, norm2_g[i], mc[3], mc[4], mc[5], *ex)
    return _add_norm(x, moe_l, final_g)
```

```python
import math
from functools import partial

import numpy as np
import jax
import jax.numpy as jnp
from jax import lax
from jax.experimental import pallas as pl
from jax.experimental.pallas import tpu as pltpu

D_MODEL = 1024
DEPTH = 2
GRID_W = 64
W_HALF = D_MODEL // 2
EPS = 1e-6
LRU_HEADS = 8
LRU_HEAD_DIM = W_HALF // LRU_HEADS
LRU_C = 8.0
HY_ORDER = 2
HY_BANDS = 16
HY_MAX_DECAY = math.log(1e-2) / 0.3
HY_MIN_DECAY = math.log(1e-2) / 1.5
HG_HEADS = 4
HG_DK = W_HALF // HG_HEADS
HG_C = 16
HG_SUB = 4
HG_TB = 256
S5_H = 16
S5_P = 64
S5_MAX_RE = -1e-4
S5_T = 16
_DFT_LOW = 64
N_EXPERTS = 16
EC_CAPACITY = 2

_SUBLANES = 8
_DW_ROWS = 256
_VMEM_LIMIT = 48 * 1024 * 1024
_VMEM_LIMIT_BIG = 56 * 1024 * 1024
_HI = lax.Precision.HIGHEST
_NEG = -1e30


def _mm_body(a_ref, b_ref, o_ref):
    o_ref[...] = jnp.dot(a_ref[...].astype(jnp.bfloat16), b_ref[...],
                         preferred_element_type=jnp.float32)


def _pick(n, pref):
    for t in pref:
        if n % t == 0:
            return t
    return n


def _mm(a, b):
    m, k = a.shape
    n = b.shape[1]
    tm = _pick(m, (512, 256, 128, 64, 32, 16, 8))
    tn = _pick(n, (512, 256, 128))
    return pl.pallas_call(
        _mm_body,
        grid=(m // tm, n // tn),
        in_specs=[pl.BlockSpec((tm, k), lambda i, j: (i, 0)),
                  pl.BlockSpec((k, tn), lambda i, j: (0, j))],
        out_specs=pl.BlockSpec((tm, tn), lambda i, j: (i, j)),
        out_shape=jax.ShapeDtypeStruct((m, n), jnp.float32),
        compiler_params=pltpu.CompilerParams(
            dimension_semantics=("parallel", "parallel"), vmem_limit_bytes=_VMEM_LIMIT),
        name="mm",
    )(a, b.astype(jnp.bfloat16))


def _mm3(a, b):
    bsz, l, k = a.shape
    return _mm(a.reshape(bsz * l, k), b).reshape(bsz, l, b.shape[1])


def _add_norm_body(x_ref, m_ref, g_ref, o_ref):
    x = x_ref[...] + m_ref[...]
    o_ref[...] = x * lax.rsqrt(jnp.mean(x * x, axis=-1, keepdims=True) + EPS) * g_ref[...]


def _add_norm(x, m, g):
    bsz, n, d = x.shape
    tm = min(n, 512)
    blk = pl.BlockSpec((1, tm, d), lambda b, i: (b, i, 0))
    return pl.pallas_call(
        _add_norm_body,
        grid=(bsz, n // tm),
        in_specs=[blk, blk, pl.BlockSpec((1, 1, d), lambda b, i: (0, 0, 0))],
        out_specs=blk,
        out_shape=jax.ShapeDtypeStruct(x.shape, jnp.float32),
        compiler_params=pltpu.CompilerParams(dimension_semantics=("parallel", "parallel")),
        name="add_norm",
    )(x, m, g.reshape(1, 1, d))


def _two_part(t_ctx, t_lat):
    bsz, _, d = t_lat.shape
    return jnp.stack([jnp.broadcast_to(t_ctx, (bsz, 1, d)), t_lat], axis=1).reshape(2 * bsz, 1, d)


def _norm_mm_body(x_ref, g_ref, sh_ref, sc_ref, w_ref, o_ref):
    x = x_ref[0]
    h = x * lax.rsqrt(jnp.mean(x * x, axis=-1, keepdims=True) + EPS) * g_ref[...]
    h = h * (1.0 + sc_ref[0]) + sh_ref[0]
    o_ref[0] = jnp.dot(h.astype(jnp.bfloat16), w_ref[...], preferred_element_type=jnp.float32)


def _norm_mm(x_all, norm_g, shift, scale, w, l_ctx):
    bsz, l_tot, d = x_all.shape
    n = w.shape[1]
    tm = l_ctx
    vec = pl.BlockSpec((1, 1, d), lambda b, j: (2 * b + jnp.minimum(j, 1), 0, 0))
    return pl.pallas_call(
        _norm_mm_body,
        grid=(bsz, l_tot // tm),
        in_specs=[pl.BlockSpec((1, tm, d), lambda b, j: (b, j, 0)),
                  pl.BlockSpec((1, d), lambda b, j: (0, 0)), vec, vec,
                  pl.BlockSpec((d, n), lambda b, j: (0, 0))],
        out_specs=pl.BlockSpec((1, tm, n), lambda b, j: (b, j, 0)),
        out_shape=jax.ShapeDtypeStruct((bsz, l_tot, n), jnp.float32),
        compiler_params=pltpu.CompilerParams(dimension_semantics=("parallel", "parallel"),
                                             vmem_limit_bytes=_VMEM_LIMIT),
        name="norm_mm",
    )(x_all, norm_g.reshape(1, d), shift, scale, w.astype(jnp.bfloat16))


def _proj_res_body(ya_ref, yb_ref, wa_ref, wb_ref, x_ref, gt_ref, o_ref):
    y = (jnp.dot(ya_ref[0].astype(jnp.bfloat16), wa_ref[...], preferred_element_type=jnp.float32)
         + jnp.dot(yb_ref[0].astype(jnp.bfloat16), wb_ref[...], preferred_element_type=jnp.float32))
    o_ref[0] = x_ref[0] + gt_ref[0] * y


def _proj_res_lru_body(hf_ref, hb_ref, u_ref, yb_ref, wa_ref, wb_ref, x_ref, gt_ref, o_ref):
    ya = (hf_ref[0] + hb_ref[0]) * _gelu_tanh(u_ref[0])
    y = (jnp.dot(ya.astype(jnp.bfloat16), wa_ref[...], preferred_element_type=jnp.float32)
         + jnp.dot(yb_ref[0].astype(jnp.bfloat16), wb_ref[...], preferred_element_type=jnp.float32))
    o_ref[0] = x_ref[0] + gt_ref[0] * y


def _proj_res_lru(h_f, h_b, u_all, yb, w, x_all, gate, l_ctx):
    bsz, l_tot, ka = h_f.shape
    d = w.shape[1]
    tm = l_ctx
    blk = lambda k, c=0: pl.BlockSpec((1, tm, k), lambda b, j: (b, j, c))
    return pl.pallas_call(
        _proj_res_lru_body,
        grid=(bsz, l_tot // tm),
        in_specs=[blk(ka), blk(ka), blk(ka, 1), blk(yb.shape[2]),
                  pl.BlockSpec((ka, d), lambda b, j: (0, 0)), pl.BlockSpec((yb.shape[2], d), lambda b, j: (0, 0)),
                  blk(d), pl.BlockSpec((1, 1, d), lambda b, j: (2 * b + jnp.minimum(j, 1), 0, 0))],
        out_specs=blk(d),
        out_shape=jax.ShapeDtypeStruct((bsz, l_tot, d), jnp.float32),
        compiler_params=pltpu.CompilerParams(dimension_semantics=("parallel", "parallel"),
                                             vmem_limit_bytes=_VMEM_LIMIT),
        name="proj_res_lru",
    )(h_f, h_b, u_all, yb, w[:ka].astype(jnp.bfloat16), w[ka:].astype(jnp.bfloat16), x_all, gate)


def _proj_res(ya, yb, w, x_all, gate, l_ctx, row0):
    bsz, l_y, ka = ya.shape
    d = w.shape[1]
    tm = l_ctx
    j0 = row0 // tm
    yblk = lambda k: pl.BlockSpec((1, tm, k), lambda b, j: (b, j, 0))
    return pl.pallas_call(
        _proj_res_body,
        grid=(bsz, l_y // tm),
        in_specs=[yblk(ka), yblk(yb.shape[2]),
                  pl.BlockSpec((ka, d), lambda b, j: (0, 0)), pl.BlockSpec((yb.shape[2], d), lambda b, j: (0, 0)),
                  pl.BlockSpec((1, tm, d), lambda b, j: (b, j + j0, 0)),
                  pl.BlockSpec((1, 1, d), lambda b, j: (2 * b + jnp.minimum(j + j0, 1), 0, 0))],
        out_specs=pl.BlockSpec((1, tm, d), lambda b, j: (b, j, 0)),
        out_shape=jax.ShapeDtypeStruct((bsz, l_y, d), jnp.float32),
        compiler_params=pltpu.CompilerParams(dimension_semantics=("parallel", "parallel"),
                                             vmem_limit_bytes=_VMEM_LIMIT),
        name="proj_res",
    )(ya, yb, w[:ka].astype(jnp.bfloat16), w[ka:].astype(jnp.bfloat16), x_all, gate)


def _gelu_tanh(x):
    return 0.5 * x * (1.0 + jnp.tanh(math.sqrt(2.0 / math.pi) * (x + 0.044715 * (x * x * x))))


def _dwconv_body(u_ref, w_ref, o_ref, *rest, l_ctx, taps, left, emit_bf16):
    l_tot = u_ref.shape[1]
    tb, h = _DW_ROWS, _SUBLANES
    row = lax.broadcasted_iota(jnp.int32, (tb, 1), 0)

    def block(j, carry):
        r0 = pl.multiple_of(j * tb, tb)
        lo = jnp.where(r0 < l_ctx, 0, l_ctx)
        hi = jnp.where(r0 < l_ctx, l_ctx, l_tot)
        prev = u_ref[0, pl.ds(pl.multiple_of(jnp.maximum(r0 - h, 0), h), h), :]
        nxt = u_ref[0, pl.ds(pl.multiple_of(jnp.minimum(r0 + tb, l_tot - h), h), h), :]
        ext = jnp.concatenate([prev, u_ref[0, pl.ds(r0, tb), :], nxt], axis=0)
        y = jnp.broadcast_to(w_ref[0, taps:taps + 1, :], (tb, ext.shape[1]))
        for t in range(taps):
            off = t - left
            src = r0 + row + off
            y = y + jnp.where((src >= lo) & (src < hi), ext[h + off:h + off + tb, :], 0.0) * w_ref[0, t:t + 1, :]
        o_ref[0, pl.ds(r0, tb), :] = y
        if emit_bf16:
            rest[0][0, pl.ds(r0, tb), :] = y.astype(jnp.bfloat16)
        return carry

    lax.fori_loop(0, l_tot // tb, block, 0)


def _dwconv_slabs(u_all, col0, w, b, l_ctx, emit_bf16):
    bsz, l_tot, _ = u_all.shape
    taps, nc = w.shape
    c = W_HALF
    n = nc // c
    wb = jnp.concatenate([w, b[None], jnp.zeros((_SUBLANES - taps - 1, nc), jnp.float32)], axis=0)
    wb = wb.reshape(_SUBLANES, n, c).transpose(1, 0, 2)
    blk = pl.BlockSpec((1, l_tot, c), lambda bb, s: (bb, 0, s))
    shapes = [jax.ShapeDtypeStruct((bsz, l_tot, nc), jnp.float32)]
    if emit_bf16:
        shapes.append(jax.ShapeDtypeStruct((bsz, l_tot, nc), jnp.bfloat16))
    out = pl.pallas_call(
        partial(_dwconv_body, l_ctx=l_ctx, taps=taps, left=(taps - 1) // 2, emit_bf16=emit_bf16),
        grid=(bsz, n),
        in_specs=[pl.BlockSpec((1, l_tot, c), lambda bb, s: (bb, 0, s + col0 // c)),
                  pl.BlockSpec((1, _SUBLANES, c), lambda bb, s: (s, 0, 0))],
        out_specs=[blk] * len(shapes),
        out_shape=shapes,
        compiler_params=pltpu.CompilerParams(dimension_semantics=("parallel", "parallel"),
                                             vmem_limit_bytes=_VMEM_LIMIT),
        name="dwconv",
    )(u_all, wb)
    return out if emit_bf16 else out[0]


def _to_col_major(t, rows):
    bsz, l, ch = t.shape
    return t.reshape(bsz, rows, GRID_W, ch).transpose(0, 2, 1, 3).reshape(bsz, l, ch)


def _to_row_major(t, rows):
    bsz, l, ch = t.shape
    return t.reshape(bsz, GRID_W, rows, ch).transpose(0, 2, 1, 3).reshape(bsz, l, ch)


def _lru_body(xf_ref, xb_ref, wa_ref, wx_ref, par_ref, hf_ref, hb_ref,
              af_scr, bf_scr, ab_scr, bb_scr, sf_scr, sb_scr, *, tb):
    @pl.when(pl.program_id(1) == 0)
    def _():
        sf_scr[...] = jnp.zeros_like(sf_scr)
        sb_scr[...] = jnp.zeros_like(sb_scr)

    def coeffs(x, d, a_scr, b_scr):
        xb = x.astype(jnp.bfloat16)
        r = jnp.dot(xb, wa_ref[d], preferred_element_type=jnp.float32) + par_ref[d, 0:1, :]
        i = jnp.dot(xb, wx_ref[d], preferred_element_type=jnp.float32) + par_ref[d, 1:2, :]
        r = 1.0 / (1.0 + jnp.exp(-r))
        i = 1.0 / (1.0 + jnp.exp(-i))
        log_a = -LRU_C * r * par_ref[d, 2:3, :]
        a = jnp.exp(log_a)
        a_scr[...] = a
        b_scr[...] = jnp.sqrt(-jnp.tanh(log_a) * (a * a + 1.0)) * i * x

    coeffs(xf_ref[0], 0, af_scr, bf_scr)
    coeffs(xb_ref[0], 1, ab_scr, bb_scr)

    def tile(i, carry):
        hf, hb = carry
        base_f = pl.multiple_of(i * _SUBLANES, _SUBLANES)
        base_b = pl.multiple_of(tb - _SUBLANES - i * _SUBLANES, _SUBLANES)
        for r in range(_SUBLANES):
            rf = pl.ds(base_f + r, 1)
            hf = af_scr[rf, :] * hf + bf_scr[rf, :]
            hf_ref[0, rf, :] = hf
            rb = pl.ds(base_b + (_SUBLANES - 1 - r), 1)
            hb = ab_scr[rb, :] * hb + bb_scr[rb, :]
            hb_ref[0, rb, :] = hb
        return hf, hb

    hf, hb = lax.fori_loop(0, tb // _SUBLANES, tile, (sf_scr[...], sb_scr[...]))
    sf_scr[...] = hf
    sb_scr[...] = hb


def _block_diag(w):
    two, h, n, _ = w.shape
    eye = jnp.eye(h, dtype=w.dtype)
    return (w[:, :, :, None, :] * eye[None, :, None, :, None]).reshape(two, h * n, h * n)


def _lru_scan(xc, l_ctx, wa, ba, wx, bx, lam):
    bsz, l_tot, ch = xc.shape
    tb = l_ctx
    nb = l_tot // tb
    par = jnp.stack([ba.reshape(2, ch), bx.reshape(2, ch), jax.nn.softplus(-lam).reshape(2, ch)], axis=1)
    par = jnp.concatenate([par, jnp.zeros((2, _SUBLANES - 3, ch), jnp.float32)], axis=1)
    fwd = pl.BlockSpec((1, tb, ch), lambda b, j: (b, j, 0))
    bwd = pl.BlockSpec((1, tb, ch), lambda b, j: (b, jnp.where(j == 0, 0, nb - j), 0))
    wspec = pl.BlockSpec((2, ch, ch), lambda b, j: (0, 0, 0))
    return pl.pallas_call(
        partial(_lru_body, tb=tb),
        grid=(bsz, nb),
        in_specs=[fwd, bwd, wspec, wspec, pl.BlockSpec((2, _SUBLANES, ch), lambda b, j: (0, 0, 0))],
        out_specs=[fwd, bwd],
        out_shape=[jax.ShapeDtypeStruct(xc.shape, jnp.float32)] * 2,
        scratch_shapes=[pltpu.VMEM((tb, ch), jnp.float32)] * 4 + [pltpu.VMEM((1, ch), jnp.float32)] * 2,
        compiler_params=pltpu.CompilerParams(dimension_semantics=("parallel", "arbitrary")),
        name="lru_scan",
    )(xc, xc, _block_diag(wa).astype(jnp.bfloat16), _block_diag(wx).astype(jnp.bfloat16), par)


def _hyena_filters(l, w1, b1, w2, b2, w3, freq):
    t = jnp.arange(l, dtype=jnp.float32)
    t_unit = t / max(l - 1, 1)
    bands = jnp.linspace(1e-4, HY_BANDS - 1, HY_BANDS, dtype=jnp.float32)
    ang = (2.0 * math.pi / l) * t[:, None] * bands[None, :]
    z = jnp.concatenate([t_unit[:, None], jnp.cos(ang), -jnp.sin(ang)], axis=-1)
    h = jnp.sin(freq * (z @ w1 + b1))
    h = jnp.sin(freq * (h @ w2 + b2))
    h = h @ w3
    dist = jnp.abs(t - (l // 2)) / (l / 2.0)
    deltas = jnp.abs(jnp.linspace(HY_MIN_DECAY, HY_MAX_DECAY, HY_ORDER * W_HALF, dtype=jnp.float32))
    h = h * jnp.exp(-dist[:, None] * deltas[None, :])
    h = h * lax.rsqrt(jnp.sum(h * h, axis=0, keepdims=True) + EPS)
    return h


def _dft_mats(l, tf):
    n = 2 * l
    nl = _DFT_LOW
    nh = l // nl
    t = jnp.arange(l, dtype=jnp.int32)
    w = 2.0 * math.pi / n

    def tables(tv):
        ah = (((nl * jnp.arange(nh, dtype=jnp.int32))[:, None] * tv[None, :]) % n).astype(jnp.float32) * w
        al = ((jnp.arange(nl, dtype=jnp.int32)[:, None] * tv[None, :]) % n).astype(jnp.float32) * w
        return jnp.cos(ah), jnp.sin(ah), jnp.cos(al), jnp.sin(al)

    ch, sh, cl, sl = tables(t)
    f_re = (ch[:, None, :] * cl[None] - sh[:, None, :] * sl[None]).reshape(l, l)
    f_im = -(sh[:, None, :] * cl[None] + ch[:, None, :] * sl[None]).reshape(l, l)
    f_im = f_im.at[0].set(jnp.where(t % 2 == 0, 1.0, -1.0))
    f = jnp.concatenate([f_re.reshape(l // tf, tf, l), f_im.reshape(l // tf, tf, l)], axis=1).reshape(n, l)
    tp = t + l // 2
    ch, sh, cl, sl = (z.T for z in tables(tp))
    g_re = (2.0 / n) * (ch[:, :, None] * cl[:, None, :] - sh[:, :, None] * sl[:, None, :]).reshape(l, l)
    g_im = (-2.0 / n) * (sh[:, :, None] * cl[:, None, :] + ch[:, :, None] * sl[:, None, :]).reshape(l, l)
    g_re = g_re.at[:, 0].set(1.0 / n)
    g_im = g_im.at[:, 0].set(jnp.where(tp % 2 == 0, 1.0, -1.0) / n)
    g = jnp.stack([g_re.reshape(l, l // tf, tf), g_im.reshape(l, l // tf, tf)], axis=2).reshape(l, n)
    return f.astype(jnp.bfloat16), g.astype(jnp.bfloat16)


def _spec_body(f_ref, u_ref, a_ref, b_ref, d_ref, y_ref, *, tf, r0, l):
    acc = jnp.dot(f_ref[...], u_ref[0, r0:r0 + l, :], preferred_element_type=jnp.float32)
    xr, xi = acc[:tf], acc[tf:]
    y_ref[0, :tf, :] = (xr * a_ref[...] - xi * b_ref[...]).astype(jnp.bfloat16)
    y_ref[0, tf:, :] = (xr * b_ref[...] + xi * d_ref[...]).astype(jnp.bfloat16)


def _spec_mul(f, u, col, r0, l, a, b, d, tf):
    bsz, l_u, _ = u.shape
    c = a.shape[1]
    coef = pl.BlockSpec((tf, c), lambda i, bb: (i, 0))
    return pl.pallas_call(
        partial(_spec_body, tf=tf, r0=r0, l=l),
        grid=(l // tf, bsz),
        in_specs=[pl.BlockSpec((2 * tf, l), lambda i, bb: (i, 0)),
                  pl.BlockSpec((1, l_u, c), lambda i, bb: (bb, 0, col)), coef, coef, coef],
        out_specs=pl.BlockSpec((1, 2 * tf, c), lambda i, bb: (bb, i, 0)),
        out_shape=jax.ShapeDtypeStruct((bsz, 2 * l, c), jnp.bfloat16),
        compiler_params=pltpu.CompilerParams(dimension_semantics=("parallel", "parallel"),
                                             vmem_limit_bytes=_VMEM_LIMIT),
        name="hy_spec",
    )(f, u, a, b, d)


def _inv_body(g_ref, y_ref, u_ref, x_ref, skip_ref, z_ref, zb_ref):
    acc = jnp.dot(g_ref[...], y_ref[0], preferred_element_type=jnp.float32)
    z = x_ref[0] * (acc + u_ref[0] * skip_ref[...])
    z_ref[0] = z
    zb_ref[0] = z.astype(jnp.bfloat16)


def _inv_gate(g, y, u, ucol, urb, xg, xcol, xrb, skip, tm):
    bsz, n, c = y.shape
    l = n // 2
    blk = pl.BlockSpec((1, tm, c), lambda bb, i: (bb, i, 0))
    return pl.pallas_call(
        _inv_body,
        grid=(bsz, l // tm),
        in_specs=[pl.BlockSpec((tm, n), lambda bb, i: (i, 0)),
                  pl.BlockSpec((1, n, c), lambda bb, i: (bb, 0, 0)),
                  pl.BlockSpec((1, tm, c), lambda bb, i: (bb, i + urb, ucol)),
                  pl.BlockSpec((1, tm, c), lambda bb, i: (bb, i + xrb, xcol)),
                  pl.BlockSpec((1, c), lambda bb, i: (0, 0))],
        out_specs=[blk, blk],
        out_shape=[jax.ShapeDtypeStruct((bsz, l, c), jnp.float32), jax.ShapeDtypeStruct((bsz, l, c), jnp.bfloat16)],
        compiler_params=pltpu.CompilerParams(dimension_semantics=("parallel", "parallel"),
                                             vmem_limit_bytes=_VMEM_LIMIT),
        name="hy_inv",
    )(g, y, u, xg, skip.reshape(1, c))


def _hyena_conv2(hc, hb, r0, l, filt, skip):
    c = hc.shape[2] // 3
    tf = min(512, l // 2)
    tm = min(_DW_ROWS, l)
    f, g = _dft_mats(l, tf)
    hs = _mm(f, filt)
    hs = hs.reshape(l // tf, 2, tf, 2 * c)
    h_re, h_im = hs[:, 0].reshape(l, 2 * c), hs[:, 1].reshape(l, 2 * c)
    slot0 = (jnp.arange(l) == 0)[:, None]
    a = h_re
    b = jnp.where(slot0, 0.0, h_im)
    d = jnp.where(slot0, h_im, h_re)
    y = _spec_mul(f, hb, 0, r0, l, a[:, :c], b[:, :c], d[:, :c], tf)
    z, zb = _inv_gate(g, y, hc, 0, r0 // tm, hc, 1, r0 // tm, skip[0], tm)
    y = _spec_mul(f, zb, 0, 0, l, a[:, c:], b[:, c:], d[:, c:], tf)
    z, _ = _inv_gate(g, y, z, 0, 0, hc, 2, r0 // tm, skip[1], tm)
    return z


def _hyena(u, l_ctx, conv_w, conv_b, w1, b1, w2, b2, w3, freq, skip):
    hc, hb = _dwconv_slabs(u, 2 * W_HALF, conv_w, conv_b, l_ctx, True)
    zs = []
    for r0, l in ((0, l_ctx), (l_ctx, u.shape[1] - l_ctx)):
        filt = _hyena_filters(l, w1, b1, w2, b2, w3, freq)
        zs.append(_hyena_conv2(hc, hb, r0, l, filt, skip))
    return jnp.concatenate(zs, axis=1)


def _even_mixer(u, l_ctx, lru_conv_w, lru_conv_b, lru_wa, lru_ba, lru_wx, lru_bx,
                lru_lam, hy_conv_w, hy_conv_b, hy_w1, hy_b1, hy_w2, hy_b2, hy_w3, hy_freq, hy_skip):
    w = W_HALF
    xc = _dwconv_slabs(u, 0, lru_conv_w, lru_conv_b, l_ctx, False)
    h_f, h_b = _lru_scan(xc, l_ctx, lru_wa, lru_ba, lru_wx, lru_bx, lru_lam)
    yb = _hyena(u, l_ctx, hy_conv_w, hy_conv_b, hy_w1, hy_b1, hy_w2, hy_b2, hy_w3, hy_freq, hy_skip)
    return h_f, h_b, yb


def _split3(x):
    hi = x.astype(jnp.bfloat16)
    r1 = x - hi.astype(jnp.float32)
    mid = r1.astype(jnp.bfloat16)
    lo = (r1 - mid.astype(jnp.float32)).astype(jnp.bfloat16)
    return jnp.concatenate([hi, mid, lo], axis=1)


def _sum3(r, c):
    return r[:, 0:c] + r[:, c:2 * c] + r[:, 2 * c:3 * c]


def _log_sigmoid(x):
    return jnp.minimum(x, 0.0) - jnp.log1p(jnp.exp(-jnp.abs(x)))


def _hgrn_consts():
    tb, c, sb = HG_TB, HG_C, HG_SUB
    r = np.arange(tb)
    same = (r[:, None] // c) == (r[None, :] // c)
    same4 = (r[:, None] // sb) == (r[None, :] // sb)
    le = r[None, :] <= r[:, None]
    sub = (r % c) // sb
    cum, msk = [], []
    for d in range(2):
        tri = le if d == 0 else le.T
        cum.append(np.concatenate([same & tri, same, same4 & tri], axis=0))
        if d == 0:
            mm = [same & (sub[:, None] == i + 1) & (sub[None, :] <= i) for i in range(c // sb - 1)]
        else:
            mm = [same & (sub[:, None] == i) & (sub[None, :] > i) for i in range(c // sb - 1)]
        msk.append(np.stack(mm))
    cum = jnp.asarray(np.stack(cum).astype(np.float32), dtype=jnp.bfloat16)
    return cum, jnp.asarray(np.stack(msk).astype(np.float32))


def _hgrn_body(ff_ref, fb_ref, v_ref, q_ref, g_ref, lbp_ref, ng_ref, cum_ref, msk_ref, o_ref,
               acc_scr, st_scr, *, n_blk, l_ctx):
    tb, c, sb, dk = HG_TB, HG_C, HG_SUB, HG_DK
    n_c = tb // c
    n_sub = c // sb
    acc_scr[...] = jnp.zeros_like(acc_scr)
    st_scr[...] = jnp.zeros_like(st_scr)
    loglb, log1mlb = lbp_ref[0, 0:1, :], lbp_ref[0, 1:2, :]
    pos = lax.broadcasted_iota(jnp.int32, (tb, dk), 0) % sb
    nt = (((1,), (1,)), ((), ()))
    tn = (((0,), (0,)), ((), ()))
    dot = partial(jnp.dot, preferred_element_type=jnp.float32)

    def direction(base, f_ref, d):
        rows = pl.ds(base, tb)
        fp = f_ref[0, rows, :]
        ls = _log_sigmoid(fp)
        b2 = log1mlb + ls
        logf = jnp.maximum(loglb, b2) + jnp.log1p(jnp.exp(-jnp.abs(loglb - b2)))
        lk = log1mlb + (ls - fp)
        v = v_ref[0, rows, :]
        qr = q_ref[0, rows, :]
        q = qr * (1.0 / (1.0 + jnp.exp(-qr)))
        r = dot(cum_ref[d], _split3(logf))
        cum, tot, pre = _sum3(r[0:tb], dk), _sum3(r[tb:2 * tb], dk), _sum3(r[2 * tb:3 * tb], dk)
        g = lk - cum
        cum3 = cum.reshape(n_c, c, dk)
        qe = (q * jnp.exp(cum)).astype(jnp.bfloat16)
        kt = jnp.exp(g + tot).astype(jnp.bfloat16)
        dec = jnp.exp(tot)
        vb = v.astype(jnp.bfloat16)
        yield
        o = jnp.sum(q * jnp.exp(lk), axis=-1, keepdims=True) * v
        for j in range(1, sb):
            sh = j if d == 0 else tb - j
            valid = (pos >= j) if d == 0 else (pos < sb - j)
            gj = pltpu.roll(g, sh, 0)
            vj = pltpu.roll(v, sh, 0)
            e = jnp.exp(jnp.where(valid, cum + gj, _NEG))
            o = o + jnp.sum(q * e, axis=-1, keepdims=True) * vj
        yield
        q4 = (q * jnp.exp(pre)).astype(jnp.bfloat16)
        att = None
        for i in range(n_sub - 1):
            ref_row = sb * (i + 1) - 1 if d == 0 else sb * (i + 1)
            r_i = jnp.broadcast_to(cum3[:, ref_row:ref_row + 1, :], (n_c, c, dk)).reshape(tb, dk)
            kh = jnp.exp(lk + jnp.minimum(r_i - cum, 0.0)).astype(jnp.bfloat16)
            a_i = lax.dot_general(q4, kh, nt, preferred_element_type=jnp.float32) * msk_ref[d, i]
            att = a_i if att is None else att + a_i
        o = o + dot(att.astype(jnp.bfloat16), vb)
        yield
        order = [i if d == 0 else n_c - 1 - i for i in range(n_c)]
        kvs = [lax.dot_general(vb[cc * c:(cc + 1) * c], kt[cc * c:(cc + 1) * c], tn,
                               preferred_element_type=jnp.float32) for cc in order]
        yield
        st = st_scr[d]
        sts = []
        for cc, kv in zip(order, kvs):
            sts.append(st.astype(jnp.bfloat16))
            st = st * dec[cc * c:cc * c + 1, :] + kv
        st_scr[d] = st
        o_st = [None] * n_c
        for cc, sb16 in zip(order, sts):
            o_st[cc] = lax.dot_general(qe[cc * c:(cc + 1) * c], sb16, nt, preferred_element_type=jnp.float32)
        acc_scr[rows, :] += o + jnp.concatenate(o_st, axis=0)

    def step(j, carry):
        base_f = pl.multiple_of(j * tb, tb)
        jb = jnp.where(j == 0, 0, n_blk - j)
        base_b = pl.multiple_of(jb * tb, tb)
        todo = [direction(base_f, ff_ref, 0), direction(base_b, fb_ref, 1)]
        while todo:
            todo = [gen for gen in todo if next(gen, True) is None]
        return carry

    lax.fori_loop(0, n_blk, step, 0)
    o = acc_scr[l_ctx:, :]
    g = g_ref[0, l_ctx:, :]
    o = o * lax.rsqrt(jnp.mean(o * o, axis=-1, keepdims=True) + EPS) * ng_ref[...]
    o_ref[0] = o * (g * (1.0 / (1.0 + jnp.exp(-g))))


def _hgrn_mix(u_all, lb, norm_g, l_ctx):
    bsz, l_tot, _ = u_all.shape
    nh = HG_HEADS
    n_blk = l_tot // HG_TB
    lbh = lb.reshape(nh, 1, HG_DK)
    lbp = jnp.concatenate([jnp.log(lbh), jnp.log1p(-lbh),
                           jnp.zeros((nh, _SUBLANES - 2, HG_DK), jnp.float32)], axis=1)
    cum_m, msk_m = _hgrn_consts()

    def slab(k):
        return pl.BlockSpec((1, l_tot, HG_DK), lambda b, h, k=k: (b, 0, k * nh + h))

    def const(a):
        return pl.BlockSpec(a.shape, lambda b, h: (0,) * a.ndim)

    return pl.pallas_call(
        partial(_hgrn_body, n_blk=n_blk, l_ctx=l_ctx),
        grid=(bsz, nh),
        in_specs=[slab(0), slab(1), slab(2), slab(4), slab(5),
                  pl.BlockSpec((1, _SUBLANES, HG_DK), lambda b, h: (h, 0, 0)),
                  pl.BlockSpec((1, HG_DK), lambda b, h: (0, 0)),
                  const(cum_m), const(msk_m)],
        out_specs=pl.BlockSpec((1, l_tot - l_ctx, HG_DK), lambda b, h: (b, 0, h)),
        out_shape=jax.ShapeDtypeStruct((bsz, l_tot - l_ctx, W_HALF), jnp.float32),
        scratch_shapes=[pltpu.VMEM((l_tot, HG_DK), jnp.float32), pltpu.VMEM((2, HG_DK, HG_DK), jnp.float32)],
        compiler_params=pltpu.CompilerParams(dimension_semantics=("parallel", "parallel"),
                                             vmem_limit_bytes=_VMEM_LIMIT),
        name="hgrn_mix",
    )(u_all, u_all, u_all, u_all, u_all, lbp, norm_g.reshape(1, HG_DK), cum_m, msk_m)


def _s5_weights(lam_re, lam_im, log_dt, b_re, b_im, c_re, c_im):
    t_len, f32 = S5_T, jnp.float32
    lre = jnp.minimum(lam_re, S5_MAX_RE)
    dt = jnp.exp(log_dt)[..., None]
    ar, ai = lre * dt, lam_im * dt
    k = jnp.arange(t_len + 1, dtype=f32)[:, None, None, None]
    mag = jnp.exp(k * ar)
    pre, pim = mag * jnp.cos(k * ai), mag * jnp.sin(k * ai)
    nr, ni, den = pre[1] - 1.0, pim[1], lre * lre + lam_im * lam_im
    cr, ci = (nr * lre + ni * lam_im) / den, (ni * lre - nr * lam_im) / den
    cbr = cr[..., None] * b_re - ci[..., None] * b_im
    cbi = cr[..., None] * b_im + ci[..., None] * b_re
    c_re_t, c_im_t = jnp.swapaxes(c_re, 1, 2), jnp.swapaxes(c_im, 1, 2)
    clre = c_re_t * pre[..., None] - c_im_t * pim[..., None]
    clim = c_re_t * pim[..., None] + c_im_t * pre[..., None]
    g = lam_re.shape[1]

    def w_dir(d, e):
        pr, pi = pre[e, d], pim[e, d]
        wr = pr[..., None] * cbr[d] - pi[..., None] * cbi[d]
        wi = pr[..., None] * cbi[d] + pi[..., None] * cbr[d]
        return [jnp.transpose(z, (1, 0, 3, 2)).reshape(g, t_len * S5_H, S5_P) for z in (wr, wi)]

    wf, wb = w_dir(0, jnp.arange(t_len - 1, -1, -1)), w_dir(1, jnp.arange(t_len))
    w = jnp.concatenate(wf + wf[::-1] + wb + wb[::-1], axis=-1)
    dec = jnp.concatenate([pre[t_len, 0], pre[t_len, 0], -pim[t_len, 0], pim[t_len, 0],
                           pre[t_len, 1], pre[t_len, 1], -pim[t_len, 1], pim[t_len, 1]], axis=-1)[:, None, :]

    def m_dir(d, e):
        mr, mi = clre[e, d], -clim[e, d]
        return [jnp.transpose(z, (1, 2, 0, 3)).reshape(g, S5_P, t_len * S5_H) for z in (mr, mi)]

    m = jnp.concatenate(m_dir(0, jnp.arange(1, t_len + 1)) + m_dir(1, jnp.arange(t_len, 0, -1)), axis=1)
    kk = (jnp.einsum('kdgpo,dgpi->kdgoi', clre[:t_len], cbr, precision=_HI)
          - jnp.einsum('kdgpo,dgpi->kdgoi', clim[:t_len], cbi, precision=_HI))
    delta = jnp.arange(t_len)[None, :] - jnp.arange(t_len)[:, None]
    kf = kk[jnp.clip(delta, 0, t_len - 1), 0]
    kb = kk[jnp.clip(-delta, 0, t_len - 1), 1]
    dl = delta[:, :, None, None, None]
    toep = jnp.where(dl > 0, kf, 0.0) + jnp.where(dl < 0, kb, 0.0) + jnp.where(dl == 0, kf + kb, 0.0)
    toep = jnp.transpose(toep, (2, 0, 4, 1, 3)).reshape(g, t_len * S5_H, t_len * S5_H)
    return w, dec, m, toep


def _s5_body(x_ref, w_ref, dec_ref, m_ref, toep_ref, y_ref, z_scr, s_scr, *, bsz, n_ctx, n_lat):
    p2 = 2 * S5_P
    n_all = n_ctx + n_lat
    xb = x_ref[0].astype(jnp.bfloat16)
    z_scr[...] = jnp.dot(xb, w_ref[0], preferred_element_type=jnp.float32)
    dec = jnp.broadcast_to(dec_ref[0], (bsz, 4 * p2))
    a1f, a2f = dec[:, 0:p2], dec[:, p2:2 * p2]
    a1b, a2b = dec[:, 2 * p2:3 * p2], dec[:, 3 * p2:4 * p2]

    def step(i, carry):
        sf, tf, sb, tb = carry
        rf = pl.multiple_of(i * bsz, bsz)
        cb = jnp.where(i < n_ctx, n_ctx - 1 - i, n_all + n_ctx - 1 - i)
        rb = pl.multiple_of(cb * bsz, bsz)
        s_scr[pl.ds(rf, bsz), 0:p2] = sf
        s_scr[pl.ds(rb, bsz), p2:2 * p2] = sb
        zf = z_scr[pl.ds(rf, bsz), 0:2 * p2]
        zb = z_scr[pl.ds(rb, bsz), 2 * p2:4 * p2]
        return (sf * a1f + tf * a2f + zf[:, :p2], tf * a1f - sf * a2f + zf[:, p2:],
                sb * a1b + tb * a2b + zb[:, :p2], tb * a1b - sb * a2b + zb[:, p2:])

    zero = jnp.zeros((bsz, p2), jnp.float32)
    lax.fori_loop(0, n_all, step, (zero, zero, zero, zero), unroll=4)
    r0 = n_ctx * bsz
    y_ref[0] = (jnp.dot(xb[r0:], toep_ref[0], preferred_element_type=jnp.float32)
                + jnp.dot(s_scr[r0:, :].astype(jnp.bfloat16), m_ref[0], preferred_element_type=jnp.float32))


def _s5_mix(u_c, u_l, lam_re, lam_im, log_dt, b_re, b_im, c_re, c_im):
    bsz, l_ctx, gh = u_c.shape
    l_lat = u_l.shape[1]
    g, t_len = gh // S5_H, S5_T
    n_ctx, n_lat = l_ctx // t_len, l_lat // t_len
    w, dec, m, toep = _s5_weights(lam_re, lam_im, log_dt, b_re, b_im, c_re, c_im)

    def chunked(u, n):
        return jnp.transpose(u.reshape(bsz, n, t_len, g, S5_H), (3, 1, 0, 2, 4)).reshape(g, n * bsz, t_len * S5_H)

    x = jnp.concatenate([chunked(u_c, n_ctx), chunked(u_l, n_lat)], axis=1)
    rows, k = (n_ctx + n_lat) * bsz, t_len * S5_H
    y = pl.pallas_call(
        partial(_s5_body, bsz=bsz, n_ctx=n_ctx, n_lat=n_lat),
        grid=(g,),
        in_specs=[pl.BlockSpec((1, rows, k), lambda i: (i, 0, 0)),
                  pl.BlockSpec((1, k, 8 * S5_P), lambda i: (i, 0, 0)),
                  pl.BlockSpec((1, 1, 8 * S5_P), lambda i: (i, 0, 0)),
                  pl.BlockSpec((1, 4 * S5_P, k), lambda i: (i, 0, 0)),
                  pl.BlockSpec((1, k, k), lambda i: (i, 0, 0))],
        out_specs=pl.BlockSpec((1, n_lat * bsz, k), lambda i: (i, 0, 0)),
        out_shape=jax.ShapeDtypeStruct((g, n_lat * bsz, k), jnp.float32),
        scratch_shapes=[pltpu.VMEM((rows, 8 * S5_P), jnp.float32), pltpu.VMEM((rows, 4 * S5_P), jnp.float32)],
        compiler_params=pltpu.CompilerParams(dimension_semantics=("parallel",), vmem_limit_bytes=_VMEM_LIMIT),
        name="s5_mix",
    )(x, w.astype(jnp.bfloat16), dec, m.astype(jnp.bfloat16), toep.astype(jnp.bfloat16))
    return jnp.transpose(y.reshape(g, n_lat, bsz, t_len, S5_H), (2, 1, 3, 0, 4)).reshape(bsz, l_lat, gh)


def _glu_body(y_ref, u_ref, d_ref, w_ref, b_ref, o_ref):
    y = _gelu_tanh(y_ref[0] + u_ref[0] * d_ref[...])
    z = jnp.dot(y.astype(jnp.bfloat16), w_ref[...], preferred_element_type=jnp.float32) + b_ref[...]
    o_ref[0] = y * (1.0 / (1.0 + jnp.exp(-z)))


def _s5_out(y, u_all, l_ctx, col0, d_skip, glu_w, glu_b):
    bsz, l, c = y.shape
    tm = l_ctx
    vec = pl.BlockSpec((1, c), lambda b, i: (0, 0))
    return pl.pallas_call(
        _glu_body,
        grid=(bsz, l // tm),
        in_specs=[pl.BlockSpec((1, tm, c), lambda b, i: (b, i, 0)),
                  pl.BlockSpec((1, tm, c), lambda b, i: (b, i + 1, col0 // c)), vec,
                  pl.BlockSpec((c, c), lambda b, i: (0, 0)), vec],
        out_specs=pl.BlockSpec((1, tm, c), lambda b, i: (b, i, 0)),
        out_shape=jax.ShapeDtypeStruct(y.shape, jnp.float32),
        compiler_params=pltpu.CompilerParams(dimension_semantics=("parallel", "parallel")),
        name="s5_glu",
    )(y, u_all, d_skip.reshape(1, c), glu_w.astype(jnp.bfloat16), glu_b.reshape(1, c))


def _odd_mixer(u, l_ctx, lb, hg_norm_g, lam_re, lam_im, log_dt, b_re, b_im, c_re, c_im, d_skip, glu_w, glu_b):
    w = W_HALF
    hg_l = _hgrn_mix(u, lb, hg_norm_g, l_ctx)
    s_c, s_l = u[:, :l_ctx, 3 * w:4 * w], u[:, l_ctx:, 3 * w:4 * w]
    y_l = _s5_mix(s_c, s_l, lam_re, lam_im, log_dt, b_re, b_im, c_re, c_im)
    return hg_l, _s5_out(y_l, u, l_ctx, 3 * w, d_skip, glu_w, glu_b)


def _router_body(x_ref, g_ref, sh_ref, sc_ref, wr_ref, h_ref, lg_ref):
    x = x_ref[0]
    h = x * lax.rsqrt(jnp.mean(x * x, axis=-1, keepdims=True) + EPS) * g_ref[...]
    h = h * (1.0 + sc_ref[0]) + sh_ref[0]
    hb = h.astype(jnp.bfloat16)
    lg_ref[0] = jnp.dot(hb, wr_ref[...], preferred_element_type=jnp.float32)
    bits = pltpu.bitcast(hb.astype(jnp.float32), jnp.uint32)
    half = bits.shape[1] // 2
    h_ref[0] = (bits[:, :half] >> 16) | (bits[:, half:] & jnp.uint32(0xFFFF0000))


def _router(x, norm_g, shift, scale, w_router):
    bsz, n, d = x.shape
    tm = min(n, 512)
    vec = pl.BlockSpec((1, 1, d), lambda b, i: (b, 0, 0))
    return pl.pallas_call(
        _router_body,
        grid=(bsz, n // tm),
        in_specs=[pl.BlockSpec((1, tm, d), lambda b, i: (b, i, 0)),
                  pl.BlockSpec((1, d), lambda b, i: (0, 0)), vec, vec,
                  pl.BlockSpec((d, N_EXPERTS), lambda b, i: (0, 0))],
        out_specs=[pl.BlockSpec((1, tm, d // 2), lambda b, i: (b, i, 0)),
                   pl.BlockSpec((1, tm, N_EXPERTS), lambda b, i: (b, i, 0))],
        out_shape=[jax.ShapeDtypeStruct((bsz, n, d // 2), jnp.uint32),
                   jax.ShapeDtypeStruct((bsz, n, N_EXPERTS), jnp.float32)],
        compiler_params=pltpu.CompilerParams(dimension_semantics=("parallel", "parallel"),
                                             vmem_limit_bytes=_VMEM_LIMIT),
        name="moe_router",
    )(x, norm_g.reshape(1, d), jnp.broadcast_to(shift, (bsz, 1, d)), jnp.broadcast_to(scale, (bsz, 1, d)),
      w_router.astype(jnp.bfloat16))


def _expert_body(idx_ref, h_ref, gate_ref, mod_ref, wg_ref, wu_ref, wd_ref, y_ref, x_scr, *, cap, n_f):
    def grab(g, carry):
        base = pl.multiple_of(g * _SUBLANES, _SUBLANES)
        for r in range(_SUBLANES):
            x_scr[pl.ds(base + r, 1), :] = h_ref[0, pl.ds(idx_ref[0, 0, base + r], 1), :]
        return carry

    lax.fori_loop(0, cap // _SUBLANES, grab, 0)
    w32 = x_scr[...]
    xb = jnp.concatenate([pltpu.bitcast(w32 << 16, jnp.float32),
                          pltpu.bitcast(w32 & jnp.uint32(0xFFFF0000), jnp.float32)], axis=1).astype(jnp.bfloat16)
    tf = wg_ref.shape[3] // n_f
    acc = None
    for f in range(n_f):
        cs = slice(f * tf, (f + 1) * tf)
        a = jnp.dot(xb, wg_ref[0, 0, :, cs], preferred_element_type=jnp.float32)
        u = jnp.dot(xb, wu_ref[0, 0, :, cs], preferred_element_type=jnp.float32)
        hh = (a * (1.0 / (1.0 + jnp.exp(-a))) * u).astype(jnp.bfloat16)
        part = jnp.dot(hh, wd_ref[0, 0, cs, :], preferred_element_type=jnp.float32)
        acc = part if acc is None else acc + part
    y_ref[0] = acc * gate_ref[0] * mod_ref[0]


def _expert_ffn(h32, idx, gate, mod, layer, w_gate, w_up, w_down):
    bsz, n, dh = h32.shape
    d = 2 * dh
    cap = idx.shape[-1]
    _, e, _, ff = w_gate.shape
    be = lambda k, b: (b * e + k, 0, 0)
    wspec = lambda s: pl.BlockSpec((1, 1) + s, lambda k, b: (layer, k, 0, 0))
    return pl.pallas_call(
        partial(_expert_body, cap=cap, n_f=3),
        grid=(e, bsz),
        in_specs=[pl.BlockSpec((1, 1, cap), be, memory_space=pltpu.SMEM),
                  pl.BlockSpec((1, n, dh), lambda k, b: (b, 0, 0)),
                  pl.BlockSpec((1, cap, 1), be),
                  pl.BlockSpec((1, 1, d), lambda k, b: (b, 0, 0)),
                  wspec((d, ff)), wspec((d, ff)), wspec((ff, d))],
        out_specs=pl.BlockSpec((1, cap, d), be),
        out_shape=jax.ShapeDtypeStruct((bsz * e, cap, d), jnp.float32),
        scratch_shapes=[pltpu.VMEM((cap, dh), jnp.uint32)],
        compiler_params=pltpu.CompilerParams(dimension_semantics=("parallel", "parallel"),
                                             vmem_limit_bytes=_VMEM_LIMIT_BIG),
        name="moe_expert",
    )(idx.reshape(bsz * e, 1, cap), h32, gate.reshape(bsz * e, cap, 1), jnp.broadcast_to(mod, (bsz, 1, d)),
      w_gate, w_up, w_down)


def _combine_body(idx_ref, y_ref, o_ref, *, cap):
    @pl.when(pl.program_id(1) == 0)
    def _():
        o_ref[...] = jnp.zeros_like(o_ref)

    def group(g, carry):
        base = pl.multiple_of(g * _SUBLANES, _SUBLANES)
        ys = y_ref[0, pl.ds(base, _SUBLANES), :]
        ts = [idx_ref[0, 0, base + r] for r in range(_SUBLANES)]
        cur = [o_ref[0, pl.ds(t, 1), :] for t in ts]
        for r in range(_SUBLANES):
            o_ref[0, pl.ds(ts[r], 1), :] = cur[r] + ys[r:r + 1, :]
        return carry

    lax.fori_loop(0, cap // _SUBLANES, group, 0)


def _combine(y, idx, n):
    be, cap, d = y.shape
    bsz = be // N_EXPERTS
    return pl.pallas_call(
        partial(_combine_body, cap=cap),
        grid=(bsz, N_EXPERTS),
        in_specs=[pl.BlockSpec((1, 1, cap), lambda b, k: (b * N_EXPERTS + k, 0, 0), memory_space=pltpu.SMEM),
                  pl.BlockSpec((1, cap, d), lambda b, k: (b * N_EXPERTS + k, 0, 0))],
        out_specs=pl.BlockSpec((1, n, d), lambda b, k: (b, 0, 0)),
        out_shape=jax.ShapeDtypeStruct((bsz, n, d), jnp.float32),
        compiler_params=pltpu.CompilerParams(dimension_semantics=("parallel", "arbitrary"),
                                             vmem_limit_bytes=_VMEM_LIMIT),
        name="moe_combine",
    )(idx.reshape(be, 1, cap), y)


def _moe(x, norm_g, shift, scale, mod, w_router, layer, w_gate, w_up, w_down):
    bsz, n, d = x.shape
    cap = max(1, (EC_CAPACITY * n) // N_EXPERTS)
    h32, logits = _router(x, norm_g, shift, scale, w_router)
    aff = jax.nn.softmax(logits, axis=-1)
    gate, idx = lax.top_k(jnp.swapaxes(aff, 1, 2), cap)
    y = _expert_ffn(h32, idx, gate, mod, layer, w_gate, w_up, w_down)
    return _combine(y, idx, n)


def kernel(x, c, ctx, c_ctx, mod_w, mod_b, norm1_g, norm2_g, out_w, even_in_w, lru_conv_w, lru_conv_b, lru_wa, lru_ba, lru_wx, lru_bx, lru_lam, hy_conv_w, hy_conv_b, hy_w1, hy_b1, hy_w2, hy_b2, hy_w3, hy_freq, hy_skip, odd_in_w, hg_lb, hg_norm_g, s5_lam_re, s5_lam_im, s5_log_dt, s5_b_re, s5_b_im, s5_c_re, s5_c_im, s5_d, s5_glu_w, s5_glu_b, router_w, ex_w_gate, ex_w_up, ex_w_down, final_g):
    rows = x.shape[1] // GRID_W
    l_ctx = ctx.shape[1]
    p = jax.nn.softmax(hg_lb, axis=0)
    lb_all = jnp.clip(jnp.cumsum(p, axis=0) - p[0], 0.0, 1.0 - 1e-4)
    ex_wg, ex_wu, ex_wd = (t.astype(jnp.bfloat16) for t in (ex_w_gate, ex_w_up, ex_w_down))
    sc_lat = jax.nn.silu(c)
    sc_ctx = jax.nn.silu(c_ctx)[None]
    for i in range(DEPTH):
        need_ctx = i < DEPTH - 1
        j = i // 2
        ml = [t[:, None, :] for t in jnp.split(sc_lat @ mod_w[i] + mod_b[i], 6, axis=-1)]
        mc = [t[:, None, :] for t in jnp.split(sc_ctx @ mod_w[i] + mod_b[i], 6, axis=-1)]
        shift, scale, gate = (_two_part(mc[k], ml[k]) for k in range(3))
        if i % 2 == 0:
            x_all = jnp.concatenate([ctx, x], axis=1)
            u = _norm_mm(x_all, norm1_g[i], shift, scale, even_in_w[j], l_ctx)
            h_f, h_b, yb = _even_mixer(u, l_ctx, lru_conv_w[j], lru_conv_b[j], lru_wa[j], lru_ba[j], lru_wx[j],
                                       lru_bx[j], lru_lam[j], hy_conv_w[j], hy_conv_b[j], hy_w1[j], hy_b1[j],
                                       hy_w2[j], hy_b2[j], hy_w3[j], hy_freq[j], hy_skip[j])
            x_all = _proj_res_lru(h_f, h_b, u, yb, out_w[i], x_all, gate, l_ctx)
            ctx, x = x_all[:, :l_ctx], x_all[:, l_ctx:]
        else:
            assert not need_ctx
            x_all = jnp.concatenate([ctx, _to_col_major(x, rows)], axis=1)
            u = _norm_mm(x_all, norm1_g[i], shift, scale, odd_in_w[j], l_ctx)
            hg_l, s5_l = _odd_mixer(u, l_ctx, lb_all[i], hg_norm_g[j], s5_lam_re[j], s5_lam_im[j],
                                    s5_log_dt[j], s5_b_re[j], s5_b_im[j], s5_c_re[j], s5_c_im[j], s5_d[j],
                                    s5_glu_w[j], s5_glu_b[j])
            x = _to_row_major(_proj_res(hg_l, s5_l, out_w[i], x_all, gate, l_ctx, l_ctx), rows)
        ex = (router_w[i], i, ex_wg, ex_wu, ex_wd)
        moe_l = _moe(x, norm2_g[i], ml[3], ml[4], ml[5], *ex)
        if need_ctx:
            x = x + moe_l
            ctx = ctx + _moe(ctx, norm2_g[i], mc[3], mc[4], mc[5], *ex)
    return _add_norm(x, moe_l, final_g)
```

```python
import math
from functools import partial

import numpy as np
import jax
import jax.numpy as jnp
from jax import lax
from jax.experimental import pallas as pl
from jax.experimental.pallas import tpu as pltpu

D_MODEL = 1024
DEPTH = 2
GRID_W = 64
W_HALF = D_MODEL // 2
EPS = 1e-6
LRU_HEADS = 8
LRU_HEAD_DIM = W_HALF // LRU_HEADS
LRU_C = 8.0
HY_ORDER = 2
HY_BANDS = 16
HY_MAX_DECAY = math.log(1e-2) / 0.3
HY_MIN_DECAY = math.log(1e-2) / 1.5
HG_HEADS = 4
HG_DK = W_HALF // HG_HEADS
HG_C = 16
HG_SUB = 4
HG_TB = 256
S5_H = 16
S5_P = 64
S5_MAX_RE = -1e-4
S5_T = 16
_DFT_LOW = 64
N_EXPERTS = 16
EC_CAPACITY = 2

_SUBLANES = 8
_DW_ROWS = 256
_VMEM_LIMIT = 48 * 1024 * 1024
_VMEM_LIMIT_BIG = 56 * 1024 * 1024
_HI = lax.Precision.HIGHEST
_NEG = -1e30


def _mm_body(a_ref, b_ref, o_ref):
    o_ref[...] = jnp.dot(a_ref[...].astype(jnp.bfloat16), b_ref[...],
                         preferred_element_type=jnp.float32)


def _pick(n, pref):
    for t in pref:
        if n % t == 0:
            return t
    return n


def _mm(a, b):
    m, k = a.shape
    n = b.shape[1]
    tm = _pick(m, (512, 256, 128, 64, 32, 16, 8))
    tn = _pick(n, (512, 256, 128))
    return pl.pallas_call(
        _mm_body,
        grid=(m // tm, n // tn),
        in_specs=[pl.BlockSpec((tm, k), lambda i, j: (i, 0)),
                  pl.BlockSpec((k, tn), lambda i, j: (0, j))],
        out_specs=pl.BlockSpec((tm, tn), lambda i, j: (i, j)),
        out_shape=jax.ShapeDtypeStruct((m, n), jnp.float32),
        compiler_params=pltpu.CompilerParams(
            dimension_semantics=("parallel", "parallel"), vmem_limit_bytes=_VMEM_LIMIT),
        name="mm",
    )(a, b.astype(jnp.bfloat16))


def _mm3(a, b):
    bsz, l, k = a.shape
    return _mm(a.reshape(bsz * l, k), b).reshape(bsz, l, b.shape[1])


def _add_norm_body(x_ref, m_ref, g_ref, o_ref):
    x = x_ref[...] + m_ref[...]
    o_ref[...] = x * lax.rsqrt(jnp.mean(x * x, axis=-1, keepdims=True) + EPS) * g_ref[...]


def _add_norm(x, m, g):
    bsz, n, d = x.shape
    tm = min(n, 512)
    blk = pl.BlockSpec((1, tm, d), lambda b, i: (b, i, 0))
    return pl.pallas_call(
        _add_norm_body,
        grid=(bsz, n // tm),
        in_specs=[blk, blk, pl.BlockSpec((1, 1, d), lambda b, i: (0, 0, 0))],
        out_specs=blk,
        out_shape=jax.ShapeDtypeStruct(x.shape, jnp.float32),
        compiler_params=pltpu.CompilerParams(dimension_semantics=("parallel", "parallel")),
        name="add_norm",
    )(x, m, g.reshape(1, 1, d))


def _two_part(t_ctx, t_lat):
    bsz, _, d = t_lat.shape
    return jnp.stack([jnp.broadcast_to(t_ctx, (bsz, 1, d)), t_lat], axis=1).reshape(2 * bsz, 1, d)


def _norm_mm_body(x_ref, g_ref, sh_ref, sc_ref, w_ref, o_ref):
    x = x_ref[0]
    h = x * lax.rsqrt(jnp.mean(x * x, axis=-1, keepdims=True) + EPS) * g_ref[...]
    h = h * (1.0 + sc_ref[0]) + sh_ref[0]
    o_ref[0] = jnp.dot(h.astype(jnp.bfloat16), w_ref[...], preferred_element_type=jnp.float32)


def _norm_mm(x_all, norm_g, shift, scale, w, l_ctx):
    bsz, l_tot, d = x_all.shape
    n = w.shape[1]
    tm = l_ctx
    vec = pl.BlockSpec((1, 1, d), lambda b, j: (2 * b + jnp.minimum(j, 1), 0, 0))
    return pl.pallas_call(
        _norm_mm_body,
        grid=(bsz, l_tot // tm),
        in_specs=[pl.BlockSpec((1, tm, d), lambda b, j: (b, j, 0)),
                  pl.BlockSpec((1, d), lambda b, j: (0, 0)), vec, vec,
                  pl.BlockSpec((d, n), lambda b, j: (0, 0))],
        out_specs=pl.BlockSpec((1, tm, n), lambda b, j: (b, j, 0)),
        out_shape=jax.ShapeDtypeStruct((bsz, l_tot, n), jnp.float32),
        compiler_params=pltpu.CompilerParams(dimension_semantics=("parallel", "parallel"),
                                             vmem_limit_bytes=_VMEM_LIMIT),
        name="norm_mm",
    )(x_all, norm_g.reshape(1, d), shift, scale, w.astype(jnp.bfloat16))


def _proj_res_body(ya_ref, yb_ref, wa_ref, wb_ref, x_ref, gt_ref, o_ref):
    y = (jnp.dot(ya_ref[0].astype(jnp.bfloat16), wa_ref[...], preferred_element_type=jnp.float32)
         + jnp.dot(yb_ref[0].astype(jnp.bfloat16), wb_ref[...], preferred_element_type=jnp.float32))
    o_ref[0] = x_ref[0] + gt_ref[0] * y


def _proj_res_lru_body(hf_ref, hb_ref, u_ref, yb_ref, wa_ref, wb_ref, x_ref, gt_ref, o_ref):
    ya = (hf_ref[0] + hb_ref[0]) * _gelu_tanh(u_ref[0])
    y = (jnp.dot(ya.astype(jnp.bfloat16), wa_ref[...], preferred_element_type=jnp.float32)
         + jnp.dot(yb_ref[0].astype(jnp.bfloat16), wb_ref[...], preferred_element_type=jnp.float32))
    o_ref[0] = x_ref[0] + gt_ref[0] * y


def _proj_res_lru(h_f, h_b, u_all, yb, w, x_all, gate, l_ctx):
    bsz, l_tot, ka = h_f.shape
    d = w.shape[1]
    tm = l_ctx
    blk = lambda k, c=0: pl.BlockSpec((1, tm, k), lambda b, j: (b, j, c))
    return pl.pallas_call(
        _proj_res_lru_body,
        grid=(bsz, l_tot // tm),
        in_specs=[blk(ka), blk(ka), blk(ka, 1), blk(yb.shape[2]),
                  pl.BlockSpec((ka, d), lambda b, j: (0, 0)), pl.BlockSpec((yb.shape[2], d), lambda b, j: (0, 0)),
                  blk(d), pl.BlockSpec((1, 1, d), lambda b, j: (2 * b + jnp.minimum(j, 1), 0, 0))],
        out_specs=blk(d),
        out_shape=jax.ShapeDtypeStruct((bsz, l_tot, d), jnp.float32),
        compiler_params=pltpu.CompilerParams(dimension_semantics=("parallel", "parallel"),
                                             vmem_limit_bytes=_VMEM_LIMIT),
        name="proj_res_lru",
    )(h_f, h_b, u_all, yb, w[:ka].astype(jnp.bfloat16), w[ka:].astype(jnp.bfloat16), x_all, gate)


def _proj_res(ya, yb, w, x_all, gate, l_ctx, row0):
    bsz, l_y, ka = ya.shape
    d = w.shape[1]
    tm = l_ctx
    j0 = row0 // tm
    yblk = lambda k: pl.BlockSpec((1, tm, k), lambda b, j: (b, j, 0))
    return pl.pallas_call(
        _proj_res_body,
        grid=(bsz, l_y // tm),
        in_specs=[yblk(ka), yblk(yb.shape[2]),
                  pl.BlockSpec((ka, d), lambda b, j: (0, 0)), pl.BlockSpec((yb.shape[2], d), lambda b, j: (0, 0)),
                  pl.BlockSpec((1, tm, d), lambda b, j: (b, j + j0, 0)),
                  pl.BlockSpec((1, 1, d), lambda b, j: (2 * b + jnp.minimum(j + j0, 1), 0, 0))],
        out_specs=pl.BlockSpec((1, tm, d), lambda b, j: (b, j, 0)),
        out_shape=jax.ShapeDtypeStruct((bsz, l_y, d), jnp.float32),
        compiler_params=pltpu.CompilerParams(dimension_semantics=("parallel", "parallel"),
                                             vmem_limit_bytes=_VMEM_LIMIT),
        name="proj_res",
    )(ya, yb, w[:ka].astype(jnp.bfloat16), w[ka:].astype(jnp.bfloat16), x_all, gate)


def _gelu_tanh(x):
    return 0.5 * x * (1.0 + jnp.tanh(math.sqrt(2.0 / math.pi) * (x + 0.044715 * (x * x * x))))


def _dwconv_body(u_ref, w_ref, o_ref, *rest, l_ctx, taps, left, split):
    l_tot = u_ref.shape[1]
    tb, h = _DW_ROWS, _SUBLANES
    row = lax.broadcasted_iota(jnp.int32, (tb, 1), 0)

    def block(j, carry):
        r0 = pl.multiple_of(j * tb, tb)
        lo = jnp.where(r0 < l_ctx, 0, l_ctx)
        hi = jnp.where(r0 < l_ctx, l_ctx, l_tot)
        prev = u_ref[0, pl.ds(pl.multiple_of(jnp.maximum(r0 - h, 0), h), h), :]
        nxt = u_ref[0, pl.ds(pl.multiple_of(jnp.minimum(r0 + tb, l_tot - h), h), h), :]
        ext = jnp.concatenate([prev, u_ref[0, pl.ds(r0, tb), :], nxt], axis=0)
        y = jnp.broadcast_to(w_ref[0, taps:taps + 1, :], (tb, ext.shape[1]))
        for t in range(taps):
            off = t - left
            src = r0 + row + off
            y = y + jnp.where((src >= lo) & (src < hi), ext[h + off:h + off + tb, :], 0.0) * w_ref[0, t:t + 1, :]
        if not split:
            o_ref[0, pl.ds(r0, tb), :] = y
        else:
            ob_c, o_l, ob_l = rest

            @pl.when(r0 < l_ctx)
            def _():
                o_ref[0, pl.ds(r0, tb), :] = y
                ob_c[0, pl.ds(r0, tb), :] = y.astype(jnp.bfloat16)

            @pl.when(r0 >= l_ctx)
            def _():
                o_l[0, pl.ds(r0 - l_ctx, tb), :] = y
                ob_l[0, pl.ds(r0 - l_ctx, tb), :] = y.astype(jnp.bfloat16)
        return carry

    lax.fori_loop(0, l_tot // tb, block, 0)


def _dwconv_slabs(u_all, col0, w, b, l_ctx, split):
    bsz, l_tot, _ = u_all.shape
    taps, nc = w.shape
    c = W_HALF
    n = nc // c
    wb = jnp.concatenate([w, b[None], jnp.zeros((_SUBLANES - taps - 1, nc), jnp.float32)], axis=0)
    wb = wb.reshape(_SUBLANES, n, c).transpose(1, 0, 2)
    if split:
        rows = (l_ctx, l_ctx, l_tot - l_ctx, l_tot - l_ctx)
        dts = (jnp.float32, jnp.bfloat16, jnp.float32, jnp.bfloat16)
    else:
        rows, dts = (l_tot,), (jnp.float32,)
    out = pl.pallas_call(
        partial(_dwconv_body, l_ctx=l_ctx, taps=taps, left=(taps - 1) // 2, split=split),
        grid=(bsz, n),
        in_specs=[pl.BlockSpec((1, l_tot, c), lambda bb, s: (bb, 0, s + col0 // c)),
                  pl.BlockSpec((1, _SUBLANES, c), lambda bb, s: (s, 0, 0))],
        out_specs=[pl.BlockSpec((1, r, c), lambda bb, s: (bb, 0, s)) for r in rows],
        out_shape=[jax.ShapeDtypeStruct((bsz, r, nc), dt) for r, dt in zip(rows, dts)],
        compiler_params=pltpu.CompilerParams(dimension_semantics=("parallel", "parallel"),
                                             vmem_limit_bytes=_VMEM_LIMIT),
        name="dwconv",
    )(u_all, wb)
    return out if split else out[0]


def _to_col_major(t, rows):
    bsz, l, ch = t.shape
    return t.reshape(bsz, rows, GRID_W, ch).transpose(0, 2, 1, 3).reshape(bsz, l, ch)


def _to_row_major(t, rows):
    bsz, l, ch = t.shape
    return t.reshape(bsz, GRID_W, rows, ch).transpose(0, 2, 1, 3).reshape(bsz, l, ch)


def _lru_body(xf_ref, xb_ref, wa_ref, wx_ref, par_ref, hf_ref, hb_ref,
              af_scr, bf_scr, ab_scr, bb_scr, sf_scr, sb_scr, *, tb):
    @pl.when(pl.program_id(1) == 0)
    def _():
        sf_scr[...] = jnp.zeros_like(sf_scr)
        sb_scr[...] = jnp.zeros_like(sb_scr)

    def coeffs(x, d, a_scr, b_scr):
        xb = x.astype(jnp.bfloat16)
        r = jnp.dot(xb, wa_ref[d], preferred_element_type=jnp.float32) + par_ref[d, 0:1, :]
        i = jnp.dot(xb, wx_ref[d], preferred_element_type=jnp.float32) + par_ref[d, 1:2, :]
        r = 1.0 / (1.0 + jnp.exp(-r))
        i = 1.0 / (1.0 + jnp.exp(-i))
        log_a = -LRU_C * r * par_ref[d, 2:3, :]
        a = jnp.exp(log_a)
        a_scr[...] = a
        b_scr[...] = jnp.sqrt(-jnp.tanh(log_a) * (a * a + 1.0)) * i * x

    coeffs(xf_ref[0], 0, af_scr, bf_scr)
    coeffs(xb_ref[0], 1, ab_scr, bb_scr)

    def tile(i, carry):
        hf, hb = carry
        base_f = pl.multiple_of(i * _SUBLANES, _SUBLANES)
        base_b = pl.multiple_of(tb - _SUBLANES - i * _SUBLANES, _SUBLANES)
        for r in range(_SUBLANES):
            rf = pl.ds(base_f + r, 1)
            hf = af_scr[rf, :] * hf + bf_scr[rf, :]
            hf_ref[0, rf, :] = hf
            rb = pl.ds(base_b + (_SUBLANES - 1 - r), 1)
            hb = ab_scr[rb, :] * hb + bb_scr[rb, :]
            hb_ref[0, rb, :] = hb
        return hf, hb

    hf, hb = lax.fori_loop(0, tb // _SUBLANES, tile, (sf_scr[...], sb_scr[...]))
    sf_scr[...] = hf
    sb_scr[...] = hb


def _block_diag(w):
    two, h, n, _ = w.shape
    eye = jnp.eye(h, dtype=w.dtype)
    return (w[:, :, :, None, :] * eye[None, :, None, :, None]).reshape(two, h * n, h * n)


def _lru_scan(xc, l_ctx, wa, ba, wx, bx, lam):
    bsz, l_tot, ch = xc.shape
    tb = l_ctx
    nb = l_tot // tb
    par = jnp.stack([ba.reshape(2, ch), bx.reshape(2, ch), jax.nn.softplus(-lam).reshape(2, ch)], axis=1)
    par = jnp.concatenate([par, jnp.zeros((2, _SUBLANES - 3, ch), jnp.float32)], axis=1)
    fwd = pl.BlockSpec((1, tb, ch), lambda b, j: (b, j, 0))
    bwd = pl.BlockSpec((1, tb, ch), lambda b, j: (b, jnp.where(j == 0, 0, nb - j), 0))
    wspec = pl.BlockSpec((2, ch, ch), lambda b, j: (0, 0, 0))
    return pl.pallas_call(
        partial(_lru_body, tb=tb),
        grid=(bsz, nb),
        in_specs=[fwd, bwd, wspec, wspec, pl.BlockSpec((2, _SUBLANES, ch), lambda b, j: (0, 0, 0))],
        out_specs=[fwd, bwd],
        out_shape=[jax.ShapeDtypeStruct(xc.shape, jnp.float32)] * 2,
        scratch_shapes=[pltpu.VMEM((tb, ch), jnp.float32)] * 4 + [pltpu.VMEM((1, ch), jnp.float32)] * 2,
        compiler_params=pltpu.CompilerParams(dimension_semantics=("parallel", "arbitrary")),
        name="lru_scan",
    )(xc, xc, _block_diag(wa).astype(jnp.bfloat16), _block_diag(wx).astype(jnp.bfloat16), par)


def _hyena_filters(l, w1, b1, w2, b2, w3, freq):
    t = jnp.arange(l, dtype=jnp.float32)
    t_unit = t / max(l - 1, 1)
    bands = jnp.linspace(1e-4, HY_BANDS - 1, HY_BANDS, dtype=jnp.float32)
    ang = (2.0 * math.pi / l) * t[:, None] * bands[None, :]
    z = jnp.concatenate([t_unit[:, None], jnp.cos(ang), -jnp.sin(ang)], axis=-1)
    h = jnp.sin(freq * (z @ w1 + b1))
    h = jnp.sin(freq * (h @ w2 + b2))
    h = h @ w3
    dist = jnp.abs(t - (l // 2)) / (l / 2.0)
    deltas = jnp.abs(jnp.linspace(HY_MIN_DECAY, HY_MAX_DECAY, HY_ORDER * W_HALF, dtype=jnp.float32))
    h = h * jnp.exp(-dist[:, None] * deltas[None, :])
    h = h * lax.rsqrt(jnp.sum(h * h, axis=0, keepdims=True) + EPS)
    return h


def _dft_mats(l, tf):
    n = 2 * l
    nl = _DFT_LOW
    nh = l // nl
    t = jnp.arange(l, dtype=jnp.int32)
    w = 2.0 * math.pi / n

    def tables(tv):
        ah = (((nl * jnp.arange(nh, dtype=jnp.int32))[:, None] * tv[None, :]) % n).astype(jnp.float32) * w
        al = ((jnp.arange(nl, dtype=jnp.int32)[:, None] * tv[None, :]) % n).astype(jnp.float32) * w
        return jnp.cos(ah), jnp.sin(ah), jnp.cos(al), jnp.sin(al)

    ch, sh, cl, sl = tables(t)
    f_re = (ch[:, None, :] * cl[None] - sh[:, None, :] * sl[None]).reshape(l, l)
    f_im = -(sh[:, None, :] * cl[None] + ch[:, None, :] * sl[None]).reshape(l, l)
    f_im = f_im.at[0].set(jnp.where(t % 2 == 0, 1.0, -1.0))
    f = jnp.concatenate([f_re.reshape(l // tf, tf, l), f_im.reshape(l // tf, tf, l)], axis=1).reshape(n, l)
    tp = t + l // 2
    ch, sh, cl, sl = (z.T for z in tables(tp))
    g_re = (2.0 / n) * (ch[:, :, None] * cl[:, None, :] - sh[:, :, None] * sl[:, None, :]).reshape(l, l)
    g_im = (-2.0 / n) * (sh[:, :, None] * cl[:, None, :] + ch[:, :, None] * sl[:, None, :]).reshape(l, l)
    g_re = g_re.at[:, 0].set(1.0 / n)
    g_im = g_im.at[:, 0].set(jnp.where(tp % 2 == 0, 1.0, -1.0) / n)
    g = jnp.stack([g_re.reshape(l, l // tf, tf), g_im.reshape(l, l // tf, tf)], axis=2).reshape(l, n)
    return f.astype(jnp.bfloat16), g.astype(jnp.bfloat16)


def _spec_body(f_ref, u_ref, a_ref, b_ref, d_ref, y_ref, *, tf):
    acc = jnp.dot(f_ref[...], u_ref[0], preferred_element_type=jnp.float32)
    xr, xi = acc[:tf], acc[tf:]
    y_ref[0, :tf, :] = (xr * a_ref[...] - xi * b_ref[...]).astype(jnp.bfloat16)
    y_ref[0, tf:, :] = (xr * b_ref[...] + xi * d_ref[...]).astype(jnp.bfloat16)


def _spec_mul(f, u, a, b, d, tf):
    bsz, l, _ = u.shape
    c = a.shape[1]
    coef = pl.BlockSpec((tf, c), lambda i, bb: (i, 0))
    return pl.pallas_call(
        partial(_spec_body, tf=tf),
        grid=(l // tf, bsz),
        in_specs=[pl.BlockSpec((2 * tf, l), lambda i, bb: (i, 0)),
                  pl.BlockSpec((1, l, c), lambda i, bb: (bb, 0, 0)), coef, coef, coef],
        out_specs=pl.BlockSpec((1, 2 * tf, c), lambda i, bb: (bb, i, 0)),
        out_shape=jax.ShapeDtypeStruct((bsz, 2 * l, c), jnp.bfloat16),
        compiler_params=pltpu.CompilerParams(dimension_semantics=("parallel", "parallel"),
                                             vmem_limit_bytes=_VMEM_LIMIT),
        name="hy_spec",
    )(f, u, a, b, d)


def _inv_body(g_ref, y_ref, u_ref, x_ref, skip_ref, z_ref, zb_ref):
    acc = jnp.dot(g_ref[...], y_ref[0], preferred_element_type=jnp.float32)
    z = x_ref[0] * (acc + u_ref[0] * skip_ref[...])
    z_ref[0] = z
    zb_ref[0] = z.astype(jnp.bfloat16)


def _inv_gate(g, y, u, ucol, xg, xcol, skip, tm):
    bsz, n, c = y.shape
    l = n // 2
    blk = pl.BlockSpec((1, tm, c), lambda bb, i: (bb, i, 0))
    return pl.pallas_call(
        _inv_body,
        grid=(bsz, l // tm),
        in_specs=[pl.BlockSpec((tm, n), lambda bb, i: (i, 0)),
                  pl.BlockSpec((1, n, c), lambda bb, i: (bb, 0, 0)),
                  pl.BlockSpec((1, tm, c), lambda bb, i: (bb, i, ucol)),
                  pl.BlockSpec((1, tm, c), lambda bb, i: (bb, i, xcol)),
                  pl.BlockSpec((1, c), lambda bb, i: (0, 0))],
        out_specs=[blk, blk],
        out_shape=[jax.ShapeDtypeStruct((bsz, l, c), jnp.float32), jax.ShapeDtypeStruct((bsz, l, c), jnp.bfloat16)],
        compiler_params=pltpu.CompilerParams(dimension_semantics=("parallel", "parallel"),
                                             vmem_limit_bytes=_VMEM_LIMIT),
        name="hy_inv",
    )(g, y, u, xg, skip.reshape(1, c))


def _hyena_conv2(hc, hb, filt, skip):
    l, c = hc.shape[1], hc.shape[2] // 3
    tf = min(512, l // 2)
    tm = min(512, l)
    f, g = _dft_mats(l, tf)
    hs = _mm(f, filt)
    hs = hs.reshape(l // tf, 2, tf, 2 * c)
    h_re, h_im = hs[:, 0].reshape(l, 2 * c), hs[:, 1].reshape(l, 2 * c)
    slot0 = (jnp.arange(l) == 0)[:, None]
    a = h_re
    b = jnp.where(slot0, 0.0, h_im)
    d = jnp.where(slot0, h_im, h_re)
    y = _spec_mul(f, hb, a[:, :c], b[:, :c], d[:, :c], tf)
    z, zb = _inv_gate(g, y, hc, 0, hc, 1, skip[0], tm)
    y = _spec_mul(f, zb, a[:, c:], b[:, c:], d[:, c:], tf)
    z, _ = _inv_gate(g, y, z, 0, hc, 2, skip[1], tm)
    return z


def _hyena(u, l_ctx, conv_w, conv_b, w1, b1, w2, b2, w3, freq, skip):
    hc_c, hb_c, hc_l, hb_l = _dwconv_slabs(u, 2 * W_HALF, conv_w, conv_b, l_ctx, True)
    zs = []
    for hc, hb in ((hc_c, hb_c), (hc_l, hb_l)):
        filt = _hyena_filters(hc.shape[1], w1, b1, w2, b2, w3, freq)
        zs.append(_hyena_conv2(hc, hb, filt, skip))
    return jnp.concatenate(zs, axis=1)


def _even_mixer(u, l_ctx, lru_conv_w, lru_conv_b, lru_wa, lru_ba, lru_wx, lru_bx,
                lru_lam, hy_conv_w, hy_conv_b, hy_w1, hy_b1, hy_w2, hy_b2, hy_w3, hy_freq, hy_skip):
    w = W_HALF
    xc = _dwconv_slabs(u, 0, lru_conv_w, lru_conv_b, l_ctx, False)
    h_f, h_b = _lru_scan(xc, l_ctx, lru_wa, lru_ba, lru_wx, lru_bx, lru_lam)
    yb = _hyena(u, l_ctx, hy_conv_w, hy_conv_b, hy_w1, hy_b1, hy_w2, hy_b2, hy_w3, hy_freq, hy_skip)
    return h_f, h_b, yb


def _split3(x):
    hi = x.astype(jnp.bfloat16)
    r1 = x - hi.astype(jnp.float32)
    mid = r1.astype(jnp.bfloat16)
    lo = (r1 - mid.astype(jnp.float32)).astype(jnp.bfloat16)
    return jnp.concatenate([hi, mid, lo], axis=1)


def _sum3(r, c):
    return r[:, 0:c] + r[:, c:2 * c] + r[:, 2 * c:3 * c]


def _log_sigmoid(x):
    return jnp.minimum(x, 0.0) - jnp.log1p(jnp.exp(-jnp.abs(x)))


def _hgrn_consts():
    tb, c, sb = HG_TB, HG_C, HG_SUB
    r = np.arange(tb)
    same = (r[:, None] // c) == (r[None, :] // c)
    same4 = (r[:, None] // sb) == (r[None, :] // sb)
    le = r[None, :] <= r[:, None]
    sub = (r % c) // sb
    cum, msk = [], []
    for d in range(2):
        tri = le if d == 0 else le.T
        cum.append(np.concatenate([same & tri, same, same4 & tri], axis=0))
        if d == 0:
            mm = [same & (sub[:, None] == i + 1) & (sub[None, :] <= i) for i in range(c // sb - 1)]
        else:
            mm = [same & (sub[:, None] == i) & (sub[None, :] > i) for i in range(c // sb - 1)]
        msk.append(np.stack(mm))
    cum = jnp.asarray(np.stack(cum).astype(np.float32), dtype=jnp.bfloat16)
    return cum, jnp.asarray(np.stack(msk).astype(np.float32))


def _hgrn_body(ff_ref, fb_ref, v_ref, q_ref, g_ref, lbp_ref, ng_ref, cum_ref, msk_ref, o_ref,
               acc_scr, st_scr, *, n_blk, l_ctx):
    tb, c, sb, dk = HG_TB, HG_C, HG_SUB, HG_DK
    n_c = tb // c
    n_sub = c // sb
    acc_scr[...] = jnp.zeros_like(acc_scr)
    st_scr[...] = jnp.zeros_like(st_scr)
    loglb, log1mlb = lbp_ref[0, 0:1, :], lbp_ref[0, 1:2, :]
    pos = lax.broadcasted_iota(jnp.int32, (tb, dk), 0) % sb
    nt = (((1,), (1,)), ((), ()))
    tn = (((0,), (0,)), ((), ()))
    dot = partial(jnp.dot, preferred_element_type=jnp.float32)

    def direction(base, f_ref, d):
        rows = pl.ds(base, tb)
        fp = f_ref[0, rows, :]
        ls = _log_sigmoid(fp)
        b2 = log1mlb + ls
        logf = jnp.maximum(loglb, b2) + jnp.log1p(jnp.exp(-jnp.abs(loglb - b2)))
        lk = log1mlb + (ls - fp)
        v = v_ref[0, rows, :]
        qr = q_ref[0, rows, :]
        q = qr * (1.0 / (1.0 + jnp.exp(-qr)))
        r = dot(cum_ref[d], _split3(logf))
        cum, tot, pre = _sum3(r[0:tb], dk), _sum3(r[tb:2 * tb], dk), _sum3(r[2 * tb:3 * tb], dk)
        g = lk - cum
        cum3 = cum.reshape(n_c, c, dk)
        qe = (q * jnp.exp(cum)).astype(jnp.bfloat16)
        kt = jnp.exp(g + tot).astype(jnp.bfloat16)
        dec = jnp.exp(tot)
        vb = v.astype(jnp.bfloat16)
        yield
        o = jnp.sum(q * jnp.exp(lk), axis=-1, keepdims=True) * v
        for j in range(1, sb):
            sh = j if d == 0 else tb - j
            valid = (pos >= j) if d == 0 else (pos < sb - j)
            gj = pltpu.roll(g, sh, 0)
            vj = pltpu.roll(v, sh, 0)
            e = jnp.exp(jnp.where(valid, cum + gj, _NEG))
            o = o + jnp.sum(q * e, axis=-1, keepdims=True) * vj
        yield
        q4 = (q * jnp.exp(pre)).astype(jnp.bfloat16)
        att = None
        for i in range(n_sub - 1):
            ref_row = sb * (i + 1) - 1 if d == 0 else sb * (i + 1)
            r_i = jnp.broadcast_to(cum3[:, ref_row:ref_row + 1, :], (n_c, c, dk)).reshape(tb, dk)
            kh = jnp.exp(lk + jnp.minimum(r_i - cum, 0.0)).astype(jnp.bfloat16)
            a_i = lax.dot_general(q4, kh, nt, preferred_element_type=jnp.float32) * msk_ref[d, i]
            att = a_i if att is None else att + a_i
        o = o + dot(att.astype(jnp.bfloat16), vb)
        yield
        order = [i if d == 0 else n_c - 1 - i for i in range(n_c)]
        kvs = [lax.dot_general(vb[cc * c:(cc + 1) * c], kt[cc * c:(cc + 1) * c], tn,
                               preferred_element_type=jnp.float32) for cc in order]
        yield
        st = st_scr[d]
        sts = []
        for cc, kv in zip(order, kvs):
            sts.append(st.astype(jnp.bfloat16))
            st = st * dec[cc * c:cc * c + 1, :] + kv
        st_scr[d] = st
        o_st = [None] * n_c
        for cc, sb16 in zip(order, sts):
            o_st[cc] = lax.dot_general(qe[cc * c:(cc + 1) * c], sb16, nt, preferred_element_type=jnp.float32)
        acc_scr[rows, :] += o + jnp.concatenate(o_st, axis=0)

    def step(j, carry):
        base_f = pl.multiple_of(j * tb, tb)
        jb = jnp.where(j == 0, 0, n_blk - j)
        base_b = pl.multiple_of(jb * tb, tb)
        todo = [direction(base_f, ff_ref, 0), direction(base_b, fb_ref, 1)]
        while todo:
            todo = [gen for gen in todo if next(gen, True) is None]
        return carry

    lax.fori_loop(0, n_blk, step, 0)
    o = acc_scr[l_ctx:, :]
    g = g_ref[0, l_ctx:, :]
    o = o * lax.rsqrt(jnp.mean(o * o, axis=-1, keepdims=True) + EPS) * ng_ref[...]
    o_ref[0] = o * (g * (1.0 / (1.0 + jnp.exp(-g))))


def _hgrn_mix(u_all, lb, norm_g, l_ctx):
    bsz, l_tot, _ = u_all.shape
    nh = HG_HEADS
    n_blk = l_tot // HG_TB
    lbh = lb.reshape(nh, 1, HG_DK)
    lbp = jnp.concatenate([jnp.log(lbh), jnp.log1p(-lbh),
                           jnp.zeros((nh, _SUBLANES - 2, HG_DK), jnp.float32)], axis=1)
    cum_m, msk_m = _hgrn_consts()

    def slab(k):
        return pl.BlockSpec((1, l_tot, HG_DK), lambda b, h, k=k: (b, 0, k * nh + h))

    def const(a):
        return pl.BlockSpec(a.shape, lambda b, h: (0,) * a.ndim)

    return pl.pallas_call(
        partial(_hgrn_body, n_blk=n_blk, l_ctx=l_ctx),
        grid=(bsz, nh),
        in_specs=[slab(0), slab(1), slab(2), slab(4), slab(5),
                  pl.BlockSpec((1, _SUBLANES, HG_DK), lambda b, h: (h, 0, 0)),
                  pl.BlockSpec((1, HG_DK), lambda b, h: (0, 0)),
                  const(cum_m), const(msk_m)],
        out_specs=pl.BlockSpec((1, l_tot - l_ctx, HG_DK), lambda b, h: (b, 0, h)),
        out_shape=jax.ShapeDtypeStruct((bsz, l_tot - l_ctx, W_HALF), jnp.float32),
        scratch_shapes=[pltpu.VMEM((l_tot, HG_DK), jnp.float32), pltpu.VMEM((2, HG_DK, HG_DK), jnp.float32)],
        compiler_params=pltpu.CompilerParams(dimension_semantics=("parallel", "parallel"),
                                             vmem_limit_bytes=_VMEM_LIMIT),
        name="hgrn_mix",
    )(u_all, u_all, u_all, u_all, u_all, lbp, norm_g.reshape(1, HG_DK), cum_m, msk_m)


def _s5_weights(lam_re, lam_im, log_dt, b_re, b_im, c_re, c_im):
    t_len, f32 = S5_T, jnp.float32
    lre = jnp.minimum(lam_re, S5_MAX_RE)
    dt = jnp.exp(log_dt)[..., None]
    ar, ai = lre * dt, lam_im * dt
    k = jnp.arange(t_len + 1, dtype=f32)[:, None, None, None]
    mag = jnp.exp(k * ar)
    pre, pim = mag * jnp.cos(k * ai), mag * jnp.sin(k * ai)
    nr, ni, den = pre[1] - 1.0, pim[1], lre * lre + lam_im * lam_im
    cr, ci = (nr * lre + ni * lam_im) / den, (ni * lre - nr * lam_im) / den
    cbr = cr[..., None] * b_re - ci[..., None] * b_im
    cbi = cr[..., None] * b_im + ci[..., None] * b_re
    c_re_t, c_im_t = jnp.swapaxes(c_re, 1, 2), jnp.swapaxes(c_im, 1, 2)
    clre = c_re_t * pre[..., None] - c_im_t * pim[..., None]
    clim = c_re_t * pim[..., None] + c_im_t * pre[..., None]
    g = lam_re.shape[1]

    def w_dir(d, e):
        pr, pi = pre[e, d], pim[e, d]
        wr = pr[..., None] * cbr[d] - pi[..., None] * cbi[d]
        wi = pr[..., None] * cbi[d] + pi[..., None] * cbr[d]
        return [jnp.transpose(z, (1, 0, 3, 2)).reshape(g, t_len * S5_H, S5_P) for z in (wr, wi)]

    wf, wb = w_dir(0, jnp.arange(t_len - 1, -1, -1)), w_dir(1, jnp.arange(t_len))
    w = jnp.concatenate(wf + wf[::-1] + wb + wb[::-1], axis=-1)
    dec = jnp.concatenate([pre[t_len, 0], pre[t_len, 0], -pim[t_len, 0], pim[t_len, 0],
                           pre[t_len, 1], pre[t_len, 1], -pim[t_len, 1], pim[t_len, 1]], axis=-1)[:, None, :]

    def m_dir(d, e):
        mr, mi = clre[e, d], -clim[e, d]
        return [jnp.transpose(z, (1, 2, 0, 3)).reshape(g, S5_P, t_len * S5_H) for z in (mr, mi)]

    m = jnp.concatenate(m_dir(0, jnp.arange(1, t_len + 1)) + m_dir(1, jnp.arange(t_len, 0, -1)), axis=1)
    kk = (jnp.einsum('kdgpo,dgpi->kdgoi', clre[:t_len], cbr, precision=_HI)
          - jnp.einsum('kdgpo,dgpi->kdgoi', clim[:t_len], cbi, precision=_HI))
    delta = jnp.arange(t_len)[None, :] - jnp.arange(t_len)[:, None]
    kf = kk[jnp.clip(delta, 0, t_len - 1), 0]
    kb = kk[jnp.clip(-delta, 0, t_len - 1), 1]
    dl = delta[:, :, None, None, None]
    toep = jnp.where(dl > 0, kf, 0.0) + jnp.where(dl < 0, kb, 0.0) + jnp.where(dl == 0, kf + kb, 0.0)
    toep = jnp.transpose(toep, (2, 0, 4, 1, 3)).reshape(g, t_len * S5_H, t_len * S5_H)
    return w, dec, m, toep


def _s5_body(x_ref, w_ref, dec_ref, m_ref, toep_ref, y_ref, z_scr, s_scr, *, bsz, n_ctx, n_lat):
    p2 = 2 * S5_P
    n_all = n_ctx + n_lat
    xb = x_ref[0].astype(jnp.bfloat16)
    z_scr[...] = jnp.dot(xb, w_ref[0], preferred_element_type=jnp.float32)
    dec = jnp.broadcast_to(dec_ref[0], (bsz, 4 * p2))
    a1f, a2f = dec[:, 0:p2], dec[:, p2:2 * p2]
    a1b, a2b = dec[:, 2 * p2:3 * p2], dec[:, 3 * p2:4 * p2]

    def step(i, carry):
        sf, tf, sb, tb = carry
        rf = pl.multiple_of(i * bsz, bsz)
        cb = jnp.where(i < n_ctx, n_ctx - 1 - i, n_all + n_ctx - 1 - i)
        rb = pl.multiple_of(cb * bsz, bsz)
        s_scr[pl.ds(rf, bsz), 0:p2] = sf
        s_scr[pl.ds(rb, bsz), p2:2 * p2] = sb
        zf = z_scr[pl.ds(rf, bsz), 0:2 * p2]
        zb = z_scr[pl.ds(rb, bsz), 2 * p2:4 * p2]
        return (sf * a1f + tf * a2f + zf[:, :p2], tf * a1f - sf * a2f + zf[:, p2:],
                sb * a1b + tb * a2b + zb[:, :p2], tb * a1b - sb * a2b + zb[:, p2:])

    zero = jnp.zeros((bsz, p2), jnp.float32)
    lax.fori_loop(0, n_all, step, (zero, zero, zero, zero), unroll=4)
    r0 = n_ctx * bsz
    y_ref[0] = (jnp.dot(xb[r0:], toep_ref[0], preferred_element_type=jnp.float32)
                + jnp.dot(s_scr[r0:, :].astype(jnp.bfloat16), m_ref[0], preferred_element_type=jnp.float32))


def _s5_mix(u_c, u_l, lam_re, lam_im, log_dt, b_re, b_im, c_re, c_im):
    bsz, l_ctx, gh = u_c.shape
    l_lat = u_l.shape[1]
    g, t_len = gh // S5_H, S5_T
    n_ctx, n_lat = l_ctx // t_len, l_lat // t_len
    w, dec, m, toep = _s5_weights(lam_re, lam_im, log_dt, b_re, b_im, c_re, c_im)

    def chunked(u, n):
        return jnp.transpose(u.reshape(bsz, n, t_len, g, S5_H), (3, 1, 0, 2, 4)).reshape(g, n * bsz, t_len * S5_H)

    x = jnp.concatenate([chunked(u_c, n_ctx), chunked(u_l, n_lat)], axis=1)
    rows, k = (n_ctx + n_lat) * bsz, t_len * S5_H
    y = pl.pallas_call(
        partial(_s5_body, bsz=bsz, n_ctx=n_ctx, n_lat=n_lat),
        grid=(g,),
        in_specs=[pl.BlockSpec((1, rows, k), lambda i: (i, 0, 0)),
                  pl.BlockSpec((1, k, 8 * S5_P), lambda i: (i, 0, 0)),
                  pl.BlockSpec((1, 1, 8 * S5_P), lambda i: (i, 0, 0)),
                  pl.BlockSpec((1, 4 * S5_P, k), lambda i: (i, 0, 0)),
                  pl.BlockSpec((1, k, k), lambda i: (i, 0, 0))],
        out_specs=pl.BlockSpec((1, n_lat * bsz, k), lambda i: (i, 0, 0)),
        out_shape=jax.ShapeDtypeStruct((g, n_lat * bsz, k), jnp.float32),
        scratch_shapes=[pltpu.VMEM((rows, 8 * S5_P), jnp.float32), pltpu.VMEM((rows, 4 * S5_P), jnp.float32)],
        compiler_params=pltpu.CompilerParams(dimension_semantics=("parallel",), vmem_limit_bytes=_VMEM_LIMIT),
        name="s5_mix",
    )(x, w.astype(jnp.bfloat16), dec, m.astype(jnp.bfloat16), toep.astype(jnp.bfloat16))
    return jnp.transpose(y.reshape(g, n_lat, bsz, t_len, S5_H), (2, 1, 3, 0, 4)).reshape(bsz, l_lat, gh)


def _glu_body(y_ref, u_ref, d_ref, w_ref, b_ref, o_ref):
    y = _gelu_tanh(y_ref[0] + u_ref[0] * d_ref[...])
    z = jnp.dot(y.astype(jnp.bfloat16), w_ref[...], preferred_element_type=jnp.float32) + b_ref[...]
    o_ref[0] = y * (1.0 / (1.0 + jnp.exp(-z)))


def _s5_out(y, u_all, l_ctx, col0, d_skip, glu_w, glu_b):
    bsz, l, c = y.shape
    tm = l_ctx
    vec = pl.BlockSpec((1, c), lambda b, i: (0, 0))
    return pl.pallas_call(
        _glu_body,
        grid=(bsz, l // tm),
        in_specs=[pl.BlockSpec((1, tm, c), lambda b, i: (b, i, 0)),
                  pl.BlockSpec((1, tm, c), lambda b, i: (b, i + 1, col0 // c)), vec,
                  pl.BlockSpec((c, c), lambda b, i: (0, 0)), vec],
        out_specs=pl.BlockSpec((1, tm, c), lambda b, i: (b, i, 0)),
        out_shape=jax.ShapeDtypeStruct(y.shape, jnp.float32),
        compiler_params=pltpu.CompilerParams(dimension_semantics=("parallel", "parallel")),
        name="s5_glu",
    )(y, u_all, d_skip.reshape(1, c), glu_w.astype(jnp.bfloat16), glu_b.reshape(1, c))


def _odd_mixer(u, l_ctx, lb, hg_norm_g, lam_re, lam_im, log_dt, b_re, b_im, c_re, c_im, d_skip, glu_w, glu_b):
    w = W_HALF
    hg_l = _hgrn_mix(u, lb, hg_norm_g, l_ctx)
    s_c, s_l = u[:, :l_ctx, 3 * w:4 * w], u[:, l_ctx:, 3 * w:4 * w]
    y_l = _s5_mix(s_c, s_l, lam_re, lam_im, log_dt, b_re, b_im, c_re, c_im)
    return hg_l, _s5_out(y_l, u, l_ctx, 3 * w, d_skip, glu_w, glu_b)


def _router_body(x_ref, g_ref, sh_ref, sc_ref, wr_ref, h_ref, lg_ref):
    x = x_ref[0]
    h = x * lax.rsqrt(jnp.mean(x * x, axis=-1, keepdims=True) + EPS) * g_ref[...]
    h = h * (1.0 + sc_ref[0]) + sh_ref[0]
    hb = h.astype(jnp.bfloat16)
    lg_ref[0] = jnp.dot(hb, wr_ref[...], preferred_element_type=jnp.float32)
    bits = pltpu.bitcast(hb.astype(jnp.float32), jnp.uint32)
    half = bits.shape[1] // 2
    h_ref[0] = (bits[:, :half] >> 16) | (bits[:, half:] & jnp.uint32(0xFFFF0000))


def _router(x, norm_g, shift, scale, w_router):
    bsz, n, d = x.shape
    tm = min(n, 512)
    vec = pl.BlockSpec((1, 1, d), lambda b, i: (b, 0, 0))
    return pl.pallas_call(
        _router_body,
        grid=(bsz, n // tm),
        in_specs=[pl.BlockSpec((1, tm, d), lambda b, i: (b, i, 0)),
                  pl.BlockSpec((1, d), lambda b, i: (0, 0)), vec, vec,
                  pl.BlockSpec((d, N_EXPERTS), lambda b, i: (0, 0))],
        out_specs=[pl.BlockSpec((1, tm, d // 2), lambda b, i: (b, i, 0)),
                   pl.BlockSpec((1, tm, N_EXPERTS), lambda b, i: (b, i, 0))],
        out_shape=[jax.ShapeDtypeStruct((bsz, n, d // 2), jnp.uint32),
                   jax.ShapeDtypeStruct((bsz, n, N_EXPERTS), jnp.float32)],
        compiler_params=pltpu.CompilerParams(dimension_semantics=("parallel", "parallel"),
                                             vmem_limit_bytes=_VMEM_LIMIT),
        name="moe_router",
    )(x, norm_g.reshape(1, d), jnp.broadcast_to(shift, (bsz, 1, d)), jnp.broadcast_to(scale, (bsz, 1, d)),
      w_router.astype(jnp.bfloat16))


def _expert_body(idx_ref, nxt_ref, h_ref, gate_ref, mod_ref, wg_ref, wu_ref, wd_ref, y_ref, x_cur, x_nxt,
                 *, cap, n_f):
    @pl.when(pl.program_id(1) == 0)
    def _():
        def grab(g, carry):
            base = pl.multiple_of(g * _SUBLANES, _SUBLANES)
            for r in range(_SUBLANES):
                x_nxt[pl.ds(base + r, 1), :] = h_ref[0, pl.ds(idx_ref[0, 0, base + r], 1), :]
            return carry

        lax.fori_loop(0, cap // _SUBLANES, grab, 0)

    x_cur[...] = x_nxt[...]
    w32 = x_cur[...]
    xb = jnp.concatenate([pltpu.bitcast(w32 << 16, jnp.float32),
                          pltpu.bitcast(w32 & jnp.uint32(0xFFFF0000), jnp.float32)], axis=1).astype(jnp.bfloat16)
    for r in range(cap):
        x_nxt[pl.ds(r, 1), :] = h_ref[0, pl.ds(nxt_ref[0, 0, r], 1), :]
    tf = wg_ref.shape[3] // n_f
    acc = None
    for f in range(n_f):
        cs = slice(f * tf, (f + 1) * tf)
        a = jnp.dot(xb, wg_ref[0, 0, :, cs], preferred_element_type=jnp.float32)
        u = jnp.dot(xb, wu_ref[0, 0, :, cs], preferred_element_type=jnp.float32)
        hh = (a * (1.0 / (1.0 + jnp.exp(-a))) * u).astype(jnp.bfloat16)
        part = jnp.dot(hh, wd_ref[0, 0, cs, :], preferred_element_type=jnp.float32)
        acc = part if acc is None else acc + part
    y_ref[0] = acc * gate_ref[0] * mod_ref[0]


def _expert_ffn(h32, idx, gate, mod, layer, w_gate, w_up, w_down):
    bsz, n, dh = h32.shape
    d = 2 * dh
    cap = idx.shape[-1]
    _, e, _, ff = w_gate.shape
    be = lambda b, k: (b * e + k, 0, 0)
    be_next = lambda b, k: (b * e + jnp.minimum(k + 1, e - 1), 0, 0)
    wspec = lambda s: pl.BlockSpec((1, 1) + s, lambda b, k: (layer, k, 0, 0))
    idx3 = idx.reshape(bsz * e, 1, cap)
    return pl.pallas_call(
        partial(_expert_body, cap=cap, n_f=3),
        grid=(bsz, e),
        in_specs=[pl.BlockSpec((1, 1, cap), be, memory_space=pltpu.SMEM),
                  pl.BlockSpec((1, 1, cap), be_next, memory_space=pltpu.SMEM),
                  pl.BlockSpec((1, n, dh), lambda b, k: (b, 0, 0)),
                  pl.BlockSpec((1, cap, 1), be),
                  pl.BlockSpec((1, 1, d), lambda b, k: (b, 0, 0)),
                  wspec((d, ff)), wspec((d, ff)), wspec((ff, d))],
        out_specs=pl.BlockSpec((1, cap, d), be),
        out_shape=jax.ShapeDtypeStruct((bsz * e, cap, d), jnp.float32),
        scratch_shapes=[pltpu.VMEM((cap, dh), jnp.uint32), pltpu.VMEM((cap, dh), jnp.uint32)],
        compiler_params=pltpu.CompilerParams(dimension_semantics=("parallel", "arbitrary"),
                                             vmem_limit_bytes=_VMEM_LIMIT_BIG),
        name="moe_expert",
    )(idx3, idx3, h32, gate.reshape(bsz * e, cap, 1), jnp.broadcast_to(mod, (bsz, 1, d)),
      w_gate, w_up, w_down)


def _combine_body(idx_ref, y_ref, o_ref, *, cap):
    @pl.when(pl.program_id(1) == 0)
    def _():
        o_ref[...] = jnp.zeros_like(o_ref)

    def group(g, carry):
        base = pl.multiple_of(g * _SUBLANES, _SUBLANES)
        ys = y_ref[0, pl.ds(base, _SUBLANES), :]
        ts = [idx_ref[0, 0, base + r] for r in range(_SUBLANES)]
        cur = [o_ref[0, pl.ds(t, 1), :] for t in ts]
        for r in range(_SUBLANES):
            o_ref[0, pl.ds(ts[r], 1), :] = cur[r] + ys[r:r + 1, :]
        return carry

    lax.fori_loop(0, cap // _SUBLANES, group, 0)


def _combine(y, idx, n):
    be, cap, d = y.shape
    bsz = be // N_EXPERTS
    return pl.pallas_call(
        partial(_combine_body, cap=cap),
        grid=(bsz, N_EXPERTS),
        in_specs=[pl.BlockSpec((1, 1, cap), lambda b, k: (b * N_EXPERTS + k, 0, 0), memory_space=pltpu.SMEM),
                  pl.BlockSpec((1, cap, d), lambda b, k: (b * N_EXPERTS + k, 0, 0))],
        out_specs=pl.BlockSpec((1, n, d), lambda b, k: (b, 0, 0)),
        out_shape=jax.ShapeDtypeStruct((bsz, n, d), jnp.float32),
        compiler_params=pltpu.CompilerParams(dimension_semantics=("parallel", "arbitrary"),
                                             vmem_limit_bytes=_VMEM_LIMIT),
        name="moe_combine",
    )(idx.reshape(be, 1, cap), y)


def _moe(x, norm_g, shift, scale, mod, w_router, layer, w_gate, w_up, w_down):
    bsz, n, d = x.shape
    cap = max(1, (EC_CAPACITY * n) // N_EXPERTS)
    h32, logits = _router(x, norm_g, shift, scale, w_router)
    aff = jax.nn.softmax(logits, axis=-1)
    gate, idx = lax.top_k(jnp.swapaxes(aff, 1, 2), cap)
    y = _expert_ffn(h32, idx, gate, mod, layer, w_gate, w_up, w_down)
    return _combine(y, idx, n)


def kernel(x, c, ctx, c_ctx, mod_w, mod_b, norm1_g, norm2_g, out_w, even_in_w, lru_conv_w, lru_conv_b, lru_wa, lru_ba, lru_wx, lru_bx, lru_lam, hy_conv_w, hy_conv_b, hy_w1, hy_b1, hy_w2, hy_b2, hy_w3, hy_freq, hy_skip, odd_in_w, hg_lb, hg_norm_g, s5_lam_re, s5_lam_im, s5_log_dt, s5_b_re, s5_b_im, s5_c_re, s5_c_im, s5_d, s5_glu_w, s5_glu_b, router_w, ex_w_gate, ex_w_up, ex_w_down, final_g):
    rows = x.shape[1] // GRID_W
    l_ctx = ctx.shape[1]
    p = jax.nn.softmax(hg_lb, axis=0)
    lb_all = jnp.clip(jnp.cumsum(p, axis=0) - p[0], 0.0, 1.0 - 1e-4)
    ex_wg, ex_wu, ex_wd = (t.astype(jnp.bfloat16) for t in (ex_w_gate, ex_w_up, ex_w_down))
    sc_lat = jax.nn.silu(c)
    sc_ctx = jax.nn.silu(c_ctx)[None]
    for i in range(DEPTH):
        need_ctx = i < DEPTH - 1
        j = i // 2
        ml = [t[:, None, :] for t in jnp.split(sc_lat @ mod_w[i] + mod_b[i], 6, axis=-1)]
        mc = [t[:, None, :] for t in jnp.split(sc_ctx @ mod_w[i] + mod_b[i], 6, axis=-1)]
        shift, scale, gate = (_two_part(mc[k], ml[k]) for k in range(3))
        if i % 2 == 0:
            x_all = jnp.concatenate([ctx, x], axis=1)
            u = _norm_mm(x_all, norm1_g[i], shift, scale, even_in_w[j], l_ctx)
            h_f, h_b, yb = _even_mixer(u, l_ctx, lru_conv_w[j], lru_conv_b[j], lru_wa[j], lru_ba[j], lru_wx[j],
                                       lru_bx[j], lru_lam[j], hy_conv_w[j], hy_conv_b[j], hy_w1[j], hy_b1[j],
                                       hy_w2[j], hy_b2[j], hy_w3[j], hy_freq[j], hy_skip[j])
            x_all = _proj_res_lru(h_f, h_b, u, yb, out_w[i], x_all, gate, l_ctx)
            ctx, x = x_all[:, :l_ctx], x_all[:, l_ctx:]
        else:
            assert not need_ctx
            x_all = jnp.concatenate([ctx, _to_col_major(x, rows)], axis=1)
            u = _norm_mm(x_all, norm1_g[i], shift, scale, odd_in_w[j], l_ctx)
            hg_l, s5_l = _odd_mixer(u, l_ctx, lb_all[i], hg_norm_g[j], s5_lam_re[j], s5_lam_im[j],
                                    s5_log_dt[j], s5_b_re[j], s5_b_im[j], s5_c_re[j], s5_c_im[j], s5_d[j],
                                    s5_glu_w[j], s5_glu_b[j])
            x = _to_row_major(_proj_res(hg_l, s5_l, out_w[i], x_all, gate, l_ctx, l_ctx), rows)
        ex = (router_w[i], i, ex_wg, ex_wu, ex_wd)
        moe_l = _moe(x, norm2_g[i], ml[3], ml[4], ml[5], *ex)
        if need_ctx:
            x = x + moe_l
            ctx = ctx + _moe(ctx, norm2_g[i], mc[3], mc[4], mc[5], *ex)
    return _add_norm(x, moe_l, final_g)
```

```python
import math
from functools import partial

import numpy as np
import jax
import jax.numpy as jnp
from jax import lax
from jax.experimental import pallas as pl
from jax.experimental.pallas import tpu as pltpu

D_MODEL = 1024
DEPTH = 2
GRID_W = 64
W_HALF = D_MODEL // 2
EPS = 1e-6
LRU_HEADS = 8
LRU_HEAD_DIM = W_HALF // LRU_HEADS
LRU_C = 8.0
HY_ORDER = 2
HY_BANDS = 16
HY_MAX_DECAY = math.log(1e-2) / 0.3
HY_MIN_DECAY = math.log(1e-2) / 1.5
HG_HEADS = 4
HG_DK = W_HALF // HG_HEADS
HG_C = 16
HG_SUB = 4
HG_TB = 256
S5_H = 16
S5_P = 64
S5_MAX_RE = -1e-4
S5_T = 16
_DFT_LOW = 64
N_EXPERTS = 16
EC_CAPACITY = 2

_SUBLANES = 8
_DW_ROWS = 256
_EXPERT_ROWS = 512
_VMEM_LIMIT = 48 * 1024 * 1024
_VMEM_LIMIT_BIG = 56 * 1024 * 1024
_HI = lax.Precision.HIGHEST
_NEG = -1e30


def _mm_body(a_ref, b_ref, o_ref):
    o_ref[...] = jnp.dot(a_ref[...].astype(jnp.bfloat16), b_ref[...],
                         preferred_element_type=jnp.float32)


def _pick(n, pref):
    for t in pref:
        if n % t == 0:
            return t
    return n


def _mm(a, b):
    m, k = a.shape
    n = b.shape[1]
    tm = _pick(m, (512, 256, 128, 64, 32, 16, 8))
    tn = _pick(n, (512, 256, 128))
    return pl.pallas_call(
        _mm_body,
        grid=(m // tm, n // tn),
        in_specs=[pl.BlockSpec((tm, k), lambda i, j: (i, 0)),
                  pl.BlockSpec((k, tn), lambda i, j: (0, j))],
        out_specs=pl.BlockSpec((tm, tn), lambda i, j: (i, j)),
        out_shape=jax.ShapeDtypeStruct((m, n), jnp.float32),
        compiler_params=pltpu.CompilerParams(
            dimension_semantics=("parallel", "parallel"), vmem_limit_bytes=_VMEM_LIMIT),
        name="mm",
    )(a, b.astype(jnp.bfloat16))


def _mm3(a, b):
    bsz, l, k = a.shape
    return _mm(a.reshape(bsz * l, k), b).reshape(bsz, l, b.shape[1])


def _add_norm_body(x_ref, m_ref, g_ref, o_ref):
    x = x_ref[...] + m_ref[...]
    o_ref[...] = x * lax.rsqrt(jnp.mean(x * x, axis=-1, keepdims=True) + EPS) * g_ref[...]


def _add_norm(x, m, g):
    bsz, n, d = x.shape
    tm = min(n, 512)
    blk = pl.BlockSpec((1, tm, d), lambda b, i: (b, i, 0))
    return pl.pallas_call(
        _add_norm_body,
        grid=(bsz, n // tm),
        in_specs=[blk, blk, pl.BlockSpec((1, 1, d), lambda b, i: (0, 0, 0))],
        out_specs=blk,
        out_shape=jax.ShapeDtypeStruct(x.shape, jnp.float32),
        compiler_params=pltpu.CompilerParams(dimension_semantics=("parallel", "parallel")),
        name="add_norm",
    )(x, m, g.reshape(1, 1, d))


def _two_part(t_ctx, t_lat):
    bsz, _, d = t_lat.shape
    return jnp.stack([jnp.broadcast_to(t_ctx, (bsz, 1, d)), t_lat], axis=1).reshape(2 * bsz, 1, d)


def _norm_mm_body(x_ref, g_ref, sh_ref, sc_ref, w_ref, o_ref):
    x = x_ref[0]
    h = x * lax.rsqrt(jnp.mean(x * x, axis=-1, keepdims=True) + EPS) * g_ref[...]
    h = h * (1.0 + sc_ref[0]) + sh_ref[0]
    o_ref[0] = jnp.dot(h.astype(jnp.bfloat16), w_ref[...], preferred_element_type=jnp.float32)


def _norm_mm(x_all, norm_g, shift, scale, w, l_ctx):
    bsz, l_tot, d = x_all.shape
    n = w.shape[1]
    tm = l_ctx
    vec = pl.BlockSpec((1, 1, d), lambda b, j: (2 * b + jnp.minimum(j, 1), 0, 0))
    return pl.pallas_call(
        _norm_mm_body,
        grid=(bsz, l_tot // tm),
        in_specs=[pl.BlockSpec((1, tm, d), lambda b, j: (b, j, 0)),
                  pl.BlockSpec((1, d), lambda b, j: (0, 0)), vec, vec,
                  pl.BlockSpec((d, n), lambda b, j: (0, 0))],
        out_specs=pl.BlockSpec((1, tm, n), lambda b, j: (b, j, 0)),
        out_shape=jax.ShapeDtypeStruct((bsz, l_tot, n), jnp.float32),
        compiler_params=pltpu.CompilerParams(dimension_semantics=("parallel", "parallel"),
                                             vmem_limit_bytes=_VMEM_LIMIT),
        name="norm_mm",
    )(x_all, norm_g.reshape(1, d), shift, scale, w.astype(jnp.bfloat16))


def _proj_res_body(ya_ref, yb_ref, wa_ref, wb_ref, x_ref, gt_ref, o_ref):
    y = (jnp.dot(ya_ref[0].astype(jnp.bfloat16), wa_ref[...], preferred_element_type=jnp.float32)
         + jnp.dot(yb_ref[0].astype(jnp.bfloat16), wb_ref[...], preferred_element_type=jnp.float32))
    o_ref[0] = x_ref[0] + gt_ref[0] * y


def _proj_res_lru_body(hf_ref, hb_ref, u_ref, yb_ref, wa_ref, wb_ref, x_ref, gt_ref, o_ref):
    ya = (hf_ref[0] + hb_ref[0]) * _gelu_tanh(u_ref[0])
    y = (jnp.dot(ya.astype(jnp.bfloat16), wa_ref[...], preferred_element_type=jnp.float32)
         + jnp.dot(yb_ref[0].astype(jnp.bfloat16), wb_ref[...], preferred_element_type=jnp.float32))
    o_ref[0] = x_ref[0] + gt_ref[0] * y


def _proj_res_lru(h_f, h_b, u_all, yb, w, x_all, gate, l_ctx):
    bsz, l_tot, ka = h_f.shape
    d = w.shape[1]
    tm = l_ctx
    blk = lambda k, c=0: pl.BlockSpec((1, tm, k), lambda b, j: (b, j, c))
    return pl.pallas_call(
        _proj_res_lru_body,
        grid=(bsz, l_tot // tm),
        in_specs=[blk(ka), blk(ka), blk(ka, 1), blk(yb.shape[2]),
                  pl.BlockSpec((ka, d), lambda b, j: (0, 0)), pl.BlockSpec((yb.shape[2], d), lambda b, j: (0, 0)),
                  blk(d), pl.BlockSpec((1, 1, d), lambda b, j: (2 * b + jnp.minimum(j, 1), 0, 0))],
        out_specs=blk(d),
        out_shape=jax.ShapeDtypeStruct((bsz, l_tot, d), jnp.float32),
        compiler_params=pltpu.CompilerParams(dimension_semantics=("parallel", "parallel"),
                                             vmem_limit_bytes=_VMEM_LIMIT),
        name="proj_res_lru",
    )(h_f, h_b, u_all, yb, w[:ka].astype(jnp.bfloat16), w[ka:].astype(jnp.bfloat16), x_all, gate)


def _proj_res(ya, yb, w, x_all, gate, l_ctx, row0):
    bsz, l_y, ka = ya.shape
    d = w.shape[1]
    tm = l_ctx
    j0 = row0 // tm
    yblk = lambda k: pl.BlockSpec((1, tm, k), lambda b, j: (b, j, 0))
    return pl.pallas_call(
        _proj_res_body,
        grid=(bsz, l_y // tm),
        in_specs=[yblk(ka), yblk(yb.shape[2]),
                  pl.BlockSpec((ka, d), lambda b, j: (0, 0)), pl.BlockSpec((yb.shape[2], d), lambda b, j: (0, 0)),
                  pl.BlockSpec((1, tm, d), lambda b, j: (b, j + j0, 0)),
                  pl.BlockSpec((1, 1, d), lambda b, j: (2 * b + jnp.minimum(j + j0, 1), 0, 0))],
        out_specs=pl.BlockSpec((1, tm, d), lambda b, j: (b, j, 0)),
        out_shape=jax.ShapeDtypeStruct((bsz, l_y, d), jnp.float32),
        compiler_params=pltpu.CompilerParams(dimension_semantics=("parallel", "parallel"),
                                             vmem_limit_bytes=_VMEM_LIMIT),
        name="proj_res",
    )(ya, yb, w[:ka].astype(jnp.bfloat16), w[ka:].astype(jnp.bfloat16), x_all, gate)


def _gelu_tanh(x):
    return 0.5 * x * (1.0 + jnp.tanh(math.sqrt(2.0 / math.pi) * (x + 0.044715 * (x * x * x))))


def _dwconv_body(u_ref, w_ref, o_ref, *rest, l_ctx, taps, left, split):
    l_tot = u_ref.shape[1]
    tb, h = _DW_ROWS, _SUBLANES
    row = lax.broadcasted_iota(jnp.int32, (tb, 1), 0)

    def block(j, carry):
        r0 = pl.multiple_of(j * tb, tb)
        lo = jnp.where(r0 < l_ctx, 0, l_ctx)
        hi = jnp.where(r0 < l_ctx, l_ctx, l_tot)
        prev = u_ref[0, pl.ds(pl.multiple_of(jnp.maximum(r0 - h, 0), h), h), :]
        nxt = u_ref[0, pl.ds(pl.multiple_of(jnp.minimum(r0 + tb, l_tot - h), h), h), :]
        ext = jnp.concatenate([prev, u_ref[0, pl.ds(r0, tb), :], nxt], axis=0)
        y = jnp.broadcast_to(w_ref[0, taps:taps + 1, :], (tb, ext.shape[1]))
        for t in range(taps):
            off = t - left
            src = r0 + row + off
            y = y + jnp.where((src >= lo) & (src < hi), ext[h + off:h + off + tb, :], 0.0) * w_ref[0, t:t + 1, :]
        if not split:
            o_ref[0, pl.ds(r0, tb), :] = y
        else:
            ob_c, o_l, ob_l = rest

            @pl.when(r0 < l_ctx)
            def _():
                o_ref[0, pl.ds(r0, tb), :] = y
                ob_c[0, pl.ds(r0, tb), :] = y.astype(jnp.bfloat16)

            @pl.when(r0 >= l_ctx)
            def _():
                o_l[0, pl.ds(r0 - l_ctx, tb), :] = y
                ob_l[0, pl.ds(r0 - l_ctx, tb), :] = y.astype(jnp.bfloat16)
        return carry

    lax.fori_loop(0, l_tot // tb, block, 0)


def _dwconv_slabs(u_all, col0, w, b, l_ctx, split):
    bsz, l_tot, _ = u_all.shape
    taps, nc = w.shape
    c = W_HALF
    n = nc // c
    wb = jnp.concatenate([w, b[None], jnp.zeros((_SUBLANES - taps - 1, nc), jnp.float32)], axis=0)
    wb = wb.reshape(_SUBLANES, n, c).transpose(1, 0, 2)
    if split:
        rows = (l_ctx, l_ctx, l_tot - l_ctx, l_tot - l_ctx)
        dts = (jnp.float32, jnp.bfloat16, jnp.float32, jnp.bfloat16)
    else:
        rows, dts = (l_tot,), (jnp.float32,)
    out = pl.pallas_call(
        partial(_dwconv_body, l_ctx=l_ctx, taps=taps, left=(taps - 1) // 2, split=split),
        grid=(bsz, n),
        in_specs=[pl.BlockSpec((1, l_tot, c), lambda bb, s: (bb, 0, s + col0 // c)),
                  pl.BlockSpec((1, _SUBLANES, c), lambda bb, s: (s, 0, 0))],
        out_specs=[pl.BlockSpec((1, r, c), lambda bb, s: (bb, 0, s)) for r in rows],
        out_shape=[jax.ShapeDtypeStruct((bsz, r, nc), dt) for r, dt in zip(rows, dts)],
        compiler_params=pltpu.CompilerParams(dimension_semantics=("parallel", "parallel"),
                                             vmem_limit_bytes=_VMEM_LIMIT),
        name="dwconv",
    )(u_all, wb)
    return out if split else out[0]


def _to_col_major(t, rows):
    bsz, l, ch = t.shape
    return t.reshape(bsz, rows, GRID_W, ch).transpose(0, 2, 1, 3).reshape(bsz, l, ch)


def _to_row_major(t, rows):
    bsz, l, ch = t.shape
    return t.reshape(bsz, GRID_W, rows, ch).transpose(0, 2, 1, 3).reshape(bsz, l, ch)


def _lru_body(xf_ref, xb_ref, wa_ref, wx_ref, par_ref, hf_ref, hb_ref,
              af_scr, bf_scr, ab_scr, bb_scr, sf_scr, sb_scr, *, tb):
    @pl.when(pl.program_id(1) == 0)
    def _():
        sf_scr[...] = jnp.zeros_like(sf_scr)
        sb_scr[...] = jnp.zeros_like(sb_scr)

    def coeffs(x, d, a_scr, b_scr):
        xb = x.astype(jnp.bfloat16)
        r = jnp.dot(xb, wa_ref[d], preferred_element_type=jnp.float32) + par_ref[d, 0:1, :]
        i = jnp.dot(xb, wx_ref[d], preferred_element_type=jnp.float32) + par_ref[d, 1:2, :]
        r = 1.0 / (1.0 + jnp.exp(-r))
        i = 1.0 / (1.0 + jnp.exp(-i))
        log_a = -LRU_C * r * par_ref[d, 2:3, :]
        a = jnp.exp(log_a)
        a_scr[...] = a
        b_scr[...] = jnp.sqrt(-jnp.tanh(log_a) * (a * a + 1.0)) * i * x

    coeffs(xf_ref[0], 0, af_scr, bf_scr)
    coeffs(xb_ref[0], 1, ab_scr, bb_scr)

    def tile(i, carry):
        hf, hb = carry
        base_f = pl.multiple_of(i * _SUBLANES, _SUBLANES)
        base_b = pl.multiple_of(tb - _SUBLANES - i * _SUBLANES, _SUBLANES)
        for r in range(_SUBLANES):
            rf = pl.ds(base_f + r, 1)
            hf = af_scr[rf, :] * hf + bf_scr[rf, :]
            hf_ref[0, rf, :] = hf
            rb = pl.ds(base_b + (_SUBLANES - 1 - r), 1)
            hb = ab_scr[rb, :] * hb + bb_scr[rb, :]
            hb_ref[0, rb, :] = hb
        return hf, hb

    hf, hb = lax.fori_loop(0, tb // _SUBLANES, tile, (sf_scr[...], sb_scr[...]))
    sf_scr[...] = hf
    sb_scr[...] = hb


def _block_diag(w):
    two, h, n, _ = w.shape
    eye = jnp.eye(h, dtype=w.dtype)
    return (w[:, :, :, None, :] * eye[None, :, None, :, None]).reshape(two, h * n, h * n)


def _lru_scan(xc, l_ctx, wa, ba, wx, bx, lam):
    bsz, l_tot, ch = xc.shape
    tb = l_ctx
    nb = l_tot // tb
    par = jnp.stack([ba.reshape(2, ch), bx.reshape(2, ch), jax.nn.softplus(-lam).reshape(2, ch)], axis=1)
    par = jnp.concatenate([par, jnp.zeros((2, _SUBLANES - 3, ch), jnp.float32)], axis=1)
    fwd = pl.BlockSpec((1, tb, ch), lambda b, j: (b, j, 0))
    bwd = pl.BlockSpec((1, tb, ch), lambda b, j: (b, jnp.where(j == 0, 0, nb - j), 0))
    wspec = pl.BlockSpec((2, ch, ch), lambda b, j: (0, 0, 0))
    return pl.pallas_call(
        partial(_lru_body, tb=tb),
        grid=(bsz, nb),
        in_specs=[fwd, bwd, wspec, wspec, pl.BlockSpec((2, _SUBLANES, ch), lambda b, j: (0, 0, 0))],
        out_specs=[fwd, bwd],
        out_shape=[jax.ShapeDtypeStruct(xc.shape, jnp.float32)] * 2,
        scratch_shapes=[pltpu.VMEM((tb, ch), jnp.float32)] * 4 + [pltpu.VMEM((1, ch), jnp.float32)] * 2,
        compiler_params=pltpu.CompilerParams(dimension_semantics=("parallel", "arbitrary")),
        name="lru_scan",
    )(xc, xc, _block_diag(wa).astype(jnp.bfloat16), _block_diag(wx).astype(jnp.bfloat16), par)


def _hyena_filters(l, w1, b1, w2, b2, w3, freq):
    t = jnp.arange(l, dtype=jnp.float32)
    t_unit = t / max(l - 1, 1)
    bands = jnp.linspace(1e-4, HY_BANDS - 1, HY_BANDS, dtype=jnp.float32)
    ang = (2.0 * math.pi / l) * t[:, None] * bands[None, :]
    z = jnp.concatenate([t_unit[:, None], jnp.cos(ang), -jnp.sin(ang)], axis=-1)
    h = jnp.sin(freq * (z @ w1 + b1))
    h = jnp.sin(freq * (h @ w2 + b2))
    h = h @ w3
    dist = jnp.abs(t - (l // 2)) / (l / 2.0)
    deltas = jnp.abs(jnp.linspace(HY_MIN_DECAY, HY_MAX_DECAY, HY_ORDER * W_HALF, dtype=jnp.float32))
    h = h * jnp.exp(-dist[:, None] * deltas[None, :])
    h = h * lax.rsqrt(jnp.sum(h * h, axis=0, keepdims=True) + EPS)
    return h


def _dft_mats(l, tf):
    n = 2 * l
    nl = _DFT_LOW
    nh = l // nl
    t = jnp.arange(l, dtype=jnp.int32)
    w = 2.0 * math.pi / n

    def tables(tv):
        ah = (((nl * jnp.arange(nh, dtype=jnp.int32))[:, None] * tv[None, :]) % n).astype(jnp.float32) * w
        al = ((jnp.arange(nl, dtype=jnp.int32)[:, None] * tv[None, :]) % n).astype(jnp.float32) * w
        return jnp.cos(ah), jnp.sin(ah), jnp.cos(al), jnp.sin(al)

    ch, sh, cl, sl = tables(t)
    f_re = (ch[:, None, :] * cl[None] - sh[:, None, :] * sl[None]).reshape(l, l)
    f_im = -(sh[:, None, :] * cl[None] + ch[:, None, :] * sl[None]).reshape(l, l)
    f_im = f_im.at[0].set(jnp.where(t % 2 == 0, 1.0, -1.0))
    f = jnp.concatenate([f_re.reshape(l // tf, tf, l), f_im.reshape(l // tf, tf, l)], axis=1).reshape(n, l)
    tp = t + l // 2
    ch, sh, cl, sl = (z.T for z in tables(tp))
    g_re = (2.0 / n) * (ch[:, :, None] * cl[:, None, :] - sh[:, :, None] * sl[:, None, :]).reshape(l, l)
    g_im = (-2.0 / n) * (sh[:, :, None] * cl[:, None, :] + ch[:, :, None] * sl[:, None, :]).reshape(l, l)
    g_re = g_re.at[:, 0].set(1.0 / n)
    g_im = g_im.at[:, 0].set(jnp.where(tp % 2 == 0, 1.0, -1.0) / n)
    g = jnp.stack([g_re.reshape(l, l // tf, tf), g_im.reshape(l, l // tf, tf)], axis=2).reshape(l, n)
    return f.astype(jnp.bfloat16), g.astype(jnp.bfloat16)


def _spec_body(f_ref, u_ref, a_ref, b_ref, d_ref, y_ref, *, tf):
    acc = jnp.dot(f_ref[...], u_ref[0], preferred_element_type=jnp.float32)
    xr, xi = acc[:tf], acc[tf:]
    y_ref[0, :tf, :] = (xr * a_ref[...] - xi * b_ref[...]).astype(jnp.bfloat16)
    y_ref[0, tf:, :] = (xr * b_ref[...] + xi * d_ref[...]).astype(jnp.bfloat16)


def _spec_mul(f, u, a, b, d, tf):
    bsz, l, _ = u.shape
    c = a.shape[1]
    coef = pl.BlockSpec((tf, c), lambda i, bb: (i, 0))
    return pl.pallas_call(
        partial(_spec_body, tf=tf),
        grid=(l // tf, bsz),
        in_specs=[pl.BlockSpec((2 * tf, l), lambda i, bb: (i, 0)),
                  pl.BlockSpec((1, l, c), lambda i, bb: (bb, 0, 0)), coef, coef, coef],
        out_specs=pl.BlockSpec((1, 2 * tf, c), lambda i, bb: (bb, i, 0)),
        out_shape=jax.ShapeDtypeStruct((bsz, 2 * l, c), jnp.bfloat16),
        compiler_params=pltpu.CompilerParams(dimension_semantics=("parallel", "parallel"),
                                             vmem_limit_bytes=_VMEM_LIMIT),
        name="hy_spec",
    )(f, u, a, b, d)


def _inv_body(g_ref, y_ref, u_ref, x_ref, skip_ref, z_ref, zb_ref):
    acc = jnp.dot(g_ref[...], y_ref[0], preferred_element_type=jnp.float32)
    z = x_ref[0] * (acc + u_ref[0] * skip_ref[...])
    z_ref[0] = z
    zb_ref[0] = z.astype(jnp.bfloat16)


def _inv_gate(g, y, u, ucol, xg, xcol, skip, tm):
    bsz, n, c = y.shape
    l = n // 2
    blk = pl.BlockSpec((1, tm, c), lambda bb, i: (bb, i, 0))
    return pl.pallas_call(
        _inv_body,
        grid=(bsz, l // tm),
        in_specs=[pl.BlockSpec((tm, n), lambda bb, i: (i, 0)),
                  pl.BlockSpec((1, n, c), lambda bb, i: (bb, 0, 0)),
                  pl.BlockSpec((1, tm, c), lambda bb, i: (bb, i, ucol)),
                  pl.BlockSpec((1, tm, c), lambda bb, i: (bb, i, xcol)),
                  pl.BlockSpec((1, c), lambda bb, i: (0, 0))],
        out_specs=[blk, blk],
        out_shape=[jax.ShapeDtypeStruct((bsz, l, c), jnp.float32), jax.ShapeDtypeStruct((bsz, l, c), jnp.bfloat16)],
        compiler_params=pltpu.CompilerParams(dimension_semantics=("parallel", "parallel"),
                                             vmem_limit_bytes=_VMEM_LIMIT),
        name="hy_inv",
    )(g, y, u, xg, skip.reshape(1, c))


def _hyena_conv2(hc, hb, filt, skip):
    l, c = hc.shape[1], hc.shape[2] // 3
    tf = min(512, l // 2)
    tm = min(512, l)
    f, g = _dft_mats(l, tf)
    hs = _mm(f, filt)
    hs = hs.reshape(l // tf, 2, tf, 2 * c)
    h_re, h_im = hs[:, 0].reshape(l, 2 * c), hs[:, 1].reshape(l, 2 * c)
    slot0 = (jnp.arange(l) == 0)[:, None]
    a = h_re
    b = jnp.where(slot0, 0.0, h_im)
    d = jnp.where(slot0, h_im, h_re)
    y = _spec_mul(f, hb, a[:, :c], b[:, :c], d[:, :c], tf)
    z, zb = _inv_gate(g, y, hc, 0, hc, 1, skip[0], tm)
    y = _spec_mul(f, zb, a[:, c:], b[:, c:], d[:, c:], tf)
    z, _ = _inv_gate(g, y, z, 0, hc, 2, skip[1], tm)
    return z


def _hyena(u, l_ctx, conv_w, conv_b, w1, b1, w2, b2, w3, freq, skip):
    hc_c, hb_c, hc_l, hb_l = _dwconv_slabs(u, 2 * W_HALF, conv_w, conv_b, l_ctx, True)
    zs = []
    for hc, hb in ((hc_c, hb_c), (hc_l, hb_l)):
        filt = _hyena_filters(hc.shape[1], w1, b1, w2, b2, w3, freq)
        zs.append(_hyena_conv2(hc, hb, filt, skip))
    return jnp.concatenate(zs, axis=1)


def _even_mixer(u, l_ctx, lru_conv_w, lru_conv_b, lru_wa, lru_ba, lru_wx, lru_bx,
                lru_lam, hy_conv_w, hy_conv_b, hy_w1, hy_b1, hy_w2, hy_b2, hy_w3, hy_freq, hy_skip):
    w = W_HALF
    xc = _dwconv_slabs(u, 0, lru_conv_w, lru_conv_b, l_ctx, False)
    h_f, h_b = _lru_scan(xc, l_ctx, lru_wa, lru_ba, lru_wx, lru_bx, lru_lam)
    yb = _hyena(u, l_ctx, hy_conv_w, hy_conv_b, hy_w1, hy_b1, hy_w2, hy_b2, hy_w3, hy_freq, hy_skip)
    return h_f, h_b, yb


def _split3(x):
    hi = x.astype(jnp.bfloat16)
    r1 = x - hi.astype(jnp.float32)
    mid = r1.astype(jnp.bfloat16)
    lo = (r1 - mid.astype(jnp.float32)).astype(jnp.bfloat16)
    return jnp.concatenate([hi, mid, lo], axis=1)


def _sum3(r, c):
    return r[:, 0:c] + r[:, c:2 * c] + r[:, 2 * c:3 * c]


def _log_sigmoid(x):
    return jnp.minimum(x, 0.0) - jnp.log1p(jnp.exp(-jnp.abs(x)))


def _hgrn_consts():
    tb, c, sb = HG_TB, HG_C, HG_SUB
    r = np.arange(tb)
    same = (r[:, None] // c) == (r[None, :] // c)
    same4 = (r[:, None] // sb) == (r[None, :] // sb)
    le = r[None, :] <= r[:, None]
    sub = (r % c) // sb
    cum, msk = [], []
    for d in range(2):
        tri = le if d == 0 else le.T
        cum.append(np.concatenate([same & tri, same, same4 & tri], axis=0))
        if d == 0:
            mm = [same & (sub[:, None] == i + 1) & (sub[None, :] <= i) for i in range(c // sb - 1)]
        else:
            mm = [same & (sub[:, None] == i) & (sub[None, :] > i) for i in range(c // sb - 1)]
        msk.append(np.stack(mm))
    cum = jnp.asarray(np.stack(cum).astype(np.float32), dtype=jnp.bfloat16)
    return cum, jnp.asarray(np.stack(msk).astype(np.float32))


def _hgrn_body(ff_ref, fb_ref, v_ref, q_ref, g_ref, lbp_ref, ng_ref, cum_ref, msk_ref, o_ref,
               acc_scr, st_scr, *, n_blk, l_ctx):
    tb, c, sb, dk = HG_TB, HG_C, HG_SUB, HG_DK
    n_c = tb // c
    n_sub = c // sb
    acc_scr[...] = jnp.zeros_like(acc_scr)
    st_scr[...] = jnp.zeros_like(st_scr)
    loglb, log1mlb = lbp_ref[0, 0:1, :], lbp_ref[0, 1:2, :]
    pos = lax.broadcasted_iota(jnp.int32, (tb, dk), 0) % sb
    nt = (((1,), (1,)), ((), ()))
    tn = (((0,), (0,)), ((), ()))
    dot = partial(jnp.dot, preferred_element_type=jnp.float32)

    def direction(base, f_ref, d):
        rows = pl.ds(base, tb)
        fp = f_ref[0, rows, :]
        ls = _log_sigmoid(fp)
        b2 = log1mlb + ls
        logf = jnp.maximum(loglb, b2) + jnp.log1p(jnp.exp(-jnp.abs(loglb - b2)))
        lk = log1mlb + (ls - fp)
        v = v_ref[0, rows, :]
        qr = q_ref[0, rows, :]
        q = qr * (1.0 / (1.0 + jnp.exp(-qr)))
        r = dot(cum_ref[d], _split3(logf))
        cum, tot, pre = _sum3(r[0:tb], dk), _sum3(r[tb:2 * tb], dk), _sum3(r[2 * tb:3 * tb], dk)
        g = lk - cum
        cum3 = cum.reshape(n_c, c, dk)
        qe = (q * jnp.exp(cum)).astype(jnp.bfloat16)
        kt = jnp.exp(g + tot).astype(jnp.bfloat16)
        dec = jnp.exp(tot)
        vb = v.astype(jnp.bfloat16)
        yield
        o = jnp.sum(q * jnp.exp(lk), axis=-1, keepdims=True) * v
        for j in range(1, sb):
            sh = j if d == 0 else tb - j
            valid = (pos >= j) if d == 0 else (pos < sb - j)
            gj = pltpu.roll(g, sh, 0)
            vj = pltpu.roll(v, sh, 0)
            e = jnp.exp(jnp.where(valid, cum + gj, _NEG))
            o = o + jnp.sum(q * e, axis=-1, keepdims=True) * vj
        yield
        q4 = (q * jnp.exp(pre)).astype(jnp.bfloat16)
        att = None
        for i in range(n_sub - 1):
            ref_row = sb * (i + 1) - 1 if d == 0 else sb * (i + 1)
            r_i = jnp.broadcast_to(cum3[:, ref_row:ref_row + 1, :], (n_c, c, dk)).reshape(tb, dk)
            kh = jnp.exp(lk + jnp.minimum(r_i - cum, 0.0)).astype(jnp.bfloat16)
            a_i = lax.dot_general(q4, kh, nt, preferred_element_type=jnp.float32) * msk_ref[d, i]
            att = a_i if att is None else att + a_i
        o = o + dot(att.astype(jnp.bfloat16), vb)
        yield
        order = [i if d == 0 else n_c - 1 - i for i in range(n_c)]
        kvs = [lax.dot_general(vb[cc * c:(cc + 1) * c], kt[cc * c:(cc + 1) * c], tn,
                               preferred_element_type=jnp.float32) for cc in order]
        yield
        st = st_scr[d]
        sts = []
        for cc, kv in zip(order, kvs):
            sts.append(st.astype(jnp.bfloat16))
            st = st * dec[cc * c:cc * c + 1, :] + kv
        st_scr[d] = st
        o_st = [None] * n_c
        for cc, sb16 in zip(order, sts):
            o_st[cc] = lax.dot_general(qe[cc * c:(cc + 1) * c], sb16, nt, preferred_element_type=jnp.float32)
        acc_scr[rows, :] += o + jnp.concatenate(o_st, axis=0)

    def step(j, carry):
        base_f = pl.multiple_of(j * tb, tb)
        jb = jnp.where(j == 0, 0, n_blk - j)
        base_b = pl.multiple_of(jb * tb, tb)
        todo = [direction(base_f, ff_ref, 0), direction(base_b, fb_ref, 1)]
        while todo:
            todo = [gen for gen in todo if next(gen, True) is None]
        return carry

    lax.fori_loop(0, n_blk, step, 0)
    o = acc_scr[l_ctx:, :]
    g = g_ref[0, l_ctx:, :]
    o = o * lax.rsqrt(jnp.mean(o * o, axis=-1, keepdims=True) + EPS) * ng_ref[...]
    o_ref[0] = o * (g * (1.0 / (1.0 + jnp.exp(-g))))


def _hgrn_mix(u_all, lb, norm_g, l_ctx):
    bsz, l_tot, _ = u_all.shape
    nh = HG_HEADS
    n_blk = l_tot // HG_TB
    lbh = lb.reshape(nh, 1, HG_DK)
    lbp = jnp.concatenate([jnp.log(lbh), jnp.log1p(-lbh),
                           jnp.zeros((nh, _SUBLANES - 2, HG_DK), jnp.float32)], axis=1)
    cum_m, msk_m = _hgrn_consts()

    def slab(k):
        return pl.BlockSpec((1, l_tot, HG_DK), lambda b, h, k=k: (b, 0, k * nh + h))

    def const(a):
        return pl.BlockSpec(a.shape, lambda b, h: (0,) * a.ndim)

    return pl.pallas_call(
        partial(_hgrn_body, n_blk=n_blk, l_ctx=l_ctx),
        grid=(bsz, nh),
        in_specs=[slab(0), slab(1), slab(2), slab(4), slab(5),
                  pl.BlockSpec((1, _SUBLANES, HG_DK), lambda b, h: (h, 0, 0)),
                  pl.BlockSpec((1, HG_DK), lambda b, h: (0, 0)),
                  const(cum_m), const(msk_m)],
        out_specs=pl.BlockSpec((1, l_tot - l_ctx, HG_DK), lambda b, h: (b, 0, h)),
        out_shape=jax.ShapeDtypeStruct((bsz, l_tot - l_ctx, W_HALF), jnp.float32),
        scratch_shapes=[pltpu.VMEM((l_tot, HG_DK), jnp.float32), pltpu.VMEM((2, HG_DK, HG_DK), jnp.float32)],
        compiler_params=pltpu.CompilerParams(dimension_semantics=("parallel", "parallel"),
                                             vmem_limit_bytes=_VMEM_LIMIT),
        name="hgrn_mix",
    )(u_all, u_all, u_all, u_all, u_all, lbp, norm_g.reshape(1, HG_DK), cum_m, msk_m)


def _s5_weights(lam_re, lam_im, log_dt, b_re, b_im, c_re, c_im):
    t_len, f32 = S5_T, jnp.float32
    lre = jnp.minimum(lam_re, S5_MAX_RE)
    dt = jnp.exp(log_dt)[..., None]
    ar, ai = lre * dt, lam_im * dt
    k = jnp.arange(t_len + 1, dtype=f32)[:, None, None, None]
    mag = jnp.exp(k * ar)
    pre, pim = mag * jnp.cos(k * ai), mag * jnp.sin(k * ai)
    nr, ni, den = pre[1] - 1.0, pim[1], lre * lre + lam_im * lam_im
    cr, ci = (nr * lre + ni * lam_im) / den, (ni * lre - nr * lam_im) / den
    cbr = cr[..., None] * b_re - ci[..., None] * b_im
    cbi = cr[..., None] * b_im + ci[..., None] * b_re
    c_re_t, c_im_t = jnp.swapaxes(c_re, 1, 2), jnp.swapaxes(c_im, 1, 2)
    clre = c_re_t * pre[..., None] - c_im_t * pim[..., None]
    clim = c_re_t * pim[..., None] + c_im_t * pre[..., None]
    g = lam_re.shape[1]

    def w_dir(d, e):
        pr, pi = pre[e, d], pim[e, d]
        wr = pr[..., None] * cbr[d] - pi[..., None] * cbi[d]
        wi = pr[..., None] * cbi[d] + pi[..., None] * cbr[d]
        return [jnp.transpose(z, (1, 0, 3, 2)).reshape(g, t_len * S5_H, S5_P) for z in (wr, wi)]

    wf, wb = w_dir(0, jnp.arange(t_len - 1, -1, -1)), w_dir(1, jnp.arange(t_len))
    w = jnp.concatenate(wf + wf[::-1] + wb + wb[::-1], axis=-1)
    dec = jnp.concatenate([pre[t_len, 0], pre[t_len, 0], -pim[t_len, 0], pim[t_len, 0],
                           pre[t_len, 1], pre[t_len, 1], -pim[t_len, 1], pim[t_len, 1]], axis=-1)[:, None, :]

    def m_dir(d, e):
        mr, mi = clre[e, d], -clim[e, d]
        return [jnp.transpose(z, (1, 2, 0, 3)).reshape(g, S5_P, t_len * S5_H) for z in (mr, mi)]

    m = jnp.concatenate(m_dir(0, jnp.arange(1, t_len + 1)) + m_dir(1, jnp.arange(t_len, 0, -1)), axis=1)
    kk = (jnp.einsum('kdgpo,dgpi->kdgoi', clre[:t_len], cbr, precision=_HI)
          - jnp.einsum('kdgpo,dgpi->kdgoi', clim[:t_len], cbi, precision=_HI))
    delta = jnp.arange(t_len)[None, :] - jnp.arange(t_len)[:, None]
    kf = kk[jnp.clip(delta, 0, t_len - 1), 0]
    kb = kk[jnp.clip(-delta, 0, t_len - 1), 1]
    dl = delta[:, :, None, None, None]
    toep = jnp.where(dl > 0, kf, 0.0) + jnp.where(dl < 0, kb, 0.0) + jnp.where(dl == 0, kf + kb, 0.0)
    toep = jnp.transpose(toep, (2, 0, 4, 1, 3)).reshape(g, t_len * S5_H, t_len * S5_H)
    return w, dec, m, toep


def _s5_body(x_ref, w_ref, dec_ref, m_ref, toep_ref, y_ref, z_scr, s_scr, *, bsz, n_ctx, n_lat):
    p2 = 2 * S5_P
    n_all = n_ctx + n_lat
    xb = x_ref[0].astype(jnp.bfloat16)
    z_scr[...] = jnp.dot(xb, w_ref[0], preferred_element_type=jnp.float32)
    dec = jnp.broadcast_to(dec_ref[0], (bsz, 4 * p2))
    a1f, a2f = dec[:, 0:p2], dec[:, p2:2 * p2]
    a1b, a2b = dec[:, 2 * p2:3 * p2], dec[:, 3 * p2:4 * p2]

    def step(i, carry):
        sf, tf, sb, tb = carry
        rf = pl.multiple_of(i * bsz, bsz)
        cb = jnp.where(i < n_ctx, n_ctx - 1 - i, n_all + n_ctx - 1 - i)
        rb = pl.multiple_of(cb * bsz, bsz)
        s_scr[pl.ds(rf, bsz), 0:p2] = sf
        s_scr[pl.ds(rb, bsz), p2:2 * p2] = sb
        zf = z_scr[pl.ds(rf, bsz), 0:2 * p2]
        zb = z_scr[pl.ds(rb, bsz), 2 * p2:4 * p2]
        return (sf * a1f + tf * a2f + zf[:, :p2], tf * a1f - sf * a2f + zf[:, p2:],
                sb * a1b + tb * a2b + zb[:, :p2], tb * a1b - sb * a2b + zb[:, p2:])

    zero = jnp.zeros((bsz, p2), jnp.float32)
    lax.fori_loop(0, n_all, step, (zero, zero, zero, zero), unroll=4)
    r0 = n_ctx * bsz
    y_ref[0] = (jnp.dot(xb[r0:], toep_ref[0], preferred_element_type=jnp.float32)
                + jnp.dot(s_scr[r0:, :].astype(jnp.bfloat16), m_ref[0], preferred_element_type=jnp.float32))


def _s5_mix(u_c, u_l, lam_re, lam_im, log_dt, b_re, b_im, c_re, c_im):
    bsz, l_ctx, gh = u_c.shape
    l_lat = u_l.shape[1]
    g, t_len = gh // S5_H, S5_T
    n_ctx, n_lat = l_ctx // t_len, l_lat // t_len
    w, dec, m, toep = _s5_weights(lam_re, lam_im, log_dt, b_re, b_im, c_re, c_im)

    def chunked(u, n):
        return jnp.transpose(u.reshape(bsz, n, t_len, g, S5_H), (3, 1, 0, 2, 4)).reshape(g, n * bsz, t_len * S5_H)

    x = jnp.concatenate([chunked(u_c, n_ctx), chunked(u_l, n_lat)], axis=1)
    rows, k = (n_ctx + n_lat) * bsz, t_len * S5_H
    y = pl.pallas_call(
        partial(_s5_body, bsz=bsz, n_ctx=n_ctx, n_lat=n_lat),
        grid=(g,),
        in_specs=[pl.BlockSpec((1, rows, k), lambda i: (i, 0, 0)),
                  pl.BlockSpec((1, k, 8 * S5_P), lambda i: (i, 0, 0)),
                  pl.BlockSpec((1, 1, 8 * S5_P), lambda i: (i, 0, 0)),
                  pl.BlockSpec((1, 4 * S5_P, k), lambda i: (i, 0, 0)),
                  pl.BlockSpec((1, k, k), lambda i: (i, 0, 0))],
        out_specs=pl.BlockSpec((1, n_lat * bsz, k), lambda i: (i, 0, 0)),
        out_shape=jax.ShapeDtypeStruct((g, n_lat * bsz, k), jnp.float32),
        scratch_shapes=[pltpu.VMEM((rows, 8 * S5_P), jnp.float32), pltpu.VMEM((rows, 4 * S5_P), jnp.float32)],
        compiler_params=pltpu.CompilerParams(dimension_semantics=("parallel",), vmem_limit_bytes=_VMEM_LIMIT),
        name="s5_mix",
    )(x, w.astype(jnp.bfloat16), dec, m.astype(jnp.bfloat16), toep.astype(jnp.bfloat16))
    return jnp.transpose(y.reshape(g, n_lat, bsz, t_len, S5_H), (2, 1, 3, 0, 4)).reshape(bsz, l_lat, gh)


def _glu_body(y_ref, u_ref, d_ref, w_ref, b_ref, o_ref):
    y = _gelu_tanh(y_ref[0] + u_ref[0] * d_ref[...])
    z = jnp.dot(y.astype(jnp.bfloat16), w_ref[...], preferred_element_type=jnp.float32) + b_ref[...]
    o_ref[0] = y * (1.0 / (1.0 + jnp.exp(-z)))


def _s5_out(y, u_all, l_ctx, col0, d_skip, glu_w, glu_b):
    bsz, l, c = y.shape
    tm = l_ctx
    vec = pl.BlockSpec((1, c), lambda b, i: (0, 0))
    return pl.pallas_call(
        _glu_body,
        grid=(bsz, l // tm),
        in_specs=[pl.BlockSpec((1, tm, c), lambda b, i: (b, i, 0)),
                  pl.BlockSpec((1, tm, c), lambda b, i: (b, i + 1, col0 // c)), vec,
                  pl.BlockSpec((c, c), lambda b, i: (0, 0)), vec],
        out_specs=pl.BlockSpec((1, tm, c), lambda b, i: (b, i, 0)),
        out_shape=jax.ShapeDtypeStruct(y.shape, jnp.float32),
        compiler_params=pltpu.CompilerParams(dimension_semantics=("parallel", "parallel")),
        name="s5_glu",
    )(y, u_all, d_skip.reshape(1, c), glu_w.astype(jnp.bfloat16), glu_b.reshape(1, c))


def _odd_mixer(u, l_ctx, lb, hg_norm_g, lam_re, lam_im, log_dt, b_re, b_im, c_re, c_im, d_skip, glu_w, glu_b):
    w = W_HALF
    hg_l = _hgrn_mix(u, lb, hg_norm_g, l_ctx)
    s_c, s_l = u[:, :l_ctx, 3 * w:4 * w], u[:, l_ctx:, 3 * w:4 * w]
    y_l = _s5_mix(s_c, s_l, lam_re, lam_im, log_dt, b_re, b_im, c_re, c_im)
    return hg_l, _s5_out(y_l, u, l_ctx, 3 * w, d_skip, glu_w, glu_b)


def _router_body(x_ref, g_ref, sh_ref, sc_ref, wr_ref, h_ref, lg_ref):
    x = x_ref[0]
    h = x * lax.rsqrt(jnp.mean(x * x, axis=-1, keepdims=True) + EPS) * g_ref[...]
    h = h * (1.0 + sc_ref[0]) + sh_ref[0]
    hb = h.astype(jnp.bfloat16)
    lg_ref[0] = jnp.dot(hb, wr_ref[...], preferred_element_type=jnp.float32)
    bits = pltpu.bitcast(hb.astype(jnp.float32), jnp.uint32)
    half = bits.shape[1] // 2
    h_ref[0] = (bits[:, :half] >> 16) | (bits[:, half:] & jnp.uint32(0xFFFF0000))


def _router(x, norm_g, shift, scale, w_router):
    bsz, n, d = x.shape
    tm = min(n, 512)
    vec = pl.BlockSpec((1, 1, d), lambda b, i: (b, 0, 0))
    return pl.pallas_call(
        _router_body,
        grid=(bsz, n // tm),
        in_specs=[pl.BlockSpec((1, tm, d), lambda b, i: (b, i, 0)),
                  pl.BlockSpec((1, d), lambda b, i: (0, 0)), vec, vec,
                  pl.BlockSpec((d, N_EXPERTS), lambda b, i: (0, 0))],
        out_specs=[pl.BlockSpec((1, tm, d // 2), lambda b, i: (b, i, 0)),
                   pl.BlockSpec((1, tm, N_EXPERTS), lambda b, i: (b, i, 0))],
        out_shape=[jax.ShapeDtypeStruct((bsz, n, d // 2), jnp.uint32),
                   jax.ShapeDtypeStruct((bsz, n, N_EXPERTS), jnp.float32)],
        compiler_params=pltpu.CompilerParams(dimension_semantics=("parallel", "parallel"),
                                             vmem_limit_bytes=_VMEM_LIMIT),
        name="moe_router",
    )(x, norm_g.reshape(1, d), jnp.broadcast_to(shift, (bsz, 1, d)), jnp.broadcast_to(scale, (bsz, 1, d)),
      w_router.astype(jnp.bfloat16))


def _expert_body(idx_ref, nxt_ref, h_ref, gate_ref, mod_ref, wg_ref, wu_ref, wd_ref, y_ref, x_cur, x_nxt,
                 *, cap, grp, n_f):
    rows = grp * cap

    @pl.when(pl.program_id(1) == 0)
    def _():
        def grab(g, carry):
            base = pl.multiple_of(g * _SUBLANES, _SUBLANES)
            for r in range(_SUBLANES):
                x_nxt[pl.ds(base + r, 1), :] = h_ref[base // cap, pl.ds(idx_ref[0, 0, base + r], 1), :]
            return carry

        lax.fori_loop(0, rows // _SUBLANES, grab, 0)

    x_cur[...] = x_nxt[...]
    w32 = x_cur[...]
    xb = jnp.concatenate([pltpu.bitcast(w32 << 16, jnp.float32),
                          pltpu.bitcast(w32 & jnp.uint32(0xFFFF0000), jnp.float32)], axis=1).astype(jnp.bfloat16)
    for r in range(rows):
        x_nxt[pl.ds(r, 1), :] = h_ref[r // cap, pl.ds(nxt_ref[0, 0, r], 1), :]
    tf = wg_ref.shape[3] // n_f
    acc = None
    for f in range(n_f):
        cs = slice(f * tf, (f + 1) * tf)
        a = jnp.dot(xb, wg_ref[0, 0, :, cs], preferred_element_type=jnp.float32)
        u = jnp.dot(xb, wu_ref[0, 0, :, cs], preferred_element_type=jnp.float32)
        hh = (a * (1.0 / (1.0 + jnp.exp(-a))) * u).astype(jnp.bfloat16)
        part = jnp.dot(hh, wd_ref[0, 0, cs, :], preferred_element_type=jnp.float32)
        acc = part if acc is None else acc + part
    acc = acc * gate_ref[0]
    for g in range(grp):
        y_ref[g, 0] = acc[g * cap:(g + 1) * cap] * mod_ref[g]


def _expert_ffn(h32, idx, gate, mod, layer, w_gate, w_up, w_down):
    bsz, n, dh = h32.shape
    d = 2 * dh
    cap = idx.shape[-1]
    _, e, _, ff = w_gate.shape
    grp = max(1, min(bsz, _EXPERT_ROWS // cap))
    ng = bsz // grp
    rows = grp * cap
    regroup = lambda t: t.reshape(ng, grp, e, cap).transpose(0, 2, 1, 3).reshape(ng * e, rows)
    idx3 = regroup(idx)[:, None, :]
    ge = lambda b, k: (b * e + k, 0, 0)
    ge_next = lambda b, k: (b * e + jnp.minimum(k + 1, e - 1), 0, 0)
    wspec = lambda s: pl.BlockSpec((1, 1) + s, lambda b, k: (layer, k, 0, 0))
    y = pl.pallas_call(
        partial(_expert_body, cap=cap, grp=grp, n_f=3),
        grid=(ng, e),
        in_specs=[pl.BlockSpec((1, 1, rows), ge, memory_space=pltpu.SMEM),
                  pl.BlockSpec((1, 1, rows), ge_next, memory_space=pltpu.SMEM),
                  pl.BlockSpec((grp, n, dh), lambda b, k: (b, 0, 0)),
                  pl.BlockSpec((1, rows, 1), ge),
                  pl.BlockSpec((grp, 1, d), lambda b, k: (b, 0, 0)),
                  wspec((d, ff)), wspec((d, ff)), wspec((ff, d))],
        out_specs=pl.BlockSpec((grp, 1, cap, d), lambda b, k: (b, k, 0, 0)),
        out_shape=jax.ShapeDtypeStruct((bsz, e, cap, d), jnp.float32),
        scratch_shapes=[pltpu.VMEM((rows, dh), jnp.uint32), pltpu.VMEM((rows, dh), jnp.uint32)],
        compiler_params=pltpu.CompilerParams(dimension_semantics=("parallel", "arbitrary"),
                                             vmem_limit_bytes=_VMEM_LIMIT_BIG),
        name="moe_expert",
    )(idx3, idx3, h32, regroup(gate)[:, :, None], jnp.broadcast_to(mod, (bsz, 1, d)), w_gate, w_up, w_down)
    return y.reshape(bsz * e, cap, d)


def _combine_body(idx_ref, y_ref, o_ref, *, cap):
    @pl.when(pl.program_id(1) == 0)
    def _():
        o_ref[...] = jnp.zeros_like(o_ref)

    def group(g, carry):
        base = pl.multiple_of(g * _SUBLANES, _SUBLANES)
        ys = y_ref[0, pl.ds(base, _SUBLANES), :]
        ts = [idx_ref[0, 0, base + r] for r in range(_SUBLANES)]
        cur = [o_ref[0, pl.ds(t, 1), :] for t in ts]
        for r in range(_SUBLANES):
            o_ref[0, pl.ds(ts[r], 1), :] = cur[r] + ys[r:r + 1, :]
        return carry

    lax.fori_loop(0, cap // _SUBLANES, group, 0)


def _combine(y, idx, n):
    be, cap, d = y.shape
    bsz = be // N_EXPERTS
    return pl.pallas_call(
        partial(_combine_body, cap=cap),
        grid=(bsz, N_EXPERTS),
        in_specs=[pl.BlockSpec((1, 1, cap), lambda b, k: (b * N_EXPERTS + k, 0, 0), memory_space=pltpu.SMEM),
                  pl.BlockSpec((1, cap, d), lambda b, k: (b * N_EXPERTS + k, 0, 0))],
        out_specs=pl.BlockSpec((1, n, d), lambda b, k: (b, 0, 0)),
        out_shape=jax.ShapeDtypeStruct((bsz, n, d), jnp.float32),
        compiler_params=pltpu.CompilerParams(dimension_semantics=("parallel", "arbitrary"),
                                             vmem_limit_bytes=_VMEM_LIMIT),
        name="moe_combine",
    )(idx.reshape(be, 1, cap), y)


def _moe(x, norm_g, shift, scale, mod, w_router, layer, w_gate, w_up, w_down):
    bsz, n, d = x.shape
    cap = max(1, (EC_CAPACITY * n) // N_EXPERTS)
    h32, logits = _router(x, norm_g, shift, scale, w_router)
    aff = jax.nn.softmax(logits, axis=-1)
    gate, idx = lax.top_k(jnp.swapaxes(aff, 1, 2), cap)
    y = _expert_ffn(h32, idx, gate, mod, layer, w_gate, w_up, w_down)
    return _combine(y, idx, n)


def kernel(x, c, ctx, c_ctx, mod_w, mod_b, norm1_g, norm2_g, out_w, even_in_w, lru_conv_w, lru_conv_b, lru_wa, lru_ba, lru_wx, lru_bx, lru_lam, hy_conv_w, hy_conv_b, hy_w1, hy_b1, hy_w2, hy_b2, hy_w3, hy_freq, hy_skip, odd_in_w, hg_lb, hg_norm_g, s5_lam_re, s5_lam_im, s5_log_dt, s5_b_re, s5_b_im, s5_c_re, s5_c_im, s5_d, s5_glu_w, s5_glu_b, router_w, ex_w_gate, ex_w_up, ex_w_down, final_g):
    rows = x.shape[1] // GRID_W
    l_ctx = ctx.shape[1]
    p = jax.nn.softmax(hg_lb, axis=0)
    lb_all = jnp.clip(jnp.cumsum(p, axis=0) - p[0], 0.0, 1.0 - 1e-4)
    ex_wg, ex_wu, ex_wd = (t.astype(jnp.bfloat16) for t in (ex_w_gate, ex_w_up, ex_w_down))
    sc_lat = jax.nn.silu(c)
    sc_ctx = jax.nn.silu(c_ctx)[None]
    for i in range(DEPTH):
        need_ctx = i < DEPTH - 1
        j = i // 2
        ml = [t[:, None, :] for t in jnp.split(sc_lat @ mod_w[i] + mod_b[i], 6, axis=-1)]
        mc = [t[:, None, :] for t in jnp.split(sc_ctx @ mod_w[i] + mod_b[i], 6, axis=-1)]
        shift, scale, gate = (_two_part(mc[k], ml[k]) for k in range(3))
        if i % 2 == 0:
            x_all = jnp.concatenate([ctx, x], axis=1)
            u = _norm_mm(x_all, norm1_g[i], shift, scale, even_in_w[j], l_ctx)
            h_f, h_b, yb = _even_mixer(u, l_ctx, lru_conv_w[j], lru_conv_b[j], lru_wa[j], lru_ba[j], lru_wx[j],
                                       lru_bx[j], lru_lam[j], hy_conv_w[j], hy_conv_b[j], hy_w1[j], hy_b1[j],
                                       hy_w2[j], hy_b2[j], hy_w3[j], hy_freq[j], hy_skip[j])
            x_all = _proj_res_lru(h_f, h_b, u, yb, out_w[i], x_all, gate, l_ctx)
            ctx, x = x_all[:, :l_ctx], x_all[:, l_ctx:]
        else:
            assert not need_ctx
            x_all = jnp.concatenate([ctx, _to_col_major(x, rows)], axis=1)
            u = _norm_mm(x_all, norm1_g[i], shift, scale, odd_in_w[j], l_ctx)
            hg_l, s5_l = _odd_mixer(u, l_ctx, lb_all[i], hg_norm_g[j], s5_lam_re[j], s5_lam_im[j],
                                    s5_log_dt[j], s5_b_re[j], s5_b_im[j], s5_c_re[j], s5_c_im[j], s5_d[j],
                                    s5_glu_w[j], s5_glu_b[j])
            x = _to_row_major(_proj_res(hg_l, s5_l, out_w[i], x_all, gate, l_ctx, l_ctx), rows)
        ex = (router_w[i], i, ex_wg, ex_wu, ex_wd)
        moe_l = _moe(x, norm2_g[i], ml[3], ml[4], ml[5], *ex)
        if need_ctx:
            x = x + moe_l
            ctx = ctx + _moe(ctx, norm2_g[i], mc[3], mc[4], mc[5], *ex)
    return _add_norm(x, moe_l, final_g)
```

```python
import math
from functools import partial

import numpy as np
import jax
import jax.numpy as jnp
from jax import lax
from jax.experimental import pallas as pl
from jax.experimental.pallas import tpu as pltpu

D_MODEL = 1024
DEPTH = 2
GRID_W = 64
W_HALF = D_MODEL // 2
EPS = 1e-6
LRU_HEADS = 8
LRU_HEAD_DIM = W_HALF // LRU_HEADS
LRU_C = 8.0
HY_ORDER = 2
HY_BANDS = 16
HY_MAX_DECAY = math.log(1e-2) / 0.3
HY_MIN_DECAY = math.log(1e-2) / 1.5
HG_HEADS = 4
HG_DK = W_HALF // HG_HEADS
HG_C = 16
HG_SUB = 4
HG_TB = 256
S5_H = 16
S5_P = 64
S5_MAX_RE = -1e-4
S5_T = 16
_DFT_LOW = 64
N_EXPERTS = 16
EC_CAPACITY = 2

_SUBLANES = 8
_DW_ROWS = 256
_EXPERT_ROWS = 512
_VMEM_LIMIT = 48 * 1024 * 1024
_VMEM_LIMIT_BIG = 56 * 1024 * 1024
_HI = lax.Precision.HIGHEST
_NEG = -1e30


def _mm_body(a_ref, b_ref, o_ref):
    o_ref[...] = jnp.dot(a_ref[...].astype(jnp.bfloat16), b_ref[...],
                         preferred_element_type=jnp.float32)


def _pick(n, pref):
    for t in pref:
        if n % t == 0:
            return t
    return n


def _mm(a, b):
    m, k = a.shape
    n = b.shape[1]
    tm = _pick(m, (512, 256, 128, 64, 32, 16, 8))
    tn = _pick(n, (512, 256, 128))
    return pl.pallas_call(
        _mm_body,
        grid=(m // tm, n // tn),
        in_specs=[pl.BlockSpec((tm, k), lambda i, j: (i, 0)),
                  pl.BlockSpec((k, tn), lambda i, j: (0, j))],
        out_specs=pl.BlockSpec((tm, tn), lambda i, j: (i, j)),
        out_shape=jax.ShapeDtypeStruct((m, n), jnp.float32),
        compiler_params=pltpu.CompilerParams(
            dimension_semantics=("parallel", "parallel"), vmem_limit_bytes=_VMEM_LIMIT),
        name="mm",
    )(a, b.astype(jnp.bfloat16))


def _mm3(a, b):
    bsz, l, k = a.shape
    return _mm(a.reshape(bsz * l, k), b).reshape(bsz, l, b.shape[1])


def _add_norm_body(x_ref, m_ref, g_ref, o_ref):
    x = x_ref[...] + m_ref[...]
    o_ref[...] = x * lax.rsqrt(jnp.mean(x * x, axis=-1, keepdims=True) + EPS) * g_ref[...]


def _add_norm(x, m, g):
    bsz, n, d = x.shape
    tm = min(n, 512)
    blk = pl.BlockSpec((1, tm, d), lambda b, i: (b, i, 0))
    return pl.pallas_call(
        _add_norm_body,
        grid=(bsz, n // tm),
        in_specs=[blk, blk, pl.BlockSpec((1, 1, d), lambda b, i: (0, 0, 0))],
        out_specs=blk,
        out_shape=jax.ShapeDtypeStruct(x.shape, jnp.float32),
        compiler_params=pltpu.CompilerParams(dimension_semantics=("parallel", "parallel")),
        name="add_norm",
    )(x, m, g.reshape(1, 1, d))


def _two_part(t_ctx, t_lat):
    bsz, _, d = t_lat.shape
    return jnp.stack([jnp.broadcast_to(t_ctx, (bsz, 1, d)), t_lat], axis=1).reshape(2 * bsz, 1, d)


def _norm_mm_body(x_ref, g_ref, sh_ref, sc_ref, w_ref, o_ref):
    x = x_ref[0]
    h = x * lax.rsqrt(jnp.mean(x * x, axis=-1, keepdims=True) + EPS) * g_ref[...]
    h = h * (1.0 + sc_ref[0]) + sh_ref[0]
    o_ref[0] = jnp.dot(h.astype(jnp.bfloat16), w_ref[...], preferred_element_type=jnp.float32)


def _norm_mm(x_all, norm_g, shift, scale, w, l_ctx):
    bsz, l_tot, d = x_all.shape
    n = w.shape[1]
    tm = l_ctx
    vec = pl.BlockSpec((1, 1, d), lambda b, j: (2 * b + jnp.minimum(j, 1), 0, 0))
    return pl.pallas_call(
        _norm_mm_body,
        grid=(bsz, l_tot // tm),
        in_specs=[pl.BlockSpec((1, tm, d), lambda b, j: (b, j, 0)),
                  pl.BlockSpec((1, d), lambda b, j: (0, 0)), vec, vec,
                  pl.BlockSpec((d, n), lambda b, j: (0, 0))],
        out_specs=pl.BlockSpec((1, tm, n), lambda b, j: (b, j, 0)),
        out_shape=jax.ShapeDtypeStruct((bsz, l_tot, n), jnp.float32),
        compiler_params=pltpu.CompilerParams(dimension_semantics=("parallel", "parallel"),
                                             vmem_limit_bytes=_VMEM_LIMIT),
        name="norm_mm",
    )(x_all, norm_g.reshape(1, d), shift, scale, w.astype(jnp.bfloat16))


def _proj_res_body(ya_ref, yb_ref, wa_ref, wb_ref, x_ref, gt_ref, o_ref):
    y = (jnp.dot(ya_ref[0].astype(jnp.bfloat16), wa_ref[...], preferred_element_type=jnp.float32)
         + jnp.dot(yb_ref[0].astype(jnp.bfloat16), wb_ref[...], preferred_element_type=jnp.float32))
    o_ref[0] = x_ref[0] + gt_ref[0] * y


def _proj_res_lru_body(hf_ref, hb_ref, u_ref, yb_ref, wa_ref, wb_ref, x_ref, gt_ref, o_ref):
    ya = (hf_ref[0] + hb_ref[0]) * _gelu_tanh(u_ref[0])
    y = (jnp.dot(ya.astype(jnp.bfloat16), wa_ref[...], preferred_element_type=jnp.float32)
         + jnp.dot(yb_ref[0].astype(jnp.bfloat16), wb_ref[...], preferred_element_type=jnp.float32))
    o_ref[0] = x_ref[0] + gt_ref[0] * y


def _proj_res_lru(h_f, h_b, u_all, yb, w, x_all, gate, l_ctx):
    bsz, l_tot, ka = h_f.shape
    d = w.shape[1]
    tm = l_ctx
    blk = lambda k, c=0: pl.BlockSpec((1, tm, k), lambda b, j: (b, j, c))
    return pl.pallas_call(
        _proj_res_lru_body,
        grid=(bsz, l_tot // tm),
        in_specs=[blk(ka), blk(ka), blk(ka, 1), blk(yb.shape[2]),
                  pl.BlockSpec((ka, d), lambda b, j: (0, 0)), pl.BlockSpec((yb.shape[2], d), lambda b, j: (0, 0)),
                  blk(d), pl.BlockSpec((1, 1, d), lambda b, j: (2 * b + jnp.minimum(j, 1), 0, 0))],
        out_specs=blk(d),
        out_shape=jax.ShapeDtypeStruct((bsz, l_tot, d), jnp.float32),
        compiler_params=pltpu.CompilerParams(dimension_semantics=("parallel", "parallel"),
                                             vmem_limit_bytes=_VMEM_LIMIT),
        name="proj_res_lru",
    )(h_f, h_b, u_all, yb, w[:ka].astype(jnp.bfloat16), w[ka:].astype(jnp.bfloat16), x_all, gate)


def _proj_res(ya, yb, w, x_all, gate, l_ctx, row0):
    bsz, l_y, ka = ya.shape
    d = w.shape[1]
    tm = l_ctx
    j0 = row0 // tm
    yblk = lambda k: pl.BlockSpec((1, tm, k), lambda b, j: (b, j, 0))
    return pl.pallas_call(
        _proj_res_body,
        grid=(bsz, l_y // tm),
        in_specs=[yblk(ka), yblk(yb.shape[2]),
                  pl.BlockSpec((ka, d), lambda b, j: (0, 0)), pl.BlockSpec((yb.shape[2], d), lambda b, j: (0, 0)),
                  pl.BlockSpec((1, tm, d), lambda b, j: (b, j + j0, 0)),
                  pl.BlockSpec((1, 1, d), lambda b, j: (2 * b + jnp.minimum(j + j0, 1), 0, 0))],
        out_specs=pl.BlockSpec((1, tm, d), lambda b, j: (b, j, 0)),
        out_shape=jax.ShapeDtypeStruct((bsz, l_y, d), jnp.float32),
        compiler_params=pltpu.CompilerParams(dimension_semantics=("parallel", "parallel"),
                                             vmem_limit_bytes=_VMEM_LIMIT),
        name="proj_res",
    )(ya, yb, w[:ka].astype(jnp.bfloat16), w[ka:].astype(jnp.bfloat16), x_all, gate)


def _gelu_tanh(x):
    return 0.5 * x * (1.0 + jnp.tanh(math.sqrt(2.0 / math.pi) * (x + 0.044715 * (x * x * x))))


def _dwconv_body(u_ref, w_ref, o_ref, *rest, l_ctx, taps, left, split):
    l_tot = u_ref.shape[1]
    tb, h = _DW_ROWS, _SUBLANES
    row = lax.broadcasted_iota(jnp.int32, (tb, 1), 0)

    def block(j, carry):
        r0 = pl.multiple_of(j * tb, tb)
        lo = jnp.where(r0 < l_ctx, 0, l_ctx)
        hi = jnp.where(r0 < l_ctx, l_ctx, l_tot)
        prev = u_ref[0, pl.ds(pl.multiple_of(jnp.maximum(r0 - h, 0), h), h), :]
        nxt = u_ref[0, pl.ds(pl.multiple_of(jnp.minimum(r0 + tb, l_tot - h), h), h), :]
        ext = jnp.concatenate([prev, u_ref[0, pl.ds(r0, tb), :], nxt], axis=0)
        y = jnp.broadcast_to(w_ref[0, taps:taps + 1, :], (tb, ext.shape[1]))
        for t in range(taps):
            off = t - left
            src = r0 + row + off
            y = y + jnp.where((src >= lo) & (src < hi), ext[h + off:h + off + tb, :], 0.0) * w_ref[0, t:t + 1, :]
        if not split:
            o_ref[0, pl.ds(r0, tb), :] = y
        else:
            ob_c, o_l, ob_l = rest

            @pl.when(r0 < l_ctx)
            def _():
                o_ref[0, pl.ds(r0, tb), :] = y
                ob_c[0, pl.ds(r0, tb), :] = y.astype(jnp.bfloat16)

            @pl.when(r0 >= l_ctx)
            def _():
                o_l[0, pl.ds(r0 - l_ctx, tb), :] = y
                ob_l[0, pl.ds(r0 - l_ctx, tb), :] = y.astype(jnp.bfloat16)
        return carry

    lax.fori_loop(0, l_tot // tb, block, 0)


def _dwconv_slabs(u_all, col0, w, b, l_ctx, split):
    bsz, l_tot, _ = u_all.shape
    taps, nc = w.shape
    c = W_HALF
    n = nc // c
    wb = jnp.concatenate([w, b[None], jnp.zeros((_SUBLANES - taps - 1, nc), jnp.float32)], axis=0)
    wb = wb.reshape(_SUBLANES, n, c).transpose(1, 0, 2)
    if split:
        rows = (l_ctx, l_ctx, l_tot - l_ctx, l_tot - l_ctx)
        dts = (jnp.float32, jnp.bfloat16, jnp.float32, jnp.bfloat16)
    else:
        rows, dts = (l_tot,), (jnp.float32,)
    out = pl.pallas_call(
        partial(_dwconv_body, l_ctx=l_ctx, taps=taps, left=(taps - 1) // 2, split=split),
        grid=(bsz, n),
        in_specs=[pl.BlockSpec((1, l_tot, c), lambda bb, s: (bb, 0, s + col0 // c)),
                  pl.BlockSpec((1, _SUBLANES, c), lambda bb, s: (s, 0, 0))],
        out_specs=[pl.BlockSpec((1, r, c), lambda bb, s: (bb, 0, s)) for r in rows],
        out_shape=[jax.ShapeDtypeStruct((bsz, r, nc), dt) for r, dt in zip(rows, dts)],
        compiler_params=pltpu.CompilerParams(dimension_semantics=("parallel", "parallel"),
                                             vmem_limit_bytes=_VMEM_LIMIT),
        name="dwconv",
    )(u_all, wb)
    return out if split else out[0]


def _to_col_major(t, rows):
    bsz, l, ch = t.shape
    return t.reshape(bsz, rows, GRID_W, ch).transpose(0, 2, 1, 3).reshape(bsz, l, ch)


def _to_row_major(t, rows):
    bsz, l, ch = t.shape
    return t.reshape(bsz, GRID_W, rows, ch).transpose(0, 2, 1, 3).reshape(bsz, l, ch)


def _lru_body(xf_ref, xb_ref, wa_ref, wx_ref, par_ref, hf_ref, hb_ref,
              af_scr, bf_scr, ab_scr, bb_scr, sf_scr, sb_scr, *, tb):
    @pl.when(pl.program_id(1) == 0)
    def _():
        sf_scr[...] = jnp.zeros_like(sf_scr)
        sb_scr[...] = jnp.zeros_like(sb_scr)

    def coeffs(x, d, a_scr, b_scr):
        xb = x.astype(jnp.bfloat16)
        r = jnp.dot(xb, wa_ref[d], preferred_element_type=jnp.float32) + par_ref[d, 0:1, :]
        i = jnp.dot(xb, wx_ref[d], preferred_element_type=jnp.float32) + par_ref[d, 1:2, :]
        r = 1.0 / (1.0 + jnp.exp(-r))
        i = 1.0 / (1.0 + jnp.exp(-i))
        log_a = -LRU_C * r * par_ref[d, 2:3, :]
        a = jnp.exp(log_a)
        a_scr[...] = a
        b_scr[...] = jnp.sqrt(-jnp.tanh(log_a) * (a * a + 1.0)) * i * x

    coeffs(xf_ref[0], 0, af_scr, bf_scr)
    coeffs(xb_ref[0], 1, ab_scr, bb_scr)

    def tile(i, carry):
        hf, hb = carry
        base_f = pl.multiple_of(i * _SUBLANES, _SUBLANES)
        base_b = pl.multiple_of(tb - _SUBLANES - i * _SUBLANES, _SUBLANES)
        for r in range(_SUBLANES):
            rf = pl.ds(base_f + r, 1)
            hf = af_scr[rf, :] * hf + bf_scr[rf, :]
            hf_ref[0, rf, :] = hf
            rb = pl.ds(base_b + (_SUBLANES - 1 - r), 1)
            hb = ab_scr[rb, :] * hb + bb_scr[rb, :]
            hb_ref[0, rb, :] = hb
        return hf, hb

    hf, hb = lax.fori_loop(0, tb // _SUBLANES, tile, (sf_scr[...], sb_scr[...]))
    sf_scr[...] = hf
    sb_scr[...] = hb


def _block_diag(w):
    two, h, n, _ = w.shape
    eye = jnp.eye(h, dtype=w.dtype)
    return (w[:, :, :, None, :] * eye[None, :, None, :, None]).reshape(two, h * n, h * n)


def _lru_scan(xc, l_ctx, wa, ba, wx, bx, lam):
    bsz, l_tot, ch = xc.shape
    tb = l_ctx
    nb = l_tot // tb
    par = jnp.stack([ba.reshape(2, ch), bx.reshape(2, ch), jax.nn.softplus(-lam).reshape(2, ch)], axis=1)
    par = jnp.concatenate([par, jnp.zeros((2, _SUBLANES - 3, ch), jnp.float32)], axis=1)
    fwd = pl.BlockSpec((1, tb, ch), lambda b, j: (b, j, 0))
    bwd = pl.BlockSpec((1, tb, ch), lambda b, j: (b, jnp.where(j == 0, 0, nb - j), 0))
    wspec = pl.BlockSpec((2, ch, ch), lambda b, j: (0, 0, 0))
    return pl.pallas_call(
        partial(_lru_body, tb=tb),
        grid=(bsz, nb),
        in_specs=[fwd, bwd, wspec, wspec, pl.BlockSpec((2, _SUBLANES, ch), lambda b, j: (0, 0, 0))],
        out_specs=[fwd, bwd],
        out_shape=[jax.ShapeDtypeStruct(xc.shape, jnp.float32)] * 2,
        scratch_shapes=[pltpu.VMEM((tb, ch), jnp.float32)] * 4 + [pltpu.VMEM((1, ch), jnp.float32)] * 2,
        compiler_params=pltpu.CompilerParams(dimension_semantics=("parallel", "arbitrary")),
        name="lru_scan",
    )(xc, xc, _block_diag(wa).astype(jnp.bfloat16), _block_diag(wx).astype(jnp.bfloat16), par)


def _hyena_filters(l, w1, b1, w2, b2, w3, freq):
    t = jnp.arange(l, dtype=jnp.float32)
    t_unit = t / max(l - 1, 1)
    bands = jnp.linspace(1e-4, HY_BANDS - 1, HY_BANDS, dtype=jnp.float32)
    ang = (2.0 * math.pi / l) * t[:, None] * bands[None, :]
    z = jnp.concatenate([t_unit[:, None], jnp.cos(ang), -jnp.sin(ang)], axis=-1)
    h = jnp.sin(freq * (z @ w1 + b1))
    h = jnp.sin(freq * (h @ w2 + b2))
    h = h @ w3
    dist = jnp.abs(t - (l // 2)) / (l / 2.0)
    deltas = jnp.abs(jnp.linspace(HY_MIN_DECAY, HY_MAX_DECAY, HY_ORDER * W_HALF, dtype=jnp.float32))
    h = h * jnp.exp(-dist[:, None] * deltas[None, :])
    h = h * lax.rsqrt(jnp.sum(h * h, axis=0, keepdims=True) + EPS)
    return h


def _dft_mats(l, tf):
    n = 2 * l
    nl = _DFT_LOW
    nh = l // nl
    t = jnp.arange(l, dtype=jnp.int32)
    w = 2.0 * math.pi / n

    def tables(tv):
        ah = (((nl * jnp.arange(nh, dtype=jnp.int32))[:, None] * tv[None, :]) % n).astype(jnp.float32) * w
        al = ((jnp.arange(nl, dtype=jnp.int32)[:, None] * tv[None, :]) % n).astype(jnp.float32) * w
        return jnp.cos(ah), jnp.sin(ah), jnp.cos(al), jnp.sin(al)

    ch, sh, cl, sl = tables(t)
    f_re = (ch[:, None, :] * cl[None] - sh[:, None, :] * sl[None]).reshape(l, l)
    f_im = -(sh[:, None, :] * cl[None] + ch[:, None, :] * sl[None]).reshape(l, l)
    f_im = f_im.at[0].set(jnp.where(t % 2 == 0, 1.0, -1.0))
    f = jnp.concatenate([f_re.reshape(l // tf, tf, l), f_im.reshape(l // tf, tf, l)], axis=1).reshape(n, l)
    tp = t + l // 2
    ch, sh, cl, sl = (z.T for z in tables(tp))
    g_re = (2.0 / n) * (ch[:, :, None] * cl[:, None, :] - sh[:, :, None] * sl[:, None, :]).reshape(l, l)
    g_im = (-2.0 / n) * (sh[:, :, None] * cl[:, None, :] + ch[:, :, None] * sl[:, None, :]).reshape(l, l)
    g_re = g_re.at[:, 0].set(1.0 / n)
    g_im = g_im.at[:, 0].set(jnp.where(tp % 2 == 0, 1.0, -1.0) / n)
    g = jnp.stack([g_re.reshape(l, l // tf, tf), g_im.reshape(l, l // tf, tf)], axis=2).reshape(l, n)
    return f.astype(jnp.bfloat16), g.astype(jnp.bfloat16)


def _spec_body(f_ref, u_ref, a_ref, b_ref, d_ref, y_ref, *, tf):
    acc = jnp.dot(f_ref[...], u_ref[0], preferred_element_type=jnp.float32)
    xr, xi = acc[:tf], acc[tf:]
    y_ref[0, :tf, :] = (xr * a_ref[...] - xi * b_ref[...]).astype(jnp.bfloat16)
    y_ref[0, tf:, :] = (xr * b_ref[...] + xi * d_ref[...]).astype(jnp.bfloat16)


def _spec_mul(f, u, a, b, d, tf):
    bsz, l, _ = u.shape
    c = a.shape[1]
    coef = pl.BlockSpec((tf, c), lambda i, bb: (i, 0))
    return pl.pallas_call(
        partial(_spec_body, tf=tf),
        grid=(l // tf, bsz),
        in_specs=[pl.BlockSpec((2 * tf, l), lambda i, bb: (i, 0)),
                  pl.BlockSpec((1, l, c), lambda i, bb: (bb, 0, 0)), coef, coef, coef],
        out_specs=pl.BlockSpec((1, 2 * tf, c), lambda i, bb: (bb, i, 0)),
        out_shape=jax.ShapeDtypeStruct((bsz, 2 * l, c), jnp.bfloat16),
        compiler_params=pltpu.CompilerParams(dimension_semantics=("parallel", "parallel"),
                                             vmem_limit_bytes=_VMEM_LIMIT),
        name="hy_spec",
    )(f, u, a, b, d)


def _inv_body(g_ref, y_ref, u_ref, x_ref, skip_ref, z_ref, zb_ref):
    acc = jnp.dot(g_ref[...], y_ref[0], preferred_element_type=jnp.float32)
    z = x_ref[0] * (acc + u_ref[0] * skip_ref[...])
    z_ref[0] = z
    zb_ref[0] = z.astype(jnp.bfloat16)


def _inv_gate(g, y, u, ucol, xg, xcol, skip, tm):
    bsz, n, c = y.shape
    l = n // 2
    blk = pl.BlockSpec((1, tm, c), lambda bb, i: (bb, i, 0))
    return pl.pallas_call(
        _inv_body,
        grid=(bsz, l // tm),
        in_specs=[pl.BlockSpec((tm, n), lambda bb, i: (i, 0)),
                  pl.BlockSpec((1, n, c), lambda bb, i: (bb, 0, 0)),
                  pl.BlockSpec((1, tm, c), lambda bb, i: (bb, i, ucol)),
                  pl.BlockSpec((1, tm, c), lambda bb, i: (bb, i, xcol)),
                  pl.BlockSpec((1, c), lambda bb, i: (0, 0))],
        out_specs=[blk, blk],
        out_shape=[jax.ShapeDtypeStruct((bsz, l, c), jnp.float32), jax.ShapeDtypeStruct((bsz, l, c), jnp.bfloat16)],
        compiler_params=pltpu.CompilerParams(dimension_semantics=("parallel", "parallel"),
                                             vmem_limit_bytes=_VMEM_LIMIT),
        name="hy_inv",
    )(g, y, u, xg, skip.reshape(1, c))


def _hyena_conv2(hc, hb, filt, skip):
    l, c = hc.shape[1], hc.shape[2] // 3
    tf = min(512, l // 2)
    tm = min(512, l)
    f, g = _dft_mats(l, tf)
    hs = _mm(f, filt)
    hs = hs.reshape(l // tf, 2, tf, 2 * c)
    h_re, h_im = hs[:, 0].reshape(l, 2 * c), hs[:, 1].reshape(l, 2 * c)
    slot0 = (jnp.arange(l) == 0)[:, None]
    a = h_re
    b = jnp.where(slot0, 0.0, h_im)
    d = jnp.where(slot0, h_im, h_re)
    y = _spec_mul(f, hb, a[:, :c], b[:, :c], d[:, :c], tf)
    z, zb = _inv_gate(g, y, hc, 0, hc, 1, skip[0], tm)
    y = _spec_mul(f, zb, a[:, c:], b[:, c:], d[:, c:], tf)
    z, _ = _inv_gate(g, y, z, 0, hc, 2, skip[1], tm)
    return z


def _hyena(u, l_ctx, conv_w, conv_b, w1, b1, w2, b2, w3, freq, skip):
    hc_c, hb_c, hc_l, hb_l = _dwconv_slabs(u, 2 * W_HALF, conv_w, conv_b, l_ctx, True)
    zs = []
    for hc, hb in ((hc_c, hb_c), (hc_l, hb_l)):
        filt = _hyena_filters(hc.shape[1], w1, b1, w2, b2, w3, freq)
        zs.append(_hyena_conv2(hc, hb, filt, skip))
    return jnp.concatenate(zs, axis=1)


def _even_mixer(u, l_ctx, lru_conv_w, lru_conv_b, lru_wa, lru_ba, lru_wx, lru_bx,
                lru_lam, hy_conv_w, hy_conv_b, hy_w1, hy_b1, hy_w2, hy_b2, hy_w3, hy_freq, hy_skip):
    w = W_HALF
    xc = _dwconv_slabs(u, 0, lru_conv_w, lru_conv_b, l_ctx, False)
    h_f, h_b = _lru_scan(xc, l_ctx, lru_wa, lru_ba, lru_wx, lru_bx, lru_lam)
    yb = _hyena(u, l_ctx, hy_conv_w, hy_conv_b, hy_w1, hy_b1, hy_w2, hy_b2, hy_w3, hy_freq, hy_skip)
    return h_f, h_b, yb


def _split3(x):
    hi = x.astype(jnp.bfloat16)
    r1 = x - hi.astype(jnp.float32)
    mid = r1.astype(jnp.bfloat16)
    lo = (r1 - mid.astype(jnp.float32)).astype(jnp.bfloat16)
    return jnp.concatenate([hi, mid, lo], axis=1)


def _sum3(r, c):
    return r[:, 0:c] + r[:, c:2 * c] + r[:, 2 * c:3 * c]


def _log_sigmoid(x):
    return jnp.minimum(x, 0.0) - jnp.log(1.0 + jnp.exp(-jnp.abs(x)))


def _hgrn_consts():
    tb, c, sb = HG_TB, HG_C, HG_SUB
    r = np.arange(tb)
    same = (r[:, None] // c) == (r[None, :] // c)
    same4 = (r[:, None] // sb) == (r[None, :] // sb)
    le = r[None, :] <= r[:, None]
    sub = (r % c) // sb
    cum, msk = [], []
    for d in range(2):
        tri = le if d == 0 else le.T
        cum.append(np.concatenate([same & tri, same, same4 & tri], axis=0))
        if d == 0:
            mm = [same & (sub[:, None] == i + 1) & (sub[None, :] <= i) for i in range(c // sb - 1)]
        else:
            mm = [same & (sub[:, None] == i) & (sub[None, :] > i) for i in range(c // sb - 1)]
        msk.append(np.stack(mm))
    cum = jnp.asarray(np.stack(cum).astype(np.float32), dtype=jnp.bfloat16)
    return cum, jnp.asarray(np.stack(msk).astype(np.float32))


def _hgrn_body(ff_ref, fb_ref, v_ref, q_ref, g_ref, lbp_ref, ng_ref, cum_ref, msk_ref, o_ref,
               acc_scr, st_scr, *, n_blk, l_ctx):
    tb, c, sb, dk = HG_TB, HG_C, HG_SUB, HG_DK
    n_c = tb // c
    n_sub = c // sb
    acc_scr[...] = jnp.zeros_like(acc_scr)
    st_scr[...] = jnp.zeros_like(st_scr)
    loglb, log1mlb = lbp_ref[0, 0:1, :], lbp_ref[0, 1:2, :]
    pos = lax.broadcasted_iota(jnp.int32, (tb, dk), 0) % sb
    nt = (((1,), (1,)), ((), ()))
    tn = (((0,), (0,)), ((), ()))
    dot = partial(jnp.dot, preferred_element_type=jnp.float32)

    def direction(base, f_ref, d):
        rows = pl.ds(base, tb)
        fp = f_ref[0, rows, :]
        ls = _log_sigmoid(fp)
        b2 = log1mlb + ls
        logf = jnp.maximum(loglb, b2) + jnp.log(1.0 + jnp.exp(-jnp.abs(loglb - b2)))
        lk = log1mlb + (ls - fp)
        v = v_ref[0, rows, :]
        qr = q_ref[0, rows, :]
        q = qr * (1.0 / (1.0 + jnp.exp(-qr)))
        r = dot(cum_ref[d], _split3(logf))
        cum, tot, pre = _sum3(r[0:tb], dk), _sum3(r[tb:2 * tb], dk), _sum3(r[2 * tb:3 * tb], dk)
        g = lk - cum
        cum3 = cum.reshape(n_c, c, dk)
        qe = (q * jnp.exp(cum)).astype(jnp.bfloat16)
        kt = jnp.exp(g + tot).astype(jnp.bfloat16)
        dec = jnp.exp(tot)
        vb = v.astype(jnp.bfloat16)
        yield
        o = jnp.sum(q * jnp.exp(lk), axis=-1, keepdims=True) * v
        for j in range(1, sb):
            sh = j if d == 0 else tb - j
            valid = (pos >= j) if d == 0 else (pos < sb - j)
            gj = pltpu.roll(g, sh, 0)
            vj = pltpu.roll(v, sh, 0)
            e = jnp.exp(jnp.where(valid, cum + gj, _NEG))
            o = o + jnp.sum(q * e, axis=-1, keepdims=True) * vj
        yield
        q4 = (q * jnp.exp(pre)).astype(jnp.bfloat16)
        att = None
        for i in range(n_sub - 1):
            ref_row = sb * (i + 1) - 1 if d == 0 else sb * (i + 1)
            r_i = jnp.broadcast_to(cum3[:, ref_row:ref_row + 1, :], (n_c, c, dk)).reshape(tb, dk)
            kh = jnp.exp(lk + jnp.minimum(r_i - cum, 0.0)).astype(jnp.bfloat16)
            a_i = lax.dot_general(q4, kh, nt, preferred_element_type=jnp.float32) * msk_ref[d, i]
            att = a_i if att is None else att + a_i
        o = o + dot(att.astype(jnp.bfloat16), vb)
        yield
        order = [i if d == 0 else n_c - 1 - i for i in range(n_c)]
        kvs = [lax.dot_general(vb[cc * c:(cc + 1) * c], kt[cc * c:(cc + 1) * c], tn,
                               preferred_element_type=jnp.float32) for cc in order]
        yield
        st = st_scr[d]
        sts = []
        for cc, kv in zip(order, kvs):
            sts.append(st.astype(jnp.bfloat16))
            st = st * dec[cc * c:cc * c + 1, :] + kv
        st_scr[d] = st
        o_st = [None] * n_c
        for cc, sb16 in zip(order, sts):
            o_st[cc] = lax.dot_general(qe[cc * c:(cc + 1) * c], sb16, nt, preferred_element_type=jnp.float32)
        acc_scr[rows, :] += o + jnp.concatenate(o_st, axis=0)

    def step(j, carry):
        base_f = pl.multiple_of(j * tb, tb)
        jb = jnp.where(j == 0, 0, n_blk - j)
        base_b = pl.multiple_of(jb * tb, tb)
        todo = [direction(base_f, ff_ref, 0), direction(base_b, fb_ref, 1)]
        while todo:
            todo = [gen for gen in todo if next(gen, True) is None]
        return carry

    lax.fori_loop(0, n_blk, step, 0)
    o = acc_scr[l_ctx:, :]
    g = g_ref[0, l_ctx:, :]
    o = o * lax.rsqrt(jnp.mean(o * o, axis=-1, keepdims=True) + EPS) * ng_ref[...]
    o_ref[0] = o * (g * (1.0 / (1.0 + jnp.exp(-g))))


def _hgrn_mix(u_all, lb, norm_g, l_ctx):
    bsz, l_tot, _ = u_all.shape
    nh = HG_HEADS
    n_blk = l_tot // HG_TB
    lbh = lb.reshape(nh, 1, HG_DK)
    lbp = jnp.concatenate([jnp.log(lbh), jnp.log1p(-lbh),
                           jnp.zeros((nh, _SUBLANES - 2, HG_DK), jnp.float32)], axis=1)
    cum_m, msk_m = _hgrn_consts()

    def slab(k):
        return pl.BlockSpec((1, l_tot, HG_DK), lambda b, h, k=k: (b, 0, k * nh + h))

    def const(a):
        return pl.BlockSpec(a.shape, lambda b, h: (0,) * a.ndim)

    return pl.pallas_call(
        partial(_hgrn_body, n_blk=n_blk, l_ctx=l_ctx),
        grid=(bsz, nh),
        in_specs=[slab(0), slab(1), slab(2), slab(4), slab(5),
                  pl.BlockSpec((1, _SUBLANES, HG_DK), lambda b, h: (h, 0, 0)),
                  pl.BlockSpec((1, HG_DK), lambda b, h: (0, 0)),
                  const(cum_m), const(msk_m)],
        out_specs=pl.BlockSpec((1, l_tot - l_ctx, HG_DK), lambda b, h: (b, 0, h)),
        out_shape=jax.ShapeDtypeStruct((bsz, l_tot - l_ctx, W_HALF), jnp.float32),
        scratch_shapes=[pltpu.VMEM((l_tot, HG_DK), jnp.float32), pltpu.VMEM((2, HG_DK, HG_DK), jnp.float32)],
        compiler_params=pltpu.CompilerParams(dimension_semantics=("parallel", "parallel"),
                                             vmem_limit_bytes=_VMEM_LIMIT),
        name="hgrn_mix",
    )(u_all, u_all, u_all, u_all, u_all, lbp, norm_g.reshape(1, HG_DK), cum_m, msk_m)


def _s5_weights(lam_re, lam_im, log_dt, b_re, b_im, c_re, c_im):
    t_len, f32 = S5_T, jnp.float32
    lre = jnp.minimum(lam_re, S5_MAX_RE)
    dt = jnp.exp(log_dt)[..., None]
    ar, ai = lre * dt, lam_im * dt
    k = jnp.arange(t_len + 1, dtype=f32)[:, None, None, None]
    mag = jnp.exp(k * ar)
    pre, pim = mag * jnp.cos(k * ai), mag * jnp.sin(k * ai)
    nr, ni, den = pre[1] - 1.0, pim[1], lre * lre + lam_im * lam_im
    cr, ci = (nr * lre + ni * lam_im) / den, (ni * lre - nr * lam_im) / den
    cbr = cr[..., None] * b_re - ci[..., None] * b_im
    cbi = cr[..., None] * b_im + ci[..., None] * b_re
    c_re_t, c_im_t = jnp.swapaxes(c_re, 1, 2), jnp.swapaxes(c_im, 1, 2)
    clre = c_re_t * pre[..., None] - c_im_t * pim[..., None]
    clim = c_re_t * pim[..., None] + c_im_t * pre[..., None]
    g = lam_re.shape[1]

    def w_dir(d, e):
        pr, pi = pre[e, d], pim[e, d]
        wr = pr[..., None] * cbr[d] - pi[..., None] * cbi[d]
        wi = pr[..., None] * cbi[d] + pi[..., None] * cbr[d]
        return [jnp.transpose(z, (1, 0, 3, 2)).reshape(g, t_len * S5_H, S5_P) for z in (wr, wi)]

    wf, wb = w_dir(0, jnp.arange(t_len - 1, -1, -1)), w_dir(1, jnp.arange(t_len))
    w = jnp.concatenate(wf + wf[::-1] + wb + wb[::-1], axis=-1)
    dec = jnp.concatenate([pre[t_len, 0], pre[t_len, 0], -pim[t_len, 0], pim[t_len, 0],
                           pre[t_len, 1], pre[t_len, 1], -pim[t_len, 1], pim[t_len, 1]], axis=-1)[:, None, :]

    def m_dir(d, e):
        mr, mi = clre[e, d], -clim[e, d]
        return [jnp.transpose(z, (1, 2, 0, 3)).reshape(g, S5_P, t_len * S5_H) for z in (mr, mi)]

    m = jnp.concatenate(m_dir(0, jnp.arange(1, t_len + 1)) + m_dir(1, jnp.arange(t_len, 0, -1)), axis=1)
    kk = (jnp.einsum('kdgpo,dgpi->kdgoi', clre[:t_len], cbr, precision=_HI)
          - jnp.einsum('kdgpo,dgpi->kdgoi', clim[:t_len], cbi, precision=_HI))
    delta = jnp.arange(t_len)[None, :] - jnp.arange(t_len)[:, None]
    kf = kk[jnp.clip(delta, 0, t_len - 1), 0]
    kb = kk[jnp.clip(-delta, 0, t_len - 1), 1]
    dl = delta[:, :, None, None, None]
    toep = jnp.where(dl > 0, kf, 0.0) + jnp.where(dl < 0, kb, 0.0) + jnp.where(dl == 0, kf + kb, 0.0)
    toep = jnp.transpose(toep, (2, 0, 4, 1, 3)).reshape(g, t_len * S5_H, t_len * S5_H)
    return w, dec, m, toep


def _s5_body(x_ref, w_ref, dec_ref, m_ref, toep_ref, y_ref, z_scr, s_scr, *, bsz, n_ctx, n_lat):
    p2 = 2 * S5_P
    n_all = n_ctx + n_lat
    xb = x_ref[0].astype(jnp.bfloat16)
    z_scr[...] = jnp.dot(xb, w_ref[0], preferred_element_type=jnp.float32)
    dec = jnp.broadcast_to(dec_ref[0], (bsz, 4 * p2))
    a1f, a2f = dec[:, 0:p2], dec[:, p2:2 * p2]
    a1b, a2b = dec[:, 2 * p2:3 * p2], dec[:, 3 * p2:4 * p2]

    def step(i, carry):
        sf, tf, sb, tb = carry
        rf = pl.multiple_of(i * bsz, bsz)
        cb = jnp.where(i < n_ctx, n_ctx - 1 - i, n_all + n_ctx - 1 - i)
        rb = pl.multiple_of(cb * bsz, bsz)
        s_scr[pl.ds(rf, bsz), 0:p2] = sf
        s_scr[pl.ds(rb, bsz), p2:2 * p2] = sb
        zf = z_scr[pl.ds(rf, bsz), 0:2 * p2]
        zb = z_scr[pl.ds(rb, bsz), 2 * p2:4 * p2]
        return (sf * a1f + tf * a2f + zf[:, :p2], tf * a1f - sf * a2f + zf[:, p2:],
                sb * a1b + tb * a2b + zb[:, :p2], tb * a1b - sb * a2b + zb[:, p2:])

    zero = jnp.zeros((bsz, p2), jnp.float32)
    lax.fori_loop(0, n_all, step, (zero, zero, zero, zero), unroll=4)
    r0 = n_ctx * bsz
    y_ref[0] = (jnp.dot(xb[r0:], toep_ref[0], preferred_element_type=jnp.float32)
                + jnp.dot(s_scr[r0:, :].astype(jnp.bfloat16), m_ref[0], preferred_element_type=jnp.float32))


def _s5_mix(u_c, u_l, lam_re, lam_im, log_dt, b_re, b_im, c_re, c_im):
    bsz, l_ctx, gh = u_c.shape
    l_lat = u_l.shape[1]
    g, t_len = gh // S5_H, S5_T
    n_ctx, n_lat = l_ctx // t_len, l_lat // t_len
    w, dec, m, toep = _s5_weights(lam_re, lam_im, log_dt, b_re, b_im, c_re, c_im)

    def chunked(u, n):
        return jnp.transpose(u.reshape(bsz, n, t_len, g, S5_H), (3, 1, 0, 2, 4)).reshape(g, n * bsz, t_len * S5_H)

    x = jnp.concatenate([chunked(u_c, n_ctx), chunked(u_l, n_lat)], axis=1)
    rows, k = (n_ctx + n_lat) * bsz, t_len * S5_H
    y = pl.pallas_call(
        partial(_s5_body, bsz=bsz, n_ctx=n_ctx, n_lat=n_lat),
        grid=(g,),
        in_specs=[pl.BlockSpec((1, rows, k), lambda i: (i, 0, 0)),
                  pl.BlockSpec((1, k, 8 * S5_P), lambda i: (i, 0, 0)),
                  pl.BlockSpec((1, 1, 8 * S5_P), lambda i: (i, 0, 0)),
                  pl.BlockSpec((1, 4 * S5_P, k), lambda i: (i, 0, 0)),
                  pl.BlockSpec((1, k, k), lambda i: (i, 0, 0))],
        out_specs=pl.BlockSpec((1, n_lat * bsz, k), lambda i: (i, 0, 0)),
        out_shape=jax.ShapeDtypeStruct((g, n_lat * bsz, k), jnp.float32),
        scratch_shapes=[pltpu.VMEM((rows, 8 * S5_P), jnp.float32), pltpu.VMEM((rows, 4 * S5_P), jnp.float32)],
        compiler_params=pltpu.CompilerParams(dimension_semantics=("parallel",), vmem_limit_bytes=_VMEM_LIMIT),
        name="s5_mix",
    )(x, w.astype(jnp.bfloat16), dec, m.astype(jnp.bfloat16), toep.astype(jnp.bfloat16))
    return jnp.transpose(y.reshape(g, n_lat, bsz, t_len, S5_H), (2, 1, 3, 0, 4)).reshape(bsz, l_lat, gh)


def _glu_body(y_ref, u_ref, d_ref, w_ref, b_ref, o_ref):
    y = _gelu_tanh(y_ref[0] + u_ref[0] * d_ref[...])
    z = jnp.dot(y.astype(jnp.bfloat16), w_ref[...], preferred_element_type=jnp.float32) + b_ref[...]
    o_ref[0] = y * (1.0 / (1.0 + jnp.exp(-z)))


def _s5_out(y, u_all, l_ctx, col0, d_skip, glu_w, glu_b):
    bsz, l, c = y.shape
    tm = l_ctx
    vec = pl.BlockSpec((1, c), lambda b, i: (0, 0))
    return pl.pallas_call(
        _glu_body,
        grid=(bsz, l // tm),
        in_specs=[pl.BlockSpec((1, tm, c), lambda b, i: (b, i, 0)),
                  pl.BlockSpec((1, tm, c), lambda b, i: (b, i + 1, col0 // c)), vec,
                  pl.BlockSpec((c, c), lambda b, i: (0, 0)), vec],
        out_specs=pl.BlockSpec((1, tm, c), lambda b, i: (b, i, 0)),
        out_shape=jax.ShapeDtypeStruct(y.shape, jnp.float32),
        compiler_params=pltpu.CompilerParams(dimension_semantics=("parallel", "parallel")),
        name="s5_glu",
    )(y, u_all, d_skip.reshape(1, c), glu_w.astype(jnp.bfloat16), glu_b.reshape(1, c))


def _odd_mixer(u, l_ctx, lb, hg_norm_g, lam_re, lam_im, log_dt, b_re, b_im, c_re, c_im, d_skip, glu_w, glu_b):
    w = W_HALF
    hg_l = _hgrn_mix(u, lb, hg_norm_g, l_ctx)
    s_c, s_l = u[:, :l_ctx, 3 * w:4 * w], u[:, l_ctx:, 3 * w:4 * w]
    y_l = _s5_mix(s_c, s_l, lam_re, lam_im, log_dt, b_re, b_im, c_re, c_im)
    return hg_l, _s5_out(y_l, u, l_ctx, 3 * w, d_skip, glu_w, glu_b)


def _router_body(x_ref, g_ref, sh_ref, sc_ref, wr_ref, h_ref, lg_ref):
    x = x_ref[0]
    h = x * lax.rsqrt(jnp.mean(x * x, axis=-1, keepdims=True) + EPS) * g_ref[...]
    h = h * (1.0 + sc_ref[0]) + sh_ref[0]
    hb = h.astype(jnp.bfloat16)
    lg_ref[0] = jnp.dot(hb, wr_ref[...], preferred_element_type=jnp.float32)
    bits = pltpu.bitcast(hb.astype(jnp.float32), jnp.uint32)
    half = bits.shape[1] // 2
    h_ref[0] = (bits[:, :half] >> 16) | (bits[:, half:] & jnp.uint32(0xFFFF0000))


def _router(x, norm_g, shift, scale, w_router):
    bsz, n, d = x.shape
    tm = min(n, 512)
    vec = pl.BlockSpec((1, 1, d), lambda b, i: (b, 0, 0))
    return pl.pallas_call(
        _router_body,
        grid=(bsz, n // tm),
        in_specs=[pl.BlockSpec((1, tm, d), lambda b, i: (b, i, 0)),
                  pl.BlockSpec((1, d), lambda b, i: (0, 0)), vec, vec,
                  pl.BlockSpec((d, N_EXPERTS), lambda b, i: (0, 0))],
        out_specs=[pl.BlockSpec((1, tm, d // 2), lambda b, i: (b, i, 0)),
                   pl.BlockSpec((1, tm, N_EXPERTS), lambda b, i: (b, i, 0))],
        out_shape=[jax.ShapeDtypeStruct((bsz, n, d // 2), jnp.uint32),
                   jax.ShapeDtypeStruct((bsz, n, N_EXPERTS), jnp.float32)],
        compiler_params=pltpu.CompilerParams(dimension_semantics=("parallel", "parallel"),
                                             vmem_limit_bytes=_VMEM_LIMIT),
        name="moe_router",
    )(x, norm_g.reshape(1, d), jnp.broadcast_to(shift, (bsz, 1, d)), jnp.broadcast_to(scale, (bsz, 1, d)),
      w_router.astype(jnp.bfloat16))


def _expert_body(idx_ref, nxt_ref, h_ref, gate_ref, mod_ref, wg_ref, wu_ref, wd_ref, y_ref, x_cur, x_nxt,
                 *, cap, grp, n_f):
    rows = grp * cap

    @pl.when(pl.program_id(1) == 0)
    def _():
        def grab(g, carry):
            base = pl.multiple_of(g * _SUBLANES, _SUBLANES)
            for r in range(_SUBLANES):
                x_nxt[pl.ds(base + r, 1), :] = h_ref[base // cap, pl.ds(idx_ref[0, 0, base + r], 1), :]
            return carry

        lax.fori_loop(0, rows // _SUBLANES, grab, 0)

    x_cur[...] = x_nxt[...]
    w32 = x_cur[...]
    xb = jnp.concatenate([pltpu.bitcast(w32 << 16, jnp.float32),
                          pltpu.bitcast(w32 & jnp.uint32(0xFFFF0000), jnp.float32)], axis=1).astype(jnp.bfloat16)
    for r in range(rows):
        x_nxt[pl.ds(r, 1), :] = h_ref[r // cap, pl.ds(nxt_ref[0, 0, r], 1), :]
    tf = wg_ref.shape[3] // n_f
    acc = None
    for f in range(n_f):
        cs = slice(f * tf, (f + 1) * tf)
        a = jnp.dot(xb, wg_ref[0, 0, :, cs], preferred_element_type=jnp.float32)
        u = jnp.dot(xb, wu_ref[0, 0, :, cs], preferred_element_type=jnp.float32)
        hh = (a * (1.0 / (1.0 + jnp.exp(-a))) * u).astype(jnp.bfloat16)
        part = jnp.dot(hh, wd_ref[0, 0, cs, :], preferred_element_type=jnp.float32)
        acc = part if acc is None else acc + part
    acc = acc * gate_ref[0]
    for g in range(grp):
        y_ref[g, 0] = acc[g * cap:(g + 1) * cap] * mod_ref[g]


def _expert_ffn(h32, idx, gate, mod, layer, w_gate, w_up, w_down):
    bsz, n, dh = h32.shape
    d = 2 * dh
    cap = idx.shape[-1]
    _, e, _, ff = w_gate.shape
    grp = max(1, min(bsz, _EXPERT_ROWS // cap))
    ng = bsz // grp
    rows = grp * cap
    regroup = lambda t: t.reshape(ng, grp, e, cap).transpose(0, 2, 1, 3).reshape(ng * e, rows)
    idx3 = regroup(idx)[:, None, :]
    ge = lambda b, k: (b * e + k, 0, 0)
    ge_next = lambda b, k: (b * e + jnp.minimum(k + 1, e - 1), 0, 0)
    wspec = lambda s: pl.BlockSpec((1, 1) + s, lambda b, k: (layer, k, 0, 0))
    y = pl.pallas_call(
        partial(_expert_body, cap=cap, grp=grp, n_f=3),
        grid=(ng, e),
        in_specs=[pl.BlockSpec((1, 1, rows), ge, memory_space=pltpu.SMEM),
                  pl.BlockSpec((1, 1, rows), ge_next, memory_space=pltpu.SMEM),
                  pl.BlockSpec((grp, n, dh), lambda b, k: (b, 0, 0)),
                  pl.BlockSpec((1, rows, 1), ge),
                  pl.BlockSpec((grp, 1, d), lambda b, k: (b, 0, 0)),
                  wspec((d, ff)), wspec((d, ff)), wspec((ff, d))],
        out_specs=pl.BlockSpec((grp, 1, cap, d), lambda b, k: (b, k, 0, 0)),
        out_shape=jax.ShapeDtypeStruct((bsz, e, cap, d), jnp.float32),
        scratch_shapes=[pltpu.VMEM((rows, dh), jnp.uint32), pltpu.VMEM((rows, dh), jnp.uint32)],
        compiler_params=pltpu.CompilerParams(dimension_semantics=("parallel", "arbitrary"),
                                             vmem_limit_bytes=_VMEM_LIMIT_BIG),
        name="moe_expert",
    )(idx3, idx3, h32, regroup(gate)[:, :, None], jnp.broadcast_to(mod, (bsz, 1, d)), w_gate, w_up, w_down)
    return y.reshape(bsz * e, cap, d)


def _combine_body(idx_ref, y_ref, o_ref, *, cap):
    @pl.when(pl.program_id(1) == 0)
    def _():
        o_ref[...] = jnp.zeros_like(o_ref)

    def group(g, carry):
        base = pl.multiple_of(g * _SUBLANES, _SUBLANES)
        ys = y_ref[0, pl.ds(base, _SUBLANES), :]
        ts = [idx_ref[0, 0, base + r] for r in range(_SUBLANES)]
        cur = [o_ref[0, pl.ds(t, 1), :] for t in ts]
        for r in range(_SUBLANES):
            o_ref[0, pl.ds(ts[r], 1), :] = cur[r] + ys[r:r + 1, :]
        return carry

    lax.fori_loop(0, cap // _SUBLANES, group, 0)


def _combine(y, idx, n):
    be, cap, d = y.shape
    bsz = be // N_EXPERTS
    return pl.pallas_call(
        partial(_combine_body, cap=cap),
        grid=(bsz, N_EXPERTS),
        in_specs=[pl.BlockSpec((1, 1, cap), lambda b, k: (b * N_EXPERTS + k, 0, 0), memory_space=pltpu.SMEM),
                  pl.BlockSpec((1, cap, d), lambda b, k: (b * N_EXPERTS + k, 0, 0))],
        out_specs=pl.BlockSpec((1, n, d), lambda b, k: (b, 0, 0)),
        out_shape=jax.ShapeDtypeStruct((bsz, n, d), jnp.float32),
        compiler_params=pltpu.CompilerParams(dimension_semantics=("parallel", "arbitrary"),
                                             vmem_limit_bytes=_VMEM_LIMIT),
        name="moe_combine",
    )(idx.reshape(be, 1, cap), y)


def _moe(x, norm_g, shift, scale, mod, w_router, layer, w_gate, w_up, w_down):
    bsz, n, d = x.shape
    cap = max(1, (EC_CAPACITY * n) // N_EXPERTS)
    h32, logits = _router(x, norm_g, shift, scale, w_router)
    aff = jax.nn.softmax(logits, axis=-1)
    gate, idx = lax.top_k(jnp.swapaxes(aff, 1, 2), cap)
    y = _expert_ffn(h32, idx, gate, mod, layer, w_gate, w_up, w_down)
    return _combine(y, idx, n)


def kernel(x, c, ctx, c_ctx, mod_w, mod_b, norm1_g, norm2_g, out_w, even_in_w, lru_conv_w, lru_conv_b, lru_wa, lru_ba, lru_wx, lru_bx, lru_lam, hy_conv_w, hy_conv_b, hy_w1, hy_b1, hy_w2, hy_b2, hy_w3, hy_freq, hy_skip, odd_in_w, hg_lb, hg_norm_g, s5_lam_re, s5_lam_im, s5_log_dt, s5_b_re, s5_b_im, s5_c_re, s5_c_im, s5_d, s5_glu_w, s5_glu_b, router_w, ex_w_gate, ex_w_up, ex_w_down, final_g):
    rows = x.shape[1] // GRID_W
    l_ctx = ctx.shape[1]
    p = jax.nn.softmax(hg_lb, axis=0)
    lb_all = jnp.clip(jnp.cumsum(p, axis=0) - p[0], 0.0, 1.0 - 1e-4)
    ex_wg, ex_wu, ex_wd = (t.astype(jnp.bfloat16) for t in (ex_w_gate, ex_w_up, ex_w_down))
    sc_lat = jax.nn.silu(c)
    sc_ctx = jax.nn.silu(c_ctx)[None]
    for i in range(DEPTH):
        need_ctx = i < DEPTH - 1
        j = i // 2
        ml = [t[:, None, :] for t in jnp.split(sc_lat @ mod_w[i] + mod_b[i], 6, axis=-1)]
        mc = [t[:, None, :] for t in jnp.split(sc_ctx @ mod_w[i] + mod_b[i], 6, axis=-1)]
        shift, scale, gate = (_two_part(mc[k], ml[k]) for k in range(3))
        if i % 2 == 0:
            x_all = jnp.concatenate([ctx, x], axis=1)
            u = _norm_mm(x_all, norm1_g[i], shift, scale, even_in_w[j], l_ctx)
            h_f, h_b, yb = _even_mixer(u, l_ctx, lru_conv_w[j], lru_conv_b[j], lru_wa[j], lru_ba[j], lru_wx[j],
                                       lru_bx[j], lru_lam[j], hy_conv_w[j], hy_conv_b[j], hy_w1[j], hy_b1[j],
                                       hy_w2[j], hy_b2[j], hy_w3[j], hy_freq[j], hy_skip[j])
            x_all = _proj_res_lru(h_f, h_b, u, yb, out_w[i], x_all, gate, l_ctx)
            ctx, x = x_all[:, :l_ctx], x_all[:, l_ctx:]
        else:
            assert not need_ctx
            x_all = jnp.concatenate([ctx, _to_col_major(x, rows)], axis=1)
            u = _norm_mm(x_all, norm1_g[i], shift, scale, odd_in_w[j], l_ctx)
            hg_l, s5_l = _odd_mixer(u, l_ctx, lb_all[i], hg_norm_g[j], s5_lam_re[j], s5_lam_im[j],
                                    s5_log_dt[j], s5_b_re[j], s5_b_im[j], s5_c_re[j], s5_c_im[j], s5_d[j],
                                    s5_glu_w[j], s5_glu_b[j])
            x = _to_row_major(_proj_res(hg_l, s5_l, out_w[i], x_all, gate, l_ctx, l_ctx), rows)
        ex = (router_w[i], i, ex_wg, ex_wu, ex_wd)
        moe_l = _moe(x, norm2_g[i], ml[3], ml[4], ml[5], *ex)
        if need_ctx:
            x = x + moe_l
            ctx = ctx + _moe(ctx, norm2_g[i], mc[3], mc[4], mc[5], *ex)
    return _add_norm(x, moe_l, final_g)
```

```python
import math
from functools import partial

import numpy as np
import jax
import jax.numpy as jnp
from jax import lax
from jax.experimental import pallas as pl
from jax.experimental.pallas import tpu as pltpu

D_MODEL = 1024
DEPTH = 2
GRID_W = 64
W_HALF = D_MODEL // 2
EPS = 1e-6
LRU_HEADS = 8
LRU_HEAD_DIM = W_HALF // LRU_HEADS
LRU_C = 8.0
HY_ORDER = 2
HY_BANDS = 16
HY_MAX_DECAY = math.log(1e-2) / 0.3
HY_MIN_DECAY = math.log(1e-2) / 1.5
HG_HEADS = 4
HG_DK = W_HALF // HG_HEADS
HG_C = 16
HG_SUB = 4
HG_TB = 256
S5_H = 16
S5_P = 64
S5_MAX_RE = -1e-4
S5_T = 16
_DFT_LOW = 64
N_EXPERTS = 16
EC_CAPACITY = 2

_SUBLANES = 8
_DW_ROWS = 256
_EXPERT_ROWS = 512
_VMEM_LIMIT = 48 * 1024 * 1024
_VMEM_LIMIT_BIG = 56 * 1024 * 1024
_HI = lax.Precision.HIGHEST
_NEG = -1e30


def _mm_body(a_ref, b_ref, o_ref):
    o_ref[...] = jnp.dot(a_ref[...].astype(jnp.bfloat16), b_ref[...],
                         preferred_element_type=jnp.float32)


def _pick(n, pref):
    for t in pref:
        if n % t == 0:
            return t
    return n


def _mm(a, b):
    m, k = a.shape
    n = b.shape[1]
    tm = _pick(m, (512, 256, 128, 64, 32, 16, 8))
    tn = _pick(n, (512, 256, 128))
    return pl.pallas_call(
        _mm_body,
        grid=(m // tm, n // tn),
        in_specs=[pl.BlockSpec((tm, k), lambda i, j: (i, 0)),
                  pl.BlockSpec((k, tn), lambda i, j: (0, j))],
        out_specs=pl.BlockSpec((tm, tn), lambda i, j: (i, j)),
        out_shape=jax.ShapeDtypeStruct((m, n), jnp.float32),
        compiler_params=pltpu.CompilerParams(
            dimension_semantics=("parallel", "parallel"), vmem_limit_bytes=_VMEM_LIMIT),
        name="mm",
    )(a, b.astype(jnp.bfloat16))


def _mm3(a, b):
    bsz, l, k = a.shape
    return _mm(a.reshape(bsz * l, k), b).reshape(bsz, l, b.shape[1])


def _add_norm_body(x_ref, m_ref, g_ref, o_ref):
    x = x_ref[...] + m_ref[...]
    o_ref[...] = x * lax.rsqrt(jnp.mean(x * x, axis=-1, keepdims=True) + EPS) * g_ref[...]


def _add_norm(x, m, g):
    bsz, n, d = x.shape
    tm = min(n, 512)
    blk = pl.BlockSpec((1, tm, d), lambda b, i: (b, i, 0))
    return pl.pallas_call(
        _add_norm_body,
        grid=(bsz, n // tm),
        in_specs=[blk, blk, pl.BlockSpec((1, 1, d), lambda b, i: (0, 0, 0))],
        out_specs=blk,
        out_shape=jax.ShapeDtypeStruct(x.shape, jnp.float32),
        compiler_params=pltpu.CompilerParams(dimension_semantics=("parallel", "parallel")),
        name="add_norm",
    )(x, m, g.reshape(1, 1, d))


def _two_part(t_ctx, t_lat):
    bsz, _, d = t_lat.shape
    return jnp.stack([jnp.broadcast_to(t_ctx, (bsz, 1, d)), t_lat], axis=1).reshape(2 * bsz, 1, d)


def _norm_mm_body(x_ref, g_ref, sh_ref, sc_ref, w_ref, o_ref):
    x = x_ref[0]
    h = x * lax.rsqrt(jnp.mean(x * x, axis=-1, keepdims=True) + EPS) * g_ref[...]
    h = h * (1.0 + sc_ref[0]) + sh_ref[0]
    o_ref[0] = jnp.dot(h.astype(jnp.bfloat16), w_ref[...], preferred_element_type=jnp.float32)


def _norm_mm(x_all, norm_g, shift, scale, w, l_ctx):
    bsz, l_tot, d = x_all.shape
    n = w.shape[1]
    tm = l_ctx
    vec = pl.BlockSpec((1, 1, d), lambda b, j: (2 * b + jnp.minimum(j, 1), 0, 0))
    return pl.pallas_call(
        _norm_mm_body,
        grid=(bsz, l_tot // tm),
        in_specs=[pl.BlockSpec((1, tm, d), lambda b, j: (b, j, 0)),
                  pl.BlockSpec((1, d), lambda b, j: (0, 0)), vec, vec,
                  pl.BlockSpec((d, n), lambda b, j: (0, 0))],
        out_specs=pl.BlockSpec((1, tm, n), lambda b, j: (b, j, 0)),
        out_shape=jax.ShapeDtypeStruct((bsz, l_tot, n), jnp.float32),
        compiler_params=pltpu.CompilerParams(dimension_semantics=("parallel", "parallel"),
                                             vmem_limit_bytes=_VMEM_LIMIT),
        name="norm_mm",
    )(x_all, norm_g.reshape(1, d), shift, scale, w.astype(jnp.bfloat16))


def _proj_res_body(ya_ref, yb_ref, wa_ref, wb_ref, x_ref, gt_ref, o_ref):
    y = (jnp.dot(ya_ref[0].astype(jnp.bfloat16), wa_ref[...], preferred_element_type=jnp.float32)
         + jnp.dot(yb_ref[0].astype(jnp.bfloat16), wb_ref[...], preferred_element_type=jnp.float32))
    o_ref[0] = x_ref[0] + gt_ref[0] * y


def _proj_res_lru_body(hf_ref, hb_ref, u_ref, yb_ref, wa_ref, wb_ref, x_ref, gt_ref, o_ref):
    ya = (hf_ref[0] + hb_ref[0]) * _gelu_tanh(u_ref[0])
    y = (jnp.dot(ya.astype(jnp.bfloat16), wa_ref[...], preferred_element_type=jnp.float32)
         + jnp.dot(yb_ref[0].astype(jnp.bfloat16), wb_ref[...], preferred_element_type=jnp.float32))
    o_ref[0] = x_ref[0] + gt_ref[0] * y


def _proj_res_lru(h_f, h_b, u_all, yb, w, x_all, gate, l_ctx):
    bsz, l_tot, ka = h_f.shape
    d = w.shape[1]
    tm = l_ctx
    blk = lambda k, c=0: pl.BlockSpec((1, tm, k), lambda b, j: (b, j, c))
    return pl.pallas_call(
        _proj_res_lru_body,
        grid=(bsz, l_tot // tm),
        in_specs=[blk(ka), blk(ka), blk(ka, 1), blk(yb.shape[2]),
                  pl.BlockSpec((ka, d), lambda b, j: (0, 0)), pl.BlockSpec((yb.shape[2], d), lambda b, j: (0, 0)),
                  blk(d), pl.BlockSpec((1, 1, d), lambda b, j: (2 * b + jnp.minimum(j, 1), 0, 0))],
        out_specs=blk(d),
        out_shape=jax.ShapeDtypeStruct((bsz, l_tot, d), jnp.float32),
        compiler_params=pltpu.CompilerParams(dimension_semantics=("parallel", "parallel"),
                                             vmem_limit_bytes=_VMEM_LIMIT),
        name="proj_res_lru",
    )(h_f, h_b, u_all, yb, w[:ka].astype(jnp.bfloat16), w[ka:].astype(jnp.bfloat16), x_all, gate)


def _proj_res(ya, yb, w, x_all, gate, l_ctx, row0):
    bsz, l_y, ka = ya.shape
    d = w.shape[1]
    tm = l_ctx
    j0 = row0 // tm
    yblk = lambda k: pl.BlockSpec((1, tm, k), lambda b, j: (b, j, 0))
    return pl.pallas_call(
        _proj_res_body,
        grid=(bsz, l_y // tm),
        in_specs=[yblk(ka), yblk(yb.shape[2]),
                  pl.BlockSpec((ka, d), lambda b, j: (0, 0)), pl.BlockSpec((yb.shape[2], d), lambda b, j: (0, 0)),
                  pl.BlockSpec((1, tm, d), lambda b, j: (b, j + j0, 0)),
                  pl.BlockSpec((1, 1, d), lambda b, j: (2 * b + jnp.minimum(j + j0, 1), 0, 0))],
        out_specs=pl.BlockSpec((1, tm, d), lambda b, j: (b, j, 0)),
        out_shape=jax.ShapeDtypeStruct((bsz, l_y, d), jnp.float32),
        compiler_params=pltpu.CompilerParams(dimension_semantics=("parallel", "parallel"),
                                             vmem_limit_bytes=_VMEM_LIMIT),
        name="proj_res",
    )(ya, yb, w[:ka].astype(jnp.bfloat16), w[ka:].astype(jnp.bfloat16), x_all, gate)


def _gelu_tanh(x):
    return 0.5 * x * (1.0 + jnp.tanh(math.sqrt(2.0 / math.pi) * (x + 0.044715 * (x * x * x))))


def _dwconv_body(u_ref, w_ref, o_ref, *rest, l_ctx, taps, left, split):
    l_tot = u_ref.shape[1]
    tb, h = _DW_ROWS, _SUBLANES
    row = lax.broadcasted_iota(jnp.int32, (tb, 1), 0)

    def block(j, carry):
        r0 = pl.multiple_of(j * tb, tb)
        lo = jnp.where(r0 < l_ctx, 0, l_ctx)
        hi = jnp.where(r0 < l_ctx, l_ctx, l_tot)
        prev = u_ref[0, pl.ds(pl.multiple_of(jnp.maximum(r0 - h, 0), h), h), :]
        nxt = u_ref[0, pl.ds(pl.multiple_of(jnp.minimum(r0 + tb, l_tot - h), h), h), :]
        ext = jnp.concatenate([prev, u_ref[0, pl.ds(r0, tb), :], nxt], axis=0)
        y = jnp.broadcast_to(w_ref[0, taps:taps + 1, :], (tb, ext.shape[1]))
        for t in range(taps):
            off = t - left
            src = r0 + row + off
            y = y + jnp.where((src >= lo) & (src < hi), ext[h + off:h + off + tb, :], 0.0) * w_ref[0, t:t + 1, :]
        if not split:
            o_ref[0, pl.ds(r0, tb), :] = y
        else:
            ob_c, o_l, ob_l = rest

            @pl.when(r0 < l_ctx)
            def _():
                o_ref[0, pl.ds(r0, tb), :] = y
                ob_c[0, pl.ds(r0, tb), :] = y.astype(jnp.bfloat16)

            @pl.when(r0 >= l_ctx)
            def _():
                o_l[0, pl.ds(r0 - l_ctx, tb), :] = y
                ob_l[0, pl.ds(r0 - l_ctx, tb), :] = y.astype(jnp.bfloat16)
        return carry

    lax.fori_loop(0, l_tot // tb, block, 0)


def _dwconv_slabs(u_all, col0, w, b, l_ctx, split):
    bsz, l_tot, _ = u_all.shape
    taps, nc = w.shape
    c = W_HALF
    n = nc // c
    wb = jnp.concatenate([w, b[None], jnp.zeros((_SUBLANES - taps - 1, nc), jnp.float32)], axis=0)
    wb = wb.reshape(_SUBLANES, n, c).transpose(1, 0, 2)
    if split:
        rows = (l_ctx, l_ctx, l_tot - l_ctx, l_tot - l_ctx)
        dts = (jnp.float32, jnp.bfloat16, jnp.float32, jnp.bfloat16)
    else:
        rows, dts = (l_tot,), (jnp.float32,)
    out = pl.pallas_call(
        partial(_dwconv_body, l_ctx=l_ctx, taps=taps, left=(taps - 1) // 2, split=split),
        grid=(bsz, n),
        in_specs=[pl.BlockSpec((1, l_tot, c), lambda bb, s: (bb, 0, s + col0 // c)),
                  pl.BlockSpec((1, _SUBLANES, c), lambda bb, s: (s, 0, 0))],
        out_specs=[pl.BlockSpec((1, r, c), lambda bb, s: (bb, 0, s)) for r in rows],
        out_shape=[jax.ShapeDtypeStruct((bsz, r, nc), dt) for r, dt in zip(rows, dts)],
        compiler_params=pltpu.CompilerParams(dimension_semantics=("parallel", "parallel"),
                                             vmem_limit_bytes=_VMEM_LIMIT),
        name="dwconv",
    )(u_all, wb)
    return out if split else out[0]


def _to_col_major(t, rows):
    bsz, l, ch = t.shape
    return t.reshape(bsz, rows, GRID_W, ch).transpose(0, 2, 1, 3).reshape(bsz, l, ch)


def _to_row_major(t, rows):
    bsz, l, ch = t.shape
    return t.reshape(bsz, GRID_W, rows, ch).transpose(0, 2, 1, 3).reshape(bsz, l, ch)


def _lru_body(xf_ref, xb_ref, wa_ref, wx_ref, par_ref, hf_ref, hb_ref,
              af_scr, bf_scr, ab_scr, bb_scr, sf_scr, sb_scr, *, tb):
    @pl.when(pl.program_id(1) == 0)
    def _():
        sf_scr[...] = jnp.zeros_like(sf_scr)
        sb_scr[...] = jnp.zeros_like(sb_scr)

    def coeffs(x, d, a_scr, b_scr):
        xb = x.astype(jnp.bfloat16)
        r = jnp.dot(xb, wa_ref[d], preferred_element_type=jnp.float32) + par_ref[d, 0:1, :]
        i = jnp.dot(xb, wx_ref[d], preferred_element_type=jnp.float32) + par_ref[d, 1:2, :]
        r = 1.0 / (1.0 + jnp.exp(-r))
        i = 1.0 / (1.0 + jnp.exp(-i))
        log_a = -LRU_C * r * par_ref[d, 2:3, :]
        a = jnp.exp(log_a)
        a_scr[...] = a
        b_scr[...] = jnp.sqrt(-jnp.tanh(log_a) * (a * a + 1.0)) * i * x

    coeffs(xf_ref[0], 0, af_scr, bf_scr)
    coeffs(xb_ref[0], 1, ab_scr, bb_scr)

    def tile(i, carry):
        hf, hb = carry
        base_f = pl.multiple_of(i * _SUBLANES, _SUBLANES)
        base_b = pl.multiple_of(tb - _SUBLANES - i * _SUBLANES, _SUBLANES)
        for r in range(_SUBLANES):
            rf = pl.ds(base_f + r, 1)
            hf = af_scr[rf, :] * hf + bf_scr[rf, :]
            hf_ref[0, rf, :] = hf
            rb = pl.ds(base_b + (_SUBLANES - 1 - r), 1)
            hb = ab_scr[rb, :] * hb + bb_scr[rb, :]
            hb_ref[0, rb, :] = hb
        return hf, hb

    hf, hb = lax.fori_loop(0, tb // _SUBLANES, tile, (sf_scr[...], sb_scr[...]))
    sf_scr[...] = hf
    sb_scr[...] = hb


def _block_diag(w):
    two, h, n, _ = w.shape
    eye = jnp.eye(h, dtype=w.dtype)
    return (w[:, :, :, None, :] * eye[None, :, None, :, None]).reshape(two, h * n, h * n)


def _lru_scan(xc, l_ctx, wa, ba, wx, bx, lam):
    bsz, l_tot, ch = xc.shape
    tb = l_ctx
    nb = l_tot // tb
    par = jnp.stack([ba.reshape(2, ch), bx.reshape(2, ch), jax.nn.softplus(-lam).reshape(2, ch)], axis=1)
    par = jnp.concatenate([par, jnp.zeros((2, _SUBLANES - 3, ch), jnp.float32)], axis=1)
    fwd = pl.BlockSpec((1, tb, ch), lambda b, j: (b, j, 0))
    bwd = pl.BlockSpec((1, tb, ch), lambda b, j: (b, jnp.where(j == 0, 0, nb - j), 0))
    wspec = pl.BlockSpec((2, ch, ch), lambda b, j: (0, 0, 0))
    return pl.pallas_call(
        partial(_lru_body, tb=tb),
        grid=(bsz, nb),
        in_specs=[fwd, bwd, wspec, wspec, pl.BlockSpec((2, _SUBLANES, ch), lambda b, j: (0, 0, 0))],
        out_specs=[fwd, bwd],
        out_shape=[jax.ShapeDtypeStruct(xc.shape, jnp.float32)] * 2,
        scratch_shapes=[pltpu.VMEM((tb, ch), jnp.float32)] * 4 + [pltpu.VMEM((1, ch), jnp.float32)] * 2,
        compiler_params=pltpu.CompilerParams(dimension_semantics=("parallel", "arbitrary")),
        name="lru_scan",
    )(xc, xc, _block_diag(wa).astype(jnp.bfloat16), _block_diag(wx).astype(jnp.bfloat16), par)


def _hyena_filters(l, w1, b1, w2, b2, w3, freq):
    t = jnp.arange(l, dtype=jnp.float32)
    t_unit = t / max(l - 1, 1)
    bands = jnp.linspace(1e-4, HY_BANDS - 1, HY_BANDS, dtype=jnp.float32)
    ang = (2.0 * math.pi / l) * t[:, None] * bands[None, :]
    z = jnp.concatenate([t_unit[:, None], jnp.cos(ang), -jnp.sin(ang)], axis=-1)
    h = jnp.sin(freq * (z @ w1 + b1))
    h = jnp.sin(freq * (h @ w2 + b2))
    h = h @ w3
    dist = jnp.abs(t - (l // 2)) / (l / 2.0)
    deltas = jnp.abs(jnp.linspace(HY_MIN_DECAY, HY_MAX_DECAY, HY_ORDER * W_HALF, dtype=jnp.float32))
    h = h * jnp.exp(-dist[:, None] * deltas[None, :])
    h = h * lax.rsqrt(jnp.sum(h * h, axis=0, keepdims=True) + EPS)
    return h


def _dft_mats(l, tf):
    n = 2 * l
    nl = _DFT_LOW
    nh = l // nl
    t = jnp.arange(l, dtype=jnp.int32)
    w = 2.0 * math.pi / n

    def tables(tv):
        ah = (((nl * jnp.arange(nh, dtype=jnp.int32))[:, None] * tv[None, :]) % n).astype(jnp.float32) * w
        al = ((jnp.arange(nl, dtype=jnp.int32)[:, None] * tv[None, :]) % n).astype(jnp.float32) * w
        return jnp.cos(ah), jnp.sin(ah), jnp.cos(al), jnp.sin(al)

    ch, sh, cl, sl = tables(t)
    f_re = (ch[:, None, :] * cl[None] - sh[:, None, :] * sl[None]).reshape(l, l)
    f_im = -(sh[:, None, :] * cl[None] + ch[:, None, :] * sl[None]).reshape(l, l)
    f_im = f_im.at[0].set(jnp.where(t % 2 == 0, 1.0, -1.0))
    f = jnp.concatenate([f_re.reshape(l // tf, tf, l), f_im.reshape(l // tf, tf, l)], axis=1).reshape(n, l)
    tp = t + l // 2
    ch, sh, cl, sl = (z.T for z in tables(tp))
    g_re = (2.0 / n) * (ch[:, :, None] * cl[:, None, :] - sh[:, :, None] * sl[:, None, :]).reshape(l, l)
    g_im = (-2.0 / n) * (sh[:, :, None] * cl[:, None, :] + ch[:, :, None] * sl[:, None, :]).reshape(l, l)
    g_re = g_re.at[:, 0].set(1.0 / n)
    g_im = g_im.at[:, 0].set(jnp.where(tp % 2 == 0, 1.0, -1.0) / n)
    g = jnp.stack([g_re.reshape(l, l // tf, tf), g_im.reshape(l, l // tf, tf)], axis=2).reshape(l, n)
    return f.astype(jnp.bfloat16), g.astype(jnp.bfloat16)


def _spec_body(f_ref, u_ref, a_ref, b_ref, d_ref, y_ref, *, tf):
    acc = jnp.dot(f_ref[...], u_ref[0], preferred_element_type=jnp.float32)
    xr, xi = acc[:tf], acc[tf:]
    y_ref[0, :tf, :] = (xr * a_ref[...] - xi * b_ref[...]).astype(jnp.bfloat16)
    y_ref[0, tf:, :] = (xr * b_ref[...] + xi * d_ref[...]).astype(jnp.bfloat16)


def _spec_mul(f, u, a, b, d, tf):
    bsz, l, _ = u.shape
    c = a.shape[1]
    coef = pl.BlockSpec((tf, c), lambda i, bb: (i, 0))
    return pl.pallas_call(
        partial(_spec_body, tf=tf),
        grid=(l // tf, bsz),
        in_specs=[pl.BlockSpec((2 * tf, l), lambda i, bb: (i, 0)),
                  pl.BlockSpec((1, l, c), lambda i, bb: (bb, 0, 0)), coef, coef, coef],
        out_specs=pl.BlockSpec((1, 2 * tf, c), lambda i, bb: (bb, i, 0)),
        out_shape=jax.ShapeDtypeStruct((bsz, 2 * l, c), jnp.bfloat16),
        compiler_params=pltpu.CompilerParams(dimension_semantics=("parallel", "parallel"),
                                             vmem_limit_bytes=_VMEM_LIMIT),
        name="hy_spec",
    )(f, u, a, b, d)


def _inv_body(g_ref, y_ref, u_ref, x_ref, skip_ref, z_ref, zb_ref):
    acc = jnp.dot(g_ref[...], y_ref[0], preferred_element_type=jnp.float32)
    z = x_ref[0] * (acc + u_ref[0] * skip_ref[...])
    z_ref[0] = z
    zb_ref[0] = z.astype(jnp.bfloat16)


def _inv_gate(g, y, u, ucol, xg, xcol, skip, tm):
    bsz, n, c = y.shape
    l = n // 2
    blk = pl.BlockSpec((1, tm, c), lambda bb, i: (bb, i, 0))
    return pl.pallas_call(
        _inv_body,
        grid=(bsz, l // tm),
        in_specs=[pl.BlockSpec((tm, n), lambda bb, i: (i, 0)),
                  pl.BlockSpec((1, n, c), lambda bb, i: (bb, 0, 0)),
                  pl.BlockSpec((1, tm, c), lambda bb, i: (bb, i, ucol)),
                  pl.BlockSpec((1, tm, c), lambda bb, i: (bb, i, xcol)),
                  pl.BlockSpec((1, c), lambda bb, i: (0, 0))],
        out_specs=[blk, blk],
        out_shape=[jax.ShapeDtypeStruct((bsz, l, c), jnp.float32), jax.ShapeDtypeStruct((bsz, l, c), jnp.bfloat16)],
        compiler_params=pltpu.CompilerParams(dimension_semantics=("parallel", "parallel"),
                                             vmem_limit_bytes=_VMEM_LIMIT),
        name="hy_inv",
    )(g, y, u, xg, skip.reshape(1, c))


def _hyena_conv2(hc, hb, filt, skip):
    l, c = hc.shape[1], hc.shape[2] // 3
    tf = min(512, l // 2)
    tm = min(512, l)
    f, g = _dft_mats(l, tf)
    hs = _mm(f, filt)
    hs = hs.reshape(l // tf, 2, tf, 2 * c)
    h_re, h_im = hs[:, 0].reshape(l, 2 * c), hs[:, 1].reshape(l, 2 * c)
    slot0 = (jnp.arange(l) == 0)[:, None]
    a = h_re
    b = jnp.where(slot0, 0.0, h_im)
    d = jnp.where(slot0, h_im, h_re)
    y = _spec_mul(f, hb, a[:, :c], b[:, :c], d[:, :c], tf)
    z, zb = _inv_gate(g, y, hc, 0, hc, 1, skip[0], tm)
    y = _spec_mul(f, zb, a[:, c:], b[:, c:], d[:, c:], tf)
    z, _ = _inv_gate(g, y, z, 0, hc, 2, skip[1], tm)
    return z


def _hyena(u, l_ctx, conv_w, conv_b, w1, b1, w2, b2, w3, freq, skip):
    hc_c, hb_c, hc_l, hb_l = _dwconv_slabs(u, 2 * W_HALF, conv_w, conv_b, l_ctx, True)
    zs = []
    for hc, hb in ((hc_c, hb_c), (hc_l, hb_l)):
        filt = _hyena_filters(hc.shape[1], w1, b1, w2, b2, w3, freq)
        zs.append(_hyena_conv2(hc, hb, filt, skip))
    return jnp.concatenate(zs, axis=1)


def _even_mixer(u, l_ctx, lru_conv_w, lru_conv_b, lru_wa, lru_ba, lru_wx, lru_bx,
                lru_lam, hy_conv_w, hy_conv_b, hy_w1, hy_b1, hy_w2, hy_b2, hy_w3, hy_freq, hy_skip):
    w = W_HALF
    xc = _dwconv_slabs(u, 0, lru_conv_w, lru_conv_b, l_ctx, False)
    h_f, h_b = _lru_scan(xc, l_ctx, lru_wa, lru_ba, lru_wx, lru_bx, lru_lam)
    yb = _hyena(u, l_ctx, hy_conv_w, hy_conv_b, hy_w1, hy_b1, hy_w2, hy_b2, hy_w3, hy_freq, hy_skip)
    return h_f, h_b, yb


def _split3(x):
    hi = x.astype(jnp.bfloat16)
    r1 = x - hi.astype(jnp.float32)
    mid = r1.astype(jnp.bfloat16)
    lo = (r1 - mid.astype(jnp.float32)).astype(jnp.bfloat16)
    return jnp.concatenate([hi, mid, lo], axis=1)


def _sum3(r, c):
    return r[:, 0:c] + r[:, c:2 * c] + r[:, 2 * c:3 * c]


def _log_sigmoid(x):
    return jnp.minimum(x, 0.0) - jnp.log(1.0 + jnp.exp(-jnp.abs(x)))


def _hgrn_consts():
    tb, c, sb = HG_TB, HG_C, HG_SUB
    r = np.arange(tb)
    same = (r[:, None] // c) == (r[None, :] // c)
    same4 = (r[:, None] // sb) == (r[None, :] // sb)
    le = r[None, :] <= r[:, None]
    sub = (r % c) // sb
    cum, msk = [], []
    for d in range(2):
        tri = le if d == 0 else le.T
        cum.append(np.concatenate([same & tri, same, same4 & tri], axis=0))
        if d == 0:
            mm = [same & (sub[:, None] == i + 1) & (sub[None, :] <= i) for i in range(c // sb - 1)]
        else:
            mm = [same & (sub[:, None] == i) & (sub[None, :] > i) for i in range(c // sb - 1)]
        msk.append(np.stack(mm))
    cum = jnp.asarray(np.stack(cum).astype(np.float32), dtype=jnp.bfloat16)
    return cum, jnp.asarray(np.stack(msk).astype(np.float32))


def _hgrn_body(ff_ref, fb_ref, v_ref, q_ref, g_ref, lbp_ref, ng_ref, cum_ref, msk_ref, o_ref,
               acc_scr, st_scr, *, n_blk, l_ctx):
    tb, c, sb, dk = HG_TB, HG_C, HG_SUB, HG_DK
    n_c = tb // c
    n_sub = c // sb
    acc_scr[...] = jnp.zeros_like(acc_scr)
    st_scr[...] = jnp.zeros_like(st_scr)
    loglb, log1mlb = lbp_ref[0, 0:1, :], lbp_ref[0, 1:2, :]
    pos = lax.broadcasted_iota(jnp.int32, (tb, dk), 0) % sb
    nt = (((1,), (1,)), ((), ()))
    tn = (((0,), (0,)), ((), ()))
    dot = partial(jnp.dot, preferred_element_type=jnp.float32)

    def direction(base, f_ref, d):
        rows = pl.ds(base, tb)
        fp = f_ref[0, rows, :]
        ls = _log_sigmoid(fp)
        b2 = log1mlb + ls
        logf = jnp.maximum(loglb, b2) + jnp.log(1.0 + jnp.exp(-jnp.abs(loglb - b2)))
        lk = log1mlb + (ls - fp)
        v = v_ref[0, rows, :]
        qr = q_ref[0, rows, :]
        q = qr * (1.0 / (1.0 + jnp.exp(-qr)))
        r = dot(cum_ref[d], _split3(logf))
        cum, tot, pre = _sum3(r[0:tb], dk), _sum3(r[tb:2 * tb], dk), _sum3(r[2 * tb:3 * tb], dk)
        g = lk - cum
        cum3 = cum.reshape(n_c, c, dk)
        qe = (q * jnp.exp(cum)).astype(jnp.bfloat16)
        kt = jnp.exp(g + tot).astype(jnp.bfloat16)
        dec = jnp.exp(tot)
        vb = v.astype(jnp.bfloat16)
        yield
        o = jnp.sum(q * jnp.exp(lk), axis=-1, keepdims=True) * v
        for j in range(1, sb):
            sh = j if d == 0 else tb - j
            valid = (pos >= j) if d == 0 else (pos < sb - j)
            gj = pltpu.roll(g, sh, 0)
            vj = pltpu.roll(v, sh, 0)
            e = jnp.exp(jnp.where(valid, cum + gj, _NEG))
            o = o + jnp.sum(q * e, axis=-1, keepdims=True) * vj
        yield
        q4 = (q * jnp.exp(pre)).astype(jnp.bfloat16)
        att = None
        for i in range(n_sub - 1):
            ref_row = sb * (i + 1) - 1 if d == 0 else sb * (i + 1)
            r_i = jnp.broadcast_to(cum3[:, ref_row:ref_row + 1, :], (n_c, c, dk)).reshape(tb, dk)
            kh = jnp.exp(lk + jnp.minimum(r_i - cum, 0.0)).astype(jnp.bfloat16)
            a_i = lax.dot_general(q4, kh, nt, preferred_element_type=jnp.float32) * msk_ref[d, i]
            att = a_i if att is None else att + a_i
        o = o + dot(att.astype(jnp.bfloat16), vb)
        yield
        order = [i if d == 0 else n_c - 1 - i for i in range(n_c)]
        kvs = [lax.dot_general(vb[cc * c:(cc + 1) * c], kt[cc * c:(cc + 1) * c], tn,
                               preferred_element_type=jnp.float32) for cc in order]
        yield
        st = st_scr[d]
        sts = []
        for cc, kv in zip(order, kvs):
            sts.append(st.astype(jnp.bfloat16))
            st = st * dec[cc * c:cc * c + 1, :] + kv
        st_scr[d] = st
        o_st = [None] * n_c
        for cc, sb16 in zip(order, sts):
            o_st[cc] = lax.dot_general(qe[cc * c:(cc + 1) * c], sb16, nt, preferred_element_type=jnp.float32)
        acc_scr[rows, :] += o + jnp.concatenate(o_st, axis=0)

    def step(j, carry):
        base_f = pl.multiple_of(j * tb, tb)
        n_cb = l_ctx // tb
        jb = jnp.where(j < n_cb, n_cb - 1 - j, n_blk + n_cb - 1 - j)
        base_b = pl.multiple_of(jb * tb, tb)
        todo = [direction(base_f, ff_ref, 0), direction(base_b, fb_ref, 1)]
        while todo:
            todo = [gen for gen in todo if next(gen, True) is None]
        return carry

    lax.fori_loop(0, n_blk, step, 0)
    o = acc_scr[l_ctx:, :]
    g = g_ref[0, l_ctx:, :]
    o = o * lax.rsqrt(jnp.mean(o * o, axis=-1, keepdims=True) + EPS) * ng_ref[...]
    o_ref[0] = o * (g * (1.0 / (1.0 + jnp.exp(-g))))


def _hgrn_mix(u_all, lb, norm_g, l_ctx):
    bsz, l_tot, _ = u_all.shape
    nh = HG_HEADS
    n_blk = l_tot // HG_TB
    lbh = lb.reshape(nh, 1, HG_DK)
    lbp = jnp.concatenate([jnp.log(lbh), jnp.log1p(-lbh),
                           jnp.zeros((nh, _SUBLANES - 2, HG_DK), jnp.float32)], axis=1)
    cum_m, msk_m = _hgrn_consts()

    def slab(k):
        return pl.BlockSpec((1, l_tot, HG_DK), lambda b, h, k=k: (b, 0, k * nh + h))

    def const(a):
        return pl.BlockSpec(a.shape, lambda b, h: (0,) * a.ndim)

    return pl.pallas_call(
        partial(_hgrn_body, n_blk=n_blk, l_ctx=l_ctx),
        grid=(bsz, nh),
        in_specs=[slab(0), slab(1), slab(2), slab(4), slab(5),
                  pl.BlockSpec((1, _SUBLANES, HG_DK), lambda b, h: (h, 0, 0)),
                  pl.BlockSpec((1, HG_DK), lambda b, h: (0, 0)),
                  const(cum_m), const(msk_m)],
        out_specs=pl.BlockSpec((1, l_tot - l_ctx, HG_DK), lambda b, h: (b, 0, h)),
        out_shape=jax.ShapeDtypeStruct((bsz, l_tot - l_ctx, W_HALF), jnp.float32),
        scratch_shapes=[pltpu.VMEM((l_tot, HG_DK), jnp.float32), pltpu.VMEM((2, HG_DK, HG_DK), jnp.float32)],
        compiler_params=pltpu.CompilerParams(dimension_semantics=("parallel", "parallel"),
                                             vmem_limit_bytes=_VMEM_LIMIT),
        name="hgrn_mix",
    )(u_all, u_all, u_all, u_all, u_all, lbp, norm_g.reshape(1, HG_DK), cum_m, msk_m)


def _s5_weights(lam_re, lam_im, log_dt, b_re, b_im, c_re, c_im):
    t_len, f32 = S5_T, jnp.float32
    lre = jnp.minimum(lam_re, S5_MAX_RE)
    dt = jnp.exp(log_dt)[..., None]
    ar, ai = lre * dt, lam_im * dt
    k = jnp.arange(t_len + 1, dtype=f32)[:, None, None, None]
    mag = jnp.exp(k * ar)
    pre, pim = mag * jnp.cos(k * ai), mag * jnp.sin(k * ai)
    nr, ni, den = pre[1] - 1.0, pim[1], lre * lre + lam_im * lam_im
    cr, ci = (nr * lre + ni * lam_im) / den, (ni * lre - nr * lam_im) / den
    cbr = cr[..., None] * b_re - ci[..., None] * b_im
    cbi = cr[..., None] * b_im + ci[..., None] * b_re
    c_re_t, c_im_t = jnp.swapaxes(c_re, 1, 2), jnp.swapaxes(c_im, 1, 2)
    clre = c_re_t * pre[..., None] - c_im_t * pim[..., None]
    clim = c_re_t * pim[..., None] + c_im_t * pre[..., None]
    g = lam_re.shape[1]

    def w_dir(d, e):
        pr, pi = pre[e, d], pim[e, d]
        wr = pr[..., None] * cbr[d] - pi[..., None] * cbi[d]
        wi = pr[..., None] * cbi[d] + pi[..., None] * cbr[d]
        return [jnp.transpose(z, (1, 0, 3, 2)).reshape(g, t_len * S5_H, S5_P) for z in (wr, wi)]

    wf, wb = w_dir(0, jnp.arange(t_len - 1, -1, -1)), w_dir(1, jnp.arange(t_len))
    w = jnp.concatenate(wf + wf[::-1] + wb + wb[::-1], axis=-1)
    dec = jnp.concatenate([pre[t_len, 0], pre[t_len, 0], -pim[t_len, 0], pim[t_len, 0],
                           pre[t_len, 1], pre[t_len, 1], -pim[t_len, 1], pim[t_len, 1]], axis=-1)[:, None, :]

    def m_dir(d, e):
        mr, mi = clre[e, d], -clim[e, d]
        return [jnp.transpose(z, (1, 2, 0, 3)).reshape(g, S5_P, t_len * S5_H) for z in (mr, mi)]

    m = jnp.concatenate(m_dir(0, jnp.arange(1, t_len + 1)) + m_dir(1, jnp.arange(t_len, 0, -1)), axis=1)
    kk = (jnp.einsum('kdgpo,dgpi->kdgoi', clre[:t_len], cbr, precision=_HI)
          - jnp.einsum('kdgpo,dgpi->kdgoi', clim[:t_len], cbi, precision=_HI))
    delta = jnp.arange(t_len)[None, :] - jnp.arange(t_len)[:, None]
    kf = kk[jnp.clip(delta, 0, t_len - 1), 0]
    kb = kk[jnp.clip(-delta, 0, t_len - 1), 1]
    dl = delta[:, :, None, None, None]
    toep = jnp.where(dl > 0, kf, 0.0) + jnp.where(dl < 0, kb, 0.0) + jnp.where(dl == 0, kf + kb, 0.0)
    toep = jnp.transpose(toep, (2, 0, 4, 1, 3)).reshape(g, t_len * S5_H, t_len * S5_H)
    return w, dec, m, toep


def _s5_body(x_ref, w_ref, dec_ref, m_ref, toep_ref, y_ref, z_scr, s_scr, *, bsz, n_ctx, n_lat):
    p2 = 2 * S5_P
    n_all = n_ctx + n_lat
    xb = x_ref[0].astype(jnp.bfloat16)
    z_scr[...] = jnp.dot(xb, w_ref[0], preferred_element_type=jnp.float32)
    dec = jnp.broadcast_to(dec_ref[0], (bsz, 4 * p2))
    a1f, a2f = dec[:, 0:p2], dec[:, p2:2 * p2]
    a1b, a2b = dec[:, 2 * p2:3 * p2], dec[:, 3 * p2:4 * p2]

    def step(i, carry):
        sf, tf, sb, tb = carry
        rf = pl.multiple_of(i * bsz, bsz)
        cb = jnp.where(i < n_ctx, n_ctx - 1 - i, n_all + n_ctx - 1 - i)
        rb = pl.multiple_of(cb * bsz, bsz)
        s_scr[pl.ds(rf, bsz), 0:p2] = sf
        s_scr[pl.ds(rb, bsz), p2:2 * p2] = sb
        zf = z_scr[pl.ds(rf, bsz), 0:2 * p2]
        zb = z_scr[pl.ds(rb, bsz), 2 * p2:4 * p2]
        return (sf * a1f + tf * a2f + zf[:, :p2], tf * a1f - sf * a2f + zf[:, p2:],
                sb * a1b + tb * a2b + zb[:, :p2], tb * a1b - sb * a2b + zb[:, p2:])

    zero = jnp.zeros((bsz, p2), jnp.float32)
    lax.fori_loop(0, n_all, step, (zero, zero, zero, zero), unroll=4)
    r0 = n_ctx * bsz
    y_ref[0] = (jnp.dot(xb[r0:], toep_ref[0], preferred_element_type=jnp.float32)
                + jnp.dot(s_scr[r0:, :].astype(jnp.bfloat16), m_ref[0], preferred_element_type=jnp.float32))


def _s5_mix(u_c, u_l, lam_re, lam_im, log_dt, b_re, b_im, c_re, c_im):
    bsz, l_ctx, gh = u_c.shape
    l_lat = u_l.shape[1]
    g, t_len = gh // S5_H, S5_T
    n_ctx, n_lat = l_ctx // t_len, l_lat // t_len
    w, dec, m, toep = _s5_weights(lam_re, lam_im, log_dt, b_re, b_im, c_re, c_im)

    def chunked(u, n):
        return jnp.transpose(u.reshape(bsz, n, t_len, g, S5_H), (3, 1, 0, 2, 4)).reshape(g, n * bsz, t_len * S5_H)

    x = jnp.concatenate([chunked(u_c, n_ctx), chunked(u_l, n_lat)], axis=1)
    rows, k = (n_ctx + n_lat) * bsz, t_len * S5_H
    y = pl.pallas_call(
        partial(_s5_body, bsz=bsz, n_ctx=n_ctx, n_lat=n_lat),
        grid=(g,),
        in_specs=[pl.BlockSpec((1, rows, k), lambda i: (i, 0, 0)),
                  pl.BlockSpec((1, k, 8 * S5_P), lambda i: (i, 0, 0)),
                  pl.BlockSpec((1, 1, 8 * S5_P), lambda i: (i, 0, 0)),
                  pl.BlockSpec((1, 4 * S5_P, k), lambda i: (i, 0, 0)),
                  pl.BlockSpec((1, k, k), lambda i: (i, 0, 0))],
        out_specs=pl.BlockSpec((1, n_lat * bsz, k), lambda i: (i, 0, 0)),
        out_shape=jax.ShapeDtypeStruct((g, n_lat * bsz, k), jnp.float32),
        scratch_shapes=[pltpu.VMEM((rows, 8 * S5_P), jnp.float32), pltpu.VMEM((rows, 4 * S5_P), jnp.float32)],
        compiler_params=pltpu.CompilerParams(dimension_semantics=("parallel",), vmem_limit_bytes=_VMEM_LIMIT),
        name="s5_mix",
    )(x, w.astype(jnp.bfloat16), dec, m.astype(jnp.bfloat16), toep.astype(jnp.bfloat16))
    return jnp.transpose(y.reshape(g, n_lat, bsz, t_len, S5_H), (2, 1, 3, 0, 4)).reshape(bsz, l_lat, gh)


def _glu_body(y_ref, u_ref, d_ref, w_ref, b_ref, o_ref):
    y = _gelu_tanh(y_ref[0] + u_ref[0] * d_ref[...])
    z = jnp.dot(y.astype(jnp.bfloat16), w_ref[...], preferred_element_type=jnp.float32) + b_ref[...]
    o_ref[0] = y * (1.0 / (1.0 + jnp.exp(-z)))


def _s5_out(y, u_all, l_ctx, col0, d_skip, glu_w, glu_b):
    bsz, l, c = y.shape
    tm = l_ctx
    vec = pl.BlockSpec((1, c), lambda b, i: (0, 0))
    return pl.pallas_call(
        _glu_body,
        grid=(bsz, l // tm),
        in_specs=[pl.BlockSpec((1, tm, c), lambda b, i: (b, i, 0)),
                  pl.BlockSpec((1, tm, c), lambda b, i: (b, i + 1, col0 // c)), vec,
                  pl.BlockSpec((c, c), lambda b, i: (0, 0)), vec],
        out_specs=pl.BlockSpec((1, tm, c), lambda b, i: (b, i, 0)),
        out_shape=jax.ShapeDtypeStruct(y.shape, jnp.float32),
        compiler_params=pltpu.CompilerParams(dimension_semantics=("parallel", "parallel")),
        name="s5_glu",
    )(y, u_all, d_skip.reshape(1, c), glu_w.astype(jnp.bfloat16), glu_b.reshape(1, c))


def _odd_mixer(u, l_ctx, lb, hg_norm_g, lam_re, lam_im, log_dt, b_re, b_im, c_re, c_im, d_skip, glu_w, glu_b):
    w = W_HALF
    hg_l = _hgrn_mix(u, lb, hg_norm_g, l_ctx)
    s_c, s_l = u[:, :l_ctx, 3 * w:4 * w], u[:, l_ctx:, 3 * w:4 * w]
    y_l = _s5_mix(s_c, s_l, lam_re, lam_im, log_dt, b_re, b_im, c_re, c_im)
    return hg_l, _s5_out(y_l, u, l_ctx, 3 * w, d_skip, glu_w, glu_b)


def _router_body(x_ref, g_ref, sh_ref, sc_ref, wr_ref, h_ref, lg_ref):
    x = x_ref[0]
    h = x * lax.rsqrt(jnp.mean(x * x, axis=-1, keepdims=True) + EPS) * g_ref[...]
    h = h * (1.0 + sc_ref[0]) + sh_ref[0]
    hb = h.astype(jnp.bfloat16)
    lg_ref[0] = jnp.dot(hb, wr_ref[...], preferred_element_type=jnp.float32)
    bits = pltpu.bitcast(hb.astype(jnp.float32), jnp.uint32)
    half = bits.shape[1] // 2
    h_ref[0] = (bits[:, :half] >> 16) | (bits[:, half:] & jnp.uint32(0xFFFF0000))


def _router(x, norm_g, shift, scale, w_router):
    bsz, n, d = x.shape
    tm = min(n, 512)
    vec = pl.BlockSpec((1, 1, d), lambda b, i: (b, 0, 0))
    return pl.pallas_call(
        _router_body,
        grid=(bsz, n // tm),
        in_specs=[pl.BlockSpec((1, tm, d), lambda b, i: (b, i, 0)),
                  pl.BlockSpec((1, d), lambda b, i: (0, 0)), vec, vec,
                  pl.BlockSpec((d, N_EXPERTS), lambda b, i: (0, 0))],
        out_specs=[pl.BlockSpec((1, tm, d // 2), lambda b, i: (b, i, 0)),
                   pl.BlockSpec((1, tm, N_EXPERTS), lambda b, i: (b, i, 0))],
        out_shape=[jax.ShapeDtypeStruct((bsz, n, d // 2), jnp.uint32),
                   jax.ShapeDtypeStruct((bsz, n, N_EXPERTS), jnp.float32)],
        compiler_params=pltpu.CompilerParams(dimension_semantics=("parallel", "parallel"),
                                             vmem_limit_bytes=_VMEM_LIMIT),
        name="moe_router",
    )(x, norm_g.reshape(1, d), jnp.broadcast_to(shift, (bsz, 1, d)), jnp.broadcast_to(scale, (bsz, 1, d)),
      w_router.astype(jnp.bfloat16))


def _expert_body(idx_ref, nxt_ref, h_ref, gate_ref, mod_ref, wg_ref, wu_ref, wd_ref, y_ref, x_cur, x_nxt,
                 *, cap, grp, n_f):
    rows = grp * cap

    @pl.when(pl.program_id(1) == 0)
    def _():
        def grab(g, carry):
            base = pl.multiple_of(g * _SUBLANES, _SUBLANES)
            for r in range(_SUBLANES):
                x_nxt[pl.ds(base + r, 1), :] = h_ref[base // cap, pl.ds(idx_ref[0, 0, base + r], 1), :]
            return carry

        lax.fori_loop(0, rows // _SUBLANES, grab, 0)

    x_cur[...] = x_nxt[...]
    w32 = x_cur[...]
    xb = jnp.concatenate([pltpu.bitcast(w32 << 16, jnp.float32),
                          pltpu.bitcast(w32 & jnp.uint32(0xFFFF0000), jnp.float32)], axis=1).astype(jnp.bfloat16)
    for r in range(rows):
        x_nxt[pl.ds(r, 1), :] = h_ref[r // cap, pl.ds(nxt_ref[0, 0, r], 1), :]
    tf = wg_ref.shape[3] // n_f
    acc = None
    for f in range(n_f):
        cs = slice(f * tf, (f + 1) * tf)
        a = jnp.dot(xb, wg_ref[0, 0, :, cs], preferred_element_type=jnp.float32)
        u = jnp.dot(xb, wu_ref[0, 0, :, cs], preferred_element_type=jnp.float32)
        hh = (a * (1.0 / (1.0 + jnp.exp(-a))) * u).astype(jnp.bfloat16)
        part = jnp.dot(hh, wd_ref[0, 0, cs, :], preferred_element_type=jnp.float32)
        acc = part if acc is None else acc + part
    acc = acc * gate_ref[0]
    for g in range(grp):
        y_ref[g, 0] = (acc[g * cap:(g + 1) * cap] * mod_ref[g]).astype(jnp.bfloat16)


def _expert_ffn(h32, idx, gate, mod, layer, w_gate, w_up, w_down):
    bsz, n, dh = h32.shape
    d = 2 * dh
    cap = idx.shape[-1]
    _, e, _, ff = w_gate.shape
    grp = max(1, min(bsz, _EXPERT_ROWS // cap))
    ng = bsz // grp
    rows = grp * cap
    regroup = lambda t: t.reshape(ng, grp, e, cap).transpose(0, 2, 1, 3).reshape(ng * e, rows)
    idx3 = regroup(idx)[:, None, :]
    ge = lambda b, k: (b * e + k, 0, 0)
    ge_next = lambda b, k: (b * e + jnp.minimum(k + 1, e - 1), 0, 0)
    wspec = lambda s: pl.BlockSpec((1, 1) + s, lambda b, k: (layer, k, 0, 0))
    y = pl.pallas_call(
        partial(_expert_body, cap=cap, grp=grp, n_f=3),
        grid=(ng, e),
        in_specs=[pl.BlockSpec((1, 1, rows), ge, memory_space=pltpu.SMEM),
                  pl.BlockSpec((1, 1, rows), ge_next, memory_space=pltpu.SMEM),
                  pl.BlockSpec((grp, n, dh), lambda b, k: (b, 0, 0)),
                  pl.BlockSpec((1, rows, 1), ge),
                  pl.BlockSpec((grp, 1, d), lambda b, k: (b, 0, 0)),
                  wspec((d, ff)), wspec((d, ff)), wspec((ff, d))],
        out_specs=pl.BlockSpec((grp, 1, cap, d), lambda b, k: (b, k, 0, 0)),
        out_shape=jax.ShapeDtypeStruct((bsz, e, cap, d), jnp.bfloat16),
        scratch_shapes=[pltpu.VMEM((rows, dh), jnp.uint32), pltpu.VMEM((rows, dh), jnp.uint32)],
        compiler_params=pltpu.CompilerParams(dimension_semantics=("parallel", "arbitrary"),
                                             vmem_limit_bytes=_VMEM_LIMIT_BIG),
        name="moe_expert",
    )(idx3, idx3, h32, regroup(gate)[:, :, None], jnp.broadcast_to(mod, (bsz, 1, d)), w_gate, w_up, w_down)
    return y.reshape(bsz * e, cap, d)


def _combine_body(idx_ref, y_ref, o_ref, *, cap):
    @pl.when(pl.program_id(1) == 0)
    def _():
        o_ref[...] = jnp.zeros_like(o_ref)

    rows = 2 * _SUBLANES

    def group(g, carry):
        base = pl.multiple_of(g * rows, rows)
        ys = y_ref[0, pl.ds(base, rows), :].astype(jnp.float32)
        ts = [idx_ref[0, 0, base + r] for r in range(rows)]
        cur = [o_ref[0, pl.ds(t, 1), :] for t in ts]
        for r in range(rows):
            o_ref[0, pl.ds(ts[r], 1), :] = cur[r] + ys[r:r + 1, :]
        return carry

    lax.fori_loop(0, cap // rows, group, 0)


def _combine(y, idx, n):
    be, cap, d = y.shape
    bsz = be // N_EXPERTS
    return pl.pallas_call(
        partial(_combine_body, cap=cap),
        grid=(bsz, N_EXPERTS),
        in_specs=[pl.BlockSpec((1, 1, cap), lambda b, k: (b * N_EXPERTS + k, 0, 0), memory_space=pltpu.SMEM),
                  pl.BlockSpec((1, cap, d), lambda b, k: (b * N_EXPERTS + k, 0, 0))],
        out_specs=pl.BlockSpec((1, n, d), lambda b, k: (b, 0, 0)),
        out_shape=jax.ShapeDtypeStruct((bsz, n, d), jnp.float32),
        compiler_params=pltpu.CompilerParams(dimension_semantics=("parallel", "arbitrary"),
                                             vmem_limit_bytes=_VMEM_LIMIT),
        name="moe_combine",
    )(idx.reshape(be, 1, cap), y)


def _moe(x, norm_g, shift, scale, mod, w_router, layer, w_gate, w_up, w_down):
    bsz, n, d = x.shape
    cap = max(1, (EC_CAPACITY * n) // N_EXPERTS)
    h32, logits = _router(x, norm_g, shift, scale, w_router)
    aff = jax.nn.softmax(logits, axis=-1)
    gate, idx = lax.top_k(jnp.swapaxes(aff, 1, 2), cap)
    y = _expert_ffn(h32, idx, gate, mod, layer, w_gate, w_up, w_down)
    return _combine(y, idx, n)


def kernel(x, c, ctx, c_ctx, mod_w, mod_b, norm1_g, norm2_g, out_w, even_in_w, lru_conv_w, lru_conv_b, lru_wa, lru_ba, lru_wx, lru_bx, lru_lam, hy_conv_w, hy_conv_b, hy_w1, hy_b1, hy_w2, hy_b2, hy_w3, hy_freq, hy_skip, odd_in_w, hg_lb, hg_norm_g, s5_lam_re, s5_lam_im, s5_log_dt, s5_b_re, s5_b_im, s5_c_re, s5_c_im, s5_d, s5_glu_w, s5_glu_b, router_w, ex_w_gate, ex_w_up, ex_w_down, final_g):
    rows = x.shape[1] // GRID_W
    l_ctx = ctx.shape[1]
    p = jax.nn.softmax(hg_lb, axis=0)
    lb_all = jnp.clip(jnp.cumsum(p, axis=0) - p[0], 0.0, 1.0 - 1e-4)
    ex_wg, ex_wu, ex_wd = (t.astype(jnp.bfloat16) for t in (ex_w_gate, ex_w_up, ex_w_down))
    sc_lat = jax.nn.silu(c)
    sc_ctx = jax.nn.silu(c_ctx)[None]
    for i in range(DEPTH):
        need_ctx = i < DEPTH - 1
        j = i // 2
        ml = [t[:, None, :] for t in jnp.split(sc_lat @ mod_w[i] + mod_b[i], 6, axis=-1)]
        mc = [t[:, None, :] for t in jnp.split(sc_ctx @ mod_w[i] + mod_b[i], 6, axis=-1)]
        shift, scale, gate = (_two_part(mc[k], ml[k]) for k in range(3))
        if i % 2 == 0:
            x_all = jnp.concatenate([ctx, x], axis=1)
            u = _norm_mm(x_all, norm1_g[i], shift, scale, even_in_w[j], l_ctx)
            h_f, h_b, yb = _even_mixer(u, l_ctx, lru_conv_w[j], lru_conv_b[j], lru_wa[j], lru_ba[j], lru_wx[j],
                                       lru_bx[j], lru_lam[j], hy_conv_w[j], hy_conv_b[j], hy_w1[j], hy_b1[j],
                                       hy_w2[j], hy_b2[j], hy_w3[j], hy_freq[j], hy_skip[j])
            x_all = _proj_res_lru(h_f, h_b, u, yb, out_w[i], x_all, gate, l_ctx)
            ctx, x = x_all[:, :l_ctx], x_all[:, l_ctx:]
        else:
            assert not need_ctx
            x_all = jnp.concatenate([ctx, _to_col_major(x, rows)], axis=1)
            u = _norm_mm(x_all, norm1_g[i], shift, scale, odd_in_w[j], l_ctx)
            hg_l, s5_l = _odd_mixer(u, l_ctx, lb_all[i], hg_norm_g[j], s5_lam_re[j], s5_lam_im[j],
                                    s5_log_dt[j], s5_b_re[j], s5_b_im[j], s5_c_re[j], s5_c_im[j], s5_d[j],
                                    s5_glu_w[j], s5_glu_b[j])
            x = _to_row_major(_proj_res(hg_l, s5_l, out_w[i], x_all, gate, l_ctx, l_ctx), rows)
        ex = (router_w[i], i, ex_wg, ex_wu, ex_wd)
        moe_l = _moe(x, norm2_g[i], ml[3], ml[4], ml[5], *ex)
        if need_ctx:
            x = x + moe_l
            ctx = ctx + _moe(ctx, norm2_g[i], mc[3], mc[4], mc[5], *ex)
    return _add_norm(x, moe_l, final_g)
```

```python
import math
from functools import partial

import numpy as np
import jax
import jax.numpy as jnp
from jax import lax
from jax.experimental import pallas as pl
from jax.experimental.pallas import tpu as pltpu

D_MODEL = 1024
DEPTH = 2
GRID_W = 64
W_HALF = D_MODEL // 2
EPS = 1e-6
LRU_C = 8.0
HY_ORDER = 2
HY_BANDS = 16
HY_MAX_DECAY = math.log(1e-2) / 0.3
HY_MIN_DECAY = math.log(1e-2) / 1.5
HG_HEADS = 4
HG_DK = W_HALF // HG_HEADS
HG_C = 16
HG_SUB = 4
HG_TB = 256
S5_H = 16
S5_P = 64
S5_MAX_RE = -1e-4
S5_T = 16
_DFT_LOW = 64
N_EXPERTS = 16
EC_CAPACITY = 2

_SUBLANES = 8
_DW_ROWS = 256
_EXPERT_ROWS = 512
_VMEM_LIMIT = 48 * 1024 * 1024
_VMEM_LIMIT_BIG = 56 * 1024 * 1024
_HI = lax.Precision.HIGHEST
_NEG = -1e30


def _mm_body(a_ref, b_ref, o_ref):
    o_ref[...] = jnp.dot(a_ref[...].astype(jnp.bfloat16), b_ref[...],
                         preferred_element_type=jnp.float32)


def _pick(n, pref):
    for t in pref:
        if n % t == 0:
            return t
    return n


def _mm(a, b):
    m, k = a.shape
    n = b.shape[1]
    tm = _pick(m, (512, 256, 128, 64, 32, 16, 8))
    tn = _pick(n, (512, 256, 128))
    return pl.pallas_call(
        _mm_body,
        grid=(m // tm, n // tn),
        in_specs=[pl.BlockSpec((tm, k), lambda i, j: (i, 0)),
                  pl.BlockSpec((k, tn), lambda i, j: (0, j))],
        out_specs=pl.BlockSpec((tm, tn), lambda i, j: (i, j)),
        out_shape=jax.ShapeDtypeStruct((m, n), jnp.float32),
        compiler_params=pltpu.CompilerParams(
            dimension_semantics=("parallel", "parallel"), vmem_limit_bytes=_VMEM_LIMIT),
        name="mm",
    )(a, b.astype(jnp.bfloat16))


def _add_norm_body(x_ref, m_ref, g_ref, o_ref):
    x = x_ref[...] + m_ref[...]
    o_ref[...] = x * lax.rsqrt(jnp.mean(x * x, axis=-1, keepdims=True) + EPS) * g_ref[...]


def _add_norm(x, m, g):
    bsz, n, d = x.shape
    tm = min(n, 512)
    blk = pl.BlockSpec((1, tm, d), lambda b, i: (b, i, 0))
    return pl.pallas_call(
        _add_norm_body,
        grid=(bsz, n // tm),
        in_specs=[blk, blk, pl.BlockSpec((1, 1, d), lambda b, i: (0, 0, 0))],
        out_specs=blk,
        out_shape=jax.ShapeDtypeStruct(x.shape, jnp.float32),
        compiler_params=pltpu.CompilerParams(dimension_semantics=("parallel", "parallel")),
        name="add_norm",
    )(x, m, g.reshape(1, 1, d))


def _two_part(t_ctx, t_lat):
    bsz, _, d = t_lat.shape
    return jnp.stack([jnp.broadcast_to(t_ctx, (bsz, 1, d)), t_lat], axis=1).reshape(2 * bsz, 1, d)


def _norm_mm_body(x_ref, g_ref, sh_ref, sc_ref, w_ref, o_ref):
    x = x_ref[0]
    h = x * lax.rsqrt(jnp.mean(x * x, axis=-1, keepdims=True) + EPS) * g_ref[...]
    h = h * (1.0 + sc_ref[0]) + sh_ref[0]
    o_ref[0] = jnp.dot(h.astype(jnp.bfloat16), w_ref[...], preferred_element_type=jnp.float32)


def _norm_mm(x_all, norm_g, shift, scale, w, l_ctx):
    bsz, l_tot, d = x_all.shape
    n = w.shape[1]
    tm = l_ctx
    vec = pl.BlockSpec((1, 1, d), lambda b, j: (2 * b + jnp.minimum(j, 1), 0, 0))
    return pl.pallas_call(
        _norm_mm_body,
        grid=(bsz, l_tot // tm),
        in_specs=[pl.BlockSpec((1, tm, d), lambda b, j: (b, j, 0)),
                  pl.BlockSpec((1, d), lambda b, j: (0, 0)), vec, vec,
                  pl.BlockSpec((d, n), lambda b, j: (0, 0))],
        out_specs=pl.BlockSpec((1, tm, n), lambda b, j: (b, j, 0)),
        out_shape=jax.ShapeDtypeStruct((bsz, l_tot, n), jnp.float32),
        compiler_params=pltpu.CompilerParams(dimension_semantics=("parallel", "parallel"),
                                             vmem_limit_bytes=_VMEM_LIMIT),
        name="norm_mm",
    )(x_all, norm_g.reshape(1, d), shift, scale, w.astype(jnp.bfloat16))


def _proj_res_body(ya_ref, yb_ref, wa_ref, wb_ref, x_ref, gt_ref, o_ref):
    y = (jnp.dot(ya_ref[0].astype(jnp.bfloat16), wa_ref[...], preferred_element_type=jnp.float32)
         + jnp.dot(yb_ref[0].astype(jnp.bfloat16), wb_ref[...], preferred_element_type=jnp.float32))
    o_ref[0] = x_ref[0] + gt_ref[0] * y


def _proj_res_lru_body(hf_ref, hb_ref, u_ref, yb_ref, wa_ref, wb_ref, x_ref, gt_ref, o_ref):
    ya = (hf_ref[0] + hb_ref[0]) * _gelu_tanh(u_ref[0])
    y = (jnp.dot(ya.astype(jnp.bfloat16), wa_ref[...], preferred_element_type=jnp.float32)
         + jnp.dot(yb_ref[0].astype(jnp.bfloat16), wb_ref[...], preferred_element_type=jnp.float32))
    o_ref[0] = x_ref[0] + gt_ref[0] * y


def _proj_res_lru(h_f, h_b, u_all, yb, w, x_all, gate, l_ctx):
    bsz, l_tot, ka = h_f.shape
    d = w.shape[1]
    tm = l_ctx
    blk = lambda k, c=0: pl.BlockSpec((1, tm, k), lambda b, j: (b, j, c))
    return pl.pallas_call(
        _proj_res_lru_body,
        grid=(bsz, l_tot // tm),
        in_specs=[blk(ka), blk(ka), blk(ka, 1), blk(yb.shape[2]),
                  pl.BlockSpec((ka, d), lambda b, j: (0, 0)), pl.BlockSpec((yb.shape[2], d), lambda b, j: (0, 0)),
                  blk(d), pl.BlockSpec((1, 1, d), lambda b, j: (2 * b + jnp.minimum(j, 1), 0, 0))],
        out_specs=blk(d),
        out_shape=jax.ShapeDtypeStruct((bsz, l_tot, d), jnp.float32),
        compiler_params=pltpu.CompilerParams(dimension_semantics=("parallel", "parallel"),
                                             vmem_limit_bytes=_VMEM_LIMIT),
        name="proj_res_lru",
    )(h_f, h_b, u_all, yb, w[:ka].astype(jnp.bfloat16), w[ka:].astype(jnp.bfloat16), x_all, gate)


def _proj_res(ya, yb, w, x_all, gate, l_ctx, row0):
    bsz, l_y, ka = ya.shape
    d = w.shape[1]
    tm = l_ctx
    j0 = row0 // tm
    yblk = lambda k: pl.BlockSpec((1, tm, k), lambda b, j: (b, j, 0))
    return pl.pallas_call(
        _proj_res_body,
        grid=(bsz, l_y // tm),
        in_specs=[yblk(ka), yblk(yb.shape[2]),
                  pl.BlockSpec((ka, d), lambda b, j: (0, 0)), pl.BlockSpec((yb.shape[2], d), lambda b, j: (0, 0)),
                  pl.BlockSpec((1, tm, d), lambda b, j: (b, j + j0, 0)),
                  pl.BlockSpec((1, 1, d), lambda b, j: (2 * b + jnp.minimum(j + j0, 1), 0, 0))],
        out_specs=pl.BlockSpec((1, tm, d), lambda b, j: (b, j, 0)),
        out_shape=jax.ShapeDtypeStruct((bsz, l_y, d), jnp.float32),
        compiler_params=pltpu.CompilerParams(dimension_semantics=("parallel", "parallel"),
                                             vmem_limit_bytes=_VMEM_LIMIT),
        name="proj_res",
    )(ya, yb, w[:ka].astype(jnp.bfloat16), w[ka:].astype(jnp.bfloat16), x_all, gate)


def _gelu_tanh(x):
    return 0.5 * x * (1.0 + jnp.tanh(math.sqrt(2.0 / math.pi) * (x + 0.044715 * (x * x * x))))


def _dwconv_body(u_ref, w_ref, o_ref, *rest, l_ctx, taps, left, split):
    l_tot = u_ref.shape[1]
    tb, h = _DW_ROWS, _SUBLANES
    row = lax.broadcasted_iota(jnp.int32, (tb, 1), 0)

    def block(j, carry):
        r0 = pl.multiple_of(j * tb, tb)
        lo = jnp.where(r0 < l_ctx, 0, l_ctx)
        hi = jnp.where(r0 < l_ctx, l_ctx, l_tot)
        prev = u_ref[0, pl.ds(pl.multiple_of(jnp.maximum(r0 - h, 0), h), h), :]
        nxt = u_ref[0, pl.ds(pl.multiple_of(jnp.minimum(r0 + tb, l_tot - h), h), h), :]
        ext = jnp.concatenate([prev, u_ref[0, pl.ds(r0, tb), :], nxt], axis=0)
        y = jnp.broadcast_to(w_ref[0, taps:taps + 1, :], (tb, ext.shape[1]))
        for t in range(taps):
            off = t - left
            src = r0 + row + off
            y = y + jnp.where((src >= lo) & (src < hi), ext[h + off:h + off + tb, :], 0.0) * w_ref[0, t:t + 1, :]
        if not split:
            o_ref[0, pl.ds(r0, tb), :] = y
        else:
            ob_c, o_l, ob_l = rest

            @pl.when(r0 < l_ctx)
            def _():
                o_ref[0, pl.ds(r0, tb), :] = y
                ob_c[0, pl.ds(r0, tb), :] = y.astype(jnp.bfloat16)

            @pl.when(r0 >= l_ctx)
            def _():
                o_l[0, pl.ds(r0 - l_ctx, tb), :] = y
                ob_l[0, pl.ds(r0 - l_ctx, tb), :] = y.astype(jnp.bfloat16)
        return carry

    lax.fori_loop(0, l_tot // tb, block, 0)


def _dwconv_slabs(u_all, col0, w, b, l_ctx, split):
    bsz, l_tot, _ = u_all.shape
    taps, nc = w.shape
    c = W_HALF
    n = nc // c
    wb = jnp.concatenate([w, b[None], jnp.zeros((_SUBLANES - taps - 1, nc), jnp.float32)], axis=0)
    wb = wb.reshape(_SUBLANES, n, c).transpose(1, 0, 2)
    if split:
        rows = (l_ctx, l_ctx, l_tot - l_ctx, l_tot - l_ctx)
        dts = (jnp.float32, jnp.bfloat16, jnp.float32, jnp.bfloat16)
    else:
        rows, dts = (l_tot,), (jnp.float32,)
    out = pl.pallas_call(
        partial(_dwconv_body, l_ctx=l_ctx, taps=taps, left=(taps - 1) // 2, split=split),
        grid=(bsz, n),
        in_specs=[pl.BlockSpec((1, l_tot, c), lambda bb, s: (bb, 0, s + col0 // c)),
                  pl.BlockSpec((1, _SUBLANES, c), lambda bb, s: (s, 0, 0))],
        out_specs=[pl.BlockSpec((1, r, c), lambda bb, s: (bb, 0, s)) for r in rows],
        out_shape=[jax.ShapeDtypeStruct((bsz, r, nc), dt) for r, dt in zip(rows, dts)],
        compiler_params=pltpu.CompilerParams(dimension_semantics=("parallel", "parallel"),
                                             vmem_limit_bytes=_VMEM_LIMIT),
        name="dwconv",
    )(u_all, wb)
    return out if split else out[0]


def _to_col_major(t, rows):
    bsz, l, ch = t.shape
    return t.reshape(bsz, rows, GRID_W, ch).transpose(0, 2, 1, 3).reshape(bsz, l, ch)


def _to_row_major(t, rows):
    bsz, l, ch = t.shape
    return t.reshape(bsz, GRID_W, rows, ch).transpose(0, 2, 1, 3).reshape(bsz, l, ch)


def _lru_body(xf_ref, xb_ref, wa_ref, wx_ref, par_ref, hf_ref, hb_ref,
              af_scr, bf_scr, ab_scr, bb_scr, sf_scr, sb_scr, *, tb):
    @pl.when(pl.program_id(1) == 0)
    def _():
        sf_scr[...] = jnp.zeros_like(sf_scr)
        sb_scr[...] = jnp.zeros_like(sb_scr)

    def coeffs(x, d, a_scr, b_scr):
        xb = x.astype(jnp.bfloat16)
        r = jnp.dot(xb, wa_ref[d], preferred_element_type=jnp.float32) + par_ref[d, 0:1, :]
        i = jnp.dot(xb, wx_ref[d], preferred_element_type=jnp.float32) + par_ref[d, 1:2, :]
        r = 1.0 / (1.0 + jnp.exp(-r))
        i = 1.0 / (1.0 + jnp.exp(-i))
        log_a = -LRU_C * r * par_ref[d, 2:3, :]
        a = jnp.exp(log_a)
        a_scr[...] = a
        b_scr[...] = jnp.sqrt(-jnp.tanh(log_a) * (a * a + 1.0)) * i * x

    coeffs(xf_ref[0], 0, af_scr, bf_scr)
    coeffs(xb_ref[0], 1, ab_scr, bb_scr)

    def tile(i, carry):
        hf, hb = carry
        base_f = pl.multiple_of(i * _SUBLANES, _SUBLANES)
        base_b = pl.multiple_of(tb - _SUBLANES - i * _SUBLANES, _SUBLANES)
        for r in range(_SUBLANES):
            rf = pl.ds(base_f + r, 1)
            hf = af_scr[rf, :] * hf + bf_scr[rf, :]
            hf_ref[0, rf, :] = hf
            rb = pl.ds(base_b + (_SUBLANES - 1 - r), 1)
            hb = ab_scr[rb, :] * hb + bb_scr[rb, :]
            hb_ref[0, rb, :] = hb
        return hf, hb

    hf, hb = lax.fori_loop(0, tb // _SUBLANES, tile, (sf_scr[...], sb_scr[...]))
    sf_scr[...] = hf
    sb_scr[...] = hb


def _block_diag(w):
    two, h, n, _ = w.shape
    eye = jnp.eye(h, dtype=w.dtype)
    return (w[:, :, :, None, :] * eye[None, :, None, :, None]).reshape(two, h * n, h * n)


def _lru_scan(xc, l_ctx, wa, ba, wx, bx, lam):
    bsz, l_tot, ch = xc.shape
    tb = l_ctx
    nb = l_tot // tb
    par = jnp.stack([ba.reshape(2, ch), bx.reshape(2, ch), jax.nn.softplus(-lam).reshape(2, ch)], axis=1)
    par = jnp.concatenate([par, jnp.zeros((2, _SUBLANES - 3, ch), jnp.float32)], axis=1)
    fwd = pl.BlockSpec((1, tb, ch), lambda b, j: (b, j, 0))
    bwd = pl.BlockSpec((1, tb, ch), lambda b, j: (b, jnp.where(j == 0, 0, nb - j), 0))
    wspec = pl.BlockSpec((2, ch, ch), lambda b, j: (0, 0, 0))
    return pl.pallas_call(
        partial(_lru_body, tb=tb),
        grid=(bsz, nb),
        in_specs=[fwd, bwd, wspec, wspec, pl.BlockSpec((2, _SUBLANES, ch), lambda b, j: (0, 0, 0))],
        out_specs=[fwd, bwd],
        out_shape=[jax.ShapeDtypeStruct(xc.shape, jnp.float32)] * 2,
        scratch_shapes=[pltpu.VMEM((tb, ch), jnp.float32)] * 4 + [pltpu.VMEM((1, ch), jnp.float32)] * 2,
        compiler_params=pltpu.CompilerParams(dimension_semantics=("parallel", "arbitrary")),
        name="lru_scan",
    )(xc, xc, _block_diag(wa).astype(jnp.bfloat16), _block_diag(wx).astype(jnp.bfloat16), par)


def _hyena_filters(l, w1, b1, w2, b2, w3, freq):
    t = jnp.arange(l, dtype=jnp.float32)
    t_unit = t / max(l - 1, 1)
    bands = jnp.linspace(1e-4, HY_BANDS - 1, HY_BANDS, dtype=jnp.float32)
    ang = (2.0 * math.pi / l) * t[:, None] * bands[None, :]
    z = jnp.concatenate([t_unit[:, None], jnp.cos(ang), -jnp.sin(ang)], axis=-1)
    h = jnp.sin(freq * (z @ w1 + b1))
    h = jnp.sin(freq * (h @ w2 + b2))
    h = h @ w3
    dist = jnp.abs(t - (l // 2)) / (l / 2.0)
    deltas = jnp.abs(jnp.linspace(HY_MIN_DECAY, HY_MAX_DECAY, HY_ORDER * W_HALF, dtype=jnp.float32))
    h = h * jnp.exp(-dist[:, None] * deltas[None, :])
    h = h * lax.rsqrt(jnp.sum(h * h, axis=0, keepdims=True) + EPS)
    return h


def _dft_mats(l, tf):
    n = 2 * l
    nl = _DFT_LOW
    nh = l // nl
    t = jnp.arange(l, dtype=jnp.int32)
    w = 2.0 * math.pi / n

    def tables(tv):
        ah = (((nl * jnp.arange(nh, dtype=jnp.int32))[:, None] * tv[None, :]) % n).astype(jnp.float32) * w
        al = ((jnp.arange(nl, dtype=jnp.int32)[:, None] * tv[None, :]) % n).astype(jnp.float32) * w
        return jnp.cos(ah), jnp.sin(ah), jnp.cos(al), jnp.sin(al)

    ch, sh, cl, sl = tables(t)
    f_re = (ch[:, None, :] * cl[None] - sh[:, None, :] * sl[None]).reshape(l, l)
    f_im = -(sh[:, None, :] * cl[None] + ch[:, None, :] * sl[None]).reshape(l, l)
    f_im = f_im.at[0].set(jnp.where(t % 2 == 0, 1.0, -1.0))
    f = jnp.concatenate([f_re.reshape(l // tf, tf, l), f_im.reshape(l // tf, tf, l)], axis=1).reshape(n, l)
    tp = t + l // 2
    ch, sh, cl, sl = (z.T for z in tables(tp))
    g_re = (2.0 / n) * (ch[:, :, None] * cl[:, None, :] - sh[:, :, None] * sl[:, None, :]).reshape(l, l)
    g_im = (-2.0 / n) * (sh[:, :, None] * cl[:, None, :] + ch[:, :, None] * sl[:, None, :]).reshape(l, l)
    g_re = g_re.at[:, 0].set(1.0 / n)
    g_im = g_im.at[:, 0].set(jnp.where(tp % 2 == 0, 1.0, -1.0) / n)
    g = jnp.stack([g_re.reshape(l, l // tf, tf), g_im.reshape(l, l // tf, tf)], axis=2).reshape(l, n)
    return f.astype(jnp.bfloat16), g.astype(jnp.bfloat16)


def _spec_body(f_ref, u_ref, a_ref, b_ref, d_ref, y_ref, *, tf):
    acc = jnp.dot(f_ref[...], u_ref[0], preferred_element_type=jnp.float32)
    xr, xi = acc[:tf], acc[tf:]
    y_ref[0, :tf, :] = (xr * a_ref[...] - xi * b_ref[...]).astype(jnp.bfloat16)
    y_ref[0, tf:, :] = (xr * b_ref[...] + xi * d_ref[...]).astype(jnp.bfloat16)


def _spec_mul(f, u, a, b, d, tf):
    bsz, l, _ = u.shape
    c = a.shape[1]
    coef = pl.BlockSpec((tf, c), lambda i, bb: (i, 0))
    return pl.pallas_call(
        partial(_spec_body, tf=tf),
        grid=(l // tf, bsz),
        in_specs=[pl.BlockSpec((2 * tf, l), lambda i, bb: (i, 0)),
                  pl.BlockSpec((1, l, c), lambda i, bb: (bb, 0, 0)), coef, coef, coef],
        out_specs=pl.BlockSpec((1, 2 * tf, c), lambda i, bb: (bb, i, 0)),
        out_shape=jax.ShapeDtypeStruct((bsz, 2 * l, c), jnp.bfloat16),
        compiler_params=pltpu.CompilerParams(dimension_semantics=("parallel", "parallel"),
                                             vmem_limit_bytes=_VMEM_LIMIT),
        name="hy_spec",
    )(f, u, a, b, d)


def _inv_body(g_ref, y_ref, u_ref, x_ref, skip_ref, z_ref, zb_ref):
    acc = jnp.dot(g_ref[...], y_ref[0], preferred_element_type=jnp.float32)
    z = x_ref[0] * (acc + u_ref[0] * skip_ref[...])
    z_ref[0] = z
    zb_ref[0] = z.astype(jnp.bfloat16)


def _inv_gate(g, y, u, ucol, xg, xcol, skip, tm):
    bsz, n, c = y.shape
    l = n // 2
    blk = pl.BlockSpec((1, tm, c), lambda bb, i: (bb, i, 0))
    return pl.pallas_call(
        _inv_body,
        grid=(bsz, l // tm),
        in_specs=[pl.BlockSpec((tm, n), lambda bb, i: (i, 0)),
                  pl.BlockSpec((1, n, c), lambda bb, i: (bb, 0, 0)),
                  pl.BlockSpec((1, tm, c), lambda bb, i: (bb, i, ucol)),
                  pl.BlockSpec((1, tm, c), lambda bb, i: (bb, i, xcol)),
                  pl.BlockSpec((1, c), lambda bb, i: (0, 0))],
        out_specs=[blk, blk],
        out_shape=[jax.ShapeDtypeStruct((bsz, l, c), jnp.float32), jax.ShapeDtypeStruct((bsz, l, c), jnp.bfloat16)],
        compiler_params=pltpu.CompilerParams(dimension_semantics=("parallel", "parallel"),
                                             vmem_limit_bytes=_VMEM_LIMIT),
        name="hy_inv",
    )(g, y, u, xg, skip.reshape(1, c))


def _hyena_conv2(hc, hb, filt, skip):
    l, c = hc.shape[1], hc.shape[2] // 3
    tf = min(512, l // 2)
    tm = min(512, l)
    f, g = _dft_mats(l, tf)
    hs = _mm(f, filt)
    hs = hs.reshape(l // tf, 2, tf, 2 * c)
    h_re, h_im = hs[:, 0].reshape(l, 2 * c), hs[:, 1].reshape(l, 2 * c)
    slot0 = (jnp.arange(l) == 0)[:, None]
    a = h_re
    b = jnp.where(slot0, 0.0, h_im)
    d = jnp.where(slot0, h_im, h_re)
    y = _spec_mul(f, hb, a[:, :c], b[:, :c], d[:, :c], tf)
    z, zb = _inv_gate(g, y, hc, 0, hc, 1, skip[0], tm)
    y = _spec_mul(f, zb, a[:, c:], b[:, c:], d[:, c:], tf)
    z, _ = _inv_gate(g, y, z, 0, hc, 2, skip[1], tm)
    return z


def _hyena(u, l_ctx, conv_w, conv_b, w1, b1, w2, b2, w3, freq, skip):
    hc_c, hb_c, hc_l, hb_l = _dwconv_slabs(u, 2 * W_HALF, conv_w, conv_b, l_ctx, True)
    zs = []
    for hc, hb in ((hc_c, hb_c), (hc_l, hb_l)):
        filt = _hyena_filters(hc.shape[1], w1, b1, w2, b2, w3, freq)
        zs.append(_hyena_conv2(hc, hb, filt, skip))
    return jnp.concatenate(zs, axis=1)


def _even_mixer(u, l_ctx, lru_conv_w, lru_conv_b, lru_wa, lru_ba, lru_wx, lru_bx,
                lru_lam, hy_conv_w, hy_conv_b, hy_w1, hy_b1, hy_w2, hy_b2, hy_w3, hy_freq, hy_skip):
    w = W_HALF
    xc = _dwconv_slabs(u, 0, lru_conv_w, lru_conv_b, l_ctx, False)
    h_f, h_b = _lru_scan(xc, l_ctx, lru_wa, lru_ba, lru_wx, lru_bx, lru_lam)
    yb = _hyena(u, l_ctx, hy_conv_w, hy_conv_b, hy_w1, hy_b1, hy_w2, hy_b2, hy_w3, hy_freq, hy_skip)
    return h_f, h_b, yb


def _split3(x):
    hi = x.astype(jnp.bfloat16)
    r1 = x - hi.astype(jnp.float32)
    mid = r1.astype(jnp.bfloat16)
    lo = (r1 - mid.astype(jnp.float32)).astype(jnp.bfloat16)
    return jnp.concatenate([hi, mid, lo], axis=1)


def _sum3(r, c):
    return r[:, 0:c] + r[:, c:2 * c] + r[:, 2 * c:3 * c]


def _log_sigmoid(x):
    return jnp.minimum(x, 0.0) - jnp.log(1.0 + jnp.exp(-jnp.abs(x)))


def _hgrn_consts():
    tb, c, sb = HG_TB, HG_C, HG_SUB
    r = np.arange(tb)
    same = (r[:, None] // c) == (r[None, :] // c)
    same4 = (r[:, None] // sb) == (r[None, :] // sb)
    le = r[None, :] <= r[:, None]
    sub = (r % c) // sb
    cum, msk = [], []
    for d in range(2):
        tri = le if d == 0 else le.T
        cum.append(np.concatenate([same & tri, same, same4 & tri], axis=0))
        if d == 0:
            mm = [same & (sub[:, None] == i + 1) & (sub[None, :] <= i) for i in range(c // sb - 1)]
        else:
            mm = [same & (sub[:, None] == i) & (sub[None, :] > i) for i in range(c // sb - 1)]
        msk.append(np.stack(mm))
    cum = jnp.asarray(np.stack(cum).astype(np.float32), dtype=jnp.bfloat16)
    return cum, jnp.asarray(np.stack(msk).astype(np.float32))


def _hgrn_body(ff_ref, fb_ref, v_ref, q_ref, g_ref, lbp_ref, ng_ref, cum_ref, msk_ref, o_ref,
               acc_scr, st_scr, *, n_blk, l_ctx):
    tb, c, sb, dk = HG_TB, HG_C, HG_SUB, HG_DK
    n_c = tb // c
    n_sub = c // sb
    acc_scr[...] = jnp.zeros_like(acc_scr)
    st_scr[...] = jnp.zeros_like(st_scr)
    loglb, log1mlb = lbp_ref[0, 0:1, :], lbp_ref[0, 1:2, :]
    pos = lax.broadcasted_iota(jnp.int32, (tb, dk), 0) % sb
    nt = (((1,), (1,)), ((), ()))
    tn = (((0,), (0,)), ((), ()))
    dot = partial(jnp.dot, preferred_element_type=jnp.float32)

    def direction(base, f_ref, d):
        rows = pl.ds(base, tb)
        fp = f_ref[0, rows, :]
        ls = _log_sigmoid(fp)
        b2 = log1mlb + ls
        logf = jnp.maximum(loglb, b2) + jnp.log(1.0 + jnp.exp(-jnp.abs(loglb - b2)))
        lk = log1mlb + (ls - fp)
        v = v_ref[0, rows, :]
        qr = q_ref[0, rows, :]
        q = qr * (1.0 / (1.0 + jnp.exp(-qr)))
        r = dot(cum_ref[d], _split3(logf))
        cum, tot, pre = _sum3(r[0:tb], dk), _sum3(r[tb:2 * tb], dk), _sum3(r[2 * tb:3 * tb], dk)
        g = lk - cum
        cum3 = cum.reshape(n_c, c, dk)
        qe = (q * jnp.exp(cum)).astype(jnp.bfloat16)
        kt = jnp.exp(g + tot).astype(jnp.bfloat16)
        dec = jnp.exp(tot)
        vb = v.astype(jnp.bfloat16)
        yield
        o = jnp.sum(q * jnp.exp(lk), axis=-1, keepdims=True) * v
        for j in range(1, sb):
            sh = j if d == 0 else tb - j
            valid = (pos >= j) if d == 0 else (pos < sb - j)
            gj = pltpu.roll(g, sh, 0)
            vj = pltpu.roll(v, sh, 0)
            e = jnp.exp(jnp.where(valid, cum + gj, _NEG))
            o = o + jnp.sum(q * e, axis=-1, keepdims=True) * vj
        yield
        q4 = (q * jnp.exp(pre)).astype(jnp.bfloat16)
        att = None
        for i in range(n_sub - 1):
            ref_row = sb * (i + 1) - 1 if d == 0 else sb * (i + 1)
            r_i = jnp.broadcast_to(cum3[:, ref_row:ref_row + 1, :], (n_c, c, dk)).reshape(tb, dk)
            kh = jnp.exp(lk + jnp.minimum(r_i - cum, 0.0)).astype(jnp.bfloat16)
            a_i = lax.dot_general(q4, kh, nt, preferred_element_type=jnp.float32) * msk_ref[d, i]
            att = a_i if att is None else att + a_i
        o = o + dot(att.astype(jnp.bfloat16), vb)
        yield
        order = [i if d == 0 else n_c - 1 - i for i in range(n_c)]
        kvs = [lax.dot_general(vb[cc * c:(cc + 1) * c], kt[cc * c:(cc + 1) * c], tn,
                               preferred_element_type=jnp.float32) for cc in order]
        yield
        st = st_scr[d]
        sts = []
        for cc, kv in zip(order, kvs):
            sts.append(st.astype(jnp.bfloat16))
            st = st * dec[cc * c:cc * c + 1, :] + kv
        st_scr[d] = st
        o_st = [None] * n_c
        for cc, sb16 in zip(order, sts):
            o_st[cc] = lax.dot_general(qe[cc * c:(cc + 1) * c], sb16, nt, preferred_element_type=jnp.float32)
        acc_scr[rows, :] += o + jnp.concatenate(o_st, axis=0)

    def step(j, carry):
        base_f = pl.multiple_of(j * tb, tb)
        jb = jnp.where(j == 0, 0, n_blk - j)
        base_b = pl.multiple_of(jb * tb, tb)
        todo = [direction(base_f, ff_ref, 0), direction(base_b, fb_ref, 1)]
        while todo:
            todo = [gen for gen in todo if next(gen, True) is None]
        return carry

    lax.fori_loop(0, n_blk, step, 0)
    o = acc_scr[l_ctx:, :]
    g = g_ref[0, l_ctx:, :]
    o = o * lax.rsqrt(jnp.mean(o * o, axis=-1, keepdims=True) + EPS) * ng_ref[...]
    o_ref[0] = o * (g * (1.0 / (1.0 + jnp.exp(-g))))


def _hgrn_mix(u_all, lb, norm_g, l_ctx):
    bsz, l_tot, _ = u_all.shape
    nh = HG_HEADS
    n_blk = l_tot // HG_TB
    lbh = lb.reshape(nh, 1, HG_DK)
    lbp = jnp.concatenate([jnp.log(lbh), jnp.log1p(-lbh),
                           jnp.zeros((nh, _SUBLANES - 2, HG_DK), jnp.float32)], axis=1)
    cum_m, msk_m = _hgrn_consts()

    def slab(k):
        return pl.BlockSpec((1, l_tot, HG_DK), lambda b, h, k=k: (b, 0, k * nh + h))

    def const(a):
        return pl.BlockSpec(a.shape, lambda b, h: (0,) * a.ndim)

    return pl.pallas_call(
        partial(_hgrn_body, n_blk=n_blk, l_ctx=l_ctx),
        grid=(bsz, nh),
        in_specs=[slab(0), slab(1), slab(2), slab(4), slab(5),
                  pl.BlockSpec((1, _SUBLANES, HG_DK), lambda b, h: (h, 0, 0)),
                  pl.BlockSpec((1, HG_DK), lambda b, h: (0, 0)),
                  const(cum_m), const(msk_m)],
        out_specs=pl.BlockSpec((1, l_tot - l_ctx, HG_DK), lambda b, h: (b, 0, h)),
        out_shape=jax.ShapeDtypeStruct((bsz, l_tot - l_ctx, W_HALF), jnp.float32),
        scratch_shapes=[pltpu.VMEM((l_tot, HG_DK), jnp.float32), pltpu.VMEM((2, HG_DK, HG_DK), jnp.float32)],
        compiler_params=pltpu.CompilerParams(dimension_semantics=("parallel", "parallel"),
                                             vmem_limit_bytes=_VMEM_LIMIT),
        name="hgrn_mix",
    )(u_all, u_all, u_all, u_all, u_all, lbp, norm_g.reshape(1, HG_DK), cum_m, msk_m)


def _s5_weights(lam_re, lam_im, log_dt, b_re, b_im, c_re, c_im):
    t_len, f32 = S5_T, jnp.float32
    lre = jnp.minimum(lam_re, S5_MAX_RE)
    dt = jnp.exp(log_dt)[..., None]
    ar, ai = lre * dt, lam_im * dt
    k = jnp.arange(t_len + 1, dtype=f32)[:, None, None, None]
    mag = jnp.exp(k * ar)
    pre, pim = mag * jnp.cos(k * ai), mag * jnp.sin(k * ai)
    nr, ni, den = pre[1] - 1.0, pim[1], lre * lre + lam_im * lam_im
    cr, ci = (nr * lre + ni * lam_im) / den, (ni * lre - nr * lam_im) / den
    cbr = cr[..., None] * b_re - ci[..., None] * b_im
    cbi = cr[..., None] * b_im + ci[..., None] * b_re
    c_re_t, c_im_t = jnp.swapaxes(c_re, 1, 2), jnp.swapaxes(c_im, 1, 2)
    clre = c_re_t * pre[..., None] - c_im_t * pim[..., None]
    clim = c_re_t * pim[..., None] + c_im_t * pre[..., None]
    g = lam_re.shape[1]

    def w_dir(d, e):
        pr, pi = pre[e, d], pim[e, d]
        wr = pr[..., None] * cbr[d] - pi[..., None] * cbi[d]
        wi = pr[..., None] * cbi[d] + pi[..., None] * cbr[d]
        return [jnp.transpose(z, (1, 0, 3, 2)).reshape(g, t_len * S5_H, S5_P) for z in (wr, wi)]

    wf, wb = w_dir(0, jnp.arange(t_len - 1, -1, -1)), w_dir(1, jnp.arange(t_len))
    w = jnp.concatenate(wf + wf[::-1] + wb + wb[::-1], axis=-1)
    dec = jnp.concatenate([pre[t_len, 0], pre[t_len, 0], -pim[t_len, 0], pim[t_len, 0],
                           pre[t_len, 1], pre[t_len, 1], -pim[t_len, 1], pim[t_len, 1]], axis=-1)[:, None, :]

    def m_dir(d, e):
        mr, mi = clre[e, d], -clim[e, d]
        return [jnp.transpose(z, (1, 2, 0, 3)).reshape(g, S5_P, t_len * S5_H) for z in (mr, mi)]

    m = jnp.concatenate(m_dir(0, jnp.arange(1, t_len + 1)) + m_dir(1, jnp.arange(t_len, 0, -1)), axis=1)
    kk = (jnp.einsum('kdgpo,dgpi->kdgoi', clre[:t_len], cbr, precision=_HI)
          - jnp.einsum('kdgpo,dgpi->kdgoi', clim[:t_len], cbi, precision=_HI))
    delta = jnp.arange(t_len)[None, :] - jnp.arange(t_len)[:, None]
    kf = kk[jnp.clip(delta, 0, t_len - 1), 0]
    kb = kk[jnp.clip(-delta, 0, t_len - 1), 1]
    dl = delta[:, :, None, None, None]
    toep = jnp.where(dl > 0, kf, 0.0) + jnp.where(dl < 0, kb, 0.0) + jnp.where(dl == 0, kf + kb, 0.0)
    toep = jnp.transpose(toep, (2, 0, 4, 1, 3)).reshape(g, t_len * S5_H, t_len * S5_H)
    return w, dec, m, toep


def _s5_body(x_ref, w_ref, dec_ref, m_ref, toep_ref, y_ref, z_scr, s_scr, *, bsz, n_ctx, n_lat):
    p2 = 2 * S5_P
    n_all = n_ctx + n_lat
    xb = x_ref[0].astype(jnp.bfloat16)
    z_scr[...] = jnp.dot(xb, w_ref[0], preferred_element_type=jnp.float32)
    dec = jnp.broadcast_to(dec_ref[0], (bsz, 4 * p2))
    a1f, a2f = dec[:, 0:p2], dec[:, p2:2 * p2]
    a1b, a2b = dec[:, 2 * p2:3 * p2], dec[:, 3 * p2:4 * p2]

    def step(i, carry):
        sf, tf, sb, tb = carry
        rf = pl.multiple_of(i * bsz, bsz)
        cb = jnp.where(i < n_ctx, n_ctx - 1 - i, n_all + n_ctx - 1 - i)
        rb = pl.multiple_of(cb * bsz, bsz)
        s_scr[pl.ds(rf, bsz), 0:p2] = sf
        s_scr[pl.ds(rb, bsz), p2:2 * p2] = sb
        zf = z_scr[pl.ds(rf, bsz), 0:2 * p2]
        zb = z_scr[pl.ds(rb, bsz), 2 * p2:4 * p2]
        return (sf * a1f + tf * a2f + zf[:, :p2], tf * a1f - sf * a2f + zf[:, p2:],
                sb * a1b + tb * a2b + zb[:, :p2], tb * a1b - sb * a2b + zb[:, p2:])

    zero = jnp.zeros((bsz, p2), jnp.float32)
    lax.fori_loop(0, n_all, step, (zero, zero, zero, zero), unroll=4)
    r0 = n_ctx * bsz
    y_ref[0] = (jnp.dot(xb[r0:], toep_ref[0], preferred_element_type=jnp.float32)
                + jnp.dot(s_scr[r0:, :].astype(jnp.bfloat16), m_ref[0], preferred_element_type=jnp.float32))


def _s5_mix(u_c, u_l, lam_re, lam_im, log_dt, b_re, b_im, c_re, c_im):
    bsz, l_ctx, gh = u_c.shape
    l_lat = u_l.shape[1]
    g, t_len = gh // S5_H, S5_T
    n_ctx, n_lat = l_ctx // t_len, l_lat // t_len
    w, dec, m, toep = _s5_weights(lam_re, lam_im, log_dt, b_re, b_im, c_re, c_im)

    def chunked(u, n):
        return jnp.transpose(u.reshape(bsz, n, t_len, g, S5_H), (3, 1, 0, 2, 4)).reshape(g, n * bsz, t_len * S5_H)

    x = jnp.concatenate([chunked(u_c, n_ctx), chunked(u_l, n_lat)], axis=1)
    rows, k = (n_ctx + n_lat) * bsz, t_len * S5_H
    y = pl.pallas_call(
        partial(_s5_body, bsz=bsz, n_ctx=n_ctx, n_lat=n_lat),
        grid=(g,),
        in_specs=[pl.BlockSpec((1, rows, k), lambda i: (i, 0, 0)),
                  pl.BlockSpec((1, k, 8 * S5_P), lambda i: (i, 0, 0)),
                  pl.BlockSpec((1, 1, 8 * S5_P), lambda i: (i, 0, 0)),
                  pl.BlockSpec((1, 4 * S5_P, k), lambda i: (i, 0, 0)),
                  pl.BlockSpec((1, k, k), lambda i: (i, 0, 0))],
        out_specs=pl.BlockSpec((1, n_lat * bsz, k), lambda i: (i, 0, 0)),
        out_shape=jax.ShapeDtypeStruct((g, n_lat * bsz, k), jnp.float32),
        scratch_shapes=[pltpu.VMEM((rows, 8 * S5_P), jnp.float32), pltpu.VMEM((rows, 4 * S5_P), jnp.float32)],
        compiler_params=pltpu.CompilerParams(dimension_semantics=("parallel",), vmem_limit_bytes=_VMEM_LIMIT),
        name="s5_mix",
    )(x, w.astype(jnp.bfloat16), dec, m.astype(jnp.bfloat16), toep.astype(jnp.bfloat16))
    return jnp.transpose(y.reshape(g, n_lat, bsz, t_len, S5_H), (2, 1, 3, 0, 4)).reshape(bsz, l_lat, gh)


def _glu_body(y_ref, u_ref, d_ref, w_ref, b_ref, o_ref):
    y = _gelu_tanh(y_ref[0] + u_ref[0] * d_ref[...])
    z = jnp.dot(y.astype(jnp.bfloat16), w_ref[...], preferred_element_type=jnp.float32) + b_ref[...]
    o_ref[0] = y * (1.0 / (1.0 + jnp.exp(-z)))


def _s5_out(y, u_all, l_ctx, col0, d_skip, glu_w, glu_b):
    bsz, l, c = y.shape
    tm = l_ctx
    vec = pl.BlockSpec((1, c), lambda b, i: (0, 0))
    return pl.pallas_call(
        _glu_body,
        grid=(bsz, l // tm),
        in_specs=[pl.BlockSpec((1, tm, c), lambda b, i: (b, i, 0)),
                  pl.BlockSpec((1, tm, c), lambda b, i: (b, i + 1, col0 // c)), vec,
                  pl.BlockSpec((c, c), lambda b, i: (0, 0)), vec],
        out_specs=pl.BlockSpec((1, tm, c), lambda b, i: (b, i, 0)),
        out_shape=jax.ShapeDtypeStruct(y.shape, jnp.float32),
        compiler_params=pltpu.CompilerParams(dimension_semantics=("parallel", "parallel")),
        name="s5_glu",
    )(y, u_all, d_skip.reshape(1, c), glu_w.astype(jnp.bfloat16), glu_b.reshape(1, c))


def _odd_mixer(u, l_ctx, lb, hg_norm_g, lam_re, lam_im, log_dt, b_re, b_im, c_re, c_im, d_skip, glu_w, glu_b):
    w = W_HALF
    hg_l = _hgrn_mix(u, lb, hg_norm_g, l_ctx)
    s_c, s_l = u[:, :l_ctx, 3 * w:4 * w], u[:, l_ctx:, 3 * w:4 * w]
    y_l = _s5_mix(s_c, s_l, lam_re, lam_im, log_dt, b_re, b_im, c_re, c_im)
    return hg_l, _s5_out(y_l, u, l_ctx, 3 * w, d_skip, glu_w, glu_b)


def _router_body(x_ref, g_ref, sh_ref, sc_ref, wr_ref, h_ref, lg_ref):
    x = x_ref[0]
    h = x * lax.rsqrt(jnp.mean(x * x, axis=-1, keepdims=True) + EPS) * g_ref[...]
    h = h * (1.0 + sc_ref[0]) + sh_ref[0]
    hb = h.astype(jnp.bfloat16)
    lg_ref[0] = jnp.dot(hb, wr_ref[...], preferred_element_type=jnp.float32)
    bits = pltpu.bitcast(hb.astype(jnp.float32), jnp.uint32)
    half = bits.shape[1] // 2
    h_ref[0] = (bits[:, :half] >> 16) | (bits[:, half:] & jnp.uint32(0xFFFF0000))


def _router(x, norm_g, shift, scale, w_router):
    bsz, n, d = x.shape
    tm = min(n, 512)
    vec = pl.BlockSpec((1, 1, d), lambda b, i: (b, 0, 0))
    return pl.pallas_call(
        _router_body,
        grid=(bsz, n // tm),
        in_specs=[pl.BlockSpec((1, tm, d), lambda b, i: (b, i, 0)),
                  pl.BlockSpec((1, d), lambda b, i: (0, 0)), vec, vec,
                  pl.BlockSpec((d, N_EXPERTS), lambda b, i: (0, 0))],
        out_specs=[pl.BlockSpec((1, tm, d // 2), lambda b, i: (b, i, 0)),
                   pl.BlockSpec((1, tm, N_EXPERTS), lambda b, i: (b, i, 0))],
        out_shape=[jax.ShapeDtypeStruct((bsz, n, d // 2), jnp.uint32),
                   jax.ShapeDtypeStruct((bsz, n, N_EXPERTS), jnp.float32)],
        compiler_params=pltpu.CompilerParams(dimension_semantics=("parallel", "parallel"),
                                             vmem_limit_bytes=_VMEM_LIMIT),
        name="moe_router",
    )(x, norm_g.reshape(1, d), jnp.broadcast_to(shift, (bsz, 1, d)), jnp.broadcast_to(scale, (bsz, 1, d)),
      w_router.astype(jnp.bfloat16))


def _expert_body(idx_ref, nxt_ref, h_ref, gate_ref, mod_ref, wg_ref, wu_ref, wd_ref, y_ref, x_cur, x_nxt,
                 *, cap, grp, n_f):
    rows = grp * cap

    @pl.when(pl.program_id(1) == 0)
    def _():
        def grab(g, carry):
            base = pl.multiple_of(g * _SUBLANES, _SUBLANES)
            for r in range(_SUBLANES):
                x_nxt[pl.ds(base + r, 1), :] = h_ref[base // cap, pl.ds(idx_ref[0, 0, base + r], 1), :]
            return carry

        lax.fori_loop(0, rows // _SUBLANES, grab, 0)

    x_cur[...] = x_nxt[...]
    w32 = x_cur[...]
    xb = jnp.concatenate([pltpu.bitcast(w32 << 16, jnp.float32),
                          pltpu.bitcast(w32 & jnp.uint32(0xFFFF0000), jnp.float32)], axis=1).astype(jnp.bfloat16)
    for r in range(rows):
        x_nxt[pl.ds(r, 1), :] = h_ref[r // cap, pl.ds(nxt_ref[0, 0, r], 1), :]
    tf = wg_ref.shape[3] // n_f
    acc = None
    for f in range(n_f):
        cs = slice(f * tf, (f + 1) * tf)
        a = jnp.dot(xb, wg_ref[0, 0, :, cs], preferred_element_type=jnp.float32)
        u = jnp.dot(xb, wu_ref[0, 0, :, cs], preferred_element_type=jnp.float32)
        hh = (a * (1.0 / (1.0 + jnp.exp(-a))) * u).astype(jnp.bfloat16)
        part = jnp.dot(hh, wd_ref[0, 0, cs, :], preferred_element_type=jnp.float32)
        acc = part if acc is None else acc + part
    acc = acc * gate_ref[0]
    for g in range(grp):
        y_ref[g, 0] = acc[g * cap:(g + 1) * cap] * mod_ref[g]


def _expert_ffn(h32, idx, gate, mod, layer, w_gate, w_up, w_down):
    bsz, n, dh = h32.shape
    d = 2 * dh
    cap = idx.shape[-1]
    _, e, _, ff = w_gate.shape
    grp = max(1, min(bsz, _EXPERT_ROWS // cap))
    ng = bsz // grp
    rows = grp * cap
    regroup = lambda t: t.reshape(ng, grp, e, cap).transpose(0, 2, 1, 3).reshape(ng * e, rows)
    idx3 = regroup(idx)[:, None, :]
    ge = lambda b, k: (b * e + k, 0, 0)
    ge_next = lambda b, k: (b * e + jnp.minimum(k + 1, e - 1), 0, 0)
    wspec = lambda s: pl.BlockSpec((1, 1) + s, lambda b, k: (layer, k, 0, 0))
    y = pl.pallas_call(
        partial(_expert_body, cap=cap, grp=grp, n_f=3),
        grid=(ng, e),
        in_specs=[pl.BlockSpec((1, 1, rows), ge, memory_space=pltpu.SMEM),
                  pl.BlockSpec((1, 1, rows), ge_next, memory_space=pltpu.SMEM),
                  pl.BlockSpec((grp, n, dh), lambda b, k: (b, 0, 0)),
                  pl.BlockSpec((1, rows, 1), ge),
                  pl.BlockSpec((grp, 1, d), lambda b, k: (b, 0, 0)),
                  wspec((d, ff)), wspec((d, ff)), wspec((ff, d))],
        out_specs=pl.BlockSpec((grp, 1, cap, d), lambda b, k: (b, k, 0, 0)),
        out_shape=jax.ShapeDtypeStruct((bsz, e, cap, d), jnp.float32),
        scratch_shapes=[pltpu.VMEM((rows, dh), jnp.uint32), pltpu.VMEM((rows, dh), jnp.uint32)],
        compiler_params=pltpu.CompilerParams(dimension_semantics=("parallel", "arbitrary"),
                                             vmem_limit_bytes=_VMEM_LIMIT_BIG),
        name="moe_expert",
    )(idx3, idx3, h32, regroup(gate)[:, :, None], jnp.broadcast_to(mod, (bsz, 1, d)), w_gate, w_up, w_down)
    return y.reshape(bsz * e, cap, d)


def _combine_body(idx_ref, y_ref, o_ref, *, cap):
    @pl.when(pl.program_id(1) == 0)
    def _():
        o_ref[...] = jnp.zeros_like(o_ref)

    def group(g, carry):
        base = pl.multiple_of(g * _SUBLANES, _SUBLANES)
        ys = y_ref[0, pl.ds(base, _SUBLANES), :]
        ts = [idx_ref[0, 0, base + r] for r in range(_SUBLANES)]
        cur = [o_ref[0, pl.ds(t, 1), :] for t in ts]
        for r in range(_SUBLANES):
            o_ref[0, pl.ds(ts[r], 1), :] = cur[r] + ys[r:r + 1, :]
        return carry

    lax.fori_loop(0, cap // _SUBLANES, group, 0)


def _combine(y, idx, n):
    be, cap, d = y.shape
    bsz = be // N_EXPERTS
    return pl.pallas_call(
        partial(_combine_body, cap=cap),
        grid=(bsz, N_EXPERTS),
        in_specs=[pl.BlockSpec((1, 1, cap), lambda b, k: (b * N_EXPERTS + k, 0, 0), memory_space=pltpu.SMEM),
                  pl.BlockSpec((1, cap, d), lambda b, k: (b * N_EXPERTS + k, 0, 0))],
        out_specs=pl.BlockSpec((1, n, d), lambda b, k: (b, 0, 0)),
        out_shape=jax.ShapeDtypeStruct((bsz, n, d), jnp.float32),
        compiler_params=pltpu.CompilerParams(dimension_semantics=("parallel", "arbitrary"),
                                             vmem_limit_bytes=_VMEM_LIMIT),
        name="moe_combine",
    )(idx.reshape(be, 1, cap), y)


def _moe(x, norm_g, shift, scale, mod, w_router, layer, w_gate, w_up, w_down):
    bsz, n, d = x.shape
    cap = max(1, (EC_CAPACITY * n) // N_EXPERTS)
    h32, logits = _router(x, norm_g, shift, scale, w_router)
    aff = jax.nn.softmax(logits, axis=-1)
    gate, idx = lax.top_k(jnp.swapaxes(aff, 1, 2), cap)
    y = _expert_ffn(h32, idx, gate, mod, layer, w_gate, w_up, w_down)
    return _combine(y, idx, n)


def kernel(x, c, ctx, c_ctx, mod_w, mod_b, norm1_g, norm2_g, out_w, even_in_w, lru_conv_w, lru_conv_b, lru_wa, lru_ba, lru_wx, lru_bx, lru_lam, hy_conv_w, hy_conv_b, hy_w1, hy_b1, hy_w2, hy_b2, hy_w3, hy_freq, hy_skip, odd_in_w, hg_lb, hg_norm_g, s5_lam_re, s5_lam_im, s5_log_dt, s5_b_re, s5_b_im, s5_c_re, s5_c_im, s5_d, s5_glu_w, s5_glu_b, router_w, ex_w_gate, ex_w_up, ex_w_down, final_g):
    rows = x.shape[1] // GRID_W
    l_ctx = ctx.shape[1]
    p = jax.nn.softmax(hg_lb, axis=0)
    lb_all = jnp.clip(jnp.cumsum(p, axis=0) - p[0], 0.0, 1.0 - 1e-4)
    ex_wg, ex_wu, ex_wd = (t.astype(jnp.bfloat16) for t in (ex_w_gate, ex_w_up, ex_w_down))
    sc_lat = jax.nn.silu(c)
    sc_ctx = jax.nn.silu(c_ctx)[None]
    for i in range(DEPTH):
        need_ctx = i < DEPTH - 1
        j = i // 2
        ml = [t[:, None, :] for t in jnp.split(sc_lat @ mod_w[i] + mod_b[i], 6, axis=-1)]
        mc = [t[:, None, :] for t in jnp.split(sc_ctx @ mod_w[i] + mod_b[i], 6, axis=-1)]
        shift, scale, gate = (_two_part(mc[k], ml[k]) for k in range(3))
        if i % 2 == 0:
            x_all = jnp.concatenate([ctx, x], axis=1)
            u = _norm_mm(x_all, norm1_g[i], shift, scale, even_in_w[j], l_ctx)
            h_f, h_b, yb = _even_mixer(u, l_ctx, lru_conv_w[j], lru_conv_b[j], lru_wa[j], lru_ba[j], lru_wx[j],
                                       lru_bx[j], lru_lam[j], hy_conv_w[j], hy_conv_b[j], hy_w1[j], hy_b1[j],
                                       hy_w2[j], hy_b2[j], hy_w3[j], hy_freq[j], hy_skip[j])
            x_all = _proj_res_lru(h_f, h_b, u, yb, out_w[i], x_all, gate, l_ctx)
            ctx, x = x_all[:, :l_ctx], x_all[:, l_ctx:]
        else:
            assert not need_ctx
            x_all = jnp.concatenate([ctx, _to_col_major(x, rows)], axis=1)
            u = _norm_mm(x_all, norm1_g[i], shift, scale, odd_in_w[j], l_ctx)
            hg_l, s5_l = _odd_mixer(u, l_ctx, lb_all[i], hg_norm_g[j], s5_lam_re[j], s5_lam_im[j],
                                    s5_log_dt[j], s5_b_re[j], s5_b_im[j], s5_c_re[j], s5_c_im[j], s5_d[j],
                                    s5_glu_w[j], s5_glu_b[j])
            x = _to_row_major(_proj_res(hg_l, s5_l, out_w[i], x_all, gate, l_ctx, l_ctx), rows)
        ex = (router_w[i], i, ex_wg, ex_wu, ex_wd)
        moe_l = _moe(x, norm2_g[i], ml[3], ml[4], ml[5], *ex)
        if need_ctx:
            x = x + moe_l
            ctx = ctx + _moe(ctx, norm2_g[i], mc[3], mc[4], mc[5], *ex)
    return _add_norm(x, moe_l, final_g)
```

```python
import math
from functools import partial

import numpy as np
import jax
import jax.numpy as jnp
from jax import lax
from jax.experimental import pallas as pl
from jax.experimental.pallas import tpu as pltpu

D_MODEL = 1024
DEPTH = 2
GRID_W = 64
W_HALF = D_MODEL // 2
EPS = 1e-6
LRU_C = 8.0
HY_ORDER = 2
HY_BANDS = 16
HY_MAX_DECAY = math.log(1e-2) / 0.3
HY_MIN_DECAY = math.log(1e-2) / 1.5
HG_HEADS = 4
HG_DK = W_HALF // HG_HEADS
HG_C = 16
HG_SUB = 4
HG_TB = 256
S5_H = 16
S5_P = 64
S5_MAX_RE = -1e-4
S5_T = 32
_DFT_LOW = 64
N_EXPERTS = 16
EC_CAPACITY = 2

_SUBLANES = 8
_DW_ROWS = 256
_EXPERT_ROWS = 512
_VMEM_LIMIT = 48 * 1024 * 1024
_VMEM_LIMIT_BIG = 56 * 1024 * 1024
_HI = lax.Precision.HIGHEST
_NEG = -1e30


def _mm_body(a_ref, b_ref, o_ref):
    o_ref[...] = jnp.dot(a_ref[...].astype(jnp.bfloat16), b_ref[...],
                         preferred_element_type=jnp.float32)


def _pick(n, pref):
    for t in pref:
        if n % t == 0:
            return t
    return n


def _mm(a, b):
    m, k = a.shape
    n = b.shape[1]
    tm = _pick(m, (512, 256, 128, 64, 32, 16, 8))
    tn = _pick(n, (512, 256, 128))
    return pl.pallas_call(
        _mm_body,
        grid=(m // tm, n // tn),
        in_specs=[pl.BlockSpec((tm, k), lambda i, j: (i, 0)),
                  pl.BlockSpec((k, tn), lambda i, j: (0, j))],
        out_specs=pl.BlockSpec((tm, tn), lambda i, j: (i, j)),
        out_shape=jax.ShapeDtypeStruct((m, n), jnp.float32),
        compiler_params=pltpu.CompilerParams(
            dimension_semantics=("parallel", "parallel"), vmem_limit_bytes=_VMEM_LIMIT),
        name="mm",
    )(a, b.astype(jnp.bfloat16))


def _add_norm_body(x_ref, m_ref, g_ref, o_ref):
    x = x_ref[...] + m_ref[...]
    o_ref[...] = x * lax.rsqrt(jnp.mean(x * x, axis=-1, keepdims=True) + EPS) * g_ref[...]


def _add_norm(x, m, g):
    bsz, n, d = x.shape
    tm = min(n, 512)
    blk = pl.BlockSpec((1, tm, d), lambda b, i: (b, i, 0))
    return pl.pallas_call(
        _add_norm_body,
        grid=(bsz, n // tm),
        in_specs=[blk, blk, pl.BlockSpec((1, 1, d), lambda b, i: (0, 0, 0))],
        out_specs=blk,
        out_shape=jax.ShapeDtypeStruct(x.shape, jnp.float32),
        compiler_params=pltpu.CompilerParams(dimension_semantics=("parallel", "parallel")),
        name="add_norm",
    )(x, m, g.reshape(1, 1, d))


def _two_part(t_ctx, t_lat):
    bsz, _, d = t_lat.shape
    return jnp.stack([jnp.broadcast_to(t_ctx, (bsz, 1, d)), t_lat], axis=1).reshape(2 * bsz, 1, d)


def _norm_mm_body(x_ref, g_ref, sh_ref, sc_ref, w_ref, o_ref):
    x = x_ref[0]
    h = x * lax.rsqrt(jnp.mean(x * x, axis=-1, keepdims=True) + EPS) * g_ref[...]
    h = h * (1.0 + sc_ref[0]) + sh_ref[0]
    o_ref[0] = jnp.dot(h.astype(jnp.bfloat16), w_ref[...], preferred_element_type=jnp.float32)


def _norm_mm(x_all, norm_g, shift, scale, w, l_ctx):
    bsz, l_tot, d = x_all.shape
    n = w.shape[1]
    tm = l_ctx
    vec = pl.BlockSpec((1, 1, d), lambda b, j: (2 * b + jnp.minimum(j, 1), 0, 0))
    return pl.pallas_call(
        _norm_mm_body,
        grid=(bsz, l_tot // tm),
        in_specs=[pl.BlockSpec((1, tm, d), lambda b, j: (b, j, 0)),
                  pl.BlockSpec((1, d), lambda b, j: (0, 0)), vec, vec,
                  pl.BlockSpec((d, n), lambda b, j: (0, 0))],
        out_specs=pl.BlockSpec((1, tm, n), lambda b, j: (b, j, 0)),
        out_shape=jax.ShapeDtypeStruct((bsz, l_tot, n), jnp.float32),
        compiler_params=pltpu.CompilerParams(dimension_semantics=("parallel", "parallel"),
                                             vmem_limit_bytes=_VMEM_LIMIT),
        name="norm_mm",
    )(x_all, norm_g.reshape(1, d), shift, scale, w.astype(jnp.bfloat16))


def _proj_res_body(ya_ref, yb_ref, wa_ref, wb_ref, x_ref, gt_ref, o_ref):
    y = (jnp.dot(ya_ref[0].astype(jnp.bfloat16), wa_ref[...], preferred_element_type=jnp.float32)
         + jnp.dot(yb_ref[0].astype(jnp.bfloat16), wb_ref[...], preferred_element_type=jnp.float32))
    o_ref[0] = x_ref[0] + gt_ref[0] * y


def _proj_res_lru_body(hf_ref, hb_ref, u_ref, yb_ref, wa_ref, wb_ref, x_ref, gt_ref, o_ref):
    ya = (hf_ref[0] + hb_ref[0]) * _gelu_tanh(u_ref[0])
    y = (jnp.dot(ya.astype(jnp.bfloat16), wa_ref[...], preferred_element_type=jnp.float32)
         + jnp.dot(yb_ref[0].astype(jnp.bfloat16), wb_ref[...], preferred_element_type=jnp.float32))
    o_ref[0] = x_ref[0] + gt_ref[0] * y


def _proj_res_lru(h_f, h_b, u_all, yb, w, x_all, gate, l_ctx):
    bsz, l_tot, ka = h_f.shape
    d = w.shape[1]
    tm = l_ctx
    blk = lambda k, c=0: pl.BlockSpec((1, tm, k), lambda b, j: (b, j, c))
    return pl.pallas_call(
        _proj_res_lru_body,
        grid=(bsz, l_tot // tm),
        in_specs=[blk(ka), blk(ka), blk(ka, 1), blk(yb.shape[2]),
                  pl.BlockSpec((ka, d), lambda b, j: (0, 0)), pl.BlockSpec((yb.shape[2], d), lambda b, j: (0, 0)),
                  blk(d), pl.BlockSpec((1, 1, d), lambda b, j: (2 * b + jnp.minimum(j, 1), 0, 0))],
        out_specs=blk(d),
        out_shape=jax.ShapeDtypeStruct((bsz, l_tot, d), jnp.float32),
        compiler_params=pltpu.CompilerParams(dimension_semantics=("parallel", "parallel"),
                                             vmem_limit_bytes=_VMEM_LIMIT),
        name="proj_res_lru",
    )(h_f, h_b, u_all, yb, w[:ka].astype(jnp.bfloat16), w[ka:].astype(jnp.bfloat16), x_all, gate)


def _proj_res(ya, yb, w, x_all, gate, l_ctx, row0):
    bsz, l_y, ka = ya.shape
    d = w.shape[1]
    tm = l_ctx
    j0 = row0 // tm
    yblk = lambda k: pl.BlockSpec((1, tm, k), lambda b, j: (b, j, 0))
    return pl.pallas_call(
        _proj_res_body,
        grid=(bsz, l_y // tm),
        in_specs=[yblk(ka), yblk(yb.shape[2]),
                  pl.BlockSpec((ka, d), lambda b, j: (0, 0)), pl.BlockSpec((yb.shape[2], d), lambda b, j: (0, 0)),
                  pl.BlockSpec((1, tm, d), lambda b, j: (b, j + j0, 0)),
                  pl.BlockSpec((1, 1, d), lambda b, j: (2 * b + jnp.minimum(j + j0, 1), 0, 0))],
        out_specs=pl.BlockSpec((1, tm, d), lambda b, j: (b, j, 0)),
        out_shape=jax.ShapeDtypeStruct((bsz, l_y, d), jnp.float32),
        compiler_params=pltpu.CompilerParams(dimension_semantics=("parallel", "parallel"),
                                             vmem_limit_bytes=_VMEM_LIMIT),
        name="proj_res",
    )(ya, yb, w[:ka].astype(jnp.bfloat16), w[ka:].astype(jnp.bfloat16), x_all, gate)


def _gelu_tanh(x):
    return 0.5 * x * (1.0 + jnp.tanh(math.sqrt(2.0 / math.pi) * (x + 0.044715 * (x * x * x))))


def _dwconv_body(u_ref, w_ref, o_ref, *rest, l_ctx, taps, left, split):
    l_tot = u_ref.shape[1]
    tb, h = _DW_ROWS, _SUBLANES
    row = lax.broadcasted_iota(jnp.int32, (tb, 1), 0)

    def block(j, carry):
        r0 = pl.multiple_of(j * tb, tb)
        lo = jnp.where(r0 < l_ctx, 0, l_ctx)
        hi = jnp.where(r0 < l_ctx, l_ctx, l_tot)
        prev = u_ref[0, pl.ds(pl.multiple_of(jnp.maximum(r0 - h, 0), h), h), :]
        nxt = u_ref[0, pl.ds(pl.multiple_of(jnp.minimum(r0 + tb, l_tot - h), h), h), :]
        ext = jnp.concatenate([prev, u_ref[0, pl.ds(r0, tb), :], nxt], axis=0)
        y = jnp.broadcast_to(w_ref[0, taps:taps + 1, :], (tb, ext.shape[1]))
        for t in range(taps):
            off = t - left
            src = r0 + row + off
            y = y + jnp.where((src >= lo) & (src < hi), ext[h + off:h + off + tb, :], 0.0) * w_ref[0, t:t + 1, :]
        if not split:
            o_ref[0, pl.ds(r0, tb), :] = y
        else:
            ob_c, o_l, ob_l = rest

            @pl.when(r0 < l_ctx)
            def _():
                o_ref[0, pl.ds(r0, tb), :] = y
                ob_c[0, pl.ds(r0, tb), :] = y.astype(jnp.bfloat16)

            @pl.when(r0 >= l_ctx)
            def _():
                o_l[0, pl.ds(r0 - l_ctx, tb), :] = y
                ob_l[0, pl.ds(r0 - l_ctx, tb), :] = y.astype(jnp.bfloat16)
        return carry

    lax.fori_loop(0, l_tot // tb, block, 0)


def _dwconv_slabs(u_all, col0, w, b, l_ctx, split):
    bsz, l_tot, _ = u_all.shape
    taps, nc = w.shape
    c = W_HALF
    n = nc // c
    wb = jnp.concatenate([w, b[None], jnp.zeros((_SUBLANES - taps - 1, nc), jnp.float32)], axis=0)
    wb = wb.reshape(_SUBLANES, n, c).transpose(1, 0, 2)
    if split:
        rows = (l_ctx, l_ctx, l_tot - l_ctx, l_tot - l_ctx)
        dts = (jnp.float32, jnp.bfloat16, jnp.float32, jnp.bfloat16)
    else:
        rows, dts = (l_tot,), (jnp.float32,)
    out = pl.pallas_call(
        partial(_dwconv_body, l_ctx=l_ctx, taps=taps, left=(taps - 1) // 2, split=split),
        grid=(bsz, n),
        in_specs=[pl.BlockSpec((1, l_tot, c), lambda bb, s: (bb, 0, s + col0 // c)),
                  pl.BlockSpec((1, _SUBLANES, c), lambda bb, s: (s, 0, 0))],
        out_specs=[pl.BlockSpec((1, r, c), lambda bb, s: (bb, 0, s)) for r in rows],
        out_shape=[jax.ShapeDtypeStruct((bsz, r, nc), dt) for r, dt in zip(rows, dts)],
        compiler_params=pltpu.CompilerParams(dimension_semantics=("parallel", "parallel"),
                                             vmem_limit_bytes=_VMEM_LIMIT),
        name="dwconv",
    )(u_all, wb)
    return out if split else out[0]


def _to_col_major(t, rows):
    bsz, l, ch = t.shape
    return t.reshape(bsz, rows, GRID_W, ch).transpose(0, 2, 1, 3).reshape(bsz, l, ch)


def _to_row_major(t, rows):
    bsz, l, ch = t.shape
    return t.reshape(bsz, GRID_W, rows, ch).transpose(0, 2, 1, 3).reshape(bsz, l, ch)


def _lru_body(xf_ref, xb_ref, wa_ref, wx_ref, par_ref, hf_ref, hb_ref,
              af_scr, bf_scr, ab_scr, bb_scr, sf_scr, sb_scr, *, tb):
    @pl.when(pl.program_id(1) == 0)
    def _():
        sf_scr[...] = jnp.zeros_like(sf_scr)
        sb_scr[...] = jnp.zeros_like(sb_scr)

    def coeffs(x, d, a_scr, b_scr):
        xb = x.astype(jnp.bfloat16)
        r = jnp.dot(xb, wa_ref[d], preferred_element_type=jnp.float32) + par_ref[d, 0:1, :]
        i = jnp.dot(xb, wx_ref[d], preferred_element_type=jnp.float32) + par_ref[d, 1:2, :]
        r = 1.0 / (1.0 + jnp.exp(-r))
        i = 1.0 / (1.0 + jnp.exp(-i))
        log_a = -LRU_C * r * par_ref[d, 2:3, :]
        a = jnp.exp(log_a)
        a_scr[...] = a
        b_scr[...] = jnp.sqrt(-jnp.tanh(log_a) * (a * a + 1.0)) * i * x

    coeffs(xf_ref[0], 0, af_scr, bf_scr)
    coeffs(xb_ref[0], 1, ab_scr, bb_scr)

    def tile(i, carry):
        hf, hb = carry
        base_f = pl.multiple_of(i * _SUBLANES, _SUBLANES)
        base_b = pl.multiple_of(tb - _SUBLANES - i * _SUBLANES, _SUBLANES)
        for r in range(_SUBLANES):
            rf = pl.ds(base_f + r, 1)
            hf = af_scr[rf, :] * hf + bf_scr[rf, :]
            hf_ref[0, rf, :] = hf
            rb = pl.ds(base_b + (_SUBLANES - 1 - r), 1)
            hb = ab_scr[rb, :] * hb + bb_scr[rb, :]
            hb_ref[0, rb, :] = hb
        return hf, hb

    hf, hb = lax.fori_loop(0, tb // _SUBLANES, tile, (sf_scr[...], sb_scr[...]))
    sf_scr[...] = hf
    sb_scr[...] = hb


def _block_diag(w):
    two, h, n, _ = w.shape
    eye = jnp.eye(h, dtype=w.dtype)
    return (w[:, :, :, None, :] * eye[None, :, None, :, None]).reshape(two, h * n, h * n)


def _lru_scan(xc, l_ctx, wa, ba, wx, bx, lam):
    bsz, l_tot, ch = xc.shape
    tb = l_ctx
    nb = l_tot // tb
    par = jnp.stack([ba.reshape(2, ch), bx.reshape(2, ch), jax.nn.softplus(-lam).reshape(2, ch)], axis=1)
    par = jnp.concatenate([par, jnp.zeros((2, _SUBLANES - 3, ch), jnp.float32)], axis=1)
    fwd = pl.BlockSpec((1, tb, ch), lambda b, j: (b, j, 0))
    bwd = pl.BlockSpec((1, tb, ch), lambda b, j: (b, jnp.where(j == 0, 0, nb - j), 0))
    wspec = pl.BlockSpec((2, ch, ch), lambda b, j: (0, 0, 0))
    return pl.pallas_call(
        partial(_lru_body, tb=tb),
        grid=(bsz, nb),
        in_specs=[fwd, bwd, wspec, wspec, pl.BlockSpec((2, _SUBLANES, ch), lambda b, j: (0, 0, 0))],
        out_specs=[fwd, bwd],
        out_shape=[jax.ShapeDtypeStruct(xc.shape, jnp.float32)] * 2,
        scratch_shapes=[pltpu.VMEM((tb, ch), jnp.float32)] * 4 + [pltpu.VMEM((1, ch), jnp.float32)] * 2,
        compiler_params=pltpu.CompilerParams(dimension_semantics=("parallel", "arbitrary")),
        name="lru_scan",
    )(xc, xc, _block_diag(wa).astype(jnp.bfloat16), _block_diag(wx).astype(jnp.bfloat16), par)


def _hyena_filters(l, w1, b1, w2, b2, w3, freq):
    t = jnp.arange(l, dtype=jnp.float32)
    t_unit = t / max(l - 1, 1)
    bands = jnp.linspace(1e-4, HY_BANDS - 1, HY_BANDS, dtype=jnp.float32)
    ang = (2.0 * math.pi / l) * t[:, None] * bands[None, :]
    z = jnp.concatenate([t_unit[:, None], jnp.cos(ang), -jnp.sin(ang)], axis=-1)
    h = jnp.sin(freq * (z @ w1 + b1))
    h = jnp.sin(freq * (h @ w2 + b2))
    h = h @ w3
    dist = jnp.abs(t - (l // 2)) / (l / 2.0)
    deltas = jnp.abs(jnp.linspace(HY_MIN_DECAY, HY_MAX_DECAY, HY_ORDER * W_HALF, dtype=jnp.float32))
    h = h * jnp.exp(-dist[:, None] * deltas[None, :])
    h = h * lax.rsqrt(jnp.sum(h * h, axis=0, keepdims=True) + EPS)
    return h


def _dft_mats(l, tf):
    n = 2 * l
    nl = _DFT_LOW
    nh = l // nl
    t = jnp.arange(l, dtype=jnp.int32)
    w = 2.0 * math.pi / n

    def tables(tv):
        ah = (((nl * jnp.arange(nh, dtype=jnp.int32))[:, None] * tv[None, :]) % n).astype(jnp.float32) * w
        al = ((jnp.arange(nl, dtype=jnp.int32)[:, None] * tv[None, :]) % n).astype(jnp.float32) * w
        return jnp.cos(ah), jnp.sin(ah), jnp.cos(al), jnp.sin(al)

    ch, sh, cl, sl = tables(t)
    f_re = (ch[:, None, :] * cl[None] - sh[:, None, :] * sl[None]).reshape(l, l)
    f_im = -(sh[:, None, :] * cl[None] + ch[:, None, :] * sl[None]).reshape(l, l)
    f_im = f_im.at[0].set(jnp.where(t % 2 == 0, 1.0, -1.0))
    f = jnp.concatenate([f_re.reshape(l // tf, tf, l), f_im.reshape(l // tf, tf, l)], axis=1).reshape(n, l)
    tp = t + l // 2
    ch, sh, cl, sl = (z.T for z in tables(tp))
    g_re = (2.0 / n) * (ch[:, :, None] * cl[:, None, :] - sh[:, :, None] * sl[:, None, :]).reshape(l, l)
    g_im = (-2.0 / n) * (sh[:, :, None] * cl[:, None, :] + ch[:, :, None] * sl[:, None, :]).reshape(l, l)
    g_re = g_re.at[:, 0].set(1.0 / n)
    g_im = g_im.at[:, 0].set(jnp.where(tp % 2 == 0, 1.0, -1.0) / n)
    g = jnp.stack([g_re.reshape(l, l // tf, tf), g_im.reshape(l, l // tf, tf)], axis=2).reshape(l, n)
    return f.astype(jnp.bfloat16), g.astype(jnp.bfloat16)


def _spec_body(f_ref, u_ref, a_ref, b_ref, d_ref, y_ref, *, tf):
    acc = jnp.dot(f_ref[...], u_ref[0], preferred_element_type=jnp.float32)
    xr, xi = acc[:tf], acc[tf:]
    y_ref[0, :tf, :] = (xr * a_ref[...] - xi * b_ref[...]).astype(jnp.bfloat16)
    y_ref[0, tf:, :] = (xr * b_ref[...] + xi * d_ref[...]).astype(jnp.bfloat16)


def _spec_mul(f, u, a, b, d, tf):
    bsz, l, _ = u.shape
    c = a.shape[1]
    coef = pl.BlockSpec((tf, c), lambda i, bb: (i, 0))
    return pl.pallas_call(
        partial(_spec_body, tf=tf),
        grid=(l // tf, bsz),
        in_specs=[pl.BlockSpec((2 * tf, l), lambda i, bb: (i, 0)),
                  pl.BlockSpec((1, l, c), lambda i, bb: (bb, 0, 0)), coef, coef, coef],
        out_specs=pl.BlockSpec((1, 2 * tf, c), lambda i, bb: (bb, i, 0)),
        out_shape=jax.ShapeDtypeStruct((bsz, 2 * l, c), jnp.bfloat16),
        compiler_params=pltpu.CompilerParams(dimension_semantics=("parallel", "parallel"),
                                             vmem_limit_bytes=_VMEM_LIMIT),
        name="hy_spec",
    )(f, u, a, b, d)


def _inv_body(g_ref, y_ref, u_ref, x_ref, skip_ref, z_ref, zb_ref):
    acc = jnp.dot(g_ref[...], y_ref[0], preferred_element_type=jnp.float32)
    z = x_ref[0] * (acc + u_ref[0] * skip_ref[...])
    z_ref[0] = z
    zb_ref[0] = z.astype(jnp.bfloat16)


def _inv_gate(g, y, u, ucol, xg, xcol, skip, tm):
    bsz, n, c = y.shape
    l = n // 2
    blk = pl.BlockSpec((1, tm, c), lambda bb, i: (bb, i, 0))
    return pl.pallas_call(
        _inv_body,
        grid=(bsz, l // tm),
        in_specs=[pl.BlockSpec((tm, n), lambda bb, i: (i, 0)),
                  pl.BlockSpec((1, n, c), lambda bb, i: (bb, 0, 0)),
                  pl.BlockSpec((1, tm, c), lambda bb, i: (bb, i, ucol)),
                  pl.BlockSpec((1, tm, c), lambda bb, i: (bb, i, xcol)),
                  pl.BlockSpec((1, c), lambda bb, i: (0, 0))],
        out_specs=[blk, blk],
        out_shape=[jax.ShapeDtypeStruct((bsz, l, c), jnp.float32), jax.ShapeDtypeStruct((bsz, l, c), jnp.bfloat16)],
        compiler_params=pltpu.CompilerParams(dimension_semantics=("parallel", "parallel"),
                                             vmem_limit_bytes=_VMEM_LIMIT),
        name="hy_inv",
    )(g, y, u, xg, skip.reshape(1, c))


def _hyena_conv2(hc, hb, filt, skip):
    l, c = hc.shape[1], hc.shape[2] // 3
    tf = min(512, l // 2)
    tm = min(512, l)
    f, g = _dft_mats(l, tf)
    hs = _mm(f, filt)
    hs = hs.reshape(l // tf, 2, tf, 2 * c)
    h_re, h_im = hs[:, 0].reshape(l, 2 * c), hs[:, 1].reshape(l, 2 * c)
    slot0 = (jnp.arange(l) == 0)[:, None]
    a = h_re
    b = jnp.where(slot0, 0.0, h_im)
    d = jnp.where(slot0, h_im, h_re)
    y = _spec_mul(f, hb, a[:, :c], b[:, :c], d[:, :c], tf)
    z, zb = _inv_gate(g, y, hc, 0, hc, 1, skip[0], tm)
    y = _spec_mul(f, zb, a[:, c:], b[:, c:], d[:, c:], tf)
    z, _ = _inv_gate(g, y, z, 0, hc, 2, skip[1], tm)
    return z


def _hyena(u, l_ctx, conv_w, conv_b, w1, b1, w2, b2, w3, freq, skip):
    hc_c, hb_c, hc_l, hb_l = _dwconv_slabs(u, 2 * W_HALF, conv_w, conv_b, l_ctx, True)
    zs = []
    for hc, hb in ((hc_c, hb_c), (hc_l, hb_l)):
        filt = _hyena_filters(hc.shape[1], w1, b1, w2, b2, w3, freq)
        zs.append(_hyena_conv2(hc, hb, filt, skip))
    return jnp.concatenate(zs, axis=1)


def _even_mixer(u, l_ctx, lru_conv_w, lru_conv_b, lru_wa, lru_ba, lru_wx, lru_bx,
                lru_lam, hy_conv_w, hy_conv_b, hy_w1, hy_b1, hy_w2, hy_b2, hy_w3, hy_freq, hy_skip):
    w = W_HALF
    xc = _dwconv_slabs(u, 0, lru_conv_w, lru_conv_b, l_ctx, False)
    h_f, h_b = _lru_scan(xc, l_ctx, lru_wa, lru_ba, lru_wx, lru_bx, lru_lam)
    yb = _hyena(u, l_ctx, hy_conv_w, hy_conv_b, hy_w1, hy_b1, hy_w2, hy_b2, hy_w3, hy_freq, hy_skip)
    return h_f, h_b, yb


def _split3(x):
    hi = x.astype(jnp.bfloat16)
    r1 = x - hi.astype(jnp.float32)
    mid = r1.astype(jnp.bfloat16)
    lo = (r1 - mid.astype(jnp.float32)).astype(jnp.bfloat16)
    return jnp.concatenate([hi, mid, lo], axis=1)


def _sum3(r, c):
    return r[:, 0:c] + r[:, c:2 * c] + r[:, 2 * c:3 * c]


def _log_sigmoid(x):
    return jnp.minimum(x, 0.0) - jnp.log(1.0 + jnp.exp(-jnp.abs(x)))


def _hgrn_consts():
    tb, c, sb = HG_TB, HG_C, HG_SUB
    r = np.arange(tb)
    same = (r[:, None] // c) == (r[None, :] // c)
    same4 = (r[:, None] // sb) == (r[None, :] // sb)
    le = r[None, :] <= r[:, None]
    sub = (r % c) // sb
    cum, msk = [], []
    for d in range(2):
        tri = le if d == 0 else le.T
        cum.append(np.concatenate([same & tri, same, same4 & tri], axis=0))
        if d == 0:
            mm = [same & (sub[:, None] == i + 1) & (sub[None, :] <= i) for i in range(c // sb - 1)]
        else:
            mm = [same & (sub[:, None] == i) & (sub[None, :] > i) for i in range(c // sb - 1)]
        msk.append(np.stack(mm))
    cum = jnp.asarray(np.stack(cum).astype(np.float32), dtype=jnp.bfloat16)
    return cum, jnp.asarray(np.stack(msk).astype(np.float32))


def _hgrn_body(ff_ref, fb_ref, v_ref, q_ref, g_ref, lbp_ref, ng_ref, cum_ref, msk_ref, o_ref,
               acc_scr, st_scr, *, n_blk, l_ctx):
    tb, c, sb, dk = HG_TB, HG_C, HG_SUB, HG_DK
    n_c = tb // c
    n_sub = c // sb
    acc_scr[...] = jnp.zeros_like(acc_scr)
    st_scr[...] = jnp.zeros_like(st_scr)
    loglb, log1mlb = lbp_ref[0, 0:1, :], lbp_ref[0, 1:2, :]
    pos = lax.broadcasted_iota(jnp.int32, (tb, dk), 0) % sb
    nt = (((1,), (1,)), ((), ()))
    tn = (((0,), (0,)), ((), ()))
    dot = partial(jnp.dot, preferred_element_type=jnp.float32)

    def direction(base, f_ref, d):
        rows = pl.ds(base, tb)
        fp = f_ref[0, rows, :]
        ls = _log_sigmoid(fp)
        b2 = log1mlb + ls
        logf = jnp.maximum(loglb, b2) + jnp.log(1.0 + jnp.exp(-jnp.abs(loglb - b2)))
        lk = log1mlb + (ls - fp)
        v = v_ref[0, rows, :]
        qr = q_ref[0, rows, :]
        q = qr * (1.0 / (1.0 + jnp.exp(-qr)))
        r = dot(cum_ref[d], _split3(logf))
        cum, tot, pre = _sum3(r[0:tb], dk), _sum3(r[tb:2 * tb], dk), _sum3(r[2 * tb:3 * tb], dk)
        g = lk - cum
        cum3 = cum.reshape(n_c, c, dk)
        qe = (q * jnp.exp(cum)).astype(jnp.bfloat16)
        kt = jnp.exp(g + tot).astype(jnp.bfloat16)
        dec = jnp.exp(tot)
        vb = v.astype(jnp.bfloat16)
        yield
        o = jnp.sum(q * jnp.exp(lk), axis=-1, keepdims=True) * v
        for j in range(1, sb):
            sh = j if d == 0 else tb - j
            valid = (pos >= j) if d == 0 else (pos < sb - j)
            gj = pltpu.roll(g, sh, 0)
            vj = pltpu.roll(v, sh, 0)
            e = jnp.exp(jnp.where(valid, cum + gj, _NEG))
            o = o + jnp.sum(q * e, axis=-1, keepdims=True) * vj
        yield
        q4 = (q * jnp.exp(pre)).astype(jnp.bfloat16)
        att = None
        for i in range(n_sub - 1):
            ref_row = sb * (i + 1) - 1 if d == 0 else sb * (i + 1)
            r_i = jnp.broadcast_to(cum3[:, ref_row:ref_row + 1, :], (n_c, c, dk)).reshape(tb, dk)
            kh = jnp.exp(lk + jnp.minimum(r_i - cum, 0.0)).astype(jnp.bfloat16)
            a_i = lax.dot_general(q4, kh, nt, preferred_element_type=jnp.float32) * msk_ref[d, i]
            att = a_i if att is None else att + a_i
        o = o + dot(att.astype(jnp.bfloat16), vb)
        yield
        order = [i if d == 0 else n_c - 1 - i for i in range(n_c)]
        kvs = [lax.dot_general(vb[cc * c:(cc + 1) * c], kt[cc * c:(cc + 1) * c], tn,
                               preferred_element_type=jnp.float32) for cc in order]
        yield
        st = st_scr[d]
        sts = []
        for cc, kv in zip(order, kvs):
            sts.append(st.astype(jnp.bfloat16))
            st = st * dec[cc * c:cc * c + 1, :] + kv
        st_scr[d] = st
        o_st = [None] * n_c
        for cc, sb16 in zip(order, sts):
            o_st[cc] = lax.dot_general(qe[cc * c:(cc + 1) * c], sb16, nt, preferred_element_type=jnp.float32)
        acc_scr[rows, :] += o + jnp.concatenate(o_st, axis=0)

    def step(j, carry):
        base_f = pl.multiple_of(j * tb, tb)
        jb = jnp.where(j == 0, 0, n_blk - j)
        base_b = pl.multiple_of(jb * tb, tb)
        todo = [direction(base_f, ff_ref, 0), direction(base_b, fb_ref, 1)]
        while todo:
            todo = [gen for gen in todo if next(gen, True) is None]
        return carry

    lax.fori_loop(0, n_blk, step, 0)
    o = acc_scr[l_ctx:, :]
    g = g_ref[0, l_ctx:, :]
    o = o * lax.rsqrt(jnp.mean(o * o, axis=-1, keepdims=True) + EPS) * ng_ref[...]
    o_ref[0] = o * (g * (1.0 / (1.0 + jnp.exp(-g))))


def _hgrn_mix(u_all, lb, norm_g, l_ctx):
    bsz, l_tot, _ = u_all.shape
    nh = HG_HEADS
    n_blk = l_tot // HG_TB
    lbh = lb.reshape(nh, 1, HG_DK)
    lbp = jnp.concatenate([jnp.log(lbh), jnp.log1p(-lbh),
                           jnp.zeros((nh, _SUBLANES - 2, HG_DK), jnp.float32)], axis=1)
    cum_m, msk_m = _hgrn_consts()

    def slab(k):
        return pl.BlockSpec((1, l_tot, HG_DK), lambda b, h, k=k: (b, 0, k * nh + h))

    def const(a):
        return pl.BlockSpec(a.shape, lambda b, h: (0,) * a.ndim)

    return pl.pallas_call(
        partial(_hgrn_body, n_blk=n_blk, l_ctx=l_ctx),
        grid=(bsz, nh),
        in_specs=[slab(0), slab(1), slab(2), slab(4), slab(5),
                  pl.BlockSpec((1, _SUBLANES, HG_DK), lambda b, h: (h, 0, 0)),
                  pl.BlockSpec((1, HG_DK), lambda b, h: (0, 0)),
                  const(cum_m), const(msk_m)],
        out_specs=pl.BlockSpec((1, l_tot - l_ctx, HG_DK), lambda b, h: (b, 0, h)),
        out_shape=jax.ShapeDtypeStruct((bsz, l_tot - l_ctx, W_HALF), jnp.float32),
        scratch_shapes=[pltpu.VMEM((l_tot, HG_DK), jnp.float32), pltpu.VMEM((2, HG_DK, HG_DK), jnp.float32)],
        compiler_params=pltpu.CompilerParams(dimension_semantics=("parallel", "parallel"),
                                             vmem_limit_bytes=_VMEM_LIMIT),
        name="hgrn_mix",
    )(u_all, u_all, u_all, u_all, u_all, lbp, norm_g.reshape(1, HG_DK), cum_m, msk_m)


def _s5_weights(lam_re, lam_im, log_dt, b_re, b_im, c_re, c_im):
    t_len, f32 = S5_T, jnp.float32
    lre = jnp.minimum(lam_re, S5_MAX_RE)
    dt = jnp.exp(log_dt)[..., None]
    ar, ai = lre * dt, lam_im * dt
    k = jnp.arange(t_len + 1, dtype=f32)[:, None, None, None]
    mag = jnp.exp(k * ar)
    pre, pim = mag * jnp.cos(k * ai), mag * jnp.sin(k * ai)
    nr, ni, den = pre[1] - 1.0, pim[1], lre * lre + lam_im * lam_im
    cr, ci = (nr * lre + ni * lam_im) / den, (ni * lre - nr * lam_im) / den
    cbr = cr[..., None] * b_re - ci[..., None] * b_im
    cbi = cr[..., None] * b_im + ci[..., None] * b_re
    c_re_t, c_im_t = jnp.swapaxes(c_re, 1, 2), jnp.swapaxes(c_im, 1, 2)
    clre = c_re_t * pre[..., None] - c_im_t * pim[..., None]
    clim = c_re_t * pim[..., None] + c_im_t * pre[..., None]
    g = lam_re.shape[1]

    def w_dir(d, e):
        pr, pi = pre[e, d], pim[e, d]
        wr = pr[..., None] * cbr[d] - pi[..., None] * cbi[d]
        wi = pr[..., None] * cbi[d] + pi[..., None] * cbr[d]
        return [jnp.transpose(z, (1, 0, 3, 2)).reshape(g, t_len * S5_H, S5_P) for z in (wr, wi)]

    wf, wb = w_dir(0, jnp.arange(t_len - 1, -1, -1)), w_dir(1, jnp.arange(t_len))
    w = jnp.concatenate(wf + wf[::-1] + wb + wb[::-1], axis=-1)
    dec = jnp.concatenate([pre[t_len, 0], pre[t_len, 0], -pim[t_len, 0], pim[t_len, 0],
                           pre[t_len, 1], pre[t_len, 1], -pim[t_len, 1], pim[t_len, 1]], axis=-1)[:, None, :]

    def m_dir(d, e):
        mr, mi = clre[e, d], -clim[e, d]
        return [jnp.transpose(z, (1, 2, 0, 3)).reshape(g, S5_P, t_len * S5_H) for z in (mr, mi)]

    m = jnp.concatenate(m_dir(0, jnp.arange(1, t_len + 1)) + m_dir(1, jnp.arange(t_len, 0, -1)), axis=1)
    kk = (jnp.einsum('kdgpo,dgpi->kdgoi', clre[:t_len], cbr, precision=_HI)
          - jnp.einsum('kdgpo,dgpi->kdgoi', clim[:t_len], cbi, precision=_HI))
    delta = jnp.arange(t_len)[None, :] - jnp.arange(t_len)[:, None]
    kf = kk[jnp.clip(delta, 0, t_len - 1), 0]
    kb = kk[jnp.clip(-delta, 0, t_len - 1), 1]
    dl = delta[:, :, None, None, None]
    toep = jnp.where(dl > 0, kf, 0.0) + jnp.where(dl < 0, kb, 0.0) + jnp.where(dl == 0, kf + kb, 0.0)
    toep = jnp.transpose(toep, (2, 0, 4, 1, 3)).reshape(g, t_len * S5_H, t_len * S5_H)
    return w, dec, m, toep


def _s5_body(x_ref, w_ref, dec_ref, m_ref, toep_ref, y_ref, z_scr, s_scr, *, bsz, n_ctx, n_lat):
    p2 = 2 * S5_P
    n_all = n_ctx + n_lat
    xb = x_ref[0].astype(jnp.bfloat16)
    z_scr[...] = jnp.dot(xb, w_ref[0], preferred_element_type=jnp.float32)
    dec = jnp.broadcast_to(dec_ref[0], (bsz, 4 * p2))
    a1f, a2f = dec[:, 0:p2], dec[:, p2:2 * p2]
    a1b, a2b = dec[:, 2 * p2:3 * p2], dec[:, 3 * p2:4 * p2]

    def step(i, carry):
        sf, tf, sb, tb = carry
        rf = pl.multiple_of(i * bsz, bsz)
        cb = jnp.where(i < n_ctx, n_ctx - 1 - i, n_all + n_ctx - 1 - i)
        rb = pl.multiple_of(cb * bsz, bsz)
        s_scr[pl.ds(rf, bsz), 0:p2] = sf
        s_scr[pl.ds(rb, bsz), p2:2 * p2] = sb
        zf = z_scr[pl.ds(rf, bsz), 0:2 * p2]
        zb = z_scr[pl.ds(rb, bsz), 2 * p2:4 * p2]
        return (sf * a1f + tf * a2f + zf[:, :p2], tf * a1f - sf * a2f + zf[:, p2:],
                sb * a1b + tb * a2b + zb[:, :p2], tb * a1b - sb * a2b + zb[:, p2:])

    zero = jnp.zeros((bsz, p2), jnp.float32)
    lax.fori_loop(0, n_all, step, (zero, zero, zero, zero), unroll=4)
    r0 = n_ctx * bsz
    y_ref[0] = (jnp.dot(xb[r0:], toep_ref[0], preferred_element_type=jnp.float32)
                + jnp.dot(s_scr[r0:, :].astype(jnp.bfloat16), m_ref[0], preferred_element_type=jnp.float32))


def _s5_mix(u_c, u_l, lam_re, lam_im, log_dt, b_re, b_im, c_re, c_im):
    bsz, l_ctx, gh = u_c.shape
    l_lat = u_l.shape[1]
    g, t_len = gh // S5_H, S5_T
    n_ctx, n_lat = l_ctx // t_len, l_lat // t_len
    w, dec, m, toep = _s5_weights(lam_re, lam_im, log_dt, b_re, b_im, c_re, c_im)

    def chunked(u, n):
        return jnp.transpose(u.reshape(bsz, n, t_len, g, S5_H), (3, 1, 0, 2, 4)).reshape(g, n * bsz, t_len * S5_H)

    x = jnp.concatenate([chunked(u_c, n_ctx), chunked(u_l, n_lat)], axis=1)
    rows, k = (n_ctx + n_lat) * bsz, t_len * S5_H
    y = pl.pallas_call(
        partial(_s5_body, bsz=bsz, n_ctx=n_ctx, n_lat=n_lat),
        grid=(g,),
        in_specs=[pl.BlockSpec((1, rows, k), lambda i: (i, 0, 0)),
                  pl.BlockSpec((1, k, 8 * S5_P), lambda i: (i, 0, 0)),
                  pl.BlockSpec((1, 1, 8 * S5_P), lambda i: (i, 0, 0)),
                  pl.BlockSpec((1, 4 * S5_P, k), lambda i: (i, 0, 0)),
                  pl.BlockSpec((1, k, k), lambda i: (i, 0, 0))],
        out_specs=pl.BlockSpec((1, n_lat * bsz, k), lambda i: (i, 0, 0)),
        out_shape=jax.ShapeDtypeStruct((g, n_lat * bsz, k), jnp.float32),
        scratch_shapes=[pltpu.VMEM((rows, 8 * S5_P), jnp.float32), pltpu.VMEM((rows, 4 * S5_P), jnp.float32)],
        compiler_params=pltpu.CompilerParams(dimension_semantics=("parallel",), vmem_limit_bytes=_VMEM_LIMIT),
        name="s5_mix",
    )(x, w.astype(jnp.bfloat16), dec, m.astype(jnp.bfloat16), toep.astype(jnp.bfloat16))
    return jnp.transpose(y.reshape(g, n_lat, bsz, t_len, S5_H), (2, 1, 3, 0, 4)).reshape(bsz, l_lat, gh)


def _glu_body(y_ref, u_ref, d_ref, w_ref, b_ref, o_ref):
    y = _gelu_tanh(y_ref[0] + u_ref[0] * d_ref[...])
    z = jnp.dot(y.astype(jnp.bfloat16), w_ref[...], preferred_element_type=jnp.float32) + b_ref[...]
    o_ref[0] = y * (1.0 / (1.0 + jnp.exp(-z)))


def _s5_out(y, u_all, l_ctx, col0, d_skip, glu_w, glu_b):
    bsz, l, c = y.shape
    tm = l_ctx
    vec = pl.BlockSpec((1, c), lambda b, i: (0, 0))
    return pl.pallas_call(
        _glu_body,
        grid=(bsz, l // tm),
        in_specs=[pl.BlockSpec((1, tm, c), lambda b, i: (b, i, 0)),
                  pl.BlockSpec((1, tm, c), lambda b, i: (b, i + 1, col0 // c)), vec,
                  pl.BlockSpec((c, c), lambda b, i: (0, 0)), vec],
        out_specs=pl.BlockSpec((1, tm, c), lambda b, i: (b, i, 0)),
        out_shape=jax.ShapeDtypeStruct(y.shape, jnp.float32),
        compiler_params=pltpu.CompilerParams(dimension_semantics=("parallel", "parallel")),
        name="s5_glu",
    )(y, u_all, d_skip.reshape(1, c), glu_w.astype(jnp.bfloat16), glu_b.reshape(1, c))


def _odd_mixer(u, l_ctx, lb, hg_norm_g, lam_re, lam_im, log_dt, b_re, b_im, c_re, c_im, d_skip, glu_w, glu_b):
    w = W_HALF
    hg_l = _hgrn_mix(u, lb, hg_norm_g, l_ctx)
    s_c, s_l = u[:, :l_ctx, 3 * w:4 * w], u[:, l_ctx:, 3 * w:4 * w]
    y_l = _s5_mix(s_c, s_l, lam_re, lam_im, log_dt, b_re, b_im, c_re, c_im)
    return hg_l, _s5_out(y_l, u, l_ctx, 3 * w, d_skip, glu_w, glu_b)


def _router_body(x_ref, g_ref, sh_ref, sc_ref, wr_ref, h_ref, lg_ref):
    x = x_ref[0]
    h = x * lax.rsqrt(jnp.mean(x * x, axis=-1, keepdims=True) + EPS) * g_ref[...]
    h = h * (1.0 + sc_ref[0]) + sh_ref[0]
    hb = h.astype(jnp.bfloat16)
    lg_ref[0] = jnp.dot(hb, wr_ref[...], preferred_element_type=jnp.float32)
    bits = pltpu.bitcast(hb.astype(jnp.float32), jnp.uint32)
    half = bits.shape[1] // 2
    h_ref[0] = (bits[:, :half] >> 16) | (bits[:, half:] & jnp.uint32(0xFFFF0000))


def _router(x, norm_g, shift, scale, w_router):
    bsz, n, d = x.shape
    tm = min(n, 512)
    vec = pl.BlockSpec((1, 1, d), lambda b, i: (b, 0, 0))
    return pl.pallas_call(
        _router_body,
        grid=(bsz, n // tm),
        in_specs=[pl.BlockSpec((1, tm, d), lambda b, i: (b, i, 0)),
                  pl.BlockSpec((1, d), lambda b, i: (0, 0)), vec, vec,
                  pl.BlockSpec((d, N_EXPERTS), lambda b, i: (0, 0))],
        out_specs=[pl.BlockSpec((1, tm, d // 2), lambda b, i: (b, i, 0)),
                   pl.BlockSpec((1, tm, N_EXPERTS), lambda b, i: (b, i, 0))],
        out_shape=[jax.ShapeDtypeStruct((bsz, n, d // 2), jnp.uint32),
                   jax.ShapeDtypeStruct((bsz, n, N_EXPERTS), jnp.float32)],
        compiler_params=pltpu.CompilerParams(dimension_semantics=("parallel", "parallel"),
                                             vmem_limit_bytes=_VMEM_LIMIT),
        name="moe_router",
    )(x, norm_g.reshape(1, d), jnp.broadcast_to(shift, (bsz, 1, d)), jnp.broadcast_to(scale, (bsz, 1, d)),
      w_router.astype(jnp.bfloat16))


def _expert_body(idx_ref, nxt_ref, h_ref, gate_ref, mod_ref, wg_ref, wu_ref, wd_ref, y_ref, x_cur, x_nxt,
                 *, cap, grp, n_f):
    rows = grp * cap

    @pl.when(pl.program_id(1) == 0)
    def _():
        def grab(g, carry):
            base = pl.multiple_of(g * _SUBLANES, _SUBLANES)
            for r in range(_SUBLANES):
                x_nxt[pl.ds(base + r, 1), :] = h_ref[base // cap, pl.ds(idx_ref[0, 0, base + r], 1), :]
            return carry

        lax.fori_loop(0, rows // _SUBLANES, grab, 0)

    x_cur[...] = x_nxt[...]
    w32 = x_cur[...]
    xb = jnp.concatenate([pltpu.bitcast(w32 << 16, jnp.float32),
                          pltpu.bitcast(w32 & jnp.uint32(0xFFFF0000), jnp.float32)], axis=1).astype(jnp.bfloat16)
    for r in range(rows):
        x_nxt[pl.ds(r, 1), :] = h_ref[r // cap, pl.ds(nxt_ref[0, 0, r], 1), :]
    tf = wg_ref.shape[3] // n_f
    acc = None
    for f in range(n_f):
        cs = slice(f * tf, (f + 1) * tf)
        a = jnp.dot(xb, wg_ref[0, 0, :, cs], preferred_element_type=jnp.float32)
        u = jnp.dot(xb, wu_ref[0, 0, :, cs], preferred_element_type=jnp.float32)
        hh = (a * (1.0 / (1.0 + jnp.exp(-a))) * u).astype(jnp.bfloat16)
        part = jnp.dot(hh, wd_ref[0, 0, cs, :], preferred_element_type=jnp.float32)
        acc = part if acc is None else acc + part
    acc = acc * gate_ref[0]
    for g in range(grp):
        y_ref[g, 0] = acc[g * cap:(g + 1) * cap] * mod_ref[g]


def _expert_ffn(h32, idx, gate, mod, layer, w_gate, w_up, w_down):
    bsz, n, dh = h32.shape
    d = 2 * dh
    cap = idx.shape[-1]
    _, e, _, ff = w_gate.shape
    grp = max(1, min(bsz, _EXPERT_ROWS // cap))
    ng = bsz // grp
    rows = grp * cap
    regroup = lambda t: t.reshape(ng, grp, e, cap).transpose(0, 2, 1, 3).reshape(ng * e, rows)
    idx3 = regroup(idx)[:, None, :]
    ge = lambda b, k: (b * e + k, 0, 0)
    ge_next = lambda b, k: (b * e + jnp.minimum(k + 1, e - 1), 0, 0)
    wspec = lambda s: pl.BlockSpec((1, 1) + s, lambda b, k: (layer, k, 0, 0))
    y = pl.pallas_call(
        partial(_expert_body, cap=cap, grp=grp, n_f=3),
        grid=(ng, e),
        in_specs=[pl.BlockSpec((1, 1, rows), ge, memory_space=pltpu.SMEM),
                  pl.BlockSpec((1, 1, rows), ge_next, memory_space=pltpu.SMEM),
                  pl.BlockSpec((grp, n, dh), lambda b, k: (b, 0, 0)),
                  pl.BlockSpec((1, rows, 1), ge),
                  pl.BlockSpec((grp, 1, d), lambda b, k: (b, 0, 0)),
                  wspec((d, ff)), wspec((d, ff)), wspec((ff, d))],
        out_specs=pl.BlockSpec((grp, 1, cap, d), lambda b, k: (b, k, 0, 0)),
        out_shape=jax.ShapeDtypeStruct((bsz, e, cap, d), jnp.float32),
        scratch_shapes=[pltpu.VMEM((rows, dh), jnp.uint32), pltpu.VMEM((rows, dh), jnp.uint32)],
        compiler_params=pltpu.CompilerParams(dimension_semantics=("parallel", "arbitrary"),
                                             vmem_limit_bytes=_VMEM_LIMIT_BIG),
        name="moe_expert",
    )(idx3, idx3, h32, regroup(gate)[:, :, None], jnp.broadcast_to(mod, (bsz, 1, d)), w_gate, w_up, w_down)
    return y.reshape(bsz * e, cap, d)


def _combine_body(idx_ref, y_ref, o_ref, *, cap):
    @pl.when(pl.program_id(1) == 0)
    def _():
        o_ref[...] = jnp.zeros_like(o_ref)

    def group(g, carry):
        base = pl.multiple_of(g * _SUBLANES, _SUBLANES)
        ys = y_ref[0, pl.ds(base, _SUBLANES), :]
        ts = [idx_ref[0, 0, base + r] for r in range(_SUBLANES)]
        cur = [o_ref[0, pl.ds(t, 1), :] for t in ts]
        for r in range(_SUBLANES):
            o_ref[0, pl.ds(ts[r], 1), :] = cur[r] + ys[r:r + 1, :]
        return carry

    lax.fori_loop(0, cap // _SUBLANES, group, 0)


def _combine(y, idx, n):
    be, cap, d = y.shape
    bsz = be // N_EXPERTS
    return pl.pallas_call(
        partial(_combine_body, cap=cap),
        grid=(bsz, N_EXPERTS),
        in_specs=[pl.BlockSpec((1, 1, cap), lambda b, k: (b * N_EXPERTS + k, 0, 0), memory_space=pltpu.SMEM),
                  pl.BlockSpec((1, cap, d), lambda b, k: (b * N_EXPERTS + k, 0, 0))],
        out_specs=pl.BlockSpec((1, n, d), lambda b, k: (b, 0, 0)),
        out_shape=jax.ShapeDtypeStruct((bsz, n, d), jnp.float32),
        compiler_params=pltpu.CompilerParams(dimension_semantics=("parallel", "arbitrary"),
                                             vmem_limit_bytes=_VMEM_LIMIT),
        name="moe_combine",
    )(idx.reshape(be, 1, cap), y)


def _moe(x, norm_g, shift, scale, mod, w_router, layer, w_gate, w_up, w_down):
    bsz, n, d = x.shape
    cap = max(1, (EC_CAPACITY * n) // N_EXPERTS)
    h32, logits = _router(x, norm_g, shift, scale, w_router)
    aff = jax.nn.softmax(logits, axis=-1)
    gate, idx = lax.top_k(jnp.swapaxes(aff, 1, 2), cap)
    y = _expert_ffn(h32, idx, gate, mod, layer, w_gate, w_up, w_down)
    return _combine(y, idx, n)


def kernel(x, c, ctx, c_ctx, mod_w, mod_b, norm1_g, norm2_g, out_w, even_in_w, lru_conv_w, lru_conv_b, lru_wa, lru_ba, lru_wx, lru_bx, lru_lam, hy_conv_w, hy_conv_b, hy_w1, hy_b1, hy_w2, hy_b2, hy_w3, hy_freq, hy_skip, odd_in_w, hg_lb, hg_norm_g, s5_lam_re, s5_lam_im, s5_log_dt, s5_b_re, s5_b_im, s5_c_re, s5_c_im, s5_d, s5_glu_w, s5_glu_b, router_w, ex_w_gate, ex_w_up, ex_w_down, final_g):
    rows = x.shape[1] // GRID_W
    l_ctx = ctx.shape[1]
    p = jax.nn.softmax(hg_lb, axis=0)
    lb_all = jnp.clip(jnp.cumsum(p, axis=0) - p[0], 0.0, 1.0 - 1e-4)
    ex_wg, ex_wu, ex_wd = (t.astype(jnp.bfloat16) for t in (ex_w_gate, ex_w_up, ex_w_down))
    sc_lat = jax.nn.silu(c)
    sc_ctx = jax.nn.silu(c_ctx)[None]
    for i in range(DEPTH):
        need_ctx = i < DEPTH - 1
        j = i // 2
        ml = [t[:, None, :] for t in jnp.split(sc_lat @ mod_w[i] + mod_b[i], 6, axis=-1)]
        mc = [t[:, None, :] for t in jnp.split(sc_ctx @ mod_w[i] + mod_b[i], 6, axis=-1)]
        shift, scale, gate = (_two_part(mc[k], ml[k]) for k in range(3))
        if i % 2 == 0:
            x_all = jnp.concatenate([ctx, x], axis=1)
            u = _norm_mm(x_all, norm1_g[i], shift, scale, even_in_w[j], l_ctx)
            h_f, h_b, yb = _even_mixer(u, l_ctx, lru_conv_w[j], lru_conv_b[j], lru_wa[j], lru_ba[j], lru_wx[j],
                                       lru_bx[j], lru_lam[j], hy_conv_w[j], hy_conv_b[j], hy_w1[j], hy_b1[j],
                                       hy_w2[j], hy_b2[j], hy_w3[j], hy_freq[j], hy_skip[j])
            x_all = _proj_res_lru(h_f, h_b, u, yb, out_w[i], x_all, gate, l_ctx)
            ctx, x = x_all[:, :l_ctx], x_all[:, l_ctx:]
        else:
            assert not need_ctx
            x_all = jnp.concatenate([ctx, _to_col_major(x, rows)], axis=1)
            u = _norm_mm(x_all, norm1_g[i], shift, scale, odd_in_w[j], l_ctx)
            hg_l, s5_l = _odd_mixer(u, l_ctx, lb_all[i], hg_norm_g[j], s5_lam_re[j], s5_lam_im[j],
                                    s5_log_dt[j], s5_b_re[j], s5_b_im[j], s5_c_re[j], s5_c_im[j], s5_d[j],
                                    s5_glu_w[j], s5_glu_b[j])
            x = _to_row_major(_proj_res(hg_l, s5_l, out_w[i], x_all, gate, l_ctx, l_ctx), rows)
        ex = (router_w[i], i, ex_wg, ex_wu, ex_wd)
        moe_l = _moe(x, norm2_g[i], ml[3], ml[4], ml[5], *ex)
        if need_ctx:
            x = x + moe_l
            ctx = ctx + _moe(ctx, norm2_g[i], mc[3], mc[4], mc[5], *ex)
    return _add_norm(x, moe_l, final_g)
```
